```python
import math
import functools
import jax
import jax.numpy as jnp
from jax import lax
import numpy as np

D_MODEL = 1024
BATCH = 4
SEQ = 4096
DEPTH = 1
DEC_BATCH = 32
DEC_SEQ = 8
PAST_LEN = 16384
PAGE_SIZE = 128

N_HEADS = 4
HEAD_DIM = D_MODEL // 16
V_DIM = 2 * HEAD_DIM
ATTN_WIDTH = N_HEADS * V_DIM
QK_WIDTH = N_HEADS * 2 * HEAD_DIM
ATTN_SCALE = HEAD_DIM ** -0.5
CONV_CH = D_MODEL - ATTN_WIDTH
CONV_WIDTH = 31
IN_WIDTH = 2 * QK_WIDTH + ATTN_WIDTH + 2 * CONV_CH
N_EXPERTS = 64
N_GROUPS = 8
TOPK_GROUPS = 4
TOP_K = 8
D_EXPERT = D_MODEL // 4
D_SHARED = D_MODEL // 4
ROUTED_SCALE = 2.5
MOE_BLOCK = 64
Q_BLOCK = 128
DEEPNORM_ALPHA = (2 * DEPTH) ** 0.25
DEEPNORM_BETA = (8 * DEPTH) ** -0.25
NORM_EPS = 1e-5
NEG_INF = -1e30

kernel_name = "diffattn_conformer_moe_deepnorm_step"


def lambda_init(layer):
    return 0.8 - 0.6 * math.exp(-0.3 * layer)


def layer_norm(x, g, b):
    xf = x.astype(jnp.float32)
    mu = jnp.mean(xf, axis=-1, keepdims=True)
    var = jnp.mean(jnp.square(xf - mu), axis=-1, keepdims=True)
    y = (xf - mu) * lax.rsqrt(var + NORM_EPS)
    return (y * g.astype(jnp.float32) + b.astype(jnp.float32)).astype(x.dtype)


def rms_norm(x, g):
    xf = x.astype(jnp.float32)
    y = xf * lax.rsqrt(jnp.mean(jnp.square(xf), axis=-1, keepdims=True) + NORM_EPS)
    return (y * g.astype(jnp.float32)).astype(x.dtype)


def swiglu(x, w_gate, w_up, w_down):
    return (jax.nn.silu(x @ w_gate) * (x @ w_up)) @ w_down


def in_projection(x, w_in):
    b, s, _ = x.shape
    p = jnp.einsum("bsd,de->bse", x, w_in)
    q = p[..., :QK_WIDTH].reshape(b, s, N_HEADS, 2 * HEAD_DIM)
    k = p[..., QK_WIDTH:2 * QK_WIDTH].reshape(b, s, N_HEADS, 2 * HEAD_DIM)
    v = p[..., 2 * QK_WIDTH:2 * QK_WIDTH + ATTN_WIDTH].reshape(b, s, N_HEADS, V_DIM)
    glu_a, glu_b = jnp.split(p[..., 2 * QK_WIDTH + ATTN_WIDTH:], 2, axis=-1)
    return q, k, v, glu_a * jax.nn.sigmoid(glu_b)


def diff_lambda(lam_q1, lam_k1, lam_q2, lam_k2, lam_init):
    f = lambda a: a.astype(jnp.float32)
    return (jnp.exp(jnp.sum(f(lam_q1) * f(lam_k1)))
            - jnp.exp(jnp.sum(f(lam_q2) * f(lam_k2))) + lam_init)


def diff_attend(q, k, v, q_pos, k_pos, lam):
    causal = k_pos[None, :] <= q_pos[:, None]

    def attn_map(qh, kh):
        s = jnp.einsum("bqhd,bkhd->bhqk", qh, kh,
                       preferred_element_type=jnp.float32) * ATTN_SCALE
        return jax.nn.softmax(jnp.where(causal, s, NEG_INF), axis=-1)

    a = (attn_map(q[..., :HEAD_DIM], k[..., :HEAD_DIM])
         - lam * attn_map(q[..., HEAD_DIM:], k[..., HEAD_DIM:]))
    return jnp.einsum("bhqk,bkhe->bqhe", a, v.astype(jnp.float32)).astype(v.dtype)


def prompt_attention(q, k, v, lam):
    b, s = q.shape[:2]
    n_qb = s // Q_BLOCK
    q_blocks = q.reshape(b, n_qb, Q_BLOCK, N_HEADS, 2 * HEAD_DIM).swapaxes(0, 1)
    k_pos = jnp.arange(s)

    def block(args):
        i, q_blk = args
        return diff_attend(q_blk, k, v, i * Q_BLOCK + jnp.arange(Q_BLOCK), k_pos, lam)

    o = lax.map(block, (jnp.arange(n_qb), q_blocks))
    return o.swapaxes(0, 1).reshape(b, s, N_HEADS, V_DIM)


def sample_attention(q, k, v, lam, layer, cache_k, cache_v, page_table):
    ds = q.shape[1]
    past = page_table.shape[1] * cache_k.shape[2]
    q_pos = past + jnp.arange(ds)
    k_pos = jnp.arange(past + ds)

    def one_sequence(args):
        q_s, k_s, v_s, pages = args
        kk = jnp.concatenate(
            [cache_k[layer, pages].reshape(past, N_HEADS, 2 * HEAD_DIM), k_s], axis=0)
        vv = jnp.concatenate(
            [cache_v[layer, pages].reshape(past, N_HEADS, V_DIM), v_s], axis=0)
        return diff_attend(q_s[None], kk[None], vv[None], q_pos, k_pos, lam)[0]

    return lax.map(one_sequence, (q, k, v, page_table))


def conv_module(u, prev, w_dw, b_dw, ln_g, ln_b):
    ext = jnp.concatenate([prev.astype(u.dtype), u], axis=1)
    y = lax.conv_general_dilated(
        ext, w_dw[:, None, :].astype(u.dtype), window_strides=(1,), padding="VALID",
        dimension_numbers=("NWC", "WIO", "NWC"), feature_group_count=CONV_CH)
    y = jax.nn.silu(layer_norm(y + b_dw, ln_g, ln_b))
    return y, ext[:, -(CONV_WIDTH - 1):]


def routed_experts(t, idx, wts, w_gate, w_up, w_down):
    n, d = t.shape
    na = n * TOP_K
    flat_e = idx.reshape(-1)
    order = jnp.argsort(flat_e)
    sorted_e = flat_e[order]
    counts = jnp.bincount(flat_e, length=N_EXPERTS)
    padded = (counts + MOE_BLOCK - 1) // MOE_BLOCK * MOE_BLOCK
    start = jnp.cumsum(counts) - counts
    pstart = jnp.cumsum(padded) - padded
    rank = jnp.arange(na, dtype=jnp.int32) - start[sorted_e]
    dest = jnp.zeros((na,), jnp.int32).at[order].set((pstart[sorted_e] + rank).astype(jnp.int32))
    n_blocks = -(-na // MOE_BLOCK) + N_EXPERTS
    rows = n_blocks * MOE_BLOCK
    row_token = jnp.full((rows,), n, jnp.int32).at[dest].set(
        jnp.arange(na, dtype=jnp.int32) // TOP_K)
    row_w = jnp.zeros((rows,), jnp.float32).at[dest].set(wts.reshape(-1))
    t_pad = jnp.concatenate([t, jnp.zeros((1, d), t.dtype)], axis=0)
    xb = t_pad[row_token].reshape(n_blocks, MOE_BLOCK, d)
    block_e = jnp.minimum(
        jnp.searchsorted(jnp.cumsum(padded), jnp.arange(n_blocks) * MOE_BLOCK, side="right"),
        N_EXPERTS - 1)

    def run(args):
        xs, rw, e = args
        return swiglu(xs, w_gate[e], w_up[e], w_down[e]) * rw[:, None].astype(xs.dtype)

    yb = lax.map(run, (xb, row_w.reshape(n_blocks, MOE_BLOCK), block_e)).reshape(rows, d)
    return jax.ops.segment_sum(yb, row_token, num_segments=n + 1)[:n]


def moe_ffn(h, w_router, b_router, w_gate, w_up, w_down, w_sh_gate, w_sh_up, w_sh_down):
    shp = h.shape
    t = h.reshape(-1, shp[-1])
    n = t.shape[0]
    scores = jax.nn.sigmoid(
        jnp.einsum("td,de->te", t, w_router, preferred_element_type=jnp.float32))
    sel = scores + b_router.astype(jnp.float32)
    grp_score = lax.top_k(sel.reshape(n, N_GROUPS, N_EXPERTS // N_GROUPS), 2)[0].sum(-1)
    _, gidx = lax.top_k(grp_score, TOPK_GROUPS)
    gmask = jnp.any(gidx[..., None] == jnp.arange(N_GROUPS), axis=1)
    sel = jnp.where(jnp.repeat(gmask, N_EXPERTS // N_GROUPS, axis=-1), sel, NEG_INF)
    _, idx = lax.top_k(sel, TOP_K)
    wts = jnp.take_along_axis(scores, idx, axis=-1)
    wts = wts / jnp.sum(wts, axis=-1, keepdims=True) * ROUTED_SCALE
    routed = routed_experts(t, idx, wts, w_gate, w_up, w_down)
    shared = swiglu(t, w_sh_gate, w_sh_up, w_sh_down)
    return (routed + shared).reshape(shp)


def decoder_layer(x, conv_prev, attend, lam_init, w_in, lam_q1, lam_k1, lam_q2, lam_k2,
                  subln_g, w_dw, b_dw, conv_ln_g, conv_ln_b, w_o, ln1_g, ln1_b,
                  w_router, b_router, w_gate, w_up, w_down, w_sh_gate, w_sh_up,
                  w_sh_down, ln2_g, ln2_b):
    b, s, _ = x.shape
    q, k, v, u = in_projection(x, w_in)
    lam = diff_lambda(lam_q1, lam_k1, lam_q2, lam_k2, lam_init)
    o = rms_norm(attend(q, k, v, lam), subln_g) * (1.0 - lam_init)
    c, conv_state = conv_module(u, conv_prev, w_dw, b_dw, conv_ln_g, conv_ln_b)
    mix = jnp.concatenate([o.reshape(b, s, ATTN_WIDTH), c], axis=-1) @ w_o
    x = layer_norm(DEEPNORM_ALPHA * x + mix, ln1_g, ln1_b)
    ffn = moe_ffn(x, w_router, b_router, w_gate, w_up, w_down, w_sh_gate, w_sh_up, w_sh_down)
    x = layer_norm(DEEPNORM_ALPHA * x + ffn, ln2_g, ln2_b)
    return x, k, v, conv_state


def setup_inputs(seed: int = 0) -> dict:
    key = jax.random.key(seed)
    ks = jax.random.split(key, 32)
    f32 = jnp.float32
    n_pages = PAST_LEN // PAGE_SIZE
    n_used = DEC_BATCH * n_pages
    n_pool = n_used + max(1, n_used // 4)

    def nrm(k, shape, scale):
        return jax.random.normal(k, shape, f32) * scale

    page_table = jax.random.permutation(ks[5], n_pool)[:n_used].reshape(
        DEC_BATCH, n_pages).astype(jnp.int32)
    return {
        "x_prompt": nrm(ks[0], (BATCH, SEQ, D_MODEL), 1.0),
        "x_sample": nrm(ks[1], (DEC_BATCH, DEC_SEQ, D_MODEL), 1.0),
        "cache_k": nrm(ks[2], (DEPTH, n_pool, PAGE_SIZE, N_HEADS, 2 * HEAD_DIM), 1.0),
        "cache_v": nrm(ks[3], (DEPTH, n_pool, PAGE_SIZE, N_HEADS, V_DIM), 1.0),
        "state_conv": nrm(ks[4], (DEPTH, DEC_BATCH, CONV_WIDTH - 1, CONV_CH), 0.5),
        "page_table": page_table,
        "w_in": nrm(ks[6], (DEPTH, D_MODEL, IN_WIDTH), D_MODEL ** -0.5),
        "lam_q1": nrm(ks[7], (DEPTH, HEAD_DIM), 0.1),
        "lam_k1": nrm(ks[8], (DEPTH, HEAD_DIM), 0.1),
        "lam_q2": nrm(ks[9], (DEPTH, HEAD_DIM), 0.1),
        "lam_k2": nrm(ks[10], (DEPTH, HEAD_DIM), 0.1),
        "subln_g": 1.0 + nrm(ks[11], (DEPTH, V_DIM), 0.02),
        "w_dw": nrm(ks[12], (DEPTH, CONV_WIDTH, CONV_CH), CONV_WIDTH ** -0.5),
        "b_dw": nrm(ks[13], (DEPTH, CONV_CH), 0.02),
        "conv_ln_g": 1.0 + nrm(ks[14], (DEPTH, CONV_CH), 0.02),
        "conv_ln_b": nrm(ks[15], (DEPTH, CONV_CH), 0.02),
        "w_o": nrm(ks[16], (DEPTH, D_MODEL, D_MODEL), DEEPNORM_BETA * D_MODEL ** -0.5),
        "ln1_g": 1.0 + nrm(ks[17], (DEPTH, D_MODEL), 0.02),
        "ln1_b": nrm(ks[18], (DEPTH, D_MODEL), 0.02),
        "w_router": nrm(ks[19], (DEPTH, D_MODEL, N_EXPERTS), D_MODEL ** -0.5),
        "b_router": nrm(ks[20], (DEPTH, N_EXPERTS), 0.01),
        "w_gate": nrm(ks[21], (DEPTH, N_EXPERTS, D_MODEL, D_EXPERT), D_MODEL ** -0.5),
        "w_up": nrm(ks[22], (DEPTH, N_EXPERTS, D_MODEL, D_EXPERT), D_MODEL ** -0.5),
        "w_down": nrm(ks[23], (DEPTH, N_EXPERTS, D_EXPERT, D_MODEL), DEEPNORM_BETA * D_EXPERT ** -0.5),
        "w_sh_gate": nrm(ks[24], (DEPTH, D_MODEL, D_SHARED), D_MODEL ** -0.5),
        "w_sh_up": nrm(ks[25], (DEPTH, D_MODEL, D_SHARED), D_MODEL ** -0.5),
        "w_sh_down": nrm(ks[26], (DEPTH, D_SHARED, D_MODEL), DEEPNORM_BETA * D_SHARED ** -0.5),
        "ln2_g": 1.0 + nrm(ks[27], (DEPTH, D_MODEL), 0.02),
        "ln2_b": nrm(ks[28], (DEPTH, D_MODEL), 0.02),
    }


def reference(x_prompt, x_sample, cache_k, cache_v, state_conv, page_table, w_in,
              lam_q1, lam_k1, lam_q2, lam_k2, subln_g, w_dw, b_dw, conv_ln_g, conv_ln_b,
              w_o, ln1_g, ln1_b, w_router, b_router, w_gate, w_up, w_down, w_sh_gate,
              w_sh_up, w_sh_down, ln2_g, ln2_b):
    y_p, y_s = x_prompt, x_sample
    k_p, v_p, c_p, k_s, v_s, c_s = [], [], [], [], [], []
    for l in range(DEPTH):
        lp = (w_in[l], lam_q1[l], lam_k1[l], lam_q2[l], lam_k2[l], subln_g[l], w_dw[l],
              b_dw[l], conv_ln_g[l], conv_ln_b[l], w_o[l], ln1_g[l], ln1_b[l],
              w_router[l], b_router[l], w_gate[l], w_up[l], w_down[l], w_sh_gate[l],
              w_sh_up[l], w_sh_down[l], ln2_g[l], ln2_b[l])
        li = lambda_init(l)
        prev_p = jnp.zeros((y_p.shape[0], CONV_WIDTH - 1, CONV_CH), y_p.dtype)
        y_p, kl, vl, cl = decoder_layer(y_p, prev_p, prompt_attention, li, *lp)
        k_p.append(kl)
        v_p.append(vl)
        c_p.append(cl)
        attend_s = functools.partial(sample_attention, layer=l, cache_k=cache_k,
                                     cache_v=cache_v, page_table=page_table)
        y_s, kl, vl, cl = decoder_layer(y_s, state_conv[l], attend_s, li, *lp)
        k_s.append(kl)
        v_s.append(vl)
        c_s.append(cl)
    return (y_p, y_s, jnp.stack(k_p), jnp.stack(v_p), jnp.stack(c_p),
            jnp.stack(k_s), jnp.stack(v_s), jnp.stack(c_s))
```

```python
import functools
import math

import jax
import jax.numpy as jnp
from jax import lax
from jax.experimental import pallas as pl
from jax.experimental.pallas import tpu as pltpu

F32 = jnp.float32
BF16 = jnp.bfloat16
I32 = jnp.int32

D_MODEL = 1024
N_HEADS = 4
HEAD_DIM = 64
V_DIM = 128
ATTN_WIDTH = N_HEADS * V_DIM
QK_WIDTH = N_HEADS * 2 * HEAD_DIM
ATTN_SCALE = HEAD_DIM ** -0.5
CONV_CH = D_MODEL - ATTN_WIDTH
CONV_WIDTH = 31
N_EXPERTS = 64
N_GROUPS = 8
GROUP_SIZE = N_EXPERTS // N_GROUPS
TOPK_GROUPS = 4
TOP_K = 8
D_EXPERT = D_MODEL // 4
ROUTED_SCALE = 2.5
DEPTH = 1
DEEPNORM_ALPHA = (2 * DEPTH) ** 0.25
NORM_EPS = 1e-5
NEG_INF = -1e30
PAGE_SIZE = 128

LANES = 128
ROW_CHUNKS = D_MODEL // LANES
VMEM_LIMIT = 48 * 1024 * 1024
PAGES_PER_STEP = 8
EXPERT_BLOCK = 256
HIST = 32


def _lambda_init(layer):
    return 0.8 - 0.6 * math.exp(-0.3 * layer)


def _sigmoid(x):
    return 1.0 / (1.0 + jnp.exp(-x))


def _silu(x):
    return x * _sigmoid(x)


def _layer_norm(x, g, b):
    mu = jnp.mean(x, axis=-1, keepdims=True)
    xc = x - mu
    var = jnp.mean(xc * xc, axis=-1, keepdims=True)
    return xc * lax.rsqrt(var + NORM_EPS) * g + b


def _dot(a, b):
    return jnp.dot(a, b, preferred_element_type=F32)


def _dot_nt(a, b):
    return lax.dot_general(a, b, (((1,), (1,)), ((), ())), preferred_element_type=F32)


def _diff_lambda(lq1, lk1, lq2, lk2, lam_init):
    a = jnp.exp(jnp.sum(lq1 * lk1, axis=-1, keepdims=True))
    b = jnp.exp(jnp.sum(lq2 * lk2, axis=-1, keepdims=True))
    return a - b + lam_init


def _params(dims):
    return pltpu.CompilerParams(dimension_semantics=dims, vmem_limit_bytes=VMEM_LIMIT)


def _inproj_body(x_ref, w_ref, q_ref, k_ref, v_ref, u_ref, kb_ref, vb_ref):
    xb = x_ref[...].astype(BF16)

    def mm(c0):
        return _dot(xb, w_ref[:, c0:c0 + QK_WIDTH])

    q_ref[...] = mm(0) * ATTN_SCALE
    k = mm(QK_WIDTH)
    k_ref[...] = k
    kb_ref[...] = k.astype(BF16)
    v = mm(2 * QK_WIDTH)
    v_ref[...] = v
    vb_ref[...] = v.astype(BF16)
    a = mm(2 * QK_WIDTH + ATTN_WIDTH)
    b = mm(2 * QK_WIDTH + ATTN_WIDTH + CONV_CH)
    u_ref[...] = a * _sigmoid(b)


def _inproj(x, w_in_b, tm):
    n = x.shape[0]
    blk = lambda w: pl.BlockSpec((tm, w), lambda i: (i, 0))
    wide = QK_WIDTH
    return pl.pallas_call(
        _inproj_body,
        grid=(n // tm,),
        in_specs=[blk(D_MODEL), pl.BlockSpec(w_in_b.shape, lambda i: (0, 0))],
        out_specs=[blk(wide)] * 6,
        out_shape=[jax.ShapeDtypeStruct((n, wide), F32)] * 4
        + [jax.ShapeDtypeStruct((n, wide), BF16)] * 2,
        compiler_params=_params(("parallel",)),
    )(x, w_in_b)


def _attn_p_body(lam_init, q_ref, k_ref, v_ref, lq1, lk1, lq2, lk2, g_ref, o_ref,
                 m_ref, l_ref, acc_ref):
    qi = pl.program_id(2)
    ki = pl.program_id(3)
    tq = q_ref.shape[0]
    tk = k_ref.shape[0]

    @pl.when(ki == 0)
    def _():
        m_ref[...] = jnp.full(m_ref.shape, -jnp.inf, F32)
        l_ref[...] = jnp.zeros(l_ref.shape, F32)
        acc_ref[...] = jnp.zeros(acc_ref.shape, F32)

    def step(masked):
        q = q_ref[...]
        lane = lax.broadcasted_iota(I32, q.shape, 1)
        k = k_ref[...]
        v = v_ref[...]
        if masked:
            row = lax.broadcasted_iota(I32, (tq, tk), 0)
            col = lax.broadcasted_iota(I32, (tq, tk), 1)
            keep = col <= row
        for a in range(2):
            first = (lane < HEAD_DIM) if a == 0 else (lane >= HEAD_DIM)
            qa = jnp.where(first, q, 0.0).astype(BF16)
            s = _dot_nt(qa, k)
            if masked:
                s = jnp.where(keep, s, NEG_INF)
            m_old = m_ref[a]
            m_new = jnp.maximum(m_old, jnp.max(s, axis=-1, keepdims=True))
            alpha = jnp.exp(m_old - m_new)
            p = jnp.exp(s - m_new)
            l_ref[a] = alpha * l_ref[a] + jnp.sum(p, axis=-1, keepdims=True)
            acc_ref[a] = alpha * acc_ref[a] + _dot(p.astype(BF16), v)
            m_ref[a] = m_new

    @pl.when(ki < qi)
    def _():
        step(False)

    @pl.when(ki == qi)
    def _():
        step(True)
        lam = _diff_lambda(lq1[...], lk1[...], lq2[...], lk2[...], lam_init)
        o = acc_ref[0] / l_ref[0] - lam * (acc_ref[1] / l_ref[1])
        ms = jnp.mean(o * o, axis=-1, keepdims=True)
        o_ref[...] = o * lax.rsqrt(ms + NORM_EPS) * g_ref[...] * (1.0 - lam_init)


def _attn_prompt(q, kb, vb, lam_vecs, subln_g, lam_init, batch, seq, tq):
    n = q.shape[0]
    nq = seq // tq
    qmap = lambda b, h, i, j: (b * nq + i, h)
    kmap = lambda b, h, i, j: (b * nq + jnp.minimum(i, j), h)
    small = lambda w: pl.BlockSpec((1, w), lambda b, h, i, j: (0, 0))
    return pl.pallas_call(
        functools.partial(_attn_p_body, lam_init),
        grid=(batch, N_HEADS, nq, nq),
        in_specs=[pl.BlockSpec((tq, V_DIM), qmap),
                  pl.BlockSpec((tq, V_DIM), kmap),
                  pl.BlockSpec((tq, V_DIM), kmap)]
        + [small(HEAD_DIM)] * 4 + [small(V_DIM)],
        out_specs=pl.BlockSpec((tq, V_DIM), qmap),
        out_shape=jax.ShapeDtypeStruct((n, ATTN_WIDTH), F32),
        scratch_shapes=[pltpu.VMEM((2, tq, 1), F32), pltpu.VMEM((2, tq, 1), F32),
                        pltpu.VMEM((2, tq, V_DIM), F32)],
        compiler_params=_params(("parallel", "parallel", "parallel", "arbitrary")),
    )(q, kb, vb, *lam_vecs, subln_g)


def _attn_s_body(lam_init, n_steps, pt_ref, q_ref, kn_ref, vn_ref, *rest):
    npg = PAGES_PER_STEP
    k_refs = rest[:npg]
    v_refs = rest[npg:2 * npg]
    lq1, lk1, lq2, lk2, g_ref, o_ref, knew, vnew, m_ref, l_ref, acc_ref = rest[2 * npg:]
    j = pl.program_id(1)
    ds = q_ref.shape[0]

    def q16(h):
        qh = q_ref[:, h * V_DIM:(h + 1) * V_DIM]
        lane = lax.broadcasted_iota(I32, qh.shape, 1)
        top = jnp.where(lane < HEAD_DIM, qh, 0.0)
        bot = jnp.where(lane >= HEAD_DIM, qh, 0.0)
        return jnp.concatenate([top, bot], axis=0).astype(BF16)

    @pl.when(j == 0)
    def _():
        knew[...] = jnp.zeros(knew.shape, F32)
        vnew[...] = jnp.zeros(vnew.shape, F32)
        knew[0:ds] = kn_ref[...]
        vnew[0:ds] = vn_ref[...]
        row = lax.broadcasted_iota(I32, (2 * ds, PAGE_SIZE), 0)
        col = lax.broadcasted_iota(I32, (2 * ds, PAGE_SIZE), 1)
        qpos = jnp.where(row >= ds, row - ds, row)
        keep = col <= qpos
        for h in range(N_HEADS):
            s = _dot_nt(q16(h), knew[:, h, :].astype(BF16))
            s = jnp.where(keep, s, NEG_INF)
            m = jnp.max(s, axis=-1, keepdims=True)
            p = jnp.exp(s - m)
            m_ref[h] = m
            l_ref[h] = jnp.sum(p, axis=-1, keepdims=True)
            acc_ref[h] = _dot(p.astype(BF16), vnew[:, h, :].astype(BF16))

    for h in range(N_HEADS):
        qh = q16(h)
        s = jnp.concatenate(
            [_dot_nt(qh, k_refs[i][:, h, :].astype(BF16)) for i in range(npg)], axis=1)
        m_old = m_ref[h]
        m_new = jnp.maximum(m_old, jnp.max(s, axis=-1, keepdims=True))
        alpha = jnp.exp(m_old - m_new)
        p = jnp.exp(s - m_new)
        l_ref[h] = alpha * l_ref[h] + jnp.sum(p, axis=-1, keepdims=True)
        pb = p.astype(BF16)
        pv = _dot(pb[:, 0:PAGE_SIZE], v_refs[0][:, h, :].astype(BF16))
        for i in range(1, npg):
            pv = pv + _dot(pb[:, i * PAGE_SIZE:(i + 1) * PAGE_SIZE],
                           v_refs[i][:, h, :].astype(BF16))
        acc_ref[h] = alpha * acc_ref[h] + pv
        m_ref[h] = m_new

    @pl.when(j == n_steps - 1)
    def _():
        lam = _diff_lambda(lq1[...], lk1[...], lq2[...], lk2[...], lam_init)
        for h in range(N_HEADS):
            on = acc_ref[h] / l_ref[h]
            o = on[0:ds] - lam * on[ds:2 * ds]
            ms = jnp.mean(o * o, axis=-1, keepdims=True)
            o_ref[:, h * V_DIM:(h + 1) * V_DIM] = (
                o * lax.rsqrt(ms + NORM_EPS) * g_ref[...] * (1.0 - lam_init))


def _attn_sample(q, k_new, v_new, cache_k, cache_v, page_table, lam_vecs, subln_g, lam_init):
    b, ds, _ = q.shape
    n_pages = page_table.shape[1]
    n_steps = n_pages // PAGES_PER_STEP
    pt = page_table.reshape(-1)

    def page_map(i, bi, j, pt_ref):
        return (pt_ref[bi * n_pages + j * PAGES_PER_STEP + i], 0, 0, 0)

    page_specs = [pl.BlockSpec((None, PAGE_SIZE, N_HEADS, V_DIM), functools.partial(page_map, i))
                  for i in range(PAGES_PER_STEP)]
    small = lambda w: pl.BlockSpec((1, w), lambda bi, j, pt_ref: (0, 0))
    new_spec = pl.BlockSpec((None, ds, N_HEADS, V_DIM), lambda bi, j, pt_ref: (bi, 0, 0, 0))
    qo_spec = pl.BlockSpec((None, ds, ATTN_WIDTH), lambda bi, j, pt_ref: (bi, 0, 0))
    grid_spec = pltpu.PrefetchScalarGridSpec(
        num_scalar_prefetch=1,
        grid=(b, n_steps),
        in_specs=[qo_spec, new_spec, new_spec] + page_specs + page_specs
        + [small(HEAD_DIM)] * 4 + [small(V_DIM)],
        out_specs=qo_spec,
        scratch_shapes=[pltpu.VMEM((PAGE_SIZE, N_HEADS, V_DIM), F32),
                        pltpu.VMEM((PAGE_SIZE, N_HEADS, V_DIM), F32),
                        pltpu.VMEM((N_HEADS, 2 * ds, 1), F32),
                        pltpu.VMEM((N_HEADS, 2 * ds, 1), F32),
                        pltpu.VMEM((N_HEADS, 2 * ds, V_DIM), F32)],
    )
    return pl.pallas_call(
        functools.partial(_attn_s_body, lam_init, n_steps),
        grid_spec=grid_spec,
        out_shape=jax.ShapeDtypeStruct((b, ds, ATTN_WIDTH), F32),
        compiler_params=_params(("parallel", "arbitrary")),
    )(pt, q, k_new, v_new, *([cache_k] * PAGES_PER_STEP), *([cache_v] * PAGES_PER_STEP),
      *lam_vecs, subln_g)


def _conv_taps(ext_ref, w_ref, bias, rows):
    off = HIST - (CONV_WIDTH - 1)
    acc = jnp.broadcast_to(bias, (rows, CONV_CH))
    for jt in range(CONV_WIDTH):
        acc = acc + w_ref[jt:jt + 1, :] * ext_ref[pl.ds(jt + off, rows), :]
    return acc


def _conv_p_body(u_ref, h_ref, w_ref, b_ref, g_ref, beta_ref, c_ref, ext_ref):
    t = pl.program_id(1)
    tm = u_ref.shape[0]
    hist = h_ref[...]
    ext_ref[0:HIST] = jnp.where(t == 0, jnp.zeros_like(hist), hist)
    ext_ref[HIST:HIST + tm] = u_ref[...]
    y = _conv_taps(ext_ref, w_ref, b_ref[...], tm)
    c_ref[...] = _silu(_layer_norm(y, g_ref[...], beta_ref[...]))


def _conv_prompt(u, w_dw, b_dw, g, beta, batch, seq, tm):
    n = u.shape[0]
    nt = seq // tm
    per = tm // HIST
    cur = lambda b, t: (b * nt + t, 0)
    prev = lambda b, t: (jnp.maximum((b * nt + t) * per - 1, 0), 0)
    small = lambda r: pl.BlockSpec((r, CONV_CH), lambda b, t: (0, 0))
    return pl.pallas_call(
        _conv_p_body,
        grid=(batch, nt),
        in_specs=[pl.BlockSpec((tm, CONV_CH), cur), pl.BlockSpec((HIST, CONV_CH), prev),
                  small(CONV_WIDTH), small(1), small(1), small(1)],
        out_specs=pl.BlockSpec((tm, CONV_CH), cur),
        out_shape=jax.ShapeDtypeStruct((n, CONV_CH), F32),
        scratch_shapes=[pltpu.VMEM((HIST + tm, CONV_CH), F32)],
        compiler_params=_params(("parallel", "parallel")),
    )(u, u, w_dw, b_dw, g, beta)


def _conv_s_body(u_ref, st_ref, w_ref, b_ref, g_ref, beta_ref, c_ref, ns_ref, ext_ref):
    ds = u_ref.shape[0]
    keep = CONV_WIDTH - 1
    off = HIST - keep
    ext_ref[0:off] = jnp.zeros((off, CONV_CH), F32)
    ext_ref[off:HIST] = st_ref[...]
    ext_ref[HIST:HIST + ds] = u_ref[...]
    y = _conv_taps(ext_ref, w_ref, b_ref[...], ds)
    c_ref[...] = _silu(_layer_norm(y, g_ref[...], beta_ref[...]))
    ns_ref[...] = ext_ref[HIST + ds - keep:HIST + ds]


def _conv_sample(u, state, w_dw, b_dw, g, beta):
    b, ds, _ = u.shape
    keep = CONV_WIDTH - 1
    per_b = lambda r: pl.BlockSpec((None, r, CONV_CH), lambda i: (i, 0, 0))
    small = lambda r: pl.BlockSpec((r, CONV_CH), lambda i: (0, 0))
    return pl.pallas_call(
        _conv_s_body,
        grid=(b,),
        in_specs=[per_b(ds), per_b(keep), small(CONV_WIDTH), small(1), small(1), small(1)],
        out_specs=[per_b(ds), per_b(keep)],
        out_shape=[jax.ShapeDtypeStruct((b, ds, CONV_CH), F32),
                   jax.ShapeDtypeStruct((b, keep, CONV_CH), F32)],
        scratch_shapes=[pltpu.VMEM((HIST + ds, CONV_CH), F32)],
        compiler_params=_params(("parallel",)),
    )(u, state, w_dw, b_dw, g, beta)


def _first_index(hit, iota, limit, axis):
    return jnp.min(jnp.where(hit, iota, limit), axis=axis, keepdims=True)


def _route(x1, wr_hi, wr_lo, bias):
    tm = x1.shape[0]
    x_hi = x1.astype(BF16)
    x_lo = (x1 - x_hi.astype(F32)).astype(BF16)
    logits = _dot_nt(wr_hi, x_hi) + (_dot_nt(wr_lo, x_hi) + _dot_nt(wr_hi, x_lo))
    scores = _sigmoid(logits)
    sel = scores + bias
    sel3 = sel.reshape(N_GROUPS, GROUP_SIZE, tm)
    member = lax.broadcasted_iota(I32, sel3.shape, 1)
    m1 = jnp.max(sel3, axis=1, keepdims=True)
    i1 = _first_index(sel3 == m1, member, GROUP_SIZE, 1)
    m2 = jnp.max(jnp.where(member == i1, -jnp.inf, sel3), axis=1, keepdims=True)
    gs = jnp.broadcast_to(m1 + m2, sel3.shape).reshape(N_EXPERTS, tm)
    eiota = lax.broadcasted_iota(I32, (N_EXPERTS, tm), 0)
    giota = eiota // GROUP_SIZE
    gmask = jnp.zeros((N_EXPERTS, tm), jnp.bool_)
    for _ in range(TOPK_GROUPS):
        m = jnp.max(gs, axis=0, keepdims=True)
        gi = _first_index(gs == m, giota, N_GROUPS, 0)
        pick = giota == gi
        gmask = jnp.logical_or(gmask, pick)
        gs = jnp.where(pick, -jnp.inf, gs)
    selm = jnp.where(gmask, sel, NEG_INF)
    idx_rows, w_rows = [], []
    for _ in range(TOP_K):
        m = jnp.max(selm, axis=0, keepdims=True)
        ei = _first_index(selm == m, eiota, N_EXPERTS, 0)
        pick = eiota == ei
        idx_rows.append(ei)
        w_rows.append(jnp.sum(jnp.where(pick, scores, 0.0), axis=0, keepdims=True))
        selm = jnp.where(pick, -jnp.inf, selm)
    idx = jnp.concatenate(idx_rows, axis=0)
    w = jnp.concatenate(w_rows, axis=0)
    w = w / jnp.sum(w, axis=0, keepdims=True) * ROUTED_SCALE
    return idx, w


def _mix_body(o_ref, c_ref, x_ref, wo_a, wo_c, g_ref, b_ref, wrh_ref, wrl_ref, br_ref,
              x1_ref, idx_ref, wts_ref, rank_ref, cnt_ref, base_ref):
    i = pl.program_id(0)
    tm = x_ref.shape[0]

    @pl.when(i == 0)
    def _():
        base_ref[...] = jnp.zeros(base_ref.shape, F32)

    mix = _dot(o_ref[...].astype(BF16), wo_a[...]) + _dot(c_ref[...].astype(BF16), wo_c[...])
    x1 = _layer_norm(DEEPNORM_ALPHA * x_ref[...] + mix, g_ref[...], b_ref[...])
    x1_ref[...] = x1

    idx, w = _route(x1, wrh_ref[...], wrl_ref[...], br_ref[...])
    idx_ref[...] = idx
    wts_ref[...] = w

    eiota = lax.broadcasted_iota(I32, (N_EXPERTS, tm), 0)
    chosen = jnp.zeros((N_EXPERTS, tm), F32)
    for k in range(TOP_K):
        chosen = chosen + jnp.where(eiota == idx[k:k + 1, :], 1.0, 0.0)
    r = lax.broadcasted_iota(I32, (tm, tm), 0)
    c = lax.broadcasted_iota(I32, (tm, tm), 1)
    before = jnp.where(r < c, 1.0, 0.0).astype(BF16)
    base = base_ref[:, 0:1]
    rank_full = _dot(chosen.astype(BF16), before) + base
    rows = [jnp.sum(jnp.where(eiota == idx[k:k + 1, :], rank_full, 0.0), axis=0, keepdims=True)
            for k in range(TOP_K)]
    rank_ref[...] = jnp.concatenate(rows, axis=0).astype(I32)
    total = base + jnp.sum(chosen, axis=1, keepdims=True)
    base_ref[...] = jnp.broadcast_to(total, base_ref.shape)
    cnt_ref[...] = jnp.broadcast_to(total, cnt_ref.shape)


def _mix(o, c, x, wo_a, wo_c, ln_g, ln_b, wr_hi, wr_lo, b_router, tm):
    n = x.shape[0]
    rows = lambda w: pl.BlockSpec((tm, w), lambda i: (i, 0))
    full = lambda a: pl.BlockSpec(a.shape, lambda i: (0, 0))
    cols = pl.BlockSpec((TOP_K, tm), lambda i: (0, i))
    cnt = pl.BlockSpec((N_EXPERTS, LANES), lambda i: (0, 0))
    return pl.pallas_call(
        _mix_body,
        grid=(n // tm,),
        in_specs=[rows(ATTN_WIDTH), rows(CONV_CH), rows(D_MODEL), full(wo_a), full(wo_c),
                  full(ln_g), full(ln_b), full(wr_hi), full(wr_lo), full(b_router)],
        out_specs=[rows(D_MODEL), cols, cols, cols, cnt],
        out_shape=[jax.ShapeDtypeStruct((n, D_MODEL), F32),
                   jax.ShapeDtypeStruct((TOP_K, n), I32),
                   jax.ShapeDtypeStruct((TOP_K, n), F32),
                   jax.ShapeDtypeStruct((TOP_K, n), I32),
                   jax.ShapeDtypeStruct((N_EXPERTS, LANES), F32)],
        scratch_shapes=[pltpu.VMEM((N_EXPERTS, LANES), F32)],
        compiler_params=_params(("arbitrary",)),
    )(o, c, x, wo_a, wo_c, ln_g, ln_b, wr_hi, wr_lo, b_router)


def _dispatch_body(dest_ref, x_ref, init_ref, xs_ref, sem):
    del init_ref
    ts = x_ref.shape[0]

    def row_copy(t, d):
        return pltpu.make_async_copy(x_ref.at[t], xs_ref.at[d], sem)

    def issue(t, carry):
        for k in range(TOP_K):
            row_copy(t, dest_ref[k, t]).start()
        return carry

    lax.fori_loop(0, ts, issue, 0)

    def drain(t, carry):
        for k in range(TOP_K):
            row_copy(t, dest_ref[k, t]).wait()
        return carry

    lax.fori_loop(0, ts, drain, 0)


def _dispatch(x1_rows, dest, xs_init, ts):
    n = x1_rows.shape[0]
    return pl.pallas_call(
        _dispatch_body,
        grid=(n // ts,),
        in_specs=[pl.BlockSpec((TOP_K, ts), lambda i: (0, i), memory_space=pltpu.SMEM),
                  pl.BlockSpec((ts, ROW_CHUNKS, LANES), lambda i: (i, 0, 0)),
                  pl.BlockSpec(memory_space=pl.ANY)],
        out_specs=pl.BlockSpec(memory_space=pl.ANY),
        out_shape=jax.ShapeDtypeStruct(xs_init.shape, xs_init.dtype),
        scratch_shapes=[pltpu.SemaphoreType.DMA],
        input_output_aliases={2: 0},
        compiler_params=_params(("arbitrary",)),
    )(dest, x1_rows, xs_init)


def _expert_body(be_ref, nb_ref, xs_ref, wg_ref, wu_ref, wd_ref, ys_ref):
    del be_ref
    i = pl.program_id(0)

    @pl.when(i < nb_ref[0])
    def _():
        x = jnp.concatenate([xs_ref[:, j, :] for j in range(ROW_CHUNKS)], axis=1).astype(BF16)
        h = (_silu(_dot(x, wg_ref[...])) * _dot(x, wu_ref[...])).astype(BF16)
        y = _dot(h, wd_ref[...])
        for j in range(ROW_CHUNKS):
            ys_ref[:, j, :] = y[:, j * LANES:(j + 1) * LANES]

    @pl.when(i >= nb_ref[0])
    def _():
        ys_ref[...] = jnp.zeros(ys_ref.shape, F32)


def _experts(xs, block_e, nb_used, wg, wu, wd):
    rows = xs.shape[0]
    n_blocks = rows // EXPERT_BLOCK
    row_spec = pl.BlockSpec((EXPERT_BLOCK, ROW_CHUNKS, LANES), lambda i, be, nb: (i, 0, 0))
    grid_spec = pltpu.PrefetchScalarGridSpec(
        num_scalar_prefetch=2,
        grid=(n_blocks,),
        in_specs=[row_spec,
                  pl.BlockSpec((None, D_MODEL, D_EXPERT), lambda i, be, nb: (be[i], 0, 0)),
                  pl.BlockSpec((None, D_MODEL, D_EXPERT), lambda i, be, nb: (be[i], 0, 0)),
                  pl.BlockSpec((None, D_EXPERT, D_MODEL), lambda i, be, nb: (be[i], 0, 0))],
        out_specs=row_spec,
    )
    return pl.pallas_call(
        _expert_body,
        grid_spec=grid_spec,
        out_shape=jax.ShapeDtypeStruct(xs.shape, F32),
        compiler_params=_params(("arbitrary",)),
    )(block_e, nb_used, xs, wg, wu, wd)


def _combine_body(dest_ref, wts_ref, x1_ref, ys_ref, wsg, wsu, wsd, g_ref, b_ref, y_ref,
                  buf, racc, sem):
    tc = x1_ref.shape[0]

    def row_copy(t, k):
        return pltpu.make_async_copy(ys_ref.at[dest_ref[k, t]], buf.at[k, t], sem)

    def issue(t, carry):
        for k in range(TOP_K):
            row_copy(t, k).start()
        return carry

    lax.fori_loop(0, tc, issue, 0)

    x1 = x1_ref[...]
    xb = x1.astype(BF16)
    hs = (_silu(_dot(xb, wsg[...])) * _dot(xb, wsu[...])).astype(BF16)
    shared = _dot(hs, wsd[...])

    def drain(t, carry):
        for k in range(TOP_K):
            row_copy(t, k).wait()
        return carry

    lax.fori_loop(0, tc, drain, 0)

    def reduce(t, carry):
        acc = wts_ref[0, t] * buf[0, t]
        for k in range(1, TOP_K):
            acc = acc + wts_ref[k, t] * buf[k, t]
        racc[t] = acc
        return carry

    lax.fori_loop(0, tc, reduce, 0)
    routed = jnp.concatenate([racc[:, j, :] for j in range(ROW_CHUNKS)], axis=1)
    y_ref[...] = _layer_norm(DEEPNORM_ALPHA * x1 + (routed + shared), g_ref[...], b_ref[...])


def _combine(dest, wts, x1, ys, wsg, wsu, wsd, ln_g, ln_b, tc):
    n = x1.shape[0]
    smem_cols = pl.BlockSpec((TOP_K, tc), lambda i: (0, i), memory_space=pltpu.SMEM)
    full = lambda a: pl.BlockSpec(a.shape, lambda i: (0, 0))
    rows = pl.BlockSpec((tc, D_MODEL), lambda i: (i, 0))
    return pl.pallas_call(
        _combine_body,
        grid=(n // tc,),
        in_specs=[smem_cols, smem_cols, rows, pl.BlockSpec(memory_space=pl.ANY),
                  full(wsg), full(wsu), full(wsd), full(ln_g), full(ln_b)],
        out_specs=rows,
        out_shape=jax.ShapeDtypeStruct((n, D_MODEL), F32),
        scratch_shapes=[pltpu.VMEM((TOP_K, tc, ROW_CHUNKS, LANES), F32),
                        pltpu.VMEM((tc, ROW_CHUNKS, LANES), F32),
                        pltpu.SemaphoreType.DMA],
        compiler_params=_params(("arbitrary",)),
    )(dest, wts, x1, ys, wsg, wsu, wsd, ln_g, ln_b)


def _moe(x1, idx, wts, rank, cnt, moe_w, ln_g, ln_b, tile):
    n = x1.shape[0]
    wg, wu, wd, wsg, wsu, wsd = moe_w
    blk = EXPERT_BLOCK
    n_blocks = -(-(n * TOP_K) // blk) + N_EXPERTS
    counts = cnt[:, 0].astype(I32)
    padded = (counts + blk - 1) // blk * blk
    ends = jnp.cumsum(padded)
    pstart = ends - padded
    nb_used = (ends[-1] // blk).astype(I32).reshape(1)
    block_e = jnp.minimum(
        jnp.searchsorted(ends, jnp.arange(n_blocks, dtype=I32) * blk, side="right"),
        N_EXPERTS - 1).astype(I32)
    dest = (pstart[idx] + rank).astype(I32)
    xs0 = jnp.zeros((n_blocks * blk, ROW_CHUNKS, LANES), F32)
    xs = _dispatch(x1.reshape(n, ROW_CHUNKS, LANES), dest, xs0, tile)
    ys = _experts(xs, block_e, nb_used, wg, wu, wd)
    return _combine(dest, wts, x1, ys, wsg, wsu, wsd, ln_g, ln_b, min(tile, 128))


def kernel(x_prompt, x_sample, cache_k, cache_v, state_conv, page_table, w_in, lam_q1, lam_k1,
           lam_q2, lam_k2, subln_g, w_dw, b_dw, conv_ln_g, conv_ln_b, w_o, ln1_g, ln1_b,
           w_router, b_router, w_gate, w_up, w_down, w_sh_gate, w_sh_up, w_sh_down, ln2_g,
           ln2_b):
    batch, seq, _ = x_prompt.shape
    dec_b, dec_s, _ = x_sample.shape
    n_p, n_s = batch * seq, dec_b * dec_s
    layer = 0
    lam_init = _lambda_init(layer)
    row = lambda a: a[layer].reshape(1, -1)

    w_in_b = w_in[layer].astype(BF16)
    wo = w_o[layer].astype(BF16)
    wo_a, wo_c = wo[:ATTN_WIDTH], wo[ATTN_WIDTH:]
    wr_t = w_router[layer].T
    wr_hi = wr_t.astype(BF16)
    wr_lo = (wr_t - wr_hi.astype(F32)).astype(BF16)
    br = b_router[layer].reshape(-1, 1)
    moe_w = tuple(w[layer].astype(BF16)
                  for w in (w_gate, w_up, w_down, w_sh_gate, w_sh_up, w_sh_down))
    lam_vecs = (row(lam_q1), row(lam_k1), row(lam_q2), row(lam_k2))
    g_sub = row(subln_g)
    conv_w = (w_dw[layer], row(b_dw), row(conv_ln_g), row(conv_ln_b))
    ln1 = (row(ln1_g), row(ln1_b))
    ln2 = (row(ln2_g), row(ln2_b))

    xp = x_prompt.reshape(n_p, D_MODEL)
    q_p, k_p, v_p, u_p, kb_p, vb_p = _inproj(xp, w_in_b, 512)
    o_p = _attn_prompt(q_p, kb_p, vb_p, lam_vecs, g_sub, lam_init, batch, seq, 512)
    c_p = _conv_prompt(u_p, *conv_w, batch, seq, 512)
    x1_p, idx_p, wts_p, rank_p, cnt_p = _mix(o_p, c_p, xp, wo_a, wo_c, *ln1, wr_hi, wr_lo, br, 512)
    y_p = _moe(x1_p, idx_p, wts_p, rank_p, cnt_p, moe_w, *ln2, 256)

    xs = x_sample.reshape(n_s, D_MODEL)
    q_s, k_s, v_s, u_s, _, _ = _inproj(xs, w_in_b, n_s)
    heads = lambda a: a.reshape(dec_b, dec_s, N_HEADS, V_DIM)
    o_s = _attn_sample(q_s.reshape(dec_b, dec_s, QK_WIDTH), heads(k_s), heads(v_s),
                       cache_k[layer], cache_v[layer], page_table, lam_vecs, g_sub, lam_init)
    c_s, st_s = _conv_sample(u_s.reshape(dec_b, dec_s, CONV_CH), state_conv[layer], *conv_w)
    x1_s, idx_s, wts_s, rank_s, cnt_s = _mix(o_s.reshape(n_s, ATTN_WIDTH),
                                             c_s.reshape(n_s, CONV_CH), xs, wo_a, wo_c, *ln1,
                                             wr_hi, wr_lo, br, n_s)
    y_s = _moe(x1_s, idx_s, wts_s, rank_s, cnt_s, moe_w, *ln2, 128)

    keep = CONV_WIDTH - 1
    u_p3 = u_p.reshape(batch, seq, CONV_CH)
    return (y_p.reshape(batch, seq, D_MODEL),
            y_s.reshape(dec_b, dec_s, D_MODEL),
            k_p.reshape(1, batch, seq, N_HEADS, V_DIM),
            v_p.reshape(1, batch, seq, N_HEADS, V_DIM),
            u_p3[:, seq - keep:, :][None],
            k_s.reshape(1, dec_b, dec_s, N_HEADS, V_DIM),
            v_s.reshape(1, dec_b, dec_s, N_HEADS, V_DIM),
            st_s[None])
```

```python
import functools
import math

import jax
import jax.numpy as jnp
from jax import lax
from jax.experimental import pallas as pl
from jax.experimental.pallas import tpu as pltpu

F32 = jnp.float32
BF16 = jnp.bfloat16
I32 = jnp.int32

D_MODEL = 1024
N_HEADS = 4
HEAD_DIM = 64
V_DIM = 128
ATTN_WIDTH = N_HEADS * V_DIM
QK_WIDTH = N_HEADS * 2 * HEAD_DIM
ATTN_SCALE = HEAD_DIM ** -0.5
CONV_CH = D_MODEL - ATTN_WIDTH
CONV_WIDTH = 31
N_EXPERTS = 64
N_GROUPS = 8
GROUP_SIZE = N_EXPERTS // N_GROUPS
TOPK_GROUPS = 4
TOP_K = 8
D_EXPERT = D_MODEL // 4
ROUTED_SCALE = 2.5
DEPTH = 1
DEEPNORM_ALPHA = (2 * DEPTH) ** 0.25
NORM_EPS = 1e-5
NEG_INF = -1e30
PAGE_SIZE = 128

LANES = 128
ROW_CHUNKS = D_MODEL // LANES
VMEM_LIMIT = 48 * 1024 * 1024
PAGES_PER_STEP = 8
PAGE_ROWS = PAGE_SIZE * N_HEADS
EXPERT_BLOCK = 256
HIST = 32


def _lambda_init(layer):
    return 0.8 - 0.6 * math.exp(-0.3 * layer)


def _sigmoid(x):
    return 1.0 / (1.0 + jnp.exp(-x))


def _silu(x):
    return x * _sigmoid(x)


def _layer_norm(x, g, b):
    mu = jnp.mean(x, axis=-1, keepdims=True)
    xc = x - mu
    var = jnp.mean(xc * xc, axis=-1, keepdims=True)
    return xc * lax.rsqrt(var + NORM_EPS) * g + b


def _dot(a, b):
    return jnp.dot(a, b, preferred_element_type=F32)


def _dot_nt(a, b):
    return lax.dot_general(a, b, (((1,), (1,)), ((), ())), preferred_element_type=F32)


def _diff_lambda(lq1, lk1, lq2, lk2, lam_init):
    a = jnp.exp(jnp.sum(lq1 * lk1, axis=-1, keepdims=True))
    b = jnp.exp(jnp.sum(lq2 * lk2, axis=-1, keepdims=True))
    return a - b + lam_init


def _params(dims):
    return pltpu.CompilerParams(dimension_semantics=dims, vmem_limit_bytes=VMEM_LIMIT)


def _inproj_body(x_ref, w_ref, q_ref, k_ref, v_ref, u_ref, kb_ref, vb_ref):
    xb = x_ref[...].astype(BF16)

    def mm(c0):
        return _dot(xb, w_ref[:, c0:c0 + QK_WIDTH])

    q_ref[...] = mm(0) * ATTN_SCALE
    k = mm(QK_WIDTH)
    k_ref[...] = k
    kb_ref[...] = k.astype(BF16)
    v = mm(2 * QK_WIDTH)
    v_ref[...] = v
    vb_ref[...] = v.astype(BF16)
    a = mm(2 * QK_WIDTH + ATTN_WIDTH)
    b = mm(2 * QK_WIDTH + ATTN_WIDTH + CONV_CH)
    u_ref[...] = a * _sigmoid(b)


def _inproj(x, w_in_b, tm):
    n = x.shape[0]
    blk = lambda w: pl.BlockSpec((tm, w), lambda i: (i, 0))
    wide = QK_WIDTH
    return pl.pallas_call(
        _inproj_body,
        name="inproj",
        grid=(n // tm,),
        in_specs=[blk(D_MODEL), pl.BlockSpec(w_in_b.shape, lambda i: (0, 0))],
        out_specs=[blk(wide)] * 6,
        out_shape=[jax.ShapeDtypeStruct((n, wide), F32)] * 4
        + [jax.ShapeDtypeStruct((n, wide), BF16)] * 2,
        compiler_params=_params(("parallel",)),
    )(x, w_in_b)


def _attn_p_body(lam_init, q_ref, k_ref, v_ref, lq1, lk1, lq2, lk2, g_ref, o_ref,
                 m_ref, l_ref, acc_ref):
    qi = pl.program_id(2)
    ki = pl.program_id(3)
    tq = q_ref.shape[0]
    tk = k_ref.shape[0]

    @pl.when(ki == 0)
    def _():
        m_ref[...] = jnp.full(m_ref.shape, -jnp.inf, F32)
        l_ref[...] = jnp.zeros(l_ref.shape, F32)
        acc_ref[...] = jnp.zeros(acc_ref.shape, F32)

    def step(masked):
        q = q_ref[...]
        lane = lax.broadcasted_iota(I32, q.shape, 1)
        k = k_ref[...]
        v = v_ref[...]
        if masked:
            row = lax.broadcasted_iota(I32, (tq, tk), 0)
            col = lax.broadcasted_iota(I32, (tq, tk), 1)
            keep = col <= row
        for a in range(2):
            first = (lane < HEAD_DIM) if a == 0 else (lane >= HEAD_DIM)
            qa = jnp.where(first, q, 0.0).astype(BF16)
            s = _dot_nt(qa, k)
            if masked:
                s = jnp.where(keep, s, NEG_INF)
            m_old = m_ref[a]
            m_new = jnp.maximum(m_old, jnp.max(s, axis=-1, keepdims=True))
            alpha = jnp.exp(m_old - m_new)
            p = jnp.exp(s - m_new)
            l_ref[a] = alpha * l_ref[a] + jnp.sum(p, axis=-1, keepdims=True)
            acc_ref[a] = alpha * acc_ref[a] + _dot(p.astype(BF16), v)
            m_ref[a] = m_new

    @pl.when(ki < qi)
    def _():
        step(False)

    @pl.when(ki == qi)
    def _():
        step(True)
        lam = _diff_lambda(lq1[...], lk1[...], lq2[...], lk2[...], lam_init)
        o = acc_ref[0] / l_ref[0] - lam * (acc_ref[1] / l_ref[1])
        ms = jnp.mean(o * o, axis=-1, keepdims=True)
        o_ref[...] = o * lax.rsqrt(ms + NORM_EPS) * g_ref[...] * (1.0 - lam_init)


def _attn_prompt(q, kb, vb, lam_vecs, subln_g, lam_init, batch, seq, tq):
    n = q.shape[0]
    nq = seq // tq
    qmap = lambda b, h, i, j: (b * nq + i, h)
    kmap = lambda b, h, i, j: (b * nq + jnp.minimum(i, j), h)
    small = lambda w: pl.BlockSpec((1, w), lambda b, h, i, j: (0, 0))
    return pl.pallas_call(
        functools.partial(_attn_p_body, lam_init),
        name="attn_prompt",
        grid=(batch, N_HEADS, nq, nq),
        in_specs=[pl.BlockSpec((tq, V_DIM), qmap),
                  pl.BlockSpec((tq, V_DIM), kmap),
                  pl.BlockSpec((tq, V_DIM), kmap)]
        + [small(HEAD_DIM)] * 4 + [small(V_DIM)],
        out_specs=pl.BlockSpec((tq, V_DIM), qmap),
        out_shape=jax.ShapeDtypeStruct((n, ATTN_WIDTH), F32),
        scratch_shapes=[pltpu.VMEM((2, tq, 1), F32), pltpu.VMEM((2, tq, 1), F32),
                        pltpu.VMEM((2, tq, V_DIM), F32)],
        compiler_params=_params(("parallel", "parallel", "parallel", "arbitrary")),
    )(q, kb, vb, *lam_vecs, subln_g)


def _attn_s_body(lam_init, n_steps, pt_ref, q_ref, kn_ref, vn_ref, *rest):
    npg = PAGES_PER_STEP
    k_refs = rest[:npg]
    v_refs = rest[npg:2 * npg]
    (lq1, lk1, lq2, lk2, g_ref, o_ref, qall, knew, vnew, m_ref, l_ref,
     acc_ref) = rest[2 * npg:]
    j = pl.program_id(1)
    ds = q_ref.shape[0]
    nrow = N_HEADS * 2 * ds
    row = lax.broadcasted_iota(I32, (nrow, PAGE_ROWS), 0)
    col = lax.broadcasted_iota(I32, (nrow, PAGE_ROWS), 1)
    head_ok = (col % N_HEADS) == (row // (2 * ds))

    @pl.when(j == 0)
    def _():
        pieces = []
        for h in range(N_HEADS):
            qh = q_ref[:, h * V_DIM:(h + 1) * V_DIM]
            lane = lax.broadcasted_iota(I32, qh.shape, 1)
            pieces.append(jnp.where(lane < HEAD_DIM, qh, 0.0))
            pieces.append(jnp.where(lane >= HEAD_DIM, qh, 0.0))
        qa = jnp.concatenate(pieces, axis=0).astype(BF16)
        qall[...] = qa
        knew[...] = jnp.zeros(knew.shape, F32)
        vnew[...] = jnp.zeros(vnew.shape, F32)
        knew[0:ds * N_HEADS] = kn_ref[...]
        vnew[0:ds * N_HEADS] = vn_ref[...]
        keep = jnp.logical_and(head_ok, (col // N_HEADS) <= (row % ds))
        s = jnp.where(keep, _dot_nt(qa, knew[...].astype(BF16)), NEG_INF)
        m = jnp.max(s, axis=-1, keepdims=True)
        p = jnp.exp(s - m)
        m_ref[...] = m
        l_ref[...] = jnp.sum(p, axis=-1, keepdims=True)
        acc_ref[...] = _dot(p.astype(BF16), vnew[...].astype(BF16))

    qa = qall[...]
    bias = jnp.where(head_ok, 0.0, NEG_INF)
    s = [_dot_nt(qa, k_refs[i][...].astype(BF16)) + bias for i in range(npg)]
    m_old = m_ref[...]
    m_new = m_old
    for i in range(npg):
        m_new = jnp.maximum(m_new, jnp.max(s[i], axis=-1, keepdims=True))
    alpha = jnp.exp(m_old - m_new)
    l_new = alpha * l_ref[...]
    acc = alpha * acc_ref[...]
    for i in range(npg):
        p = jnp.exp(s[i] - m_new)
        l_new = l_new + jnp.sum(p, axis=-1, keepdims=True)
        acc = acc + _dot(p.astype(BF16), v_refs[i][...].astype(BF16))
    m_ref[...] = m_new
    l_ref[...] = l_new
    acc_ref[...] = acc

    @pl.when(j == n_steps - 1)
    def _():
        lam = _diff_lambda(lq1[...], lk1[...], lq2[...], lk2[...], lam_init)
        on = acc / l_new
        for h in range(N_HEADS):
            r0 = h * 2 * ds
            o = on[r0:r0 + ds] - lam * on[r0 + ds:r0 + 2 * ds]
            ms = jnp.mean(o * o, axis=-1, keepdims=True)
            o_ref[:, h * V_DIM:(h + 1) * V_DIM] = (
                o * lax.rsqrt(ms + NORM_EPS) * g_ref[...] * (1.0 - lam_init))


def _attn_sample(q, k_new, v_new, cache_k, cache_v, page_table, lam_vecs, subln_g, lam_init):
    b, ds, _ = q.shape
    n_pages = page_table.shape[1]
    n_steps = n_pages // PAGES_PER_STEP
    nrow = N_HEADS * 2 * ds
    pt = page_table.reshape(-1)

    def page_map(i, bi, j, pt_ref):
        return (pt_ref[bi * n_pages + j * PAGES_PER_STEP + i], 0, 0)

    page_specs = [pl.BlockSpec((None, PAGE_ROWS, V_DIM), functools.partial(page_map, i))
                  for i in range(PAGES_PER_STEP)]
    small = lambda w: pl.BlockSpec((1, w), lambda bi, j, pt_ref: (0, 0))
    new_spec = pl.BlockSpec((None, ds * N_HEADS, V_DIM), lambda bi, j, pt_ref: (bi, 0, 0))
    qo_spec = pl.BlockSpec((None, ds, ATTN_WIDTH), lambda bi, j, pt_ref: (bi, 0, 0))
    grid_spec = pltpu.PrefetchScalarGridSpec(
        num_scalar_prefetch=1,
        grid=(b, n_steps),
        in_specs=[qo_spec, new_spec, new_spec] + page_specs + page_specs
        + [small(HEAD_DIM)] * 4 + [small(V_DIM)],
        out_specs=qo_spec,
        scratch_shapes=[pltpu.VMEM((nrow, V_DIM), BF16),
                        pltpu.VMEM((PAGE_ROWS, V_DIM), F32),
                        pltpu.VMEM((PAGE_ROWS, V_DIM), F32),
                        pltpu.VMEM((nrow, 1), F32),
                        pltpu.VMEM((nrow, 1), F32),
                        pltpu.VMEM((nrow, V_DIM), F32)],
    )
    return pl.pallas_call(
        functools.partial(_attn_s_body, lam_init, n_steps),
        name="attn_sample",
        grid_spec=grid_spec,
        out_shape=jax.ShapeDtypeStruct((b, ds, ATTN_WIDTH), F32),
        compiler_params=_params(("parallel", "arbitrary")),
    )(pt, q, k_new, v_new, *([cache_k] * PAGES_PER_STEP), *([cache_v] * PAGES_PER_STEP),
      *lam_vecs, subln_g)


def _conv_taps(ext_ref, w_ref, bias, rows):
    off = HIST - (CONV_WIDTH - 1)
    acc = jnp.broadcast_to(bias, (rows, CONV_CH))
    for jt in range(CONV_WIDTH):
        acc = acc + w_ref[jt:jt + 1, :] * ext_ref[pl.ds(jt + off, rows), :]
    return acc


def _conv_p_body(u_ref, h_ref, w_ref, b_ref, g_ref, beta_ref, c_ref, ext_ref):
    t = pl.program_id(1)
    tm = u_ref.shape[0]
    hist = h_ref[...]
    ext_ref[0:HIST] = jnp.where(t == 0, jnp.zeros_like(hist), hist)
    ext_ref[HIST:HIST + tm] = u_ref[...]
    y = _conv_taps(ext_ref, w_ref, b_ref[...], tm)
    c_ref[...] = _silu(_layer_norm(y, g_ref[...], beta_ref[...]))


def _conv_prompt(u, w_dw, b_dw, g, beta, batch, seq, tm):
    n = u.shape[0]
    nt = seq // tm
    per = tm // HIST
    cur = lambda b, t: (b * nt + t, 0)
    prev = lambda b, t: (jnp.maximum((b * nt + t) * per - 1, 0), 0)
    small = lambda r: pl.BlockSpec((r, CONV_CH), lambda b, t: (0, 0))
    return pl.pallas_call(
        _conv_p_body,
        name="conv_prompt",
        grid=(batch, nt),
        in_specs=[pl.BlockSpec((tm, CONV_CH), cur), pl.BlockSpec((HIST, CONV_CH), prev),
                  small(CONV_WIDTH), small(1), small(1), small(1)],
        out_specs=pl.BlockSpec((tm, CONV_CH), cur),
        out_shape=jax.ShapeDtypeStruct((n, CONV_CH), F32),
        scratch_shapes=[pltpu.VMEM((HIST + tm, CONV_CH), F32)],
        compiler_params=_params(("parallel", "parallel")),
    )(u, u, w_dw, b_dw, g, beta)


def _conv_s_body(u_ref, st_ref, w_ref, b_ref, g_ref, beta_ref, c_ref, ns_ref, ext_ref):
    ds = u_ref.shape[0]
    keep = CONV_WIDTH - 1
    off = HIST - keep
    ext_ref[0:off] = jnp.zeros((off, CONV_CH), F32)
    ext_ref[off:HIST] = st_ref[...]
    ext_ref[HIST:HIST + ds] = u_ref[...]
    y = _conv_taps(ext_ref, w_ref, b_ref[...], ds)
    c_ref[...] = _silu(_layer_norm(y, g_ref[...], beta_ref[...]))
    ns_ref[...] = ext_ref[HIST + ds - keep:HIST + ds]


def _conv_sample(u, state, w_dw, b_dw, g, beta):
    b, ds, _ = u.shape
    keep = CONV_WIDTH - 1
    per_b = lambda r: pl.BlockSpec((None, r, CONV_CH), lambda i: (i, 0, 0))
    small = lambda r: pl.BlockSpec((r, CONV_CH), lambda i: (0, 0))
    return pl.pallas_call(
        _conv_s_body,
        name="conv_sample",
        grid=(b,),
        in_specs=[per_b(ds), per_b(keep), small(CONV_WIDTH), small(1), small(1), small(1)],
        out_specs=[per_b(ds), per_b(keep)],
        out_shape=[jax.ShapeDtypeStruct((b, ds, CONV_CH), F32),
                   jax.ShapeDtypeStruct((b, keep, CONV_CH), F32)],
        scratch_shapes=[pltpu.VMEM((HIST + ds, CONV_CH), F32)],
        compiler_params=_params(("parallel",)),
    )(u, state, w_dw, b_dw, g, beta)


def _first_index(hit, iota, limit, axis):
    return jnp.min(jnp.where(hit, iota, limit), axis=axis, keepdims=True)


def _route(x1, wr_hi, wr_lo, bias):
    tm = x1.shape[0]
    x_hi = x1.astype(BF16)
    x_lo = (x1 - x_hi.astype(F32)).astype(BF16)
    logits = _dot_nt(wr_hi, x_hi) + (_dot_nt(wr_lo, x_hi) + _dot_nt(wr_hi, x_lo))
    scores = _sigmoid(logits)
    sel = scores + bias
    sel3 = sel.reshape(N_GROUPS, GROUP_SIZE, tm)
    member = lax.broadcasted_iota(I32, sel3.shape, 1)
    m1 = jnp.max(sel3, axis=1, keepdims=True)
    i1 = _first_index(sel3 == m1, member, GROUP_SIZE, 1)
    m2 = jnp.max(jnp.where(member == i1, -jnp.inf, sel3), axis=1, keepdims=True)
    gs = jnp.broadcast_to(m1 + m2, sel3.shape).reshape(N_EXPERTS, tm)
    eiota = lax.broadcasted_iota(I32, (N_EXPERTS, tm), 0)
    giota = eiota // GROUP_SIZE
    gmask = jnp.zeros((N_EXPERTS, tm), jnp.bool_)
    for _ in range(TOPK_GROUPS):
        m = jnp.max(gs, axis=0, keepdims=True)
        gi = _first_index(gs == m, giota, N_GROUPS, 0)
        pick = giota == gi
        gmask = jnp.logical_or(gmask, pick)
        gs = jnp.where(pick, -jnp.inf, gs)
    selm = jnp.where(gmask, sel, NEG_INF)
    idx_rows, w_rows = [], []
    for _ in range(TOP_K):
        m = jnp.max(selm, axis=0, keepdims=True)
        ei = _first_index(selm == m, eiota, N_EXPERTS, 0)
        pick = eiota == ei
        idx_rows.append(ei)
        w_rows.append(jnp.sum(jnp.where(pick, scores, 0.0), axis=0, keepdims=True))
        selm = jnp.where(pick, -jnp.inf, selm)
    idx = jnp.concatenate(idx_rows, axis=0)
    w = jnp.concatenate(w_rows, axis=0)
    w = w / jnp.sum(w, axis=0, keepdims=True) * ROUTED_SCALE
    return idx, w


def _mix_body(o_ref, c_ref, x_ref, wo_a, wo_c, g_ref, b_ref, wrh_ref, wrl_ref, br_ref,
              x1_ref, idx_ref, wts_ref, rank_ref, cnt_ref, base_ref):
    i = pl.program_id(0)
    tm = x_ref.shape[0]

    @pl.when(i == 0)
    def _():
        base_ref[...] = jnp.zeros(base_ref.shape, F32)

    mix = _dot(o_ref[...].astype(BF16), wo_a[...]) + _dot(c_ref[...].astype(BF16), wo_c[...])
    x1 = _layer_norm(DEEPNORM_ALPHA * x_ref[...] + mix, g_ref[...], b_ref[...])
    x1_ref[...] = x1

    idx, w = _route(x1, wrh_ref[...], wrl_ref[...], br_ref[...])
    idx_ref[...] = idx
    wts_ref[...] = w

    eiota = lax.broadcasted_iota(I32, (N_EXPERTS, tm), 0)
    chosen = jnp.zeros((N_EXPERTS, tm), F32)
    for k in range(TOP_K):
        chosen = chosen + jnp.where(eiota == idx[k:k + 1, :], 1.0, 0.0)
    r = lax.broadcasted_iota(I32, (tm, tm), 0)
    c = lax.broadcasted_iota(I32, (tm, tm), 1)
    before = jnp.where(r < c, 1.0, 0.0).astype(BF16)
    base = base_ref[:, 0:1]
    rank_full = _dot(chosen.astype(BF16), before) + base
    rows = [jnp.sum(jnp.where(eiota == idx[k:k + 1, :], rank_full, 0.0), axis=0, keepdims=True)
            for k in range(TOP_K)]
    rank_ref[...] = jnp.concatenate(rows, axis=0).astype(I32)
    total = base + jnp.sum(chosen, axis=1, keepdims=True)
    base_ref[...] = jnp.broadcast_to(total, base_ref.shape)
    cnt_ref[...] = jnp.broadcast_to(total, cnt_ref.shape)


def _mix(o, c, x, wo_a, wo_c, ln_g, ln_b, wr_hi, wr_lo, b_router, tm):
    n = x.shape[0]
    rows = lambda w: pl.BlockSpec((tm, w), lambda i: (i, 0))
    full = lambda a: pl.BlockSpec(a.shape, lambda i: (0, 0))
    cols = pl.BlockSpec((TOP_K, tm), lambda i: (0, i))
    cnt = pl.BlockSpec((N_EXPERTS, LANES), lambda i: (0, 0))
    return pl.pallas_call(
        _mix_body,
        name="mix",
        grid=(n // tm,),
        in_specs=[rows(ATTN_WIDTH), rows(CONV_CH), rows(D_MODEL), full(wo_a), full(wo_c),
                  full(ln_g), full(ln_b), full(wr_hi), full(wr_lo), full(b_router)],
        out_specs=[rows(D_MODEL), cols, cols, cols, cnt],
        out_shape=[jax.ShapeDtypeStruct((n, D_MODEL), F32),
                   jax.ShapeDtypeStruct((TOP_K, n), I32),
                   jax.ShapeDtypeStruct((TOP_K, n), F32),
                   jax.ShapeDtypeStruct((TOP_K, n), I32),
                   jax.ShapeDtypeStruct((N_EXPERTS, LANES), F32)],
        scratch_shapes=[pltpu.VMEM((N_EXPERTS, LANES), F32)],
        compiler_params=_params(("arbitrary",)),
    )(o, c, x, wo_a, wo_c, ln_g, ln_b, wr_hi, wr_lo, b_router)


def _dest_body(idx_ref, rank_ref, ps_ref, dest_ref):
    idx = idx_ref[...]
    tm = idx.shape[1]
    eiota = lax.broadcasted_iota(I32, (N_EXPERTS, tm), 0)
    ps = ps_ref[...]
    rows = [jnp.sum(jnp.where(eiota == idx[k:k + 1, :], ps, 0.0), axis=0, keepdims=True)
            for k in range(TOP_K)]
    dest_ref[...] = jnp.concatenate(rows, axis=0).astype(I32) + rank_ref[...]


def _dest(idx, rank, pstart_col, tm):
    n = idx.shape[1]
    cols = pl.BlockSpec((TOP_K, tm), lambda i: (0, i))
    return pl.pallas_call(
        _dest_body,
        name="dest",
        grid=(n // tm,),
        in_specs=[cols, cols, pl.BlockSpec((N_EXPERTS, 1), lambda i: (0, 0))],
        out_specs=cols,
        out_shape=jax.ShapeDtypeStruct((TOP_K, n), I32),
        compiler_params=_params(("parallel",)),
    )(idx, rank, pstart_col)


PAD_PIECES = tuple(1 << s for s in reversed(range(EXPERT_BLOCK.bit_length() - 1)))


def _dispatch_body(ps_ref, cnt_ref, dest_ref, x_ref, xs_ref, zbuf, sem, zsem):
    i = pl.program_id(0)
    ts = x_ref.shape[0]

    @pl.when(i == 0)
    def _():
        zbuf[...] = jnp.zeros(zbuf.shape, F32)

        def pad_rows(wait, e, carry):
            cnt = cnt_ref[e]
            base = ps_ref[e] + cnt
            pad = (-cnt) & (EXPERT_BLOCK - 1)
            for p in PAD_PIECES:
                @pl.when((pad & p) != 0)
                def _():
                    cp = pltpu.make_async_copy(zbuf.at[pl.ds(0, p)], xs_ref.at[pl.ds(base, p)],
                                               zsem)
                    if wait:
                        cp.wait()
                    else:
                        cp.start()
                base = base + (pad & p)
            return carry

        lax.fori_loop(0, N_EXPERTS, functools.partial(pad_rows, False), 0)
        lax.fori_loop(0, N_EXPERTS, functools.partial(pad_rows, True), 0)

    def row_copy(t, d):
        return pltpu.make_async_copy(x_ref.at[t], xs_ref.at[d], sem)

    def issue(t, carry):
        for k in range(TOP_K):
            row_copy(t, dest_ref[k, t]).start()
        return carry

    lax.fori_loop(0, ts, issue, 0)

    def drain(t, carry):
        for k in range(TOP_K):
            row_copy(t, dest_ref[k, t]).wait()
        return carry

    lax.fori_loop(0, ts, drain, 0)


def _dispatch(x1_rows, dest, pstart, counts, n_rows, ts):
    n = x1_rows.shape[0]
    grid_spec = pltpu.PrefetchScalarGridSpec(
        num_scalar_prefetch=2,
        grid=(n // ts,),
        in_specs=[pl.BlockSpec((TOP_K, ts), lambda i, ps, cn: (0, i), memory_space=pltpu.SMEM),
                  pl.BlockSpec((ts, ROW_CHUNKS, LANES), lambda i, ps, cn: (i, 0, 0))],
        out_specs=pl.BlockSpec(memory_space=pl.ANY),
        scratch_shapes=[pltpu.VMEM((PAD_PIECES[0], ROW_CHUNKS, LANES), F32),
                        pltpu.SemaphoreType.DMA, pltpu.SemaphoreType.DMA],
    )
    return pl.pallas_call(
        _dispatch_body,
        name="dispatch",
        grid_spec=grid_spec,
        out_shape=jax.ShapeDtypeStruct((n_rows, ROW_CHUNKS, LANES), F32),
        compiler_params=_params(("arbitrary",)),
    )(pstart, counts, dest, x1_rows)


def _expert_body(be_ref, nb_ref, xs_ref, wg_ref, wu_ref, wd_ref, ys_ref):
    del be_ref
    i = pl.program_id(0)

    @pl.when(i < nb_ref[0])
    def _():
        x = jnp.concatenate([xs_ref[:, j, :] for j in range(ROW_CHUNKS)], axis=1).astype(BF16)
        h = (_silu(_dot(x, wg_ref[...])) * _dot(x, wu_ref[...])).astype(BF16)
        y = _dot(h, wd_ref[...])
        for j in range(ROW_CHUNKS):
            ys_ref[:, j, :] = y[:, j * LANES:(j + 1) * LANES]

    @pl.when(i >= nb_ref[0])
    def _():
        ys_ref[...] = jnp.zeros(ys_ref.shape, F32)


def _experts(xs, block_e, nb_used, wg, wu, wd):
    rows = xs.shape[0]
    n_blocks = rows // EXPERT_BLOCK
    blk = (EXPERT_BLOCK, ROW_CHUNKS, LANES)
    in_rows = pl.BlockSpec(blk, lambda i, be, nb: (jnp.minimum(i, nb[0] - 1), 0, 0))
    grid_spec = pltpu.PrefetchScalarGridSpec(
        num_scalar_prefetch=2,
        grid=(n_blocks,),
        in_specs=[in_rows,
                  pl.BlockSpec((None, D_MODEL, D_EXPERT), lambda i, be, nb: (be[i], 0, 0)),
                  pl.BlockSpec((None, D_MODEL, D_EXPERT), lambda i, be, nb: (be[i], 0, 0)),
                  pl.BlockSpec((None, D_EXPERT, D_MODEL), lambda i, be, nb: (be[i], 0, 0))],
        out_specs=pl.BlockSpec(blk, lambda i, be, nb: (i, 0, 0)),
    )
    return pl.pallas_call(
        _expert_body,
        name="experts",
        grid_spec=grid_spec,
        out_shape=jax.ShapeDtypeStruct(xs.shape, F32),
        compiler_params=_params(("arbitrary",)),
    )(block_e, nb_used, xs, wg, wu, wd)


def _combine_body(dest_ref, wts_ref, x1_ref, ys_ref, wsg, wsu, wsd, g_ref, b_ref, y_ref,
                  buf, racc, sem):
    tc = x1_ref.shape[0]

    def row_copy(t, k):
        return pltpu.make_async_copy(ys_ref.at[dest_ref[k, t]], buf.at[k, t], sem)

    def issue(t, carry):
        for k in range(TOP_K):
            row_copy(t, k).start()
        return carry

    lax.fori_loop(0, tc, issue, 0)

    x1 = x1_ref[...]
    xb = x1.astype(BF16)
    hs = (_silu(_dot(xb, wsg[...])) * _dot(xb, wsu[...])).astype(BF16)
    shared = _dot(hs, wsd[...])

    def drain(t, carry):
        for k in range(TOP_K):
            row_copy(t, k).wait()
        return carry

    lax.fori_loop(0, tc, drain, 0)

    def reduce(t, carry):
        acc = wts_ref[0, t] * buf[0, t]
        for k in range(1, TOP_K):
            acc = acc + wts_ref[k, t] * buf[k, t]
        racc[t] = acc
        return carry

    lax.fori_loop(0, tc, reduce, 0)
    routed = jnp.concatenate([racc[:, j, :] for j in range(ROW_CHUNKS)], axis=1)
    y_ref[...] = _layer_norm(DEEPNORM_ALPHA * x1 + (routed + shared), g_ref[...], b_ref[...])


def _combine(dest, wts, x1, ys, wsg, wsu, wsd, ln_g, ln_b, tc):
    n = x1.shape[0]
    smem_cols = pl.BlockSpec((TOP_K, tc), lambda i: (0, i), memory_space=pltpu.SMEM)
    full = lambda a: pl.BlockSpec(a.shape, lambda i: (0, 0))
    rows = pl.BlockSpec((tc, D_MODEL), lambda i: (i, 0))
    return pl.pallas_call(
        _combine_body,
        name="combine",
        grid=(n // tc,),
        in_specs=[smem_cols, smem_cols, rows, pl.BlockSpec(memory_space=pl.ANY),
                  full(wsg), full(wsu), full(wsd), full(ln_g), full(ln_b)],
        out_specs=rows,
        out_shape=jax.ShapeDtypeStruct((n, D_MODEL), F32),
        scratch_shapes=[pltpu.VMEM((TOP_K, tc, ROW_CHUNKS, LANES), F32),
                        pltpu.VMEM((tc, ROW_CHUNKS, LANES), F32),
                        pltpu.SemaphoreType.DMA],
        compiler_params=_params(("arbitrary",)),
    )(dest, wts, x1, ys, wsg, wsu, wsd, ln_g, ln_b)


def _moe(x1, idx, wts, rank, cnt, moe_w, ln_g, ln_b, tile):
    n = x1.shape[0]
    wg, wu, wd, wsg, wsu, wsd = moe_w
    blk = EXPERT_BLOCK
    n_blocks = -(-(n * TOP_K) // blk) + N_EXPERTS
    counts = cnt[:, 0].astype(I32)
    padded = (counts + blk - 1) // blk * blk
    ends = jnp.cumsum(padded)
    pstart = ends - padded
    nb_used = (ends[-1] // blk).astype(I32).reshape(1)
    block_row0 = jnp.arange(n_blocks, dtype=I32) * blk
    block_e = jnp.minimum(jnp.sum((ends[None, :] <= block_row0[:, None]).astype(I32), axis=1),
                          N_EXPERTS - 1).astype(I32)
    dest = _dest(idx, rank, pstart.astype(F32).reshape(N_EXPERTS, 1), tile)
    xs = _dispatch(x1.reshape(n, ROW_CHUNKS, LANES), dest, pstart, counts, n_blocks * blk, tile)
    ys = _experts(xs, block_e, nb_used, wg, wu, wd)
    return _combine(dest, wts, x1, ys, wsg, wsu, wsd, ln_g, ln_b, min(tile, 128))


def kernel(x_prompt, x_sample, cache_k, cache_v, state_conv, page_table, w_in, lam_q1, lam_k1,
           lam_q2, lam_k2, subln_g, w_dw, b_dw, conv_ln_g, conv_ln_b, w_o, ln1_g, ln1_b,
           w_router, b_router, w_gate, w_up, w_down, w_sh_gate, w_sh_up, w_sh_down, ln2_g,
           ln2_b):
    batch, seq, _ = x_prompt.shape
    dec_b, dec_s, _ = x_sample.shape
    n_p, n_s = batch * seq, dec_b * dec_s
    layer = 0
    lam_init = _lambda_init(layer)
    row = lambda a: a[layer].reshape(1, -1)

    w_in_b = w_in[layer].astype(BF16)
    wo = w_o[layer].astype(BF16)
    wo_a, wo_c = wo[:ATTN_WIDTH], wo[ATTN_WIDTH:]
    wr_t = w_router[layer].T
    wr_hi = wr_t.astype(BF16)
    wr_lo = (wr_t - wr_hi.astype(F32)).astype(BF16)
    br = b_router[layer].reshape(-1, 1)
    moe_w = tuple(w[layer].astype(BF16)
                  for w in (w_gate, w_up, w_down, w_sh_gate, w_sh_up, w_sh_down))
    lam_vecs = (row(lam_q1), row(lam_k1), row(lam_q2), row(lam_k2))
    g_sub = row(subln_g)
    conv_w = (w_dw[layer], row(b_dw), row(conv_ln_g), row(conv_ln_b))
    ln1 = (row(ln1_g), row(ln1_b))
    ln2 = (row(ln2_g), row(ln2_b))

    xp = x_prompt.reshape(n_p, D_MODEL)
    q_p, k_p, v_p, u_p, kb_p, vb_p = _inproj(xp, w_in_b, 512)
    o_p = _attn_prompt(q_p, kb_p, vb_p, lam_vecs, g_sub, lam_init, batch, seq, 512)
    c_p = _conv_prompt(u_p, *conv_w, batch, seq, 512)
    x1_p, idx_p, wts_p, rank_p, cnt_p = _mix(o_p, c_p, xp, wo_a, wo_c, *ln1, wr_hi, wr_lo, br, 512)
    y_p = _moe(x1_p, idx_p, wts_p, rank_p, cnt_p, moe_w, *ln2, 256)

    xs = x_sample.reshape(n_s, D_MODEL)
    q_s, k_s, v_s, u_s, _, _ = _inproj(xs, w_in_b, n_s)
    pool = cache_k.shape[1]
    page_rows = lambda a: a[layer].reshape(pool, PAGE_ROWS, V_DIM)
    new_rows = lambda a: a.reshape(dec_b, dec_s * N_HEADS, V_DIM)
    o_s = _attn_sample(q_s.reshape(dec_b, dec_s, QK_WIDTH), new_rows(k_s), new_rows(v_s),
                       page_rows(cache_k), page_rows(cache_v), page_table, lam_vecs, g_sub,
                       lam_init)
    c_s, st_s = _conv_sample(u_s.reshape(dec_b, dec_s, CONV_CH), state_conv[layer], *conv_w)
    x1_s, idx_s, wts_s, rank_s, cnt_s = _mix(o_s.reshape(n_s, ATTN_WIDTH),
                                             c_s.reshape(n_s, CONV_CH), xs, wo_a, wo_c, *ln1,
                                             wr_hi, wr_lo, br, n_s)
    y_s = _moe(x1_s, idx_s, wts_s, rank_s, cnt_s, moe_w, *ln2, 128)

    keep = CONV_WIDTH - 1
    u_p3 = u_p.reshape(batch, seq, CONV_CH)
    return (y_p.reshape(batch, seq, D_MODEL),
            y_s.reshape(dec_b, dec_s, D_MODEL),
            k_p.reshape(1, batch, seq, N_HEADS, V_DIM),
            v_p.reshape(1, batch, seq, N_HEADS, V_DIM),
            u_p3[:, seq - keep:, :][None],
            k_s.reshape(1, dec_b, dec_s, N_HEADS, V_DIM),
            v_s.reshape(1, dec_b, dec_s, N_HEADS, V_DIM),
            st_s[None])
```

```python
import functools
import math

import jax
import jax.numpy as jnp
from jax import lax
from jax.experimental import pallas as pl
from jax.experimental.pallas import tpu as pltpu

F32 = jnp.float32
BF16 = jnp.bfloat16
I32 = jnp.int32

D_MODEL = 1024
N_HEADS = 4
HEAD_DIM = 64
V_DIM = 128
ATTN_WIDTH = N_HEADS * V_DIM
QK_WIDTH = N_HEADS * 2 * HEAD_DIM
ATTN_SCALE = HEAD_DIM ** -0.5
CONV_CH = D_MODEL - ATTN_WIDTH
CONV_WIDTH = 31
N_EXPERTS = 64
N_GROUPS = 8
GROUP_SIZE = N_EXPERTS // N_GROUPS
TOPK_GROUPS = 4
TOP_K = 8
D_EXPERT = D_MODEL // 4
ROUTED_SCALE = 2.5
DEPTH = 1
DEEPNORM_ALPHA = (2 * DEPTH) ** 0.25
NORM_EPS = 1e-5
NEG_INF = -1e30
PAGE_SIZE = 128

LANES = 128
ROW_CHUNKS = D_MODEL // LANES
VMEM_LIMIT = 48 * 1024 * 1024
PAGES_PER_STEP = 8
PAGE_ROWS = PAGE_SIZE * N_HEADS
EXPERT_BLOCK = 256
HIST = 32


def _lambda_init(layer):
    return 0.8 - 0.6 * math.exp(-0.3 * layer)


def _sigmoid(x):
    return 1.0 / (1.0 + jnp.exp(-x))


def _silu(x):
    return x * _sigmoid(x)


def _layer_norm(x, g, b):
    mu = jnp.mean(x, axis=-1, keepdims=True)
    xc = x - mu
    var = jnp.mean(xc * xc, axis=-1, keepdims=True)
    return xc * lax.rsqrt(var + NORM_EPS) * g + b


def _dot(a, b):
    return jnp.dot(a, b, preferred_element_type=F32)


def _dot_nt(a, b):
    return lax.dot_general(a, b, (((1,), (1,)), ((), ())), preferred_element_type=F32)


def _diff_lambda(lq1, lk1, lq2, lk2, lam_init):
    a = jnp.exp(jnp.sum(lq1 * lk1, axis=-1, keepdims=True))
    b = jnp.exp(jnp.sum(lq2 * lk2, axis=-1, keepdims=True))
    return a - b + lam_init


def _params(dims):
    return pltpu.CompilerParams(dimension_semantics=dims, vmem_limit_bytes=VMEM_LIMIT)


def _inproj_body(x_ref, w_ref, *refs):
    xb = x_ref[...].astype(BF16)

    def mm(c0):
        return _dot(xb, w_ref[:, c0:c0 + QK_WIDTH])

    if len(refs) == 4:
        q_ref, k_ref, v_ref, u_ref = refs
        q_ref[...] = mm(0) * ATTN_SCALE
        k_ref[...] = mm(QK_WIDTH)
    else:
        wqt_ref, wvt_ref, q_ref, k_ref, v_ref, u_ref, kb_ref, vt_ref = refs
        q_ref[...] = _dot_nt(wqt_ref[...], xb) * ATTN_SCALE
        vt_ref[...] = _dot_nt(wvt_ref[...], xb).astype(BF16)
        k = mm(QK_WIDTH)
        k_ref[...] = k
        kb_ref[...] = k.astype(BF16)
    v_ref[...] = mm(2 * QK_WIDTH)
    a = mm(2 * QK_WIDTH + ATTN_WIDTH)
    b = mm(2 * QK_WIDTH + ATTN_WIDTH + CONV_CH)
    u_ref[...] = a * _sigmoid(b)


def _inproj(x, w_in_b, tm, transposed_w=()):
    n = x.shape[0]
    wide = QK_WIDTH
    blk = pl.BlockSpec((tm, wide), lambda i: (i, 0))
    full = lambda a: pl.BlockSpec(a.shape, lambda i: (0, 0))
    row_major = jax.ShapeDtypeStruct((n, wide), F32)
    out_specs = [blk] * 4
    out_shape = [row_major] * 4
    if transposed_w:
        t_blk = pl.BlockSpec((None, wide, tm), lambda i: (i, 0, 0))
        out_specs = [t_blk, blk, blk, blk, blk, t_blk]
        out_shape = [jax.ShapeDtypeStruct((n // tm, wide, tm), F32), row_major, row_major,
                     row_major, jax.ShapeDtypeStruct((n, wide), BF16),
                     jax.ShapeDtypeStruct((n // tm, wide, tm), BF16)]
    return pl.pallas_call(
        _inproj_body,
        name="inproj",
        grid=(n // tm,),
        in_specs=[pl.BlockSpec((tm, D_MODEL), lambda i: (i, 0)), full(w_in_b)]
        + [full(w) for w in transposed_w],
        out_specs=out_specs,
        out_shape=out_shape,
        compiler_params=_params(("parallel",)),
    )(x, w_in_b, *transposed_w)


def _attn_p_body(lam_init, q_ref, k_ref, v_ref, lq1, lk1, lq2, lk2, g_ref, o_ref, acc_ref):
    qi = pl.program_id(2)
    tq = q_ref.shape[1]
    tk = v_ref.shape[2]
    q_t = q_ref[...]
    feat = lax.broadcasted_iota(I32, q_t.shape, 0)
    q_maps = (jnp.where(feat < HEAD_DIM, q_t, 0.0).astype(BF16),
              jnp.where(feat >= HEAD_DIM, q_t, 0.0).astype(BF16))
    acc_ref[...] = jnp.zeros(acc_ref.shape, F32)

    def chunk(masked, j, carry):
        kc = k_ref[pl.ds(pl.multiple_of(j * tk, tk), tk), :]
        vc = v_ref[j]
        if masked:
            key = lax.broadcasted_iota(I32, (tk, tq), 0)
            qry = lax.broadcasted_iota(I32, (tk, tq), 1)
            keep = key <= qry
        out = []
        for a in range(2):
            m_old, l_old = carry[2 * a], carry[2 * a + 1]
            s_t = _dot(kc, q_maps[a])
            if masked:
                s_t = jnp.where(keep, s_t, NEG_INF)
            m_new = jnp.maximum(m_old, jnp.max(s_t, axis=0, keepdims=True))
            alpha = jnp.exp(m_old - m_new)
            p_t = jnp.exp(s_t - m_new)
            out.append(m_new)
            out.append(alpha * l_old + jnp.sum(p_t, axis=0, keepdims=True))
            acc_ref[a] = alpha * acc_ref[a] + _dot(vc, p_t.astype(BF16))
        return tuple(out)

    neg = jnp.full((1, tq), -jnp.inf, F32)
    zero = jnp.zeros((1, tq), F32)
    carry = lax.fori_loop(0, qi, functools.partial(chunk, False), (neg, zero, neg, zero))
    _, l0, _, l1 = chunk(True, qi, carry)

    lam = _diff_lambda(lq1[...], lk1[...], lq2[...], lk2[...], lam_init)
    o_t = acc_ref[0] / l0 - lam * (acc_ref[1] / l1)
    ms = jnp.mean(o_t * o_t, axis=0, keepdims=True)
    o_t = o_t * lax.rsqrt(ms + NORM_EPS) * g_ref[...] * (1.0 - lam_init)
    o_ref[...] = o_t.T


def _attn_prompt(q_t, kb, v_t, lam_vecs, subln_g_col, lam_init, batch, seq):
    t = q_t.shape[2]
    n = kb.shape[0]
    nq = seq // t
    small = lambda w: pl.BlockSpec((1, w), lambda b, h, i: (0, 0))
    return pl.pallas_call(
        functools.partial(_attn_p_body, lam_init),
        name="attn_prompt",
        grid=(batch, N_HEADS, nq),
        in_specs=[pl.BlockSpec((None, V_DIM, t), lambda b, h, i: (b * nq + i, h, 0)),
                  pl.BlockSpec((seq, V_DIM), lambda b, h, i: (b, h)),
                  pl.BlockSpec((nq, V_DIM, t), lambda b, h, i: (b, h, 0))]
        + [small(HEAD_DIM)] * 4 + [pl.BlockSpec((V_DIM, 1), lambda b, h, i: (0, 0))],
        out_specs=pl.BlockSpec((t, V_DIM), lambda b, h, i: (b * nq + i, h)),
        out_shape=jax.ShapeDtypeStruct((n, ATTN_WIDTH), F32),
        scratch_shapes=[pltpu.VMEM((2, V_DIM, t), F32)],
        compiler_params=_params(("parallel", "parallel", "parallel")),
    )(q_t, kb, v_t, *lam_vecs, subln_g_col)


def _attn_s_body(lam_init, n_steps, pt_ref, q_ref, kn_ref, vn_ref, *rest):
    npg = PAGES_PER_STEP
    k_refs = rest[:npg]
    v_refs = rest[npg:2 * npg]
    (lq1, lk1, lq2, lk2, g_ref, o_ref, qall, knew, vnew, m_ref, l_ref,
     acc_ref) = rest[2 * npg:]
    j = pl.program_id(1)
    ds = q_ref.shape[0]
    nrow = N_HEADS * 2 * ds
    row = lax.broadcasted_iota(I32, (nrow, PAGE_ROWS), 0)
    col = lax.broadcasted_iota(I32, (nrow, PAGE_ROWS), 1)
    head_ok = (col % N_HEADS) == (row // (2 * ds))

    @pl.when(j == 0)
    def _():
        pieces = []
        for h in range(N_HEADS):
            qh = q_ref[:, h * V_DIM:(h + 1) * V_DIM]
            lane = lax.broadcasted_iota(I32, qh.shape, 1)
            pieces.append(jnp.where(lane < HEAD_DIM, qh, 0.0))
            pieces.append(jnp.where(lane >= HEAD_DIM, qh, 0.0))
        qa = jnp.concatenate(pieces, axis=0).astype(BF16)
        qall[...] = qa
        knew[...] = jnp.zeros(knew.shape, F32)
        vnew[...] = jnp.zeros(vnew.shape, F32)
        knew[0:ds * N_HEADS] = kn_ref[...]
        vnew[0:ds * N_HEADS] = vn_ref[...]
        keep = jnp.logical_and(head_ok, (col // N_HEADS) <= (row % ds))
        s = jnp.where(keep, _dot_nt(qa, knew[...].astype(BF16)), NEG_INF)
        m = jnp.max(s, axis=-1, keepdims=True)
        p = jnp.exp(s - m)
        m_ref[...] = m
        l_ref[...] = jnp.sum(p, axis=-1, keepdims=True)
        acc_ref[...] = _dot(p.astype(BF16), vnew[...].astype(BF16))

    qa = qall[...]
    bias = jnp.where(head_ok, 0.0, NEG_INF)
    s = [_dot_nt(qa, k_refs[i][...].astype(BF16)) + bias for i in range(npg)]
    m_old = m_ref[...]
    m_new = m_old
    for i in range(npg):
        m_new = jnp.maximum(m_new, jnp.max(s[i], axis=-1, keepdims=True))
    alpha = jnp.exp(m_old - m_new)
    l_new = alpha * l_ref[...]
    acc = alpha * acc_ref[...]
    for i in range(npg):
        p = jnp.exp(s[i] - m_new)
        l_new = l_new + jnp.sum(p, axis=-1, keepdims=True)
        acc = acc + _dot(p.astype(BF16), v_refs[i][...].astype(BF16))
    m_ref[...] = m_new
    l_ref[...] = l_new
    acc_ref[...] = acc

    @pl.when(j == n_steps - 1)
    def _():
        lam = _diff_lambda(lq1[...], lk1[...], lq2[...], lk2[...], lam_init)
        on = acc / l_new
        for h in range(N_HEADS):
            r0 = h * 2 * ds
            o = on[r0:r0 + ds] - lam * on[r0 + ds:r0 + 2 * ds]
            ms = jnp.mean(o * o, axis=-1, keepdims=True)
            o_ref[:, h * V_DIM:(h + 1) * V_DIM] = (
                o * lax.rsqrt(ms + NORM_EPS) * g_ref[...] * (1.0 - lam_init))


def _attn_sample(q, k_new, v_new, cache_k, cache_v, page_table, lam_vecs, subln_g, lam_init):
    b, ds, _ = q.shape
    n_pages = page_table.shape[1]
    n_steps = n_pages // PAGES_PER_STEP
    nrow = N_HEADS * 2 * ds
    pt = page_table.reshape(-1)

    def page_map(i, bi, j, pt_ref):
        return (pt_ref[bi * n_pages + j * PAGES_PER_STEP + i], 0, 0)

    page_specs = [pl.BlockSpec((None, PAGE_ROWS, V_DIM), functools.partial(page_map, i))
                  for i in range(PAGES_PER_STEP)]
    small = lambda w: pl.BlockSpec((1, w), lambda bi, j, pt_ref: (0, 0))
    new_spec = pl.BlockSpec((None, ds * N_HEADS, V_DIM), lambda bi, j, pt_ref: (bi, 0, 0))
    qo_spec = pl.BlockSpec((None, ds, ATTN_WIDTH), lambda bi, j, pt_ref: (bi, 0, 0))
    grid_spec = pltpu.PrefetchScalarGridSpec(
        num_scalar_prefetch=1,
        grid=(b, n_steps),
        in_specs=[qo_spec, new_spec, new_spec] + page_specs + page_specs
        + [small(HEAD_DIM)] * 4 + [small(V_DIM)],
        out_specs=qo_spec,
        scratch_shapes=[pltpu.VMEM((nrow, V_DIM), BF16),
                        pltpu.VMEM((PAGE_ROWS, V_DIM), F32),
                        pltpu.VMEM((PAGE_ROWS, V_DIM), F32),
                        pltpu.VMEM((nrow, 1), F32),
                        pltpu.VMEM((nrow, 1), F32),
                        pltpu.VMEM((nrow, V_DIM), F32)],
    )
    return pl.pallas_call(
        functools.partial(_attn_s_body, lam_init, n_steps),
        name="attn_sample",
        grid_spec=grid_spec,
        out_shape=jax.ShapeDtypeStruct((b, ds, ATTN_WIDTH), F32),
        compiler_params=_params(("parallel", "arbitrary")),
    )(pt, q, k_new, v_new, *([cache_k] * PAGES_PER_STEP), *([cache_v] * PAGES_PER_STEP),
      *lam_vecs, subln_g)


def _conv_taps(ext_ref, w_ref, bias, rows):
    off = HIST - (CONV_WIDTH - 1)
    acc = jnp.broadcast_to(bias, (rows, CONV_CH))
    for jt in range(CONV_WIDTH):
        acc = acc + w_ref[jt:jt + 1, :] * ext_ref[pl.ds(jt + off, rows), :]
    return acc


def _conv_p_body(u_ref, h_ref, w_ref, b_ref, g_ref, beta_ref, c_ref, ext_ref):
    t = pl.program_id(1)
    tm = u_ref.shape[0]
    hist = h_ref[...]
    ext_ref[0:HIST] = jnp.where(t == 0, jnp.zeros_like(hist), hist)
    ext_ref[HIST:HIST + tm] = u_ref[...]
    y = _conv_taps(ext_ref, w_ref, b_ref[...], tm)
    c_ref[...] = _silu(_layer_norm(y, g_ref[...], beta_ref[...]))


def _conv_prompt(u, w_dw, b_dw, g, beta, batch, seq, tm):
    n = u.shape[0]
    nt = seq // tm
    per = tm // HIST
    cur = lambda b, t: (b * nt + t, 0)
    prev = lambda b, t: (jnp.maximum((b * nt + t) * per - 1, 0), 0)
    small = lambda r: pl.BlockSpec((r, CONV_CH), lambda b, t: (0, 0))
    return pl.pallas_call(
        _conv_p_body,
        name="conv_prompt",
        grid=(batch, nt),
        in_specs=[pl.BlockSpec((tm, CONV_CH), cur), pl.BlockSpec((HIST, CONV_CH), prev),
                  small(CONV_WIDTH), small(1), small(1), small(1)],
        out_specs=pl.BlockSpec((tm, CONV_CH), cur),
        out_shape=jax.ShapeDtypeStruct((n, CONV_CH), F32),
        scratch_shapes=[pltpu.VMEM((HIST + tm, CONV_CH), F32)],
        compiler_params=_params(("parallel", "parallel")),
    )(u, u, w_dw, b_dw, g, beta)


def _conv_s_body(u_ref, st_ref, w_ref, b_ref, g_ref, beta_ref, c_ref, ns_ref, ext_ref):
    ds = u_ref.shape[0]
    keep = CONV_WIDTH - 1
    off = HIST - keep
    ext_ref[0:off] = jnp.zeros((off, CONV_CH), F32)
    ext_ref[off:HIST] = st_ref[...]
    ext_ref[HIST:HIST + ds] = u_ref[...]
    y = _conv_taps(ext_ref, w_ref, b_ref[...], ds)
    c_ref[...] = _silu(_layer_norm(y, g_ref[...], beta_ref[...]))
    ns_ref[...] = ext_ref[HIST + ds - keep:HIST + ds]


def _conv_sample(u, state, w_dw, b_dw, g, beta):
    b, ds, _ = u.shape
    keep = CONV_WIDTH - 1
    per_b = lambda r: pl.BlockSpec((None, r, CONV_CH), lambda i: (i, 0, 0))
    small = lambda r: pl.BlockSpec((r, CONV_CH), lambda i: (0, 0))
    return pl.pallas_call(
        _conv_s_body,
        name="conv_sample",
        grid=(b,),
        in_specs=[per_b(ds), per_b(keep), small(CONV_WIDTH), small(1), small(1), small(1)],
        out_specs=[per_b(ds), per_b(keep)],
        out_shape=[jax.ShapeDtypeStruct((b, ds, CONV_CH), F32),
                   jax.ShapeDtypeStruct((b, keep, CONV_CH), F32)],
        scratch_shapes=[pltpu.VMEM((HIST + ds, CONV_CH), F32)],
        compiler_params=_params(("parallel",)),
    )(u, state, w_dw, b_dw, g, beta)


def _first_index(hit, iota, limit, axis):
    return jnp.min(jnp.where(hit, iota, limit), axis=axis, keepdims=True)


def _route(x1, wr_hi, wr_lo, bias):
    tm = x1.shape[0]
    x_hi = x1.astype(BF16)
    x_lo = (x1 - x_hi.astype(F32)).astype(BF16)
    logits = _dot_nt(wr_hi, x_hi) + (_dot_nt(wr_lo, x_hi) + _dot_nt(wr_hi, x_lo))
    scores = _sigmoid(logits)
    sel = scores + bias
    sel3 = sel.reshape(N_GROUPS, GROUP_SIZE, tm)
    member = lax.broadcasted_iota(I32, sel3.shape, 1)
    m1 = jnp.max(sel3, axis=1, keepdims=True)
    i1 = _first_index(sel3 == m1, member, GROUP_SIZE, 1)
    m2 = jnp.max(jnp.where(member == i1, -jnp.inf, sel3), axis=1, keepdims=True)
    gs = jnp.broadcast_to(m1 + m2, sel3.shape).reshape(N_EXPERTS, tm)
    eiota = lax.broadcasted_iota(I32, (N_EXPERTS, tm), 0)
    giota = eiota // GROUP_SIZE
    gmask = jnp.zeros((N_EXPERTS, tm), jnp.bool_)
    for _ in range(TOPK_GROUPS):
        m = jnp.max(gs, axis=0, keepdims=True)
        gi = _first_index(gs == m, giota, N_GROUPS, 0)
        pick = giota == gi
        gmask = jnp.logical_or(gmask, pick)
        gs = jnp.where(pick, -jnp.inf, gs)
    selm = jnp.where(gmask, sel, NEG_INF)
    idx_rows, w_rows = [], []
    for _ in range(TOP_K):
        m = jnp.max(selm, axis=0, keepdims=True)
        ei = _first_index(selm == m, eiota, N_EXPERTS, 0)
        pick = eiota == ei
        idx_rows.append(ei)
        w_rows.append(jnp.sum(jnp.where(pick, scores, 0.0), axis=0, keepdims=True))
        selm = jnp.where(pick, -jnp.inf, selm)
    idx = jnp.concatenate(idx_rows, axis=0)
    w = jnp.concatenate(w_rows, axis=0)
    w = w / jnp.sum(w, axis=0, keepdims=True) * ROUTED_SCALE
    return idx, w


def _mix_body(o_ref, c_ref, x_ref, wo_a, wo_c, g_ref, b_ref, wrh_ref, wrl_ref, br_ref,
              x1_ref, idx_ref, wts_ref, rank_ref, cnt_ref, base_ref):
    i = pl.program_id(0)
    tm = x_ref.shape[0]

    @pl.when(i == 0)
    def _():
        base_ref[...] = jnp.zeros(base_ref.shape, F32)

    mix = _dot(o_ref[...].astype(BF16), wo_a[...]) + _dot(c_ref[...].astype(BF16), wo_c[...])
    x1 = _layer_norm(DEEPNORM_ALPHA * x_ref[...] + mix, g_ref[...], b_ref[...])
    x1_ref[...] = x1

    idx, w = _route(x1, wrh_ref[...], wrl_ref[...], br_ref[...])
    idx_ref[...] = idx
    wts_ref[...] = w

    eiota = lax.broadcasted_iota(I32, (N_EXPERTS, tm), 0)
    chosen = jnp.zeros((N_EXPERTS, tm), F32)
    for k in range(TOP_K):
        chosen = chosen + jnp.where(eiota == idx[k:k + 1, :], 1.0, 0.0)
    r = lax.broadcasted_iota(I32, (tm, tm), 0)
    c = lax.broadcasted_iota(I32, (tm, tm), 1)
    before = jnp.where(r < c, 1.0, 0.0).astype(BF16)
    base = base_ref[:, 0:1]
    rank_full = _dot(chosen.astype(BF16), before) + base
    rows = [jnp.sum(jnp.where(eiota == idx[k:k + 1, :], rank_full, 0.0), axis=0, keepdims=True)
            for k in range(TOP_K)]
    rank_ref[...] = jnp.concatenate(rows, axis=0).astype(I32)
    total = base + jnp.sum(chosen, axis=1, keepdims=True)
    base_ref[...] = jnp.broadcast_to(total, base_ref.shape)
    cnt_ref[...] = jnp.broadcast_to(total, cnt_ref.shape)


def _mix(o, c, x, wo_a, wo_c, ln_g, ln_b, wr_hi, wr_lo, b_router, tm):
    n = x.shape[0]
    rows = lambda w: pl.BlockSpec((tm, w), lambda i: (i, 0))
    full = lambda a: pl.BlockSpec(a.shape, lambda i: (0, 0))
    cols = pl.BlockSpec((TOP_K, tm), lambda i: (0, i))
    cnt = pl.BlockSpec((N_EXPERTS, LANES), lambda i: (0, 0))
    return pl.pallas_call(
        _mix_body,
        name="mix",
        grid=(n // tm,),
        in_specs=[rows(ATTN_WIDTH), rows(CONV_CH), rows(D_MODEL), full(wo_a), full(wo_c),
                  full(ln_g), full(ln_b), full(wr_hi), full(wr_lo), full(b_router)],
        out_specs=[rows(D_MODEL), cols, cols, cols, cnt],
        out_shape=[jax.ShapeDtypeStruct((n, D_MODEL), F32),
                   jax.ShapeDtypeStruct((TOP_K, n), I32),
                   jax.ShapeDtypeStruct((TOP_K, n), F32),
                   jax.ShapeDtypeStruct((TOP_K, n), I32),
                   jax.ShapeDtypeStruct((N_EXPERTS, LANES), F32)],
        scratch_shapes=[pltpu.VMEM((N_EXPERTS, LANES), F32)],
        compiler_params=_params(("arbitrary",)),
    )(o, c, x, wo_a, wo_c, ln_g, ln_b, wr_hi, wr_lo, b_router)


def _dest_body(idx_ref, rank_ref, ps_ref, dest_ref):
    idx = idx_ref[...]
    tm = idx.shape[1]
    eiota = lax.broadcasted_iota(I32, (N_EXPERTS, tm), 0)
    ps = ps_ref[...]
    rows = [jnp.sum(jnp.where(eiota == idx[k:k + 1, :], ps, 0.0), axis=0, keepdims=True)
            for k in range(TOP_K)]
    dest_ref[...] = jnp.concatenate(rows, axis=0).astype(I32) + rank_ref[...]


def _dest(idx, rank, pstart_col, tm):
    n = idx.shape[1]
    cols = pl.BlockSpec((TOP_K, tm), lambda i: (0, i))
    return pl.pallas_call(
        _dest_body,
        name="dest",
        grid=(n // tm,),
        in_specs=[cols, cols, pl.BlockSpec((N_EXPERTS, 1), lambda i: (0, 0))],
        out_specs=cols,
        out_shape=jax.ShapeDtypeStruct((TOP_K, n), I32),
        compiler_params=_params(("parallel",)),
    )(idx, rank, pstart_col)


PAD_PIECES = tuple(1 << s for s in reversed(range(EXPERT_BLOCK.bit_length() - 1)))


def _dispatch_body(ps_ref, cnt_ref, dest_ref, x_ref, xs_ref, zbuf, sem, zsem):
    i = pl.program_id(0)
    ts = x_ref.shape[0]

    @pl.when(i == 0)
    def _():
        zbuf[...] = jnp.zeros(zbuf.shape, F32)

        def pad_rows(wait, e, carry):
            cnt = cnt_ref[e]
            base = ps_ref[e] + cnt
            pad = (-cnt) & (EXPERT_BLOCK - 1)
            for p in PAD_PIECES:
                @pl.when((pad & p) != 0)
                def _():
                    cp = pltpu.make_async_copy(zbuf.at[pl.ds(0, p)], xs_ref.at[pl.ds(base, p)],
                                               zsem)
                    if wait:
                        cp.wait()
                    else:
                        cp.start()
                base = base + (pad & p)
            return carry

        lax.fori_loop(0, N_EXPERTS, functools.partial(pad_rows, False), 0)
        lax.fori_loop(0, N_EXPERTS, functools.partial(pad_rows, True), 0)

    def row_copy(t, d):
        return pltpu.make_async_copy(x_ref.at[t], xs_ref.at[d], sem)

    def issue(t, carry):
        for k in range(TOP_K):
            row_copy(t, dest_ref[k, t]).start()
        return carry

    lax.fori_loop(0, ts, issue, 0)

    def drain(t, carry):
        for k in range(TOP_K):
            row_copy(t, dest_ref[k, t]).wait()
        return carry

    lax.fori_loop(0, ts, drain, 0)


def _dispatch(x1_rows, dest, pstart, counts, n_rows, ts):
    n = x1_rows.shape[0]
    grid_spec = pltpu.PrefetchScalarGridSpec(
        num_scalar_prefetch=2,
        grid=(n // ts,),
        in_specs=[pl.BlockSpec((TOP_K, ts), lambda i, ps, cn: (0, i), memory_space=pltpu.SMEM),
                  pl.BlockSpec((ts, ROW_CHUNKS, LANES), lambda i, ps, cn: (i, 0, 0))],
        out_specs=pl.BlockSpec(memory_space=pl.ANY),
        scratch_shapes=[pltpu.VMEM((PAD_PIECES[0], ROW_CHUNKS, LANES), F32),
                        pltpu.SemaphoreType.DMA, pltpu.SemaphoreType.DMA],
    )
    return pl.pallas_call(
        _dispatch_body,
        name="dispatch",
        grid_spec=grid_spec,
        out_shape=jax.ShapeDtypeStruct((n_rows, ROW_CHUNKS, LANES), F32),
        compiler_params=_params(("arbitrary",)),
    )(pstart, counts, dest, x1_rows)


def _expert_body(be_ref, nb_ref, xs_ref, wg_ref, wu_ref, wd_ref, ys_ref):
    del be_ref
    i = pl.program_id(0)

    @pl.when(i < nb_ref[0])
    def _():
        x = jnp.concatenate([xs_ref[:, j, :] for j in range(ROW_CHUNKS)], axis=1).astype(BF16)
        h = (_silu(_dot(x, wg_ref[...])) * _dot(x, wu_ref[...])).astype(BF16)
        y = _dot(h, wd_ref[...])
        for j in range(ROW_CHUNKS):
            ys_ref[:, j, :] = y[:, j * LANES:(j + 1) * LANES]

    @pl.when(i >= nb_ref[0])
    def _():
        ys_ref[...] = jnp.zeros(ys_ref.shape, F32)


def _experts(xs, block_e, nb_used, wg, wu, wd):
    rows = xs.shape[0]
    n_blocks = rows // EXPERT_BLOCK
    blk = (EXPERT_BLOCK, ROW_CHUNKS, LANES)
    in_rows = pl.BlockSpec(blk, lambda i, be, nb: (jnp.minimum(i, nb[0] - 1), 0, 0))
    grid_spec = pltpu.PrefetchScalarGridSpec(
        num_scalar_prefetch=2,
        grid=(n_blocks,),
        in_specs=[in_rows,
                  pl.BlockSpec((None, D_MODEL, D_EXPERT), lambda i, be, nb: (be[i], 0, 0)),
                  pl.BlockSpec((None, D_MODEL, D_EXPERT), lambda i, be, nb: (be[i], 0, 0)),
                  pl.BlockSpec((None, D_EXPERT, D_MODEL), lambda i, be, nb: (be[i], 0, 0))],
        out_specs=pl.BlockSpec(blk, lambda i, be, nb: (i, 0, 0)),
    )
    return pl.pallas_call(
        _expert_body,
        name="experts",
        grid_spec=grid_spec,
        out_shape=jax.ShapeDtypeStruct(xs.shape, F32),
        compiler_params=_params(("arbitrary",)),
    )(block_e, nb_used, xs, wg, wu, wd)


def _combine_body(dest_ref, wts_ref, x1_ref, ys_ref, wsg, wsu, wsd, g_ref, b_ref, y_ref,
                  buf, racc, sem):
    tc = x1_ref.shape[0]

    def row_copy(t, k):
        return pltpu.make_async_copy(ys_ref.at[dest_ref[k, t]], buf.at[k, t], sem)

    def issue(t, carry):
        for k in range(TOP_K):
            row_copy(t, k).start()
        return carry

    lax.fori_loop(0, tc, issue, 0)

    x1 = x1_ref[...]
    xb = x1.astype(BF16)
    hs = (_silu(_dot(xb, wsg[...])) * _dot(xb, wsu[...])).astype(BF16)
    shared = _dot(hs, wsd[...])

    def drain(t, carry):
        for k in range(TOP_K):
            row_copy(t, k).wait()
        return carry

    lax.fori_loop(0, tc, drain, 0)

    def reduce(t, carry):
        acc = wts_ref[0, t] * buf[0, t]
        for k in range(1, TOP_K):
            acc = acc + wts_ref[k, t] * buf[k, t]
        racc[t] = acc
        return carry

    lax.fori_loop(0, tc, reduce, 0)
    routed = jnp.concatenate([racc[:, j, :] for j in range(ROW_CHUNKS)], axis=1)
    y_ref[...] = _layer_norm(DEEPNORM_ALPHA * x1 + (routed + shared), g_ref[...], b_ref[...])


def _combine(dest, wts, x1, ys, wsg, wsu, wsd, ln_g, ln_b, tc):
    n = x1.shape[0]
    smem_cols = pl.BlockSpec((TOP_K, tc), lambda i: (0, i), memory_space=pltpu.SMEM)
    full = lambda a: pl.BlockSpec(a.shape, lambda i: (0, 0))
    rows = pl.BlockSpec((tc, D_MODEL), lambda i: (i, 0))
    return pl.pallas_call(
        _combine_body,
        name="combine",
        grid=(n // tc,),
        in_specs=[smem_cols, smem_cols, rows, pl.BlockSpec(memory_space=pl.ANY),
                  full(wsg), full(wsu), full(wsd), full(ln_g), full(ln_b)],
        out_specs=rows,
        out_shape=jax.ShapeDtypeStruct((n, D_MODEL), F32),
        scratch_shapes=[pltpu.VMEM((TOP_K, tc, ROW_CHUNKS, LANES), F32),
                        pltpu.VMEM((tc, ROW_CHUNKS, LANES), F32),
                        pltpu.SemaphoreType.DMA],
        compiler_params=_params(("arbitrary",)),
    )(dest, wts, x1, ys, wsg, wsu, wsd, ln_g, ln_b)


def _moe(x1, idx, wts, rank, cnt, moe_w, ln_g, ln_b, tile):
    n = x1.shape[0]
    wg, wu, wd, wsg, wsu, wsd = moe_w
    blk = EXPERT_BLOCK
    n_blocks = -(-(n * TOP_K) // blk) + N_EXPERTS
    counts = cnt[:, 0].astype(I32)
    padded = (counts + blk - 1) // blk * blk
    ends = jnp.cumsum(padded)
    pstart = ends - padded
    nb_used = (ends[-1] // blk).astype(I32).reshape(1)
    block_row0 = jnp.arange(n_blocks, dtype=I32) * blk
    block_e = jnp.minimum(jnp.sum((ends[None, :] <= block_row0[:, None]).astype(I32), axis=1),
                          N_EXPERTS - 1).astype(I32)
    dest = _dest(idx, rank, pstart.astype(F32).reshape(N_EXPERTS, 1), tile)
    xs = _dispatch(x1.reshape(n, ROW_CHUNKS, LANES), dest, pstart, counts, n_blocks * blk, tile)
    ys = _experts(xs, block_e, nb_used, wg, wu, wd)
    return _combine(dest, wts, x1, ys, wsg, wsu, wsd, ln_g, ln_b, min(tile, 128))


def kernel(x_prompt, x_sample, cache_k, cache_v, state_conv, page_table, w_in, lam_q1, lam_k1,
           lam_q2, lam_k2, subln_g, w_dw, b_dw, conv_ln_g, conv_ln_b, w_o, ln1_g, ln1_b,
           w_router, b_router, w_gate, w_up, w_down, w_sh_gate, w_sh_up, w_sh_down, ln2_g,
           ln2_b):
    batch, seq, _ = x_prompt.shape
    dec_b, dec_s, _ = x_sample.shape
    n_p, n_s = batch * seq, dec_b * dec_s
    layer = 0
    lam_init = _lambda_init(layer)
    row = lambda a: a[layer].reshape(1, -1)

    w_in_b = w_in[layer].astype(BF16)
    wo = w_o[layer].astype(BF16)
    wo_a, wo_c = wo[:ATTN_WIDTH], wo[ATTN_WIDTH:]
    wr_t = w_router[layer].T
    wr_hi = wr_t.astype(BF16)
    wr_lo = (wr_t - wr_hi.astype(F32)).astype(BF16)
    br = b_router[layer].reshape(-1, 1)
    moe_w = tuple(w[layer].astype(BF16)
                  for w in (w_gate, w_up, w_down, w_sh_gate, w_sh_up, w_sh_down))
    lam_vecs = (row(lam_q1), row(lam_k1), row(lam_q2), row(lam_k2))
    g_sub = row(subln_g)
    conv_w = (w_dw[layer], row(b_dw), row(conv_ln_g), row(conv_ln_b))
    ln1 = (row(ln1_g), row(ln1_b))
    ln2 = (row(ln2_g), row(ln2_b))

    xp = x_prompt.reshape(n_p, D_MODEL)
    wq_t = w_in_b[:, :QK_WIDTH].T
    wv_t = w_in_b[:, 2 * QK_WIDTH:2 * QK_WIDTH + ATTN_WIDTH].T
    qt_p, k_p, v_p, u_p, kb_p, vt_p = _inproj(xp, w_in_b, 512, (wq_t, wv_t))
    o_p = _attn_prompt(qt_p, kb_p, vt_p, lam_vecs, g_sub.reshape(V_DIM, 1), lam_init, batch, seq)
    c_p = _conv_prompt(u_p, *conv_w, batch, seq, 512)
    x1_p, idx_p, wts_p, rank_p, cnt_p = _mix(o_p, c_p, xp, wo_a, wo_c, *ln1, wr_hi, wr_lo, br, 512)
    y_p = _moe(x1_p, idx_p, wts_p, rank_p, cnt_p, moe_w, *ln2, 256)

    xs = x_sample.reshape(n_s, D_MODEL)
    q_s, k_s, v_s, u_s = _inproj(xs, w_in_b, n_s)
    pool = cache_k.shape[1]
    page_rows = lambda a: a[layer].reshape(pool, PAGE_ROWS, V_DIM)
    new_rows = lambda a: a.reshape(dec_b, dec_s * N_HEADS, V_DIM)
    o_s = _attn_sample(q_s.reshape(dec_b, dec_s, QK_WIDTH), new_rows(k_s), new_rows(v_s),
                       page_rows(cache_k), page_rows(cache_v), page_table, lam_vecs, g_sub,
                       lam_init)
    c_s, st_s = _conv_sample(u_s.reshape(dec_b, dec_s, CONV_CH), state_conv[layer], *conv_w)
    x1_s, idx_s, wts_s, rank_s, cnt_s = _mix(o_s.reshape(n_s, ATTN_WIDTH),
                                             c_s.reshape(n_s, CONV_CH), xs, wo_a, wo_c, *ln1,
                                             wr_hi, wr_lo, br, n_s)
    y_s = _moe(x1_s, idx_s, wts_s, rank_s, cnt_s, moe_w, *ln2, 128)

    keep = CONV_WIDTH - 1
    u_p3 = u_p.reshape(batch, seq, CONV_CH)
    return (y_p.reshape(batch, seq, D_MODEL),
            y_s.reshape(dec_b, dec_s, D_MODEL),
            k_p.reshape(1, batch, seq, N_HEADS, V_DIM),
            v_p.reshape(1, batch, seq, N_HEADS, V_DIM),
            u_p3[:, seq - keep:, :][None],
            k_s.reshape(1, dec_b, dec_s, N_HEADS, V_DIM),
            v_s.reshape(1, dec_b, dec_s, N_HEADS, V_DIM),
            st_s[None])
```

```python
import functools
import math

import jax
import jax.numpy as jnp
from jax import lax
from jax.experimental import pallas as pl
from jax.experimental.pallas import tpu as pltpu

F32 = jnp.float32
BF16 = jnp.bfloat16
I32 = jnp.int32
U32 = jnp.uint32

D_MODEL = 1024
N_HEADS = 4
HEAD_DIM = 64
V_DIM = 128
ATTN_WIDTH = N_HEADS * V_DIM
QK_WIDTH = N_HEADS * 2 * HEAD_DIM
ATTN_SCALE = HEAD_DIM ** -0.5
CONV_CH = D_MODEL - ATTN_WIDTH
CONV_WIDTH = 31
N_EXPERTS = 64
N_GROUPS = 8
GROUP_SIZE = N_EXPERTS // N_GROUPS
TOPK_GROUPS = 4
TOP_K = 8
D_EXPERT = D_MODEL // 4
ROUTED_SCALE = 2.5
DEPTH = 1
DEEPNORM_ALPHA = (2 * DEPTH) ** 0.25
NORM_EPS = 1e-5
NEG_INF = -1e30
PAGE_SIZE = 128

LANES = 128
ROW_TILE = D_MODEL // LANES
VMEM_LIMIT = 48 * 1024 * 1024
PAGES_PER_STEP = 8
PAGE_ROWS = PAGE_SIZE * N_HEADS
EXPERT_BLOCK = 256
HIST = 32


def _lambda_init(layer):
    return 0.8 - 0.6 * math.exp(-0.3 * layer)


def _sigmoid(x):
    return 1.0 / (1.0 + jnp.exp(-x))


def _silu(x):
    return x * _sigmoid(x)


def _layer_norm(x, g, b):
    mu = jnp.mean(x, axis=-1, keepdims=True)
    xc = x - mu
    var = jnp.mean(xc * xc, axis=-1, keepdims=True)
    return xc * lax.rsqrt(var + NORM_EPS) * g + b


def _dot(a, b):
    return jnp.dot(a, b, preferred_element_type=F32)


def _dot_nt(a, b):
    return lax.dot_general(a, b, (((1,), (1,)), ((), ())), preferred_element_type=F32)


def _diff_lambda(lq1, lk1, lq2, lk2, lam_init):
    a = jnp.exp(jnp.sum(lq1 * lk1, axis=-1, keepdims=True))
    b = jnp.exp(jnp.sum(lq2 * lk2, axis=-1, keepdims=True))
    return a - b + lam_init


def _params(dims):
    return pltpu.CompilerParams(dimension_semantics=dims, vmem_limit_bytes=VMEM_LIMIT)


def _store_row_tiles(ref, x):
    m = x.shape[0]
    for j in range(ROW_TILE):
        ref[pl.ds(j, m, stride=ROW_TILE), :] = x[:, j * LANES:(j + 1) * LANES]


def _load_row_tiles(ref, m):
    return jnp.concatenate([ref[pl.ds(j, m, stride=ROW_TILE), :] for j in range(ROW_TILE)],
                           axis=1)


def _token_tile(ref, t):
    return ref.at[pl.ds(pl.multiple_of(t * ROW_TILE, ROW_TILE), ROW_TILE)]


def _inproj_body(x_ref, w_ref, *refs):
    xb = x_ref[...].astype(BF16)

    def mm(c0):
        return _dot(xb, w_ref[:, c0:c0 + QK_WIDTH])

    if len(refs) == 4:
        q_ref, k_ref, v_ref, u_ref = refs
        q_ref[...] = mm(0) * ATTN_SCALE
        k_ref[...] = mm(QK_WIDTH)
    else:
        wqt_ref, wvt_ref, q_ref, k_ref, v_ref, u_ref, kb_ref, vt_ref = refs
        q_ref[...] = _dot_nt(wqt_ref[...], xb) * ATTN_SCALE
        vt_ref[...] = _dot_nt(wvt_ref[...], xb).astype(BF16)
        k = mm(QK_WIDTH)
        k_ref[...] = k
        kb_ref[...] = k.astype(BF16)
    v_ref[...] = mm(2 * QK_WIDTH)
    a = mm(2 * QK_WIDTH + ATTN_WIDTH)
    b = mm(2 * QK_WIDTH + ATTN_WIDTH + CONV_CH)
    u_ref[...] = a * _sigmoid(b)


def _inproj(x, w_in_b, tm, transposed_w=()):
    n = x.shape[0]
    wide = QK_WIDTH
    blk = pl.BlockSpec((tm, wide), lambda i: (i, 0))
    full = lambda a: pl.BlockSpec(a.shape, lambda i: (0, 0))
    row_major = jax.ShapeDtypeStruct((n, wide), F32)
    out_specs = [blk] * 4
    out_shape = [row_major] * 4
    if transposed_w:
        t_blk = pl.BlockSpec((None, wide, tm), lambda i: (i, 0, 0))
        out_specs = [t_blk, blk, blk, blk, blk, t_blk]
        out_shape = [jax.ShapeDtypeStruct((n // tm, wide, tm), F32), row_major, row_major,
                     row_major, jax.ShapeDtypeStruct((n, wide), BF16),
                     jax.ShapeDtypeStruct((n // tm, wide, tm), BF16)]
    return pl.pallas_call(
        _inproj_body,
        name="inproj",
        grid=(n // tm,),
        in_specs=[pl.BlockSpec((tm, D_MODEL), lambda i: (i, 0)), full(w_in_b)]
        + [full(w) for w in transposed_w],
        out_specs=out_specs,
        out_shape=out_shape,
        compiler_params=_params(("parallel",)),
    )(x, w_in_b, *transposed_w)


def _attn_p_body(lam_init, q_ref, k_ref, v_ref, lq1, lk1, lq2, lk2, g_ref, o_ref, acc_ref):
    qi = pl.program_id(2)
    tq = q_ref.shape[1]
    tk = v_ref.shape[2]
    q_t = q_ref[...]
    feat = lax.broadcasted_iota(I32, q_t.shape, 0)
    q_maps = (jnp.where(feat < HEAD_DIM, q_t, 0.0).astype(BF16),
              jnp.where(feat >= HEAD_DIM, q_t, 0.0).astype(BF16))
    acc_ref[...] = jnp.zeros(acc_ref.shape, F32)

    def chunk(masked, j, carry):
        kc = k_ref[pl.ds(pl.multiple_of(j * tk, tk), tk), :]
        vc = v_ref[j]
        if masked:
            key = lax.broadcasted_iota(I32, (tk, tq), 0)
            qry = lax.broadcasted_iota(I32, (tk, tq), 1)
            keep = key <= qry
        out = []
        for a in range(2):
            m_old, l_old = carry[2 * a], carry[2 * a + 1]
            s_t = _dot(kc, q_maps[a])
            if masked:
                s_t = jnp.where(keep, s_t, NEG_INF)
            m_new = jnp.maximum(m_old, jnp.max(s_t, axis=0, keepdims=True))
            alpha = jnp.exp(m_old - m_new)
            p_t = jnp.exp(s_t - m_new)
            out.append(m_new)
            out.append(alpha * l_old + jnp.sum(p_t, axis=0, keepdims=True))
            acc_ref[a] = alpha * acc_ref[a] + _dot(vc, p_t.astype(BF16))
        return tuple(out)

    neg = jnp.full((1, tq), -jnp.inf, F32)
    zero = jnp.zeros((1, tq), F32)
    carry = lax.fori_loop(0, qi, functools.partial(chunk, False), (neg, zero, neg, zero))
    _, l0, _, l1 = chunk(True, qi, carry)

    lam = _diff_lambda(lq1[...], lk1[...], lq2[...], lk2[...], lam_init)
    o_t = acc_ref[0] / l0 - lam * (acc_ref[1] / l1)
    ms = jnp.mean(o_t * o_t, axis=0, keepdims=True)
    o_t = o_t * lax.rsqrt(ms + NORM_EPS) * g_ref[...] * (1.0 - lam_init)
    o_ref[...] = o_t.T


def _attn_prompt(q_t, kb, v_t, lam_vecs, subln_g_col, lam_init, batch, seq):
    t = q_t.shape[2]
    n = kb.shape[0]
    nq = seq // t
    small = lambda w: pl.BlockSpec((1, w), lambda b, h, i: (0, 0))
    return pl.pallas_call(
        functools.partial(_attn_p_body, lam_init),
        name="attn_prompt",
        grid=(batch, N_HEADS, nq),
        in_specs=[pl.BlockSpec((None, V_DIM, t), lambda b, h, i: (b * nq + i, h, 0)),
                  pl.BlockSpec((seq, V_DIM), lambda b, h, i: (b, h)),
                  pl.BlockSpec((nq, V_DIM, t), lambda b, h, i: (b, h, 0))]
        + [small(HEAD_DIM)] * 4 + [pl.BlockSpec((V_DIM, 1), lambda b, h, i: (0, 0))],
        out_specs=pl.BlockSpec((t, V_DIM), lambda b, h, i: (b * nq + i, h)),
        out_shape=jax.ShapeDtypeStruct((n, ATTN_WIDTH), F32),
        scratch_shapes=[pltpu.VMEM((2, V_DIM, t), F32)],
        compiler_params=_params(("parallel", "parallel", "parallel")),
    )(q_t, kb, v_t, *lam_vecs, subln_g_col)


def _attn_s_body(lam_init, n_steps, pt_ref, q_ref, kn_ref, vn_ref, *rest):
    npg = PAGES_PER_STEP
    k_refs = rest[:npg]
    v_refs = rest[npg:2 * npg]
    (lq1, lk1, lq2, lk2, g_ref, o_ref, qall, knew, vnew, m_ref, l_ref,
     acc_ref) = rest[2 * npg:]
    j = pl.program_id(1)
    ds = q_ref.shape[0]
    nrow = N_HEADS * 2 * ds
    row = lax.broadcasted_iota(I32, (nrow, PAGE_ROWS), 0)
    col = lax.broadcasted_iota(I32, (nrow, PAGE_ROWS), 1)
    head_ok = (col % N_HEADS) == (row // (2 * ds))

    @pl.when(j == 0)
    def _():
        pieces = []
        for h in range(N_HEADS):
            qh = q_ref[:, h * V_DIM:(h + 1) * V_DIM]
            lane = lax.broadcasted_iota(I32, qh.shape, 1)
            pieces.append(jnp.where(lane < HEAD_DIM, qh, 0.0))
            pieces.append(jnp.where(lane >= HEAD_DIM, qh, 0.0))
        qa = jnp.concatenate(pieces, axis=0).astype(BF16)
        qall[...] = qa
        knew[...] = jnp.zeros(knew.shape, F32)
        vnew[...] = jnp.zeros(vnew.shape, F32)
        knew[0:ds * N_HEADS] = kn_ref[...]
        vnew[0:ds * N_HEADS] = vn_ref[...]
        keep = jnp.logical_and(head_ok, (col // N_HEADS) <= (row % ds))
        s = jnp.where(keep, _dot_nt(qa, knew[...].astype(BF16)), NEG_INF)
        m = jnp.max(s, axis=-1, keepdims=True)
        p = jnp.exp(s - m)
        m_ref[...] = m
        l_ref[...] = jnp.sum(p, axis=-1, keepdims=True)
        acc_ref[...] = _dot(p.astype(BF16), vnew[...].astype(BF16))

    qa = qall[...]
    bias = jnp.where(head_ok, 0.0, NEG_INF)
    s = [_dot_nt(qa, k_refs[i][...].astype(BF16)) + bias for i in range(npg)]
    m_old = m_ref[...]
    m_new = m_old
    for i in range(npg):
        m_new = jnp.maximum(m_new, jnp.max(s[i], axis=-1, keepdims=True))
    alpha = jnp.exp(m_old - m_new)
    l_new = alpha * l_ref[...]
    acc = alpha * acc_ref[...]
    for i in range(npg):
        p = jnp.exp(s[i] - m_new)
        l_new = l_new + jnp.sum(p, axis=-1, keepdims=True)
        acc = acc + _dot(p.astype(BF16), v_refs[i][...].astype(BF16))
    m_ref[...] = m_new
    l_ref[...] = l_new
    acc_ref[...] = acc

    @pl.when(j == n_steps - 1)
    def _():
        lam = _diff_lambda(lq1[...], lk1[...], lq2[...], lk2[...], lam_init)
        on = acc / l_new
        for h in range(N_HEADS):
            r0 = h * 2 * ds
            o = on[r0:r0 + ds] - lam * on[r0 + ds:r0 + 2 * ds]
            ms = jnp.mean(o * o, axis=-1, keepdims=True)
            o_ref[:, h * V_DIM:(h + 1) * V_DIM] = (
                o * lax.rsqrt(ms + NORM_EPS) * g_ref[...] * (1.0 - lam_init))


def _attn_sample(q, k_new, v_new, cache_k, cache_v, page_table, lam_vecs, subln_g, lam_init):
    b, ds, _ = q.shape
    n_pages = page_table.shape[1]
    n_steps = n_pages // PAGES_PER_STEP
    nrow = N_HEADS * 2 * ds
    pt = page_table.reshape(-1)

    def page_map(i, bi, j, pt_ref):
        return (pt_ref[bi * n_pages + j * PAGES_PER_STEP + i], 0, 0)

    page_specs = [pl.BlockSpec((None, PAGE_ROWS, V_DIM), functools.partial(page_map, i))
                  for i in range(PAGES_PER_STEP)]
    small = lambda w: pl.BlockSpec((1, w), lambda bi, j, pt_ref: (0, 0))
    new_spec = pl.BlockSpec((None, ds * N_HEADS, V_DIM), lambda bi, j, pt_ref: (bi, 0, 0))
    qo_spec = pl.BlockSpec((None, ds, ATTN_WIDTH), lambda bi, j, pt_ref: (bi, 0, 0))
    grid_spec = pltpu.PrefetchScalarGridSpec(
        num_scalar_prefetch=1,
        grid=(b, n_steps),
        in_specs=[qo_spec, new_spec, new_spec] + page_specs + page_specs
        + [small(HEAD_DIM)] * 4 + [small(V_DIM)],
        out_specs=qo_spec,
        scratch_shapes=[pltpu.VMEM((nrow, V_DIM), BF16),
                        pltpu.VMEM((PAGE_ROWS, V_DIM), F32),
                        pltpu.VMEM((PAGE_ROWS, V_DIM), F32),
                        pltpu.VMEM((nrow, 1), F32),
                        pltpu.VMEM((nrow, 1), F32),
                        pltpu.VMEM((nrow, V_DIM), F32)],
    )
    return pl.pallas_call(
        functools.partial(_attn_s_body, lam_init, n_steps),
        name="attn_sample",
        grid_spec=grid_spec,
        out_shape=jax.ShapeDtypeStruct((b, ds, ATTN_WIDTH), F32),
        compiler_params=_params(("parallel", "arbitrary")),
    )(pt, q, k_new, v_new, *([cache_k] * PAGES_PER_STEP), *([cache_v] * PAGES_PER_STEP),
      *lam_vecs, subln_g)


def _conv_taps(ext_ref, w_ref, bias, rows):
    off = HIST - (CONV_WIDTH - 1)
    acc = jnp.broadcast_to(bias, (rows, CONV_CH))
    for jt in range(CONV_WIDTH):
        acc = acc + w_ref[jt:jt + 1, :] * ext_ref[pl.ds(jt + off, rows), :]
    return acc


def _conv_p_body(u_ref, h_ref, w_ref, b_ref, g_ref, beta_ref, c_ref, ext_ref):
    t = pl.program_id(1)
    tm = u_ref.shape[0]
    hist = h_ref[...]
    ext_ref[0:HIST] = jnp.where(t == 0, jnp.zeros_like(hist), hist)
    ext_ref[HIST:HIST + tm] = u_ref[...]
    y = _conv_taps(ext_ref, w_ref, b_ref[...], tm)
    c_ref[...] = _silu(_layer_norm(y, g_ref[...], beta_ref[...]))


def _conv_prompt(u, w_dw, b_dw, g, beta, batch, seq, tm):
    n = u.shape[0]
    nt = seq // tm
    per = tm // HIST
    cur = lambda b, t: (b * nt + t, 0)
    prev = lambda b, t: (jnp.maximum((b * nt + t) * per - 1, 0), 0)
    small = lambda r: pl.BlockSpec((r, CONV_CH), lambda b, t: (0, 0))
    return pl.pallas_call(
        _conv_p_body,
        name="conv_prompt",
        grid=(batch, nt),
        in_specs=[pl.BlockSpec((tm, CONV_CH), cur), pl.BlockSpec((HIST, CONV_CH), prev),
                  small(CONV_WIDTH), small(1), small(1), small(1)],
        out_specs=pl.BlockSpec((tm, CONV_CH), cur),
        out_shape=jax.ShapeDtypeStruct((n, CONV_CH), F32),
        scratch_shapes=[pltpu.VMEM((HIST + tm, CONV_CH), F32)],
        compiler_params=_params(("parallel", "parallel")),
    )(u, u, w_dw, b_dw, g, beta)


def _conv_s_body(u_ref, st_ref, w_ref, b_ref, g_ref, beta_ref, c_ref, ns_ref, ext_ref):
    ds = u_ref.shape[0]
    keep = CONV_WIDTH - 1
    off = HIST - keep
    ext_ref[0:off] = jnp.zeros((off, CONV_CH), F32)
    ext_ref[off:HIST] = st_ref[...]
    ext_ref[HIST:HIST + ds] = u_ref[...]
    y = _conv_taps(ext_ref, w_ref, b_ref[...], ds)
    c_ref[...] = _silu(_layer_norm(y, g_ref[...], beta_ref[...]))
    ns_ref[...] = ext_ref[HIST + ds - keep:HIST + ds]


def _conv_sample(u, state, w_dw, b_dw, g, beta):
    b, ds, _ = u.shape
    keep = CONV_WIDTH - 1
    per_b = lambda r: pl.BlockSpec((None, r, CONV_CH), lambda i: (i, 0, 0))
    small = lambda r: pl.BlockSpec((r, CONV_CH), lambda i: (0, 0))
    return pl.pallas_call(
        _conv_s_body,
        name="conv_sample",
        grid=(b,),
        in_specs=[per_b(ds), per_b(keep), small(CONV_WIDTH), small(1), small(1), small(1)],
        out_specs=[per_b(ds), per_b(keep)],
        out_shape=[jax.ShapeDtypeStruct((b, ds, CONV_CH), F32),
                   jax.ShapeDtypeStruct((b, keep, CONV_CH), F32)],
        scratch_shapes=[pltpu.VMEM((HIST + ds, CONV_CH), F32)],
        compiler_params=_params(("parallel",)),
    )(u, state, w_dw, b_dw, g, beta)


def _first_index(hit, iota, limit, axis):
    return jnp.min(jnp.where(hit, iota, limit), axis=axis, keepdims=True)


def _route(x1, wr_hi, wr_lo, bias):
    tm = x1.shape[0]
    x_hi = x1.astype(BF16)
    x_lo = (x1 - x_hi.astype(F32)).astype(BF16)
    logits = _dot_nt(wr_hi, x_hi) + (_dot_nt(wr_lo, x_hi) + _dot_nt(wr_hi, x_lo))
    scores = _sigmoid(logits)
    sel = scores + bias
    sel3 = sel.reshape(N_GROUPS, GROUP_SIZE, tm)
    member = lax.broadcasted_iota(I32, sel3.shape, 1)
    m1 = jnp.max(sel3, axis=1, keepdims=True)
    i1 = _first_index(sel3 == m1, member, GROUP_SIZE, 1)
    m2 = jnp.max(jnp.where(member == i1, -jnp.inf, sel3), axis=1, keepdims=True)
    gs = jnp.broadcast_to(m1 + m2, sel3.shape).reshape(N_EXPERTS, tm)
    eiota = lax.broadcasted_iota(I32, (N_EXPERTS, tm), 0)
    giota = eiota // GROUP_SIZE
    gmask = jnp.zeros((N_EXPERTS, tm), jnp.bool_)
    for _ in range(TOPK_GROUPS):
        m = jnp.max(gs, axis=0, keepdims=True)
        gi = _first_index(gs == m, giota, N_GROUPS, 0)
        pick = giota == gi
        gmask = jnp.logical_or(gmask, pick)
        gs = jnp.where(pick, -jnp.inf, gs)
    selm = jnp.where(gmask, sel, NEG_INF)
    idx_rows, w_rows = [], []
    for _ in range(TOP_K):
        m = jnp.max(selm, axis=0, keepdims=True)
        ei = _first_index(selm == m, eiota, N_EXPERTS, 0)
        pick = eiota == ei
        idx_rows.append(ei)
        w_rows.append(jnp.sum(jnp.where(pick, scores, 0.0), axis=0, keepdims=True))
        selm = jnp.where(pick, -jnp.inf, selm)
    idx = jnp.concatenate(idx_rows, axis=0)
    w = jnp.concatenate(w_rows, axis=0)
    w = w / jnp.sum(w, axis=0, keepdims=True) * ROUTED_SCALE
    return idx, w


def _mix_body(o_ref, c_ref, x_ref, wo_a, wo_c, g_ref, b_ref, wrh_ref, wrl_ref, br_ref,
              x1_ref, xrt_ref, idx_ref, wts_ref, rank_ref, cnt_ref, base_ref):
    i = pl.program_id(0)
    tm = x_ref.shape[0]

    @pl.when(i == 0)
    def _():
        base_ref[...] = jnp.zeros(base_ref.shape, F32)

    mix = _dot(o_ref[...].astype(BF16), wo_a[...]) + _dot(c_ref[...].astype(BF16), wo_c[...])
    x1 = _layer_norm(DEEPNORM_ALPHA * x_ref[...] + mix, g_ref[...], b_ref[...])
    x1_ref[...] = x1
    _store_row_tiles(xrt_ref, x1)

    idx, w = _route(x1, wrh_ref[...], wrl_ref[...], br_ref[...])
    idx_ref[...] = idx
    w_rows = jnp.concatenate([w, jnp.zeros((LANES - TOP_K, tm), F32)], axis=0)
    wts_ref[...] = w_rows.T

    eiota = lax.broadcasted_iota(I32, (N_EXPERTS, tm), 0)
    chosen = jnp.zeros((N_EXPERTS, tm), F32)
    for k in range(TOP_K):
        chosen = chosen + jnp.where(eiota == idx[k:k + 1, :], 1.0, 0.0)
    r = lax.broadcasted_iota(I32, (tm, tm), 0)
    c = lax.broadcasted_iota(I32, (tm, tm), 1)
    before = jnp.where(r < c, 1.0, 0.0).astype(BF16)
    base = base_ref[:, 0:1]
    rank_full = _dot(chosen.astype(BF16), before) + base
    rows = [jnp.sum(jnp.where(eiota == idx[k:k + 1, :], rank_full, 0.0), axis=0, keepdims=True)
            for k in range(TOP_K)]
    rank_ref[...] = jnp.concatenate(rows, axis=0).astype(I32)
    total = base + jnp.sum(chosen, axis=1, keepdims=True)
    base_ref[...] = jnp.broadcast_to(total, base_ref.shape)
    cnt_ref[...] = jnp.broadcast_to(total, cnt_ref.shape)


def _mix(o, c, x, wo_a, wo_c, ln_g, ln_b, wr_hi, wr_lo, b_router, tm):
    n = x.shape[0]
    rows = lambda w: pl.BlockSpec((tm, w), lambda i: (i, 0))
    full = lambda a: pl.BlockSpec(a.shape, lambda i: (0, 0))
    cols = pl.BlockSpec((TOP_K, tm), lambda i: (0, i))
    cnt = pl.BlockSpec((N_EXPERTS, LANES), lambda i: (0, 0))
    return pl.pallas_call(
        _mix_body,
        name="mix",
        grid=(n // tm,),
        in_specs=[rows(ATTN_WIDTH), rows(CONV_CH), rows(D_MODEL), full(wo_a), full(wo_c),
                  full(ln_g), full(ln_b), full(wr_hi), full(wr_lo), full(b_router)],
        out_specs=[rows(D_MODEL), pl.BlockSpec((tm * ROW_TILE, LANES), lambda i: (i, 0)), cols,
                   rows(LANES), cols, cnt],
        out_shape=[jax.ShapeDtypeStruct((n, D_MODEL), F32),
                   jax.ShapeDtypeStruct((n * ROW_TILE, LANES), F32),
                   jax.ShapeDtypeStruct((TOP_K, n), I32),
                   jax.ShapeDtypeStruct((n, LANES), F32),
                   jax.ShapeDtypeStruct((TOP_K, n), I32),
                   jax.ShapeDtypeStruct((N_EXPERTS, LANES), F32)],
        scratch_shapes=[pltpu.VMEM((N_EXPERTS, LANES), F32)],
        compiler_params=_params(("arbitrary",)),
    )(o, c, x, wo_a, wo_c, ln_g, ln_b, wr_hi, wr_lo, b_router)


def _dest_body(idx_ref, rank_ref, ps_ref, dest_ref):
    idx = idx_ref[...]
    tm = idx.shape[1]
    eiota = lax.broadcasted_iota(I32, (N_EXPERTS, tm), 0)
    ps = ps_ref[...]
    rows = [jnp.sum(jnp.where(eiota == idx[k:k + 1, :], ps, 0.0), axis=0, keepdims=True)
            for k in range(TOP_K)]
    dest_ref[...] = jnp.concatenate(rows, axis=0).astype(I32) + rank_ref[...]


def _dest(idx, rank, pstart_col, tm):
    n = idx.shape[1]
    cols = pl.BlockSpec((TOP_K, tm), lambda i: (0, i))
    return pl.pallas_call(
        _dest_body,
        name="dest",
        grid=(n // tm,),
        in_specs=[cols, cols, pl.BlockSpec((N_EXPERTS, 1), lambda i: (0, 0))],
        out_specs=cols,
        out_shape=jax.ShapeDtypeStruct((TOP_K, n), I32),
        compiler_params=_params(("parallel",)),
    )(idx, rank, pstart_col)


PAD_PIECES = tuple(1 << s for s in reversed(range(EXPERT_BLOCK.bit_length() - 1)))
ROW_UNROLL = 4


def _dispatch_body(ps_ref, cnt_ref, dest_ref, x_ref, xs_ref, zbuf, sem, zsem):
    i = pl.program_id(0)
    ts = x_ref.shape[0] // ROW_TILE

    @pl.when(i == 0)
    def _():
        zbuf[...] = jnp.zeros(zbuf.shape, F32)

        def pad_rows(wait, e, carry):
            cnt = cnt_ref[e]
            base = ps_ref[e] + cnt
            pad = (-cnt) & (EXPERT_BLOCK - 1)
            for p in PAD_PIECES:
                @pl.when((pad & p) != 0)
                def _():
                    first = pl.multiple_of(base * ROW_TILE, ROW_TILE)
                    cp = pltpu.make_async_copy(zbuf.at[pl.ds(0, p * ROW_TILE)],
                                               xs_ref.at[pl.ds(first, p * ROW_TILE)], zsem)
                    if wait:
                        cp.wait()
                    else:
                        cp.start()
                base = base + (pad & p)
            return carry

        lax.fori_loop(0, N_EXPERTS, functools.partial(pad_rows, False), 0)
        lax.fori_loop(0, N_EXPERTS, functools.partial(pad_rows, True), 0)

    def row_copy(t, k):
        return pltpu.make_async_copy(_token_tile(x_ref, t), _token_tile(xs_ref, dest_ref[k, t]),
                                     sem)

    def for_rows(wait, g, carry):
        for u in range(ROW_UNROLL):
            for k in range(TOP_K):
                cp = row_copy(g * ROW_UNROLL + u, k)
                if wait:
                    cp.wait()
                else:
                    cp.start(priority=k % 2)
        return carry

    lax.fori_loop(0, ts // ROW_UNROLL, functools.partial(for_rows, False), 0)
    lax.fori_loop(0, ts // ROW_UNROLL, functools.partial(for_rows, True), 0)


def _dispatch(x_tiles, dest, pstart, counts, n_rows, ts):
    n = x_tiles.shape[0] // ROW_TILE
    grid_spec = pltpu.PrefetchScalarGridSpec(
        num_scalar_prefetch=2,
        grid=(n // ts,),
        in_specs=[pl.BlockSpec((TOP_K, ts), lambda i, ps, cn: (0, i), memory_space=pltpu.SMEM),
                  pl.BlockSpec((ts * ROW_TILE, LANES), lambda i, ps, cn: (i, 0))],
        out_specs=pl.BlockSpec(memory_space=pl.ANY),
        scratch_shapes=[pltpu.VMEM((PAD_PIECES[0] * ROW_TILE, LANES), F32),
                        pltpu.SemaphoreType.DMA, pltpu.SemaphoreType.DMA],
    )
    return pl.pallas_call(
        _dispatch_body,
        name="dispatch",
        grid_spec=grid_spec,
        out_shape=jax.ShapeDtypeStruct((n_rows * ROW_TILE, LANES), F32),
        compiler_params=_params(("arbitrary",)),
    )(pstart, counts, dest, x_tiles)


def _expert_body(be_ref, nb_ref, xs_ref, wg_ref, wu_ref, wd_ref, ys_ref):
    del be_ref
    i = pl.program_id(0)

    @pl.when(i < nb_ref[0])
    def _():
        x = _load_row_tiles(xs_ref, EXPERT_BLOCK).astype(BF16)
        h = (_silu(_dot(x, wg_ref[...])) * _dot(x, wu_ref[...])).astype(BF16)
        _store_row_tiles(ys_ref, _dot(h, wd_ref[...]))

    @pl.when(i >= nb_ref[0])
    def _():
        ys_ref[...] = jnp.zeros(ys_ref.shape, F32)


def _experts(xs, block_e, nb_used, wg, wu, wd):
    n_blocks = xs.shape[0] // (EXPERT_BLOCK * ROW_TILE)
    blk = (EXPERT_BLOCK * ROW_TILE, LANES)
    in_rows = pl.BlockSpec(blk, lambda i, be, nb: (jnp.minimum(i, nb[0] - 1), 0))
    grid_spec = pltpu.PrefetchScalarGridSpec(
        num_scalar_prefetch=2,
        grid=(n_blocks,),
        in_specs=[in_rows,
                  pl.BlockSpec((None, D_MODEL, D_EXPERT), lambda i, be, nb: (be[i], 0, 0)),
                  pl.BlockSpec((None, D_MODEL, D_EXPERT), lambda i, be, nb: (be[i], 0, 0)),
                  pl.BlockSpec((None, D_EXPERT, D_MODEL), lambda i, be, nb: (be[i], 0, 0))],
        out_specs=pl.BlockSpec(blk, lambda i, be, nb: (i, 0)),
    )
    return pl.pallas_call(
        _expert_body,
        name="experts",
        grid_spec=grid_spec,
        out_shape=jax.ShapeDtypeStruct(xs.shape, F32),
        compiler_params=_params(("arbitrary",)),
    )(block_e, nb_used, xs, wg, wu, wd)


def _combine_body(dest_ref, dnext_ref, wts_ref, x1_ref, ys_ref, wsg, wsu, wsd, g_ref, b_ref,
                  y_ref, buf, sems):
    i = pl.program_id(0)
    n_tiles = pl.num_programs(0)
    tc = x1_ref.shape[0]
    slot = i % 2

    def for_rows(wait, d_ref, s, g, carry):
        for u in range(ROW_UNROLL):
            t = g * ROW_UNROLL + u
            for k in range(TOP_K):
                cp = pltpu.make_async_copy(_token_tile(ys_ref, d_ref[k, t]),
                                           _token_tile(buf.at[s, k], t), sems.at[s])
                if wait:
                    cp.wait()
                else:
                    cp.start(priority=k % 2)
        return carry

    n_groups = tc // ROW_UNROLL

    @pl.when(i == 0)
    def _():
        lax.fori_loop(0, n_groups, functools.partial(for_rows, False, dest_ref, slot), 0)

    @pl.when(i + 1 < n_tiles)
    def _():
        lax.fori_loop(0, n_groups, functools.partial(for_rows, False, dnext_ref, 1 - slot), 0)

    x1 = x1_ref[...]
    xb = x1.astype(BF16)
    hs = (_silu(_dot(xb, wsg[...])) * _dot(xb, wsu[...])).astype(BF16)
    shared = _dot(hs, wsd[...])

    lax.fori_loop(0, n_groups, functools.partial(for_rows, True, dest_ref, slot), 0)

    w = wts_ref[...]
    routed = w[:, 0:1] * _load_row_tiles(buf.at[slot, 0], tc)
    for k in range(1, TOP_K):
        routed = routed + w[:, k:k + 1] * _load_row_tiles(buf.at[slot, k], tc)
    y_ref[...] = _layer_norm(DEEPNORM_ALPHA * x1 + (routed + shared), g_ref[...], b_ref[...])


def _combine(dest, wts_t, x1, ys, wsg, wsu, wsd, ln_g, ln_b, tc):
    n = x1.shape[0]
    last = n // tc - 1
    dest_cur = pl.BlockSpec((TOP_K, tc), lambda i: (0, i), memory_space=pltpu.SMEM)
    dest_next = pl.BlockSpec((TOP_K, tc), lambda i: (0, jnp.minimum(i + 1, last)),
                             memory_space=pltpu.SMEM)
    full = lambda a: pl.BlockSpec(a.shape, lambda i: (0, 0))
    rows = lambda w: pl.BlockSpec((tc, w), lambda i: (i, 0))
    return pl.pallas_call(
        _combine_body,
        name="combine",
        grid=(n // tc,),
        in_specs=[dest_cur, dest_next, rows(LANES), rows(D_MODEL),
                  pl.BlockSpec(memory_space=pl.ANY),
                  full(wsg), full(wsu), full(wsd), full(ln_g), full(ln_b)],
        out_specs=rows(D_MODEL),
        out_shape=jax.ShapeDtypeStruct((n, D_MODEL), F32),
        scratch_shapes=[pltpu.VMEM((2, TOP_K, tc * ROW_TILE, LANES), F32),
                        pltpu.SemaphoreType.DMA((2,))],
        compiler_params=_params(("arbitrary",)),
    )(dest, dest, wts_t, x1, ys, wsg, wsu, wsd, ln_g, ln_b)


def _moe(x1, x_tiles, idx, wts_t, rank, cnt, moe_w, ln_g, ln_b, tile):
    n = x1.shape[0]
    wg, wu, wd, wsg, wsu, wsd = moe_w
    blk = EXPERT_BLOCK
    n_blocks = -(-(n * TOP_K) // blk) + N_EXPERTS
    counts = cnt[:, 0].astype(I32)
    padded = (counts + blk - 1) // blk * blk
    ends = jnp.cumsum(padded)
    pstart = ends - padded
    nb_used = (ends[-1] // blk).astype(I32).reshape(1)
    block_row0 = jnp.arange(n_blocks, dtype=I32) * blk
    block_e = jnp.minimum(jnp.sum((ends[None, :] <= block_row0[:, None]).astype(I32), axis=1),
                          N_EXPERTS - 1).astype(I32)
    dest = _dest(idx, rank, pstart.astype(F32).reshape(N_EXPERTS, 1), tile)
    xs = _dispatch(x_tiles, dest, pstart, counts, n_blocks * blk, tile)
    ys = _experts(xs, block_e, nb_used, wg, wu, wd)
    return _combine(dest, wts_t, x1, ys, wsg, wsu, wsd, ln_g, ln_b, min(tile, 128))


def kernel(x_prompt, x_sample, cache_k, cache_v, state_conv, page_table, w_in, lam_q1, lam_k1,
           lam_q2, lam_k2, subln_g, w_dw, b_dw, conv_ln_g, conv_ln_b, w_o, ln1_g, ln1_b,
           w_router, b_router, w_gate, w_up, w_down, w_sh_gate, w_sh_up, w_sh_down, ln2_g,
           ln2_b):
    batch, seq, _ = x_prompt.shape
    dec_b, dec_s, _ = x_sample.shape
    n_p, n_s = batch * seq, dec_b * dec_s
    layer = 0
    lam_init = _lambda_init(layer)
    row = lambda a: a[layer].reshape(1, -1)

    w_in_b = w_in[layer].astype(BF16)
    wo = w_o[layer].astype(BF16)
    wo_a, wo_c = wo[:ATTN_WIDTH], wo[ATTN_WIDTH:]
    wr_t = w_router[layer].T
    wr_hi = wr_t.astype(BF16)
    wr_lo = (wr_t - wr_hi.astype(F32)).astype(BF16)
    br = b_router[layer].reshape(-1, 1)
    moe_w = tuple(w[layer].astype(BF16)
                  for w in (w_gate, w_up, w_down, w_sh_gate, w_sh_up, w_sh_down))
    lam_vecs = (row(lam_q1), row(lam_k1), row(lam_q2), row(lam_k2))
    g_sub = row(subln_g)
    conv_w = (w_dw[layer], row(b_dw), row(conv_ln_g), row(conv_ln_b))
    ln1 = (row(ln1_g), row(ln1_b))
    ln2 = (row(ln2_g), row(ln2_b))

    xp = x_prompt.reshape(n_p, D_MODEL)
    wq_t = w_in_b[:, :QK_WIDTH].T
    wv_t = w_in_b[:, 2 * QK_WIDTH:2 * QK_WIDTH + ATTN_WIDTH].T
    qt_p, k_p, v_p, u_p, kb_p, vt_p = _inproj(xp, w_in_b, 512, (wq_t, wv_t))
    o_p = _attn_prompt(qt_p, kb_p, vt_p, lam_vecs, g_sub.reshape(V_DIM, 1), lam_init, batch, seq)
    c_p = _conv_prompt(u_p, *conv_w, batch, seq, 512)
    routed_p = _mix(o_p, c_p, xp, wo_a, wo_c, *ln1, wr_hi, wr_lo, br, 512)
    y_p = _moe(*routed_p, moe_w, *ln2, 256)

    xs = x_sample.reshape(n_s, D_MODEL)
    q_s, k_s, v_s, u_s = _inproj(xs, w_in_b, n_s)
    pool = cache_k.shape[1]
    page_rows = lambda a: a[layer].reshape(pool, PAGE_ROWS, V_DIM)
    new_rows = lambda a: a.reshape(dec_b, dec_s * N_HEADS, V_DIM)
    o_s = _attn_sample(q_s.reshape(dec_b, dec_s, QK_WIDTH), new_rows(k_s), new_rows(v_s),
                       page_rows(cache_k), page_rows(cache_v), page_table, lam_vecs, g_sub,
                       lam_init)
    c_s, st_s = _conv_sample(u_s.reshape(dec_b, dec_s, CONV_CH), state_conv[layer], *conv_w)
    routed_s = _mix(o_s.reshape(n_s, ATTN_WIDTH), c_s.reshape(n_s, CONV_CH), xs, wo_a, wo_c,
                    *ln1, wr_hi, wr_lo, br, n_s)
    y_s = _moe(*routed_s, moe_w, *ln2, 128)

    keep = CONV_WIDTH - 1
    u_p3 = u_p.reshape(batch, seq, CONV_CH)
    return (y_p.reshape(batch, seq, D_MODEL),
            y_s.reshape(dec_b, dec_s, D_MODEL),
            k_p.reshape(1, batch, seq, N_HEADS, V_DIM),
            v_p.reshape(1, batch, seq, N_HEADS, V_DIM),
            u_p3[:, seq - keep:, :][None],
            k_s.reshape(1, dec_b, dec_s, N_HEADS, V_DIM),
            v_s.reshape(1, dec_b, dec_s, N_HEADS, V_DIM),
            st_s[None])
```

```python
import functools
import math

import jax
import jax.numpy as jnp
from jax import lax
from jax.experimental import pallas as pl
from jax.experimental.pallas import tpu as pltpu

F32 = jnp.float32
BF16 = jnp.bfloat16
I32 = jnp.int32
U32 = jnp.uint32

D_MODEL = 1024
N_HEADS = 4
HEAD_DIM = 64
V_DIM = 128
ATTN_WIDTH = N_HEADS * V_DIM
QK_WIDTH = N_HEADS * 2 * HEAD_DIM
ATTN_SCALE = HEAD_DIM ** -0.5
CONV_CH = D_MODEL - ATTN_WIDTH
CONV_WIDTH = 31
N_EXPERTS = 64
N_GROUPS = 8
GROUP_SIZE = N_EXPERTS // N_GROUPS
TOPK_GROUPS = 4
TOP_K = 8
D_EXPERT = D_MODEL // 4
ROUTED_SCALE = 2.5
DEPTH = 1
DEEPNORM_ALPHA = (2 * DEPTH) ** 0.25
NORM_EPS = 1e-5
NEG_INF = -1e30
PAGE_SIZE = 128

LANES = 128
ROW_TILE = D_MODEL // LANES
VMEM_LIMIT = 48 * 1024 * 1024
PAGES_PER_STEP = 8
PAGE_ROWS = PAGE_SIZE * N_HEADS
EXPERT_BLOCK = 256
HIST = 32


def _lambda_init(layer):
    return 0.8 - 0.6 * math.exp(-0.3 * layer)


def _sigmoid(x):
    return 1.0 / (1.0 + jnp.exp(-x))


def _silu(x):
    return x * _sigmoid(x)


def _layer_norm(x, g, b):
    mu = jnp.mean(x, axis=-1, keepdims=True)
    xc = x - mu
    var = jnp.mean(xc * xc, axis=-1, keepdims=True)
    return xc * lax.rsqrt(var + NORM_EPS) * g + b


def _dot(a, b):
    return jnp.dot(a, b, preferred_element_type=F32)


def _dot_nt(a, b):
    return lax.dot_general(a, b, (((1,), (1,)), ((), ())), preferred_element_type=F32)


def _diff_lambda(lq1, lk1, lq2, lk2, lam_init):
    a = jnp.exp(jnp.sum(lq1 * lk1, axis=-1, keepdims=True))
    b = jnp.exp(jnp.sum(lq2 * lk2, axis=-1, keepdims=True))
    return a - b + lam_init


def _params(dims):
    return pltpu.CompilerParams(dimension_semantics=dims, vmem_limit_bytes=VMEM_LIMIT)


def _store_row_tiles(ref, x):
    m = x.shape[0]
    for j in range(ROW_TILE):
        ref[pl.ds(j, m, stride=ROW_TILE), :] = x[:, j * LANES:(j + 1) * LANES]


def _load_row_tiles(ref, m):
    return jnp.concatenate([ref[pl.ds(j, m, stride=ROW_TILE), :] for j in range(ROW_TILE)],
                           axis=1)


def _token_tile(ref, t):
    return ref.at[pl.ds(pl.multiple_of(t * ROW_TILE, ROW_TILE), ROW_TILE)]


def _store_head_rows(ref, x):
    m = x.shape[0]
    for h in range(N_HEADS):
        ref[pl.ds(h, m, stride=N_HEADS), :] = x[:, h * V_DIM:(h + 1) * V_DIM]


def _inproj_body(x_ref, w_ref, *refs):
    xb = x_ref[...].astype(BF16)

    def mm(c0):
        return _dot(xb, w_ref[:, c0:c0 + QK_WIDTH])

    k = mm(QK_WIDTH)
    if len(refs) == 4:
        q_ref, k_ref, v_ref, u_ref = refs
        q_ref[...] = mm(0) * ATTN_SCALE
    else:
        wqt_ref, wvt_ref, q_ref, k_ref, v_ref, u_ref, kb_ref, vt_ref = refs
        q_ref[...] = _dot_nt(wqt_ref[...], xb) * ATTN_SCALE
        vt_ref[...] = _dot_nt(wvt_ref[...], xb).astype(BF16)
        kb_ref[...] = k.astype(BF16)
    _store_head_rows(k_ref, k)
    _store_head_rows(v_ref, mm(2 * QK_WIDTH))
    a = mm(2 * QK_WIDTH + ATTN_WIDTH)
    b = mm(2 * QK_WIDTH + ATTN_WIDTH + CONV_CH)
    u_ref[...] = a * _sigmoid(b)


def _inproj(x, w_in_b, tm, transposed_w=()):
    n = x.shape[0]
    wide = QK_WIDTH
    blk = pl.BlockSpec((tm, wide), lambda i: (i, 0))
    hblk = pl.BlockSpec((tm * N_HEADS, V_DIM), lambda i: (i, 0))
    full = lambda a: pl.BlockSpec(a.shape, lambda i: (0, 0))
    row_major = jax.ShapeDtypeStruct((n, wide), F32)
    head_rows = jax.ShapeDtypeStruct((n * N_HEADS, V_DIM), F32)
    out_specs = [blk, hblk, hblk, blk]
    out_shape = [row_major, head_rows, head_rows, row_major]
    if transposed_w:
        t_blk = pl.BlockSpec((None, wide, tm), lambda i: (i, 0, 0))
        out_specs = [t_blk, hblk, hblk, blk, blk, t_blk]
        out_shape = [jax.ShapeDtypeStruct((n // tm, wide, tm), F32), head_rows, head_rows,
                     row_major, jax.ShapeDtypeStruct((n, wide), BF16),
                     jax.ShapeDtypeStruct((n // tm, wide, tm), BF16)]
    return pl.pallas_call(
        _inproj_body,
        name="inproj",
        grid=(n // tm,),
        in_specs=[pl.BlockSpec((tm, D_MODEL), lambda i: (i, 0)), full(w_in_b)]
        + [full(w) for w in transposed_w],
        out_specs=out_specs,
        out_shape=out_shape,
        compiler_params=_params(("parallel",)),
    )(x, w_in_b, *transposed_w)


def _attn_p_body(lam_init, q_ref, k_ref, v_ref, lq1, lk1, lq2, lk2, g_ref, o_ref, acc_ref):
    qi = pl.program_id(2)
    tq = q_ref.shape[1]
    tk = v_ref.shape[2]
    q_t = q_ref[...]
    feat = lax.broadcasted_iota(I32, q_t.shape, 0)
    q_maps = (jnp.where(feat < HEAD_DIM, q_t, 0.0).astype(BF16),
              jnp.where(feat >= HEAD_DIM, q_t, 0.0).astype(BF16))
    acc_ref[...] = jnp.zeros(acc_ref.shape, F32)

    def chunk(masked, j, carry):
        kc = k_ref[pl.ds(pl.multiple_of(j * tk, tk), tk), :]
        vc = v_ref[j]
        if masked:
            key = lax.broadcasted_iota(I32, (tk, tq), 0)
            qry = lax.broadcasted_iota(I32, (tk, tq), 1)
            keep = key <= qry
        out = []
        for a in range(2):
            m_old, l_old = carry[2 * a], carry[2 * a + 1]
            s_t = _dot(kc, q_maps[a])
            if masked:
                s_t = jnp.where(keep, s_t, NEG_INF)
            m_new = jnp.maximum(m_old, jnp.max(s_t, axis=0, keepdims=True))
            alpha = jnp.exp(m_old - m_new)
            p_t = jnp.exp(s_t - m_new)
            out.append(m_new)
            out.append(alpha * l_old + jnp.sum(p_t, axis=0, keepdims=True))
            acc_ref[a] = alpha * acc_ref[a] + _dot(vc, p_t.astype(BF16))
        return tuple(out)

    neg = jnp.full((1, tq), -jnp.inf, F32)
    zero = jnp.zeros((1, tq), F32)
    carry = lax.fori_loop(0, qi, functools.partial(chunk, False), (neg, zero, neg, zero))
    _, l0, _, l1 = chunk(True, qi, carry)

    lam = _diff_lambda(lq1[...], lk1[...], lq2[...], lk2[...], lam_init)
    o_t = acc_ref[0] / l0 - lam * (acc_ref[1] / l1)
    ms = jnp.mean(o_t * o_t, axis=0, keepdims=True)
    o_t = o_t * lax.rsqrt(ms + NORM_EPS) * g_ref[...] * (1.0 - lam_init)
    o_ref[...] = o_t.T


def _attn_prompt(q_t, kb, v_t, lam_vecs, subln_g_col, lam_init, batch, seq):
    t = q_t.shape[2]
    n = kb.shape[0]
    nq = seq // t
    small = lambda w: pl.BlockSpec((1, w), lambda b, h, i: (0, 0))
    return pl.pallas_call(
        functools.partial(_attn_p_body, lam_init),
        name="attn_prompt",
        grid=(batch, N_HEADS, nq),
        in_specs=[pl.BlockSpec((None, V_DIM, t), lambda b, h, i: (b * nq + i, h, 0)),
                  pl.BlockSpec((seq, V_DIM), lambda b, h, i: (b, h)),
                  pl.BlockSpec((nq, V_DIM, t), lambda b, h, i: (b, h, 0))]
        + [small(HEAD_DIM)] * 4 + [pl.BlockSpec((V_DIM, 1), lambda b, h, i: (0, 0))],
        out_specs=pl.BlockSpec((t, V_DIM), lambda b, h, i: (b * nq + i, h)),
        out_shape=jax.ShapeDtypeStruct((n, ATTN_WIDTH), F32),
        scratch_shapes=[pltpu.VMEM((2, V_DIM, t), F32)],
        compiler_params=_params(("parallel", "parallel", "parallel")),
    )(q_t, kb, v_t, *lam_vecs, subln_g_col)


def _attn_s_body(lam_init, n_steps, pt_ref, q_ref, kn_ref, vn_ref, *rest):
    npg = PAGES_PER_STEP
    k_refs = rest[:npg]
    v_refs = rest[npg:2 * npg]
    (lq1, lk1, lq2, lk2, g_ref, o_ref, qall, knew, vnew, m_ref, l_ref,
     acc_ref) = rest[2 * npg:]
    j = pl.program_id(1)
    ds = q_ref.shape[0]
    hr = 2 * ds

    def head_rows(ref, h):
        return ref[pl.ds(h, PAGE_SIZE, stride=N_HEADS), :].astype(BF16)

    @pl.when(j == 0)
    def _():
        pieces = []
        for h in range(N_HEADS):
            qh = q_ref[:, h * V_DIM:(h + 1) * V_DIM]
            lane = lax.broadcasted_iota(I32, qh.shape, 1)
            pieces.append(jnp.where(lane < HEAD_DIM, qh, 0.0))
            pieces.append(jnp.where(lane >= HEAD_DIM, qh, 0.0))
        qall[...] = jnp.concatenate(pieces, axis=0).astype(BF16)
        knew[...] = jnp.zeros(knew.shape, F32)
        vnew[...] = jnp.zeros(vnew.shape, F32)
        knew[0:ds * N_HEADS] = kn_ref[...]
        vnew[0:ds * N_HEADS] = vn_ref[...]
        row = lax.broadcasted_iota(I32, (hr, PAGE_SIZE), 0)
        key = lax.broadcasted_iota(I32, (hr, PAGE_SIZE), 1)
        keep = key <= (row % ds)
        for h in range(N_HEADS):
            rows = slice(h * hr, (h + 1) * hr)
            s = jnp.where(keep, _dot_nt(qall[rows], head_rows(knew, h)), NEG_INF)
            m = jnp.max(s, axis=-1, keepdims=True)
            p = jnp.exp(s - m)
            m_ref[rows] = m
            l_ref[rows] = jnp.sum(p, axis=-1, keepdims=True)
            acc_ref[rows] = _dot(p.astype(BF16), head_rows(vnew, h))

    q_all, m_all, l_all, acc_all = qall[...], m_ref[...], l_ref[...], acc_ref[...]
    heads = range(N_HEADS)
    rows = [slice(h * hr, (h + 1) * hr) for h in heads]
    s = [_dot_nt(q_all[rows[h]],
                 jnp.concatenate([head_rows(k_refs[i], h) for i in range(npg)], axis=0))
         for h in heads]
    s = jnp.concatenate(s, axis=0)
    m_new = jnp.maximum(m_all, jnp.max(s, axis=-1, keepdims=True))
    alpha = jnp.exp(m_all - m_new)
    p = jnp.exp(s - m_new)
    l_new = alpha * l_all + jnp.sum(p, axis=-1, keepdims=True)
    pb = p.astype(BF16)
    pv = [_dot(pb[rows[h]],
               jnp.concatenate([head_rows(v_refs[i], h) for i in range(npg)], axis=0))
          for h in heads]
    acc_new = alpha * acc_all + jnp.concatenate(pv, axis=0)
    m_ref[...] = m_new
    l_ref[...] = l_new
    acc_ref[...] = acc_new

    @pl.when(j == n_steps - 1)
    def _():
        lam = _diff_lambda(lq1[...], lk1[...], lq2[...], lk2[...], lam_init)
        on = acc_new / l_new
        for h in range(N_HEADS):
            r0 = h * hr
            o = on[r0:r0 + ds] - lam * on[r0 + ds:r0 + 2 * ds]
            ms = jnp.mean(o * o, axis=-1, keepdims=True)
            o_ref[:, h * V_DIM:(h + 1) * V_DIM] = (
                o * lax.rsqrt(ms + NORM_EPS) * g_ref[...] * (1.0 - lam_init))


def _attn_sample(q, k_new, v_new, cache_k, cache_v, page_table, lam_vecs, subln_g, lam_init):
    b, ds, _ = q.shape
    n_pages = page_table.shape[1]
    n_steps = n_pages // PAGES_PER_STEP
    nrow = N_HEADS * 2 * ds
    pt = page_table.reshape(-1)

    def page_map(i, bi, j, pt_ref):
        return (pt_ref[bi * n_pages + j * PAGES_PER_STEP + i], 0, 0)

    page_specs = [pl.BlockSpec((None, PAGE_ROWS, V_DIM), functools.partial(page_map, i))
                  for i in range(PAGES_PER_STEP)]
    small = lambda w: pl.BlockSpec((1, w), lambda bi, j, pt_ref: (0, 0))
    new_spec = pl.BlockSpec((None, ds * N_HEADS, V_DIM), lambda bi, j, pt_ref: (bi, 0, 0))
    qo_spec = pl.BlockSpec((None, ds, ATTN_WIDTH), lambda bi, j, pt_ref: (bi, 0, 0))
    grid_spec = pltpu.PrefetchScalarGridSpec(
        num_scalar_prefetch=1,
        grid=(b, n_steps),
        in_specs=[qo_spec, new_spec, new_spec] + page_specs + page_specs
        + [small(HEAD_DIM)] * 4 + [small(V_DIM)],
        out_specs=qo_spec,
        scratch_shapes=[pltpu.VMEM((nrow, V_DIM), BF16),
                        pltpu.VMEM((PAGE_ROWS, V_DIM), F32),
                        pltpu.VMEM((PAGE_ROWS, V_DIM), F32),
                        pltpu.VMEM((nrow, 1), F32),
                        pltpu.VMEM((nrow, 1), F32),
                        pltpu.VMEM((nrow, V_DIM), F32)],
    )
    return pl.pallas_call(
        functools.partial(_attn_s_body, lam_init, n_steps),
        name="attn_sample",
        grid_spec=grid_spec,
        out_shape=jax.ShapeDtypeStruct((b, ds, ATTN_WIDTH), F32),
        compiler_params=_params(("parallel", "arbitrary")),
    )(pt, q, k_new, v_new, *([cache_k] * PAGES_PER_STEP), *([cache_v] * PAGES_PER_STEP),
      *lam_vecs, subln_g)


def _conv_taps(ext_ref, w_ref, bias, rows):
    off = HIST - (CONV_WIDTH - 1)
    acc = jnp.broadcast_to(bias, (rows, CONV_CH))
    for jt in range(CONV_WIDTH):
        acc = acc + w_ref[jt:jt + 1, :] * ext_ref[pl.ds(jt + off, rows), :]
    return acc


def _conv_p_body(u_ref, h_ref, w_ref, b_ref, g_ref, beta_ref, c_ref, ext_ref):
    t = pl.program_id(1)
    tm = u_ref.shape[0]
    hist = h_ref[...]
    ext_ref[0:HIST] = jnp.where(t == 0, jnp.zeros_like(hist), hist)
    ext_ref[HIST:HIST + tm] = u_ref[...]
    y = _conv_taps(ext_ref, w_ref, b_ref[...], tm)
    c_ref[...] = _silu(_layer_norm(y, g_ref[...], beta_ref[...]))


def _conv_prompt(u, w_dw, b_dw, g, beta, batch, seq, tm):
    n = u.shape[0]
    nt = seq // tm
    per = tm // HIST
    cur = lambda b, t: (b * nt + t, 0)
    prev = lambda b, t: (jnp.maximum((b * nt + t) * per - 1, 0), 0)
    small = lambda r: pl.BlockSpec((r, CONV_CH), lambda b, t: (0, 0))
    return pl.pallas_call(
        _conv_p_body,
        name="conv_prompt",
        grid=(batch, nt),
        in_specs=[pl.BlockSpec((tm, CONV_CH), cur), pl.BlockSpec((HIST, CONV_CH), prev),
                  small(CONV_WIDTH), small(1), small(1), small(1)],
        out_specs=pl.BlockSpec((tm, CONV_CH), cur),
        out_shape=jax.ShapeDtypeStruct((n, CONV_CH), F32),
        scratch_shapes=[pltpu.VMEM((HIST + tm, CONV_CH), F32)],
        compiler_params=_params(("parallel", "parallel")),
    )(u, u, w_dw, b_dw, g, beta)


def _conv_s_body(u_ref, st_ref, w_ref, b_ref, g_ref, beta_ref, c_ref, ns_ref, ext_ref):
    ds = u_ref.shape[0]
    keep = CONV_WIDTH - 1
    off = HIST - keep
    ext_ref[0:off] = jnp.zeros((off, CONV_CH), F32)
    ext_ref[off:HIST] = st_ref[...]
    ext_ref[HIST:HIST + ds] = u_ref[...]
    y = _conv_taps(ext_ref, w_ref, b_ref[...], ds)
    c_ref[...] = _silu(_layer_norm(y, g_ref[...], beta_ref[...]))
    ns_ref[...] = ext_ref[HIST + ds - keep:HIST + ds]


def _conv_sample(u, state, w_dw, b_dw, g, beta):
    b, ds, _ = u.shape
    keep = CONV_WIDTH - 1
    per_b = lambda r: pl.BlockSpec((None, r, CONV_CH), lambda i: (i, 0, 0))
    small = lambda r: pl.BlockSpec((r, CONV_CH), lambda i: (0, 0))
    return pl.pallas_call(
        _conv_s_body,
        name="conv_sample",
        grid=(b,),
        in_specs=[per_b(ds), per_b(keep), small(CONV_WIDTH), small(1), small(1), small(1)],
        out_specs=[per_b(ds), per_b(keep)],
        out_shape=[jax.ShapeDtypeStruct((b, ds, CONV_CH), F32),
                   jax.ShapeDtypeStruct((b, keep, CONV_CH), F32)],
        scratch_shapes=[pltpu.VMEM((HIST + ds, CONV_CH), F32)],
        compiler_params=_params(("parallel",)),
    )(u, state, w_dw, b_dw, g, beta)


def _first_index(hit, iota, limit, axis):
    return jnp.min(jnp.where(hit, iota, limit), axis=axis, keepdims=True)


def _route(x1, wr_hi, wr_lo, bias):
    tm = x1.shape[0]
    x_hi = x1.astype(BF16)
    x_lo = (x1 - x_hi.astype(F32)).astype(BF16)
    logits = _dot_nt(wr_hi, x_hi) + (_dot_nt(wr_lo, x_hi) + _dot_nt(wr_hi, x_lo))
    scores = _sigmoid(logits)
    sel = scores + bias
    sel3 = sel.reshape(N_GROUPS, GROUP_SIZE, tm)
    member = lax.broadcasted_iota(I32, sel3.shape, 1)
    m1 = jnp.max(sel3, axis=1, keepdims=True)
    i1 = _first_index(sel3 == m1, member, GROUP_SIZE, 1)
    m2 = jnp.max(jnp.where(member == i1, -jnp.inf, sel3), axis=1, keepdims=True)
    gs = jnp.broadcast_to(m1 + m2, sel3.shape).reshape(N_EXPERTS, tm)
    eiota = lax.broadcasted_iota(I32, (N_EXPERTS, tm), 0)
    giota = eiota // GROUP_SIZE
    gmask = jnp.zeros((N_EXPERTS, tm), jnp.bool_)
    for _ in range(TOPK_GROUPS):
        m = jnp.max(gs, axis=0, keepdims=True)
        gi = _first_index(gs == m, giota, N_GROUPS, 0)
        pick = giota == gi
        gmask = jnp.logical_or(gmask, pick)
        gs = jnp.where(pick, -jnp.inf, gs)
    selm = jnp.where(gmask, sel, NEG_INF)
    idx_rows, w_rows = [], []
    for _ in range(TOP_K):
        m = jnp.max(selm, axis=0, keepdims=True)
        ei = _first_index(selm == m, eiota, N_EXPERTS, 0)
        pick = eiota == ei
        idx_rows.append(ei)
        w_rows.append(jnp.sum(jnp.where(pick, scores, 0.0), axis=0, keepdims=True))
        selm = jnp.where(pick, -jnp.inf, selm)
    idx = jnp.concatenate(idx_rows, axis=0)
    w = jnp.concatenate(w_rows, axis=0)
    w = w / jnp.sum(w, axis=0, keepdims=True) * ROUTED_SCALE
    return idx, w


def _mix_body(o_ref, c_ref, x_ref, wo_a, wo_c, g_ref, b_ref, wrh_ref, wrl_ref, br_ref,
              x1_ref, xrt_ref, idx_ref, wts_ref, rank_ref, cnt_ref, base_ref):
    i = pl.program_id(0)
    tm = x_ref.shape[0]

    @pl.when(i == 0)
    def _():
        base_ref[...] = jnp.zeros(base_ref.shape, F32)

    mix = _dot(o_ref[...].astype(BF16), wo_a[...]) + _dot(c_ref[...].astype(BF16), wo_c[...])
    x1 = _layer_norm(DEEPNORM_ALPHA * x_ref[...] + mix, g_ref[...], b_ref[...])
    x1_ref[...] = x1
    _store_row_tiles(xrt_ref, x1)

    idx, w = _route(x1, wrh_ref[...], wrl_ref[...], br_ref[...])
    idx_ref[...] = idx
    w_rows = jnp.concatenate([w, jnp.zeros((LANES - TOP_K, tm), F32)], axis=0)
    wts_ref[...] = w_rows.T

    eiota = lax.broadcasted_iota(I32, (N_EXPERTS, tm), 0)
    chosen = jnp.zeros((N_EXPERTS, tm), F32)
    for k in range(TOP_K):
        chosen = chosen + jnp.where(eiota == idx[k:k + 1, :], 1.0, 0.0)
    r = lax.broadcasted_iota(I32, (tm, tm), 0)
    c = lax.broadcasted_iota(I32, (tm, tm), 1)
    before = jnp.where(r < c, 1.0, 0.0).astype(BF16)
    base = base_ref[:, 0:1]
    rank_full = _dot(chosen.astype(BF16), before) + base
    rows = [jnp.sum(jnp.where(eiota == idx[k:k + 1, :], rank_full, 0.0), axis=0, keepdims=True)
            for k in range(TOP_K)]
    rank_ref[...] = jnp.concatenate(rows, axis=0).astype(I32)
    total = base + jnp.sum(chosen, axis=1, keepdims=True)
    base_ref[...] = jnp.broadcast_to(total, base_ref.shape)
    cnt_ref[...] = jnp.broadcast_to(total, cnt_ref.shape)


def _mix(o, c, x, wo_a, wo_c, ln_g, ln_b, wr_hi, wr_lo, b_router, tm):
    n = x.shape[0]
    rows = lambda w: pl.BlockSpec((tm, w), lambda i: (i, 0))
    full = lambda a: pl.BlockSpec(a.shape, lambda i: (0, 0))
    cols = pl.BlockSpec((TOP_K, tm), lambda i: (0, i))
    cnt = pl.BlockSpec((N_EXPERTS, LANES), lambda i: (0, 0))
    return pl.pallas_call(
        _mix_body,
        name="mix",
        grid=(n // tm,),
        in_specs=[rows(ATTN_WIDTH), rows(CONV_CH), rows(D_MODEL), full(wo_a), full(wo_c),
                  full(ln_g), full(ln_b), full(wr_hi), full(wr_lo), full(b_router)],
        out_specs=[rows(D_MODEL), pl.BlockSpec((tm * ROW_TILE, LANES), lambda i: (i, 0)), cols,
                   rows(LANES), cols, cnt],
        out_shape=[jax.ShapeDtypeStruct((n, D_MODEL), F32),
                   jax.ShapeDtypeStruct((n * ROW_TILE, LANES), F32),
                   jax.ShapeDtypeStruct((TOP_K, n), I32),
                   jax.ShapeDtypeStruct((n, LANES), F32),
                   jax.ShapeDtypeStruct((TOP_K, n), I32),
                   jax.ShapeDtypeStruct((N_EXPERTS, LANES), F32)],
        scratch_shapes=[pltpu.VMEM((N_EXPERTS, LANES), F32)],
        compiler_params=_params(("arbitrary",)),
    )(o, c, x, wo_a, wo_c, ln_g, ln_b, wr_hi, wr_lo, b_router)


def _dest_body(idx_ref, rank_ref, ps_ref, dest_ref):
    idx = idx_ref[...]
    tm = idx.shape[1]
    eiota = lax.broadcasted_iota(I32, (N_EXPERTS, tm), 0)
    ps = ps_ref[...]
    rows = [jnp.sum(jnp.where(eiota == idx[k:k + 1, :], ps, 0.0), axis=0, keepdims=True)
            for k in range(TOP_K)]
    dest_ref[...] = jnp.concatenate(rows, axis=0).astype(I32) + rank_ref[...]


def _dest(idx, rank, pstart_col, tm):
    n = idx.shape[1]
    cols = pl.BlockSpec((TOP_K, tm), lambda i: (0, i))
    return pl.pallas_call(
        _dest_body,
        name="dest",
        grid=(n // tm,),
        in_specs=[cols, cols, pl.BlockSpec((N_EXPERTS, 1), lambda i: (0, 0))],
        out_specs=cols,
        out_shape=jax.ShapeDtypeStruct((TOP_K, n), I32),
        compiler_params=_params(("parallel",)),
    )(idx, rank, pstart_col)


PAD_PIECES = tuple(1 << s for s in reversed(range(EXPERT_BLOCK.bit_length() - 1)))
ROW_UNROLL = 4


def _dispatch_body(ps_ref, cnt_ref, dest_ref, x_ref, xs_ref, zbuf, sem, zsem):
    i = pl.program_id(0)
    ts = x_ref.shape[0] // ROW_TILE

    @pl.when(i == 0)
    def _():
        zbuf[...] = jnp.zeros(zbuf.shape, F32)

        def pad_rows(wait, e, carry):
            cnt = cnt_ref[e]
            base = ps_ref[e] + cnt
            pad = (-cnt) & (EXPERT_BLOCK - 1)
            for p in PAD_PIECES:
                @pl.when((pad & p) != 0)
                def _():
                    first = pl.multiple_of(base * ROW_TILE, ROW_TILE)
                    cp = pltpu.make_async_copy(zbuf.at[pl.ds(0, p * ROW_TILE)],
                                               xs_ref.at[pl.ds(first, p * ROW_TILE)], zsem)
                    if wait:
                        cp.wait()
                    else:
                        cp.start()
                base = base + (pad & p)
            return carry

        lax.fori_loop(0, N_EXPERTS, functools.partial(pad_rows, False), 0)
        lax.fori_loop(0, N_EXPERTS, functools.partial(pad_rows, True), 0)

    def row_copy(t, k):
        return pltpu.make_async_copy(_token_tile(x_ref, t), _token_tile(xs_ref, dest_ref[k, t]),
                                     sem)

    def issue(g, carry):
        for u in range(ROW_UNROLL):
            for k in range(TOP_K):
                row_copy(g * ROW_UNROLL + u, k).start(priority=k % 2)
        return carry

    lax.fori_loop(0, ts // ROW_UNROLL, issue, 0)
    for k in range(TOP_K):
        pltpu.make_async_copy(x_ref, xs_ref.at[pl.ds(0, ts * ROW_TILE)], sem).wait()


def _dispatch(x_tiles, dest, pstart, counts, n_rows, ts):
    n = x_tiles.shape[0] // ROW_TILE
    grid_spec = pltpu.PrefetchScalarGridSpec(
        num_scalar_prefetch=2,
        grid=(n // ts,),
        in_specs=[pl.BlockSpec((TOP_K, ts), lambda i, ps, cn: (0, i), memory_space=pltpu.SMEM),
                  pl.BlockSpec((ts * ROW_TILE, LANES), lambda i, ps, cn: (i, 0))],
        out_specs=pl.BlockSpec(memory_space=pl.ANY),
        scratch_shapes=[pltpu.VMEM((PAD_PIECES[0] * ROW_TILE, LANES), F32),
                        pltpu.SemaphoreType.DMA, pltpu.SemaphoreType.DMA],
    )
    return pl.pallas_call(
        _dispatch_body,
        name="dispatch",
        grid_spec=grid_spec,
        out_shape=jax.ShapeDtypeStruct((n_rows * ROW_TILE, LANES), F32),
        compiler_params=_params(("arbitrary",)),
    )(pstart, counts, dest, x_tiles)


def _expert_body(be_ref, nb_ref, xs_ref, wg_ref, wu_ref, wd_ref, ys_ref):
    del be_ref
    i = pl.program_id(0)

    @pl.when(i < nb_ref[0])
    def _():
        x = _load_row_tiles(xs_ref, EXPERT_BLOCK).astype(BF16)
        h = (_silu(_dot(x, wg_ref[...])) * _dot(x, wu_ref[...])).astype(BF16)
        _store_row_tiles(ys_ref, _dot(h, wd_ref[...]))

    @pl.when(i >= nb_ref[0])
    def _():
        ys_ref[...] = jnp.zeros(ys_ref.shape, F32)


def _experts(xs, block_e, nb_used, wg, wu, wd):
    n_blocks = xs.shape[0] // (EXPERT_BLOCK * ROW_TILE)
    blk = (EXPERT_BLOCK * ROW_TILE, LANES)
    in_rows = pl.BlockSpec(blk, lambda i, be, nb: (jnp.minimum(i, nb[0] - 1), 0))
    grid_spec = pltpu.PrefetchScalarGridSpec(
        num_scalar_prefetch=2,
        grid=(n_blocks,),
        in_specs=[in_rows,
                  pl.BlockSpec((None, D_MODEL, D_EXPERT), lambda i, be, nb: (be[i], 0, 0)),
                  pl.BlockSpec((None, D_MODEL, D_EXPERT), lambda i, be, nb: (be[i], 0, 0)),
                  pl.BlockSpec((None, D_EXPERT, D_MODEL), lambda i, be, nb: (be[i], 0, 0))],
        out_specs=pl.BlockSpec(blk, lambda i, be, nb: (i, 0)),
    )
    return pl.pallas_call(
        _expert_body,
        name="experts",
        grid_spec=grid_spec,
        out_shape=jax.ShapeDtypeStruct(xs.shape, F32),
        compiler_params=_params(("arbitrary",)),
    )(block_e, nb_used, xs, wg, wu, wd)


def _combine_body(dest_ref, dnext_ref, wts_ref, x1_ref, ys_ref, wsg, wsu, wsd, g_ref, b_ref,
                  y_ref, buf, sems):
    i = pl.program_id(0)
    n_tiles = pl.num_programs(0)
    tc = x1_ref.shape[0]
    slot = i % 2

    def issue(d_ref, s, g, carry):
        for u in range(ROW_UNROLL):
            t = g * ROW_UNROLL + u
            for k in range(TOP_K):
                pltpu.make_async_copy(_token_tile(ys_ref, d_ref[k, t]),
                                      _token_tile(buf.at[s, k], t),
                                      sems.at[s]).start(priority=k % 2)
        return carry

    n_groups = tc // ROW_UNROLL

    @pl.when(i == 0)
    def _():
        lax.fori_loop(0, n_groups, functools.partial(issue, dest_ref, slot), 0)

    @pl.when(i + 1 < n_tiles)
    def _():
        lax.fori_loop(0, n_groups, functools.partial(issue, dnext_ref, 1 - slot), 0)

    x1 = x1_ref[...]
    xb = x1.astype(BF16)
    hs = (_silu(_dot(xb, wsg[...])) * _dot(xb, wsu[...])).astype(BF16)
    shared = _dot(hs, wsd[...])

    for k in range(TOP_K):
        pltpu.make_async_copy(ys_ref.at[pl.ds(0, tc * ROW_TILE)], buf.at[slot, k],
                              sems.at[slot]).wait()

    w = wts_ref[...]
    routed = w[:, 0:1] * _load_row_tiles(buf.at[slot, 0], tc)
    for k in range(1, TOP_K):
        routed = routed + w[:, k:k + 1] * _load_row_tiles(buf.at[slot, k], tc)
    y_ref[...] = _layer_norm(DEEPNORM_ALPHA * x1 + (routed + shared), g_ref[...], b_ref[...])


def _combine(dest, wts_t, x1, ys, wsg, wsu, wsd, ln_g, ln_b, tc):
    n = x1.shape[0]
    last = n // tc - 1
    dest_cur = pl.BlockSpec((TOP_K, tc), lambda i: (0, i), memory_space=pltpu.SMEM)
    dest_next = pl.BlockSpec((TOP_K, tc), lambda i: (0, jnp.minimum(i + 1, last)),
                             memory_space=pltpu.SMEM)
    full = lambda a: pl.BlockSpec(a.shape, lambda i: (0, 0))
    rows = lambda w: pl.BlockSpec((tc, w), lambda i: (i, 0))
    return pl.pallas_call(
        _combine_body,
        name="combine",
        grid=(n // tc,),
        in_specs=[dest_cur, dest_next, rows(LANES), rows(D_MODEL),
                  pl.BlockSpec(memory_space=pl.ANY),
                  full(wsg), full(wsu), full(wsd), full(ln_g), full(ln_b)],
        out_specs=rows(D_MODEL),
        out_shape=jax.ShapeDtypeStruct((n, D_MODEL), F32),
        scratch_shapes=[pltpu.VMEM((2, TOP_K, tc * ROW_TILE, LANES), F32),
                        pltpu.SemaphoreType.DMA((2,))],
        compiler_params=_params(("arbitrary",)),
    )(dest, dest, wts_t, x1, ys, wsg, wsu, wsd, ln_g, ln_b)


def _moe(x1, x_tiles, idx, wts_t, rank, cnt, moe_w, ln_g, ln_b, tile):
    n = x1.shape[0]
    wg, wu, wd, wsg, wsu, wsd = moe_w
    blk = EXPERT_BLOCK
    n_blocks = -(-(n * TOP_K) // blk) + N_EXPERTS
    counts = cnt[:, 0].astype(I32)
    padded = (counts + blk - 1) // blk * blk
    ends = jnp.cumsum(padded)
    pstart = ends - padded
    nb_used = (ends[-1] // blk).astype(I32).reshape(1)
    block_row0 = jnp.arange(n_blocks, dtype=I32) * blk
    block_e = jnp.minimum(jnp.sum((ends[None, :] <= block_row0[:, None]).astype(I32), axis=1),
                          N_EXPERTS - 1).astype(I32)
    dest = _dest(idx, rank, pstart.astype(F32).reshape(N_EXPERTS, 1), tile)
    xs = _dispatch(x_tiles, dest, pstart, counts, n_blocks * blk, tile)
    ys = _experts(xs, block_e, nb_used, wg, wu, wd)
    return _combine(dest, wts_t, x1, ys, wsg, wsu, wsd, ln_g, ln_b, min(tile, 128))


def kernel(x_prompt, x_sample, cache_k, cache_v, state_conv, page_table, w_in, lam_q1, lam_k1,
           lam_q2, lam_k2, subln_g, w_dw, b_dw, conv_ln_g, conv_ln_b, w_o, ln1_g, ln1_b,
           w_router, b_router, w_gate, w_up, w_down, w_sh_gate, w_sh_up, w_sh_down, ln2_g,
           ln2_b):
    batch, seq, _ = x_prompt.shape
    dec_b, dec_s, _ = x_sample.shape
    n_p, n_s = batch * seq, dec_b * dec_s
    layer = 0
    lam_init = _lambda_init(layer)
    row = lambda a: a[layer].reshape(1, -1)

    w_in_b = w_in[layer].astype(BF16)
    wo = w_o[layer].astype(BF16)
    wo_a, wo_c = wo[:ATTN_WIDTH], wo[ATTN_WIDTH:]
    wr_t = w_router[layer].T
    wr_hi = wr_t.astype(BF16)
    wr_lo = (wr_t - wr_hi.astype(F32)).astype(BF16)
    br = b_router[layer].reshape(-1, 1)
    moe_w = tuple(w[layer].astype(BF16)
                  for w in (w_gate, w_up, w_down, w_sh_gate, w_sh_up, w_sh_down))
    lam_vecs = (row(lam_q1), row(lam_k1), row(lam_q2), row(lam_k2))
    g_sub = row(subln_g)
    conv_w = (w_dw[layer], row(b_dw), row(conv_ln_g), row(conv_ln_b))
    ln1 = (row(ln1_g), row(ln1_b))
    ln2 = (row(ln2_g), row(ln2_b))

    xp = x_prompt.reshape(n_p, D_MODEL)
    wq_t = w_in_b[:, :QK_WIDTH].T
    wv_t = w_in_b[:, 2 * QK_WIDTH:2 * QK_WIDTH + ATTN_WIDTH].T
    qt_p, k_p, v_p, u_p, kb_p, vt_p = _inproj(xp, w_in_b, 512, (wq_t, wv_t))
    o_p = _attn_prompt(qt_p, kb_p, vt_p, lam_vecs, g_sub.reshape(V_DIM, 1), lam_init, batch, seq)
    c_p = _conv_prompt(u_p, *conv_w, batch, seq, 512)
    routed_p = _mix(o_p, c_p, xp, wo_a, wo_c, *ln1, wr_hi, wr_lo, br, 512)
    y_p = _moe(*routed_p, moe_w, *ln2, 256)

    xs = x_sample.reshape(n_s, D_MODEL)
    q_s, k_s, v_s, u_s = _inproj(xs, w_in_b, n_s)
    pool = cache_k.shape[1]
    page_rows = lambda a: a[layer].reshape(pool, PAGE_ROWS, V_DIM)
    new_rows = lambda a: a.reshape(dec_b, dec_s * N_HEADS, V_DIM)
    o_s = _attn_sample(q_s.reshape(dec_b, dec_s, QK_WIDTH), new_rows(k_s), new_rows(v_s),
                       page_rows(cache_k), page_rows(cache_v), page_table, lam_vecs, g_sub,
                       lam_init)
    c_s, st_s = _conv_sample(u_s.reshape(dec_b, dec_s, CONV_CH), state_conv[layer], *conv_w)
    routed_s = _mix(o_s.reshape(n_s, ATTN_WIDTH), c_s.reshape(n_s, CONV_CH), xs, wo_a, wo_c,
                    *ln1, wr_hi, wr_lo, br, n_s)
    y_s = _moe(*routed_s, moe_w, *ln2, 128)

    keep = CONV_WIDTH - 1
    u_p3 = u_p.reshape(batch, seq, CONV_CH)
    return (y_p.reshape(batch, seq, D_MODEL),
            y_s.reshape(dec_b, dec_s, D_MODEL),
            k_p.reshape(1, batch, seq, N_HEADS, V_DIM),
            v_p.reshape(1, batch, seq, N_HEADS, V_DIM),
            u_p3[:, seq - keep:, :][None],
            k_s.reshape(1, dec_b, dec_s, N_HEADS, V_DIM),
            v_s.reshape(1, dec_b, dec_s, N_HEADS, V_DIM),
            st_s[None])
```

```python
import functools
import math

import jax
import jax.numpy as jnp
from jax import lax
from jax.experimental import pallas as pl
from jax.experimental.pallas import tpu as pltpu

F32 = jnp.float32
BF16 = jnp.bfloat16
I32 = jnp.int32
U32 = jnp.uint32

D_MODEL = 1024
N_HEADS = 4
HEAD_DIM = 64
V_DIM = 128
ATTN_WIDTH = N_HEADS * V_DIM
QK_WIDTH = N_HEADS * 2 * HEAD_DIM
ATTN_SCALE = HEAD_DIM ** -0.5
CONV_CH = D_MODEL - ATTN_WIDTH
CONV_WIDTH = 31
N_EXPERTS = 64
N_GROUPS = 8
GROUP_SIZE = N_EXPERTS // N_GROUPS
TOPK_GROUPS = 4
TOP_K = 8
D_EXPERT = D_MODEL // 4
ROUTED_SCALE = 2.5
DEPTH = 1
DEEPNORM_ALPHA = (2 * DEPTH) ** 0.25
NORM_EPS = 1e-5
NEG_INF = -1e30
PAGE_SIZE = 128

LANES = 128
ROW_TILE = D_MODEL // LANES
VMEM_LIMIT = 48 * 1024 * 1024
PAGES_PER_STEP = 16
PAGE_ROWS = PAGE_SIZE * N_HEADS
EXPERT_BLOCK = 256
HIST = 32


def _lambda_init(layer):
    return 0.8 - 0.6 * math.exp(-0.3 * layer)


def _sigmoid(x):
    return 1.0 / (1.0 + jnp.exp(-x))


def _silu(x):
    return x * _sigmoid(x)


def _layer_norm(x, g, b):
    mu = jnp.mean(x, axis=-1, keepdims=True)
    xc = x - mu
    var = jnp.mean(xc * xc, axis=-1, keepdims=True)
    return xc * lax.rsqrt(var + NORM_EPS) * g + b


def _dot(a, b):
    return jnp.dot(a, b, preferred_element_type=F32)


def _dot_nt(a, b):
    return lax.dot_general(a, b, (((1,), (1,)), ((), ())), preferred_element_type=F32)


def _diff_lambda(lq1, lk1, lq2, lk2, lam_init):
    a = jnp.exp(jnp.sum(lq1 * lk1, axis=-1, keepdims=True))
    b = jnp.exp(jnp.sum(lq2 * lk2, axis=-1, keepdims=True))
    return a - b + lam_init


def _params(dims):
    return pltpu.CompilerParams(dimension_semantics=dims, vmem_limit_bytes=VMEM_LIMIT)


def _store_row_tiles(ref, x):
    m = x.shape[0]
    for j in range(ROW_TILE):
        ref[pl.ds(j, m, stride=ROW_TILE), :] = x[:, j * LANES:(j + 1) * LANES]


def _load_row_tiles(ref, m):
    return jnp.concatenate([ref[pl.ds(j, m, stride=ROW_TILE), :] for j in range(ROW_TILE)],
                           axis=1)


def _token_tile(ref, t):
    return ref.at[pl.ds(pl.multiple_of(t * ROW_TILE, ROW_TILE), ROW_TILE)]


def _store_head_rows(ref, x):
    m = x.shape[0]
    for h in range(N_HEADS):
        ref[pl.ds(h, m, stride=N_HEADS), :] = x[:, h * V_DIM:(h + 1) * V_DIM]


def _inproj_body(x_ref, w_ref, *refs):
    xb = x_ref[...].astype(BF16)

    def mm(c0):
        return _dot(xb, w_ref[:, c0:c0 + QK_WIDTH])

    k = mm(QK_WIDTH)
    if len(refs) == 4:
        q_ref, k_ref, v_ref, u_ref = refs
        q_ref[...] = mm(0) * ATTN_SCALE
    else:
        wqt_ref, wvt_ref, q_ref, k_ref, v_ref, u_ref, kb_ref, vt_ref = refs
        q_ref[...] = _dot_nt(wqt_ref[...], xb) * ATTN_SCALE
        vt_ref[...] = _dot_nt(wvt_ref[...], xb).astype(BF16)
        kb_ref[...] = k.astype(BF16)
    _store_head_rows(k_ref, k)
    _store_head_rows(v_ref, mm(2 * QK_WIDTH))
    a = mm(2 * QK_WIDTH + ATTN_WIDTH)
    b = mm(2 * QK_WIDTH + ATTN_WIDTH + CONV_CH)
    u_ref[...] = a * _sigmoid(b)


def _inproj(x, w_in_b, tm, transposed_w=()):
    n = x.shape[0]
    wide = QK_WIDTH
    blk = pl.BlockSpec((tm, wide), lambda i: (i, 0))
    hblk = pl.BlockSpec((tm * N_HEADS, V_DIM), lambda i: (i, 0))
    full = lambda a: pl.BlockSpec(a.shape, lambda i: (0, 0))
    row_major = jax.ShapeDtypeStruct((n, wide), F32)
    head_rows = jax.ShapeDtypeStruct((n * N_HEADS, V_DIM), F32)
    out_specs = [blk, hblk, hblk, blk]
    out_shape = [row_major, head_rows, head_rows, row_major]
    if transposed_w:
        t_blk = pl.BlockSpec((None, wide, tm), lambda i: (i, 0, 0))
        out_specs = [t_blk, hblk, hblk, blk, blk, t_blk]
        out_shape = [jax.ShapeDtypeStruct((n // tm, wide, tm), F32), head_rows, head_rows,
                     row_major, jax.ShapeDtypeStruct((n, wide), BF16),
                     jax.ShapeDtypeStruct((n // tm, wide, tm), BF16)]
    return pl.pallas_call(
        _inproj_body,
        name="inproj",
        grid=(n // tm,),
        in_specs=[pl.BlockSpec((tm, D_MODEL), lambda i: (i, 0)), full(w_in_b)]
        + [full(w) for w in transposed_w],
        out_specs=out_specs,
        out_shape=out_shape,
        compiler_params=_params(("parallel",)),
    )(x, w_in_b, *transposed_w)


def _attn_p_body(lam_init, q_ref, k_ref, v_ref, lq1, lk1, lq2, lk2, g_ref, o_ref, acc_ref):
    qi = pl.program_id(2)
    tq = q_ref.shape[1]
    tk = v_ref.shape[2]
    q_t = q_ref[...]
    feat = lax.broadcasted_iota(I32, q_t.shape, 0)
    q_maps = (jnp.where(feat < HEAD_DIM, q_t, 0.0).astype(BF16),
              jnp.where(feat >= HEAD_DIM, q_t, 0.0).astype(BF16))
    acc_ref[...] = jnp.zeros(acc_ref.shape, F32)

    def chunk(masked, j, carry):
        kc = k_ref[pl.ds(pl.multiple_of(j * tk, tk), tk), :]
        vc = v_ref[j]
        if masked:
            key = lax.broadcasted_iota(I32, (tk, tq), 0)
            qry = lax.broadcasted_iota(I32, (tk, tq), 1)
            keep = key <= qry
        out = []
        for a in range(2):
            m_old, l_old = carry[2 * a], carry[2 * a + 1]
            s_t = _dot(kc, q_maps[a])
            if masked:
                s_t = jnp.where(keep, s_t, NEG_INF)
            m_new = jnp.maximum(m_old, jnp.max(s_t, axis=0, keepdims=True))
            alpha = jnp.exp(m_old - m_new)
            p_t = jnp.exp(s_t - m_new)
            out.append(m_new)
            out.append(alpha * l_old + jnp.sum(p_t, axis=0, keepdims=True))
            acc_ref[a] = alpha * acc_ref[a] + _dot(vc, p_t.astype(BF16))
        return tuple(out)

    neg = jnp.full((1, tq), -jnp.inf, F32)
    zero = jnp.zeros((1, tq), F32)
    carry = lax.fori_loop(0, qi, functools.partial(chunk, False), (neg, zero, neg, zero))
    _, l0, _, l1 = chunk(True, qi, carry)

    lam = _diff_lambda(lq1[...], lk1[...], lq2[...], lk2[...], lam_init)
    o_t = acc_ref[0] / l0 - lam * (acc_ref[1] / l1)
    ms = jnp.mean(o_t * o_t, axis=0, keepdims=True)
    o_t = o_t * lax.rsqrt(ms + NORM_EPS) * g_ref[...] * (1.0 - lam_init)
    o_ref[...] = o_t.T


def _attn_prompt(q_t, kb, v_t, lam_vecs, subln_g_col, lam_init, batch, seq):
    t = q_t.shape[2]
    n = kb.shape[0]
    nq = seq // t
    small = lambda w: pl.BlockSpec((1, w), lambda b, h, i: (0, 0))
    return pl.pallas_call(
        functools.partial(_attn_p_body, lam_init),
        name="attn_prompt",
        grid=(batch, N_HEADS, nq),
        in_specs=[pl.BlockSpec((None, V_DIM, t), lambda b, h, i: (b * nq + i, h, 0)),
                  pl.BlockSpec((seq, V_DIM), lambda b, h, i: (b, h)),
                  pl.BlockSpec((nq, V_DIM, t), lambda b, h, i: (b, h, 0))]
        + [small(HEAD_DIM)] * 4 + [pl.BlockSpec((V_DIM, 1), lambda b, h, i: (0, 0))],
        out_specs=pl.BlockSpec((t, V_DIM), lambda b, h, i: (b * nq + i, h)),
        out_shape=jax.ShapeDtypeStruct((n, ATTN_WIDTH), F32),
        scratch_shapes=[pltpu.VMEM((2, V_DIM, t), F32)],
        compiler_params=_params(("parallel", "parallel", "parallel")),
    )(q_t, kb, v_t, *lam_vecs, subln_g_col)


def _attn_s_body(lam_init, n_steps, pt_ref, q_ref, kn_ref, vn_ref, *rest):
    npg = PAGES_PER_STEP
    k_refs = rest[:npg]
    v_refs = rest[npg:2 * npg]
    (lq1, lk1, lq2, lk2, g_ref, o_ref, qall, knew, vnew, m_ref, l_ref,
     acc_ref) = rest[2 * npg:]
    j = pl.program_id(1)
    ds = q_ref.shape[0]
    hr = 2 * ds

    def head_rows(ref, h):
        return ref[pl.ds(h, PAGE_SIZE, stride=N_HEADS), :].astype(BF16)

    @pl.when(j == 0)
    def _():
        pieces = []
        for h in range(N_HEADS):
            qh = q_ref[:, h * V_DIM:(h + 1) * V_DIM]
            lane = lax.broadcasted_iota(I32, qh.shape, 1)
            pieces.append(jnp.where(lane < HEAD_DIM, qh, 0.0))
            pieces.append(jnp.where(lane >= HEAD_DIM, qh, 0.0))
        qall[...] = jnp.concatenate(pieces, axis=0).astype(BF16)
        knew[...] = jnp.zeros(knew.shape, F32)
        vnew[...] = jnp.zeros(vnew.shape, F32)
        knew[0:ds * N_HEADS] = kn_ref[...]
        vnew[0:ds * N_HEADS] = vn_ref[...]
        row = lax.broadcasted_iota(I32, (hr, PAGE_SIZE), 0)
        key = lax.broadcasted_iota(I32, (hr, PAGE_SIZE), 1)
        keep = key <= (row % ds)
        for h in range(N_HEADS):
            rows = slice(h * hr, (h + 1) * hr)
            s = jnp.where(keep, _dot_nt(qall[rows], head_rows(knew, h)), NEG_INF)
            m = jnp.max(s, axis=-1, keepdims=True)
            p = jnp.exp(s - m)
            m_ref[rows] = m
            l_ref[rows] = jnp.sum(p, axis=-1, keepdims=True)
            acc_ref[rows] = _dot(p.astype(BF16), head_rows(vnew, h))

    q_all, m_all, l_all, acc_all = qall[...], m_ref[...], l_ref[...], acc_ref[...]
    heads = range(N_HEADS)
    rows = [slice(h * hr, (h + 1) * hr) for h in heads]
    s = [_dot_nt(q_all[rows[h]],
                 jnp.concatenate([head_rows(k_refs[i], h) for i in range(npg)], axis=0))
         for h in heads]
    s = jnp.concatenate(s, axis=0)
    m_new = jnp.maximum(m_all, jnp.max(s, axis=-1, keepdims=True))
    alpha = jnp.exp(m_all - m_new)
    p = jnp.exp(s - m_new)
    l_new = alpha * l_all + jnp.sum(p, axis=-1, keepdims=True)
    pb = p.astype(BF16)
    pv = [_dot(pb[rows[h]],
               jnp.concatenate([head_rows(v_refs[i], h) for i in range(npg)], axis=0))
          for h in heads]
    acc_new = alpha * acc_all + jnp.concatenate(pv, axis=0)
    m_ref[...] = m_new
    l_ref[...] = l_new
    acc_ref[...] = acc_new

    @pl.when(j == n_steps - 1)
    def _():
        lam = _diff_lambda(lq1[...], lk1[...], lq2[...], lk2[...], lam_init)
        on = acc_new / l_new
        for h in range(N_HEADS):
            r0 = h * hr
            o = on[r0:r0 + ds] - lam * on[r0 + ds:r0 + 2 * ds]
            ms = jnp.mean(o * o, axis=-1, keepdims=True)
            o_ref[:, h * V_DIM:(h + 1) * V_DIM] = (
                o * lax.rsqrt(ms + NORM_EPS) * g_ref[...] * (1.0 - lam_init))


def _attn_sample(q, k_new, v_new, cache_k, cache_v, page_table, lam_vecs, subln_g, lam_init):
    b, ds, _ = q.shape
    n_pages = page_table.shape[1]
    n_steps = n_pages // PAGES_PER_STEP
    nrow = N_HEADS * 2 * ds
    pt = page_table.reshape(-1)

    def page_map(i, bi, j, pt_ref):
        return (pt_ref[bi * n_pages + j * PAGES_PER_STEP + i], 0, 0)

    page_specs = [pl.BlockSpec((None, PAGE_ROWS, V_DIM), functools.partial(page_map, i))
                  for i in range(PAGES_PER_STEP)]
    small = lambda w: pl.BlockSpec((1, w), lambda bi, j, pt_ref: (0, 0))
    new_spec = pl.BlockSpec((None, ds * N_HEADS, V_DIM), lambda bi, j, pt_ref: (bi, 0, 0))
    qo_spec = pl.BlockSpec((None, ds, ATTN_WIDTH), lambda bi, j, pt_ref: (bi, 0, 0))
    grid_spec = pltpu.PrefetchScalarGridSpec(
        num_scalar_prefetch=1,
        grid=(b, n_steps),
        in_specs=[qo_spec, new_spec, new_spec] + page_specs + page_specs
        + [small(HEAD_DIM)] * 4 + [small(V_DIM)],
        out_specs=qo_spec,
        scratch_shapes=[pltpu.VMEM((nrow, V_DIM), BF16),
                        pltpu.VMEM((PAGE_ROWS, V_DIM), F32),
                        pltpu.VMEM((PAGE_ROWS, V_DIM), F32),
                        pltpu.VMEM((nrow, 1), F32),
                        pltpu.VMEM((nrow, 1), F32),
                        pltpu.VMEM((nrow, V_DIM), F32)],
    )
    return pl.pallas_call(
        functools.partial(_attn_s_body, lam_init, n_steps),
        name="attn_sample",
        grid_spec=grid_spec,
        out_shape=jax.ShapeDtypeStruct((b, ds, ATTN_WIDTH), F32),
        compiler_params=_params(("parallel", "arbitrary")),
    )(pt, q, k_new, v_new, *([cache_k] * PAGES_PER_STEP), *([cache_v] * PAGES_PER_STEP),
      *lam_vecs, subln_g)


def _conv_taps(ext_ref, w_ref, bias, rows):
    off = HIST - (CONV_WIDTH - 1)
    acc = jnp.broadcast_to(bias, (rows, CONV_CH))
    for jt in range(CONV_WIDTH):
        acc = acc + w_ref[jt:jt + 1, :] * ext_ref[pl.ds(jt + off, rows), :]
    return acc


CONV_ROWS = 64
SUBLANES = 8


def _conv_p_body(u_ref, h_ref, w_ref, b_ref, g_ref, beta_ref, c_ref, ext_ref, sh_ref):
    t = pl.program_id(1)
    tm = u_ref.shape[0]
    hist = h_ref[...]
    ext_ref[0:HIST] = jnp.where(t == 0, jnp.zeros_like(hist), hist)
    ext_ref[HIST:HIST + tm] = u_ref[...]
    span = sh_ref.shape[1]
    for r in range(1, SUBLANES):
        sh_ref[r - 1] = ext_ref[pl.ds(r, span), :]
    off = HIST - (CONV_WIDTH - 1)
    bias = b_ref[...]
    for c0 in range(0, tm, CONV_ROWS):
        acc = jnp.broadcast_to(bias, (CONV_ROWS, CONV_CH))
        for jt in range(CONV_WIDTH):
            r = (jt + off) % SUBLANES
            base = c0 + jt + off - r
            src = ext_ref if r == 0 else sh_ref.at[r - 1]
            acc = acc + w_ref[jt:jt + 1, :] * src[pl.ds(base, CONV_ROWS), :]
        c_ref[c0:c0 + CONV_ROWS, :] = _silu(_layer_norm(acc, g_ref[...], beta_ref[...]))


def _conv_prompt(u, w_dw, b_dw, g, beta, batch, seq, tm):
    n = u.shape[0]
    nt = seq // tm
    per = tm // HIST
    cur = lambda b, t: (b * nt + t, 0)
    prev = lambda b, t: (jnp.maximum((b * nt + t) * per - 1, 0), 0)
    small = lambda r: pl.BlockSpec((r, CONV_CH), lambda b, t: (0, 0))
    return pl.pallas_call(
        _conv_p_body,
        name="conv_prompt",
        grid=(batch, nt),
        in_specs=[pl.BlockSpec((tm, CONV_CH), cur), pl.BlockSpec((HIST, CONV_CH), prev),
                  small(CONV_WIDTH), small(1), small(1), small(1)],
        out_specs=pl.BlockSpec((tm, CONV_CH), cur),
        out_shape=jax.ShapeDtypeStruct((n, CONV_CH), F32),
        scratch_shapes=[pltpu.VMEM((HIST + tm, CONV_CH), F32),
                        pltpu.VMEM((SUBLANES - 1, HIST + tm - SUBLANES, CONV_CH), F32)],
        compiler_params=_params(("parallel", "parallel")),
    )(u, u, w_dw, b_dw, g, beta)


def _conv_s_body(u_ref, st_ref, w_ref, b_ref, g_ref, beta_ref, c_ref, ns_ref, ext_ref):
    ds = u_ref.shape[0]
    keep = CONV_WIDTH - 1
    off = HIST - keep
    ext_ref[0:off] = jnp.zeros((off, CONV_CH), F32)
    ext_ref[off:HIST] = st_ref[...]
    ext_ref[HIST:HIST + ds] = u_ref[...]
    y = _conv_taps(ext_ref, w_ref, b_ref[...], ds)
    c_ref[...] = _silu(_layer_norm(y, g_ref[...], beta_ref[...]))
    ns_ref[...] = ext_ref[HIST + ds - keep:HIST + ds]


def _conv_sample(u, state, w_dw, b_dw, g, beta):
    b, ds, _ = u.shape
    keep = CONV_WIDTH - 1
    per_b = lambda r: pl.BlockSpec((None, r, CONV_CH), lambda i: (i, 0, 0))
    small = lambda r: pl.BlockSpec((r, CONV_CH), lambda i: (0, 0))
    return pl.pallas_call(
        _conv_s_body,
        name="conv_sample",
        grid=(b,),
        in_specs=[per_b(ds), per_b(keep), small(CONV_WIDTH), small(1), small(1), small(1)],
        out_specs=[per_b(ds), per_b(keep)],
        out_shape=[jax.ShapeDtypeStruct((b, ds, CONV_CH), F32),
                   jax.ShapeDtypeStruct((b, keep, CONV_CH), F32)],
        scratch_shapes=[pltpu.VMEM((HIST + ds, CONV_CH), F32)],
        compiler_params=_params(("parallel",)),
    )(u, state, w_dw, b_dw, g, beta)


def _first_index(hit, iota, limit, axis):
    return jnp.min(jnp.where(hit, iota, limit), axis=axis, keepdims=True)


def _route(x1, wr_hi, wr_lo, bias):
    tm = x1.shape[0]
    x_hi = x1.astype(BF16)
    x_lo = (x1 - x_hi.astype(F32)).astype(BF16)
    logits = _dot_nt(wr_hi, x_hi) + (_dot_nt(wr_lo, x_hi) + _dot_nt(wr_hi, x_lo))
    scores = _sigmoid(logits)
    sel = scores + bias
    sel3 = sel.reshape(N_GROUPS, GROUP_SIZE, tm)
    member = lax.broadcasted_iota(I32, sel3.shape, 1)
    m1 = jnp.max(sel3, axis=1, keepdims=True)
    i1 = _first_index(sel3 == m1, member, GROUP_SIZE, 1)
    m2 = jnp.max(jnp.where(member == i1, -jnp.inf, sel3), axis=1, keepdims=True)
    gs = jnp.broadcast_to(m1 + m2, sel3.shape).reshape(N_EXPERTS, tm)
    eiota = lax.broadcasted_iota(I32, (N_EXPERTS, tm), 0)
    giota = eiota // GROUP_SIZE
    gmask = jnp.zeros((N_EXPERTS, tm), jnp.bool_)
    for _ in range(TOPK_GROUPS):
        m = jnp.max(gs, axis=0, keepdims=True)
        gi = _first_index(gs == m, giota, N_GROUPS, 0)
        pick = giota == gi
        gmask = jnp.logical_or(gmask, pick)
        gs = jnp.where(pick, -jnp.inf, gs)
    selm = jnp.where(gmask, sel, NEG_INF)
    idx_rows, w_rows = [], []
    for _ in range(TOP_K):
        m = jnp.max(selm, axis=0, keepdims=True)
        ei = _first_index(selm == m, eiota, N_EXPERTS, 0)
        pick = eiota == ei
        idx_rows.append(ei)
        w_rows.append(jnp.sum(jnp.where(pick, scores, 0.0), axis=0, keepdims=True))
        selm = jnp.where(pick, -jnp.inf, selm)
    idx = jnp.concatenate(idx_rows, axis=0)
    w = jnp.concatenate(w_rows, axis=0)
    w = w / jnp.sum(w, axis=0, keepdims=True) * ROUTED_SCALE
    return idx, w


def _mix_body(o_ref, c_ref, x_ref, wo_a, wo_c, g_ref, b_ref, wrh_ref, wrl_ref, br_ref,
              x1_ref, xrt_ref, idx_ref, wts_ref, rank_ref, cnt_ref, base_ref):
    i = pl.program_id(0)
    tm = x_ref.shape[0]

    @pl.when(i == 0)
    def _():
        base_ref[...] = jnp.zeros(base_ref.shape, F32)

    mix = _dot(o_ref[...].astype(BF16), wo_a[...]) + _dot(c_ref[...].astype(BF16), wo_c[...])
    x1 = _layer_norm(DEEPNORM_ALPHA * x_ref[...] + mix, g_ref[...], b_ref[...])
    x1_ref[...] = x1
    _store_row_tiles(xrt_ref, x1)

    idx, w = _route(x1, wrh_ref[...], wrl_ref[...], br_ref[...])
    idx_ref[...] = idx
    w_rows = jnp.concatenate([w, jnp.zeros((LANES - TOP_K, tm), F32)], axis=0)
    wts_ref[...] = w_rows.T

    eiota = lax.broadcasted_iota(I32, (N_EXPERTS, tm), 0)
    chosen = jnp.zeros((N_EXPERTS, tm), F32)
    for k in range(TOP_K):
        chosen = chosen + jnp.where(eiota == idx[k:k + 1, :], 1.0, 0.0)
    r = lax.broadcasted_iota(I32, (tm, tm), 0)
    c = lax.broadcasted_iota(I32, (tm, tm), 1)
    before = jnp.where(r < c, 1.0, 0.0).astype(BF16)
    base = base_ref[:, 0:1]
    rank_full = _dot(chosen.astype(BF16), before) + base
    rows = [jnp.sum(jnp.where(eiota == idx[k:k + 1, :], rank_full, 0.0), axis=0, keepdims=True)
            for k in range(TOP_K)]
    rank_ref[...] = jnp.concatenate(rows, axis=0).astype(I32)
    total = base + jnp.sum(chosen, axis=1, keepdims=True)
    base_ref[...] = jnp.broadcast_to(total, base_ref.shape)
    cnt_ref[...] = jnp.broadcast_to(total, cnt_ref.shape)


def _mix(o, c, x, wo_a, wo_c, ln_g, ln_b, wr_hi, wr_lo, b_router, tm):
    n = x.shape[0]
    rows = lambda w: pl.BlockSpec((tm, w), lambda i: (i, 0))
    full = lambda a: pl.BlockSpec(a.shape, lambda i: (0, 0))
    cols = pl.BlockSpec((TOP_K, tm), lambda i: (0, i))
    cnt = pl.BlockSpec((N_EXPERTS, LANES), lambda i: (0, 0))
    return pl.pallas_call(
        _mix_body,
        name="mix",
        grid=(n // tm,),
        in_specs=[rows(ATTN_WIDTH), rows(CONV_CH), rows(D_MODEL), full(wo_a), full(wo_c),
                  full(ln_g), full(ln_b), full(wr_hi), full(wr_lo), full(b_router)],
        out_specs=[rows(D_MODEL), pl.BlockSpec((tm * ROW_TILE, LANES), lambda i: (i, 0)), cols,
                   rows(LANES), cols, cnt],
        out_shape=[jax.ShapeDtypeStruct((n, D_MODEL), F32),
                   jax.ShapeDtypeStruct((n * ROW_TILE, LANES), F32),
                   jax.ShapeDtypeStruct((TOP_K, n), I32),
                   jax.ShapeDtypeStruct((n, LANES), F32),
                   jax.ShapeDtypeStruct((TOP_K, n), I32),
                   jax.ShapeDtypeStruct((N_EXPERTS, LANES), F32)],
        scratch_shapes=[pltpu.VMEM((N_EXPERTS, LANES), F32)],
        compiler_params=_params(("arbitrary",)),
    )(o, c, x, wo_a, wo_c, ln_g, ln_b, wr_hi, wr_lo, b_router)


def _dest_body(idx_ref, rank_ref, ps_ref, dest_ref):
    idx = idx_ref[...]
    tm = idx.shape[1]
    eiota = lax.broadcasted_iota(I32, (N_EXPERTS, tm), 0)
    ps = ps_ref[...]
    rows = [jnp.sum(jnp.where(eiota == idx[k:k + 1, :], ps, 0.0), axis=0, keepdims=True)
            for k in range(TOP_K)]
    dest_ref[...] = jnp.concatenate(rows, axis=0).astype(I32) + rank_ref[...]


def _dest(idx, rank, pstart_col, tm):
    n = idx.shape[1]
    cols = pl.BlockSpec((TOP_K, tm), lambda i: (0, i))
    return pl.pallas_call(
        _dest_body,
        name="dest",
        grid=(n // tm,),
        in_specs=[cols, cols, pl.BlockSpec((N_EXPERTS, 1), lambda i: (0, 0))],
        out_specs=cols,
        out_shape=jax.ShapeDtypeStruct((TOP_K, n), I32),
        compiler_params=_params(("parallel",)),
    )(idx, rank, pstart_col)


PAD_PIECES = tuple(1 << s for s in reversed(range(EXPERT_BLOCK.bit_length() - 1)))
ROW_UNROLL = 4


def _dispatch_body(ps_ref, cnt_ref, dest_ref, x_ref, xs_ref, zbuf, sem, zsem):
    i = pl.program_id(0)
    ts = x_ref.shape[0] // ROW_TILE

    @pl.when(i == 0)
    def _():
        zbuf[...] = jnp.zeros(zbuf.shape, F32)

        def pad_rows(wait, e, carry):
            cnt = cnt_ref[e]
            base = ps_ref[e] + cnt
            pad = (-cnt) & (EXPERT_BLOCK - 1)
            for p in PAD_PIECES:
                @pl.when((pad & p) != 0)
                def _():
                    first = pl.multiple_of(base * ROW_TILE, ROW_TILE)
                    cp = pltpu.make_async_copy(zbuf.at[pl.ds(0, p * ROW_TILE)],
                                               xs_ref.at[pl.ds(first, p * ROW_TILE)], zsem)
                    if wait:
                        cp.wait()
                    else:
                        cp.start()
                base = base + (pad & p)
            return carry

        lax.fori_loop(0, N_EXPERTS, functools.partial(pad_rows, False), 0)
        lax.fori_loop(0, N_EXPERTS, functools.partial(pad_rows, True), 0)

    def row_copy(t, k):
        return pltpu.make_async_copy(_token_tile(x_ref, t), _token_tile(xs_ref, dest_ref[k, t]),
                                     sem)

    def issue(g, carry):
        for u in range(ROW_UNROLL):
            for k in range(TOP_K):
                row_copy(g * ROW_UNROLL + u, k).start(priority=k % 2)
        return carry

    lax.fori_loop(0, ts // ROW_UNROLL, issue, 0)
    for k in range(TOP_K):
        pltpu.make_async_copy(x_ref, xs_ref.at[pl.ds(0, ts * ROW_TILE)], sem).wait()


def _dispatch(x_tiles, dest, pstart, counts, n_rows, ts):
    n = x_tiles.shape[0] // ROW_TILE
    grid_spec = pltpu.PrefetchScalarGridSpec(
        num_scalar_prefetch=2,
        grid=(n // ts,),
        in_specs=[pl.BlockSpec((TOP_K, ts), lambda i, ps, cn: (0, i), memory_space=pltpu.SMEM),
                  pl.BlockSpec((ts * ROW_TILE, LANES), lambda i, ps, cn: (i, 0))],
        out_specs=pl.BlockSpec(memory_space=pl.ANY),
        scratch_shapes=[pltpu.VMEM((PAD_PIECES[0] * ROW_TILE, LANES), F32),
                        pltpu.SemaphoreType.DMA, pltpu.SemaphoreType.DMA],
    )
    return pl.pallas_call(
        _dispatch_body,
        name="dispatch",
        grid_spec=grid_spec,
        out_shape=jax.ShapeDtypeStruct((n_rows * ROW_TILE, LANES), F32),
        compiler_params=_params(("arbitrary",)),
    )(pstart, counts, dest, x_tiles)


def _expert_body(be_ref, nb_ref, xs_ref, wg_ref, wu_ref, wd_ref, ys_ref):
    del be_ref
    i = pl.program_id(0)

    @pl.when(i < nb_ref[0])
    def _():
        x = _load_row_tiles(xs_ref, EXPERT_BLOCK).astype(BF16)
        h = (_silu(_dot(x, wg_ref[...])) * _dot(x, wu_ref[...])).astype(BF16)
        _store_row_tiles(ys_ref, _dot(h, wd_ref[...]))

    @pl.when(i >= nb_ref[0])
    def _():
        ys_ref[...] = jnp.zeros(ys_ref.shape, F32)


def _experts(xs, block_e, nb_used, wg, wu, wd):
    n_blocks = xs.shape[0] // (EXPERT_BLOCK * ROW_TILE)
    blk = (EXPERT_BLOCK * ROW_TILE, LANES)
    in_rows = pl.BlockSpec(blk, lambda i, be, nb: (jnp.minimum(i, nb[0] - 1), 0))
    grid_spec = pltpu.PrefetchScalarGridSpec(
        num_scalar_prefetch=2,
        grid=(n_blocks,),
        in_specs=[in_rows,
                  pl.BlockSpec((None, D_MODEL, D_EXPERT), lambda i, be, nb: (be[i], 0, 0)),
                  pl.BlockSpec((None, D_MODEL, D_EXPERT), lambda i, be, nb: (be[i], 0, 0)),
                  pl.BlockSpec((None, D_EXPERT, D_MODEL), lambda i, be, nb: (be[i], 0, 0))],
        out_specs=pl.BlockSpec(blk, lambda i, be, nb: (i, 0)),
    )
    return pl.pallas_call(
        _expert_body,
        name="experts",
        grid_spec=grid_spec,
        out_shape=jax.ShapeDtypeStruct(xs.shape, F32),
        compiler_params=_params(("arbitrary",)),
    )(block_e, nb_used, xs, wg, wu, wd)


def _combine_body(dest_ref, dnext_ref, wts_ref, x1_ref, ys_ref, wsg, wsu, wsd, g_ref, b_ref,
                  y_ref, buf, sems):
    i = pl.program_id(0)
    n_tiles = pl.num_programs(0)
    tc = x1_ref.shape[0]
    slot = i % 2

    def issue(d_ref, s, g, carry):
        for u in range(ROW_UNROLL):
            t = g * ROW_UNROLL + u
            for k in range(TOP_K):
                pltpu.make_async_copy(_token_tile(ys_ref, d_ref[k, t]),
                                      _token_tile(buf.at[s, k], t),
                                      sems.at[s]).start(priority=k % 2)
        return carry

    n_groups = tc // ROW_UNROLL

    @pl.when(i == 0)
    def _():
        lax.fori_loop(0, n_groups, functools.partial(issue, dest_ref, slot), 0)

    @pl.when(i + 1 < n_tiles)
    def _():
        lax.fori_loop(0, n_groups, functools.partial(issue, dnext_ref, 1 - slot), 0)

    x1 = x1_ref[...]
    xb = x1.astype(BF16)
    hs = (_silu(_dot(xb, wsg[...])) * _dot(xb, wsu[...])).astype(BF16)
    shared = _dot(hs, wsd[...])

    for k in range(TOP_K):
        pltpu.make_async_copy(ys_ref.at[pl.ds(0, tc * ROW_TILE)], buf.at[slot, k],
                              sems.at[slot]).wait()

    w = wts_ref[...]
    routed = w[:, 0:1] * _load_row_tiles(buf.at[slot, 0], tc)
    for k in range(1, TOP_K):
        routed = routed + w[:, k:k + 1] * _load_row_tiles(buf.at[slot, k], tc)
    y_ref[...] = _layer_norm(DEEPNORM_ALPHA * x1 + (routed + shared), g_ref[...], b_ref[...])


def _combine(dest, wts_t, x1, ys, wsg, wsu, wsd, ln_g, ln_b, tc):
    n = x1.shape[0]
    last = n // tc - 1
    dest_cur = pl.BlockSpec((TOP_K, tc), lambda i: (0, i), memory_space=pltpu.SMEM)
    dest_next = pl.BlockSpec((TOP_K, tc), lambda i: (0, jnp.minimum(i + 1, last)),
                             memory_space=pltpu.SMEM)
    full = lambda a: pl.BlockSpec(a.shape, lambda i: (0, 0))
    rows = lambda w: pl.BlockSpec((tc, w), lambda i: (i, 0))
    return pl.pallas_call(
        _combine_body,
        name="combine",
        grid=(n // tc,),
        in_specs=[dest_cur, dest_next, rows(LANES), rows(D_MODEL),
                  pl.BlockSpec(memory_space=pl.ANY),
                  full(wsg), full(wsu), full(wsd), full(ln_g), full(ln_b)],
        out_specs=rows(D_MODEL),
        out_shape=jax.ShapeDtypeStruct((n, D_MODEL), F32),
        scratch_shapes=[pltpu.VMEM((2, TOP_K, tc * ROW_TILE, LANES), F32),
                        pltpu.SemaphoreType.DMA((2,))],
        compiler_params=_params(("arbitrary",)),
    )(dest, dest, wts_t, x1, ys, wsg, wsu, wsd, ln_g, ln_b)


def _moe(x1, x_tiles, idx, wts_t, rank, cnt, moe_w, ln_g, ln_b, tile):
    n = x1.shape[0]
    wg, wu, wd, wsg, wsu, wsd = moe_w
    blk = EXPERT_BLOCK
    n_blocks = -(-(n * TOP_K) // blk) + N_EXPERTS
    counts = cnt[:, 0].astype(I32)
    padded = (counts + blk - 1) // blk * blk
    ends = jnp.cumsum(padded)
    pstart = ends - padded
    nb_used = (ends[-1] // blk).astype(I32).reshape(1)
    block_row0 = jnp.arange(n_blocks, dtype=I32) * blk
    block_e = jnp.minimum(jnp.sum((ends[None, :] <= block_row0[:, None]).astype(I32), axis=1),
                          N_EXPERTS - 1).astype(I32)
    dest = _dest(idx, rank, pstart.astype(F32).reshape(N_EXPERTS, 1), tile)
    xs = _dispatch(x_tiles, dest, pstart, counts, n_blocks * blk, tile)
    ys = _experts(xs, block_e, nb_used, wg, wu, wd)
    return _combine(dest, wts_t, x1, ys, wsg, wsu, wsd, ln_g, ln_b, min(tile, 128))


def kernel(x_prompt, x_sample, cache_k, cache_v, state_conv, page_table, w_in, lam_q1, lam_k1,
           lam_q2, lam_k2, subln_g, w_dw, b_dw, conv_ln_g, conv_ln_b, w_o, ln1_g, ln1_b,
           w_router, b_router, w_gate, w_up, w_down, w_sh_gate, w_sh_up, w_sh_down, ln2_g,
           ln2_b):
    batch, seq, _ = x_prompt.shape
    dec_b, dec_s, _ = x_sample.shape
    n_p, n_s = batch * seq, dec_b * dec_s
    layer = 0
    lam_init = _lambda_init(layer)
    row = lambda a: a[layer].reshape(1, -1)

    w_in_b = w_in[layer].astype(BF16)
    wo = w_o[layer].astype(BF16)
    wo_a, wo_c = wo[:ATTN_WIDTH], wo[ATTN_WIDTH:]
    wr_t = w_router[layer].T
    wr_hi = wr_t.astype(BF16)
    wr_lo = (wr_t - wr_hi.astype(F32)).astype(BF16)
    br = b_router[layer].reshape(-1, 1)
    moe_w = tuple(w[layer].astype(BF16)
                  for w in (w_gate, w_up, w_down, w_sh_gate, w_sh_up, w_sh_down))
    lam_vecs = (row(lam_q1), row(lam_k1), row(lam_q2), row(lam_k2))
    g_sub = row(subln_g)
    conv_w = (w_dw[layer], row(b_dw), row(conv_ln_g), row(conv_ln_b))
    ln1 = (row(ln1_g), row(ln1_b))
    ln2 = (row(ln2_g), row(ln2_b))

    xp = x_prompt.reshape(n_p, D_MODEL)
    wq_t = w_in_b[:, :QK_WIDTH].T
    wv_t = w_in_b[:, 2 * QK_WIDTH:2 * QK_WIDTH + ATTN_WIDTH].T
    qt_p, k_p, v_p, u_p, kb_p, vt_p = _inproj(xp, w_in_b, 512, (wq_t, wv_t))
    o_p = _attn_prompt(qt_p, kb_p, vt_p, lam_vecs, g_sub.reshape(V_DIM, 1), lam_init, batch, seq)
    c_p = _conv_prompt(u_p, *conv_w, batch, seq, 512)
    routed_p = _mix(o_p, c_p, xp, wo_a, wo_c, *ln1, wr_hi, wr_lo, br, 512)
    y_p = _moe(*routed_p, moe_w, *ln2, 256)

    xs = x_sample.reshape(n_s, D_MODEL)
    q_s, k_s, v_s, u_s = _inproj(xs, w_in_b, n_s)
    pool = cache_k.shape[1]
    page_rows = lambda a: a[layer].reshape(pool, PAGE_ROWS, V_DIM)
    new_rows = lambda a: a.reshape(dec_b, dec_s * N_HEADS, V_DIM)
    o_s = _attn_sample(q_s.reshape(dec_b, dec_s, QK_WIDTH), new_rows(k_s), new_rows(v_s),
                       page_rows(cache_k), page_rows(cache_v), page_table, lam_vecs, g_sub,
                       lam_init)
    c_s, st_s = _conv_sample(u_s.reshape(dec_b, dec_s, CONV_CH), state_conv[layer], *conv_w)
    routed_s = _mix(o_s.reshape(n_s, ATTN_WIDTH), c_s.reshape(n_s, CONV_CH), xs, wo_a, wo_c,
                    *ln1, wr_hi, wr_lo, br, n_s)
    y_s = _moe(*routed_s, moe_w, *ln2, 128)

    keep = CONV_WIDTH - 1
    u_p3 = u_p.reshape(batch, seq, CONV_CH)
    return (y_p.reshape(batch, seq, D_MODEL),
            y_s.reshape(dec_b, dec_s, D_MODEL),
            k_p.reshape(1, batch, seq, N_HEADS, V_DIM),
            v_p.reshape(1, batch, seq, N_HEADS, V_DIM),
            u_p3[:, seq - keep:, :][None],
            k_s.reshape(1, dec_b, dec_s, N_HEADS, V_DIM),
            v_s.reshape(1, dec_b, dec_s, N_HEADS, V_DIM),
            st_s[None])
```

```python
import functools
import math

import jax
import jax.numpy as jnp
from jax import lax
from jax.experimental import pallas as pl
from jax.experimental.pallas import tpu as pltpu

F32 = jnp.float32
BF16 = jnp.bfloat16
I32 = jnp.int32
U32 = jnp.uint32

D_MODEL = 1024
N_HEADS = 4
HEAD_DIM = 64
V_DIM = 128
ATTN_WIDTH = N_HEADS * V_DIM
QK_WIDTH = N_HEADS * 2 * HEAD_DIM
ATTN_SCALE = HEAD_DIM ** -0.5
CONV_CH = D_MODEL - ATTN_WIDTH
CONV_WIDTH = 31
N_EXPERTS = 64
N_GROUPS = 8
GROUP_SIZE = N_EXPERTS // N_GROUPS
TOPK_GROUPS = 4
TOP_K = 8
D_EXPERT = D_MODEL // 4
ROUTED_SCALE = 2.5
DEPTH = 1
DEEPNORM_ALPHA = (2 * DEPTH) ** 0.25
NORM_EPS = 1e-5
NEG_INF = -1e30
PAGE_SIZE = 128

LANES = 128
ROW_TILE = D_MODEL // LANES
PACK_TILE = ROW_TILE // 2
VMEM_LIMIT = 48 * 1024 * 1024
PAGES_PER_STEP = 16
PAGE_ROWS = PAGE_SIZE * N_HEADS
EXPERT_BLOCK = 256
HIST = 32


def _lambda_init(layer):
    return 0.8 - 0.6 * math.exp(-0.3 * layer)


def _sigmoid(x):
    return 1.0 / (1.0 + jnp.exp(-x))


def _silu(x):
    return x * _sigmoid(x)


def _layer_norm(x, g, b):
    mu = jnp.mean(x, axis=-1, keepdims=True)
    xc = x - mu
    var = jnp.mean(xc * xc, axis=-1, keepdims=True)
    return xc * lax.rsqrt(var + NORM_EPS) * g + b


def _dot(a, b):
    return jnp.dot(a, b, preferred_element_type=F32)


def _dot_nt(a, b):
    return lax.dot_general(a, b, (((1,), (1,)), ((), ())), preferred_element_type=F32)


def _diff_lambda(lq1, lk1, lq2, lk2, lam_init):
    a = jnp.exp(jnp.sum(lq1 * lk1, axis=-1, keepdims=True))
    b = jnp.exp(jnp.sum(lq2 * lk2, axis=-1, keepdims=True))
    return a - b + lam_init


def _params(dims):
    return pltpu.CompilerParams(dimension_semantics=dims, vmem_limit_bytes=VMEM_LIMIT)


def _store_row_tiles(ref, x):
    m = x.shape[0]
    for j in range(ROW_TILE):
        ref[pl.ds(j, m, stride=ROW_TILE), :] = x[:, j * LANES:(j + 1) * LANES]


def _load_row_tiles(ref, m):
    return jnp.concatenate([ref[pl.ds(j, m, stride=ROW_TILE), :] for j in range(ROW_TILE)],
                           axis=1)


def _token_tile(ref, t):
    return ref.at[pl.ds(pl.multiple_of(t * ROW_TILE, ROW_TILE), ROW_TILE)]


def _store_packed_tiles(ref, x):
    m = x.shape[0]
    bits = pltpu.bitcast(x.astype(BF16).astype(F32), U32)
    half = D_MODEL // 2
    for j in range(PACK_TILE):
        lo = bits[:, j * LANES:(j + 1) * LANES] >> 16
        hi = bits[:, half + j * LANES:half + (j + 1) * LANES]
        ref[pl.ds(j, m, stride=PACK_TILE), :] = lo | hi


def _load_packed_tiles(ref, m):
    words = [ref[pl.ds(j, m, stride=PACK_TILE), :] for j in range(PACK_TILE)]
    lo = [pltpu.bitcast(w << 16, F32) for w in words]
    hi = [pltpu.bitcast(w & jnp.uint32(0xFFFF0000), F32) for w in words]
    return jnp.concatenate(lo + hi, axis=1).astype(BF16)


def _store_head_rows(ref, x):
    m = x.shape[0]
    for h in range(N_HEADS):
        ref[pl.ds(h, m, stride=N_HEADS), :] = x[:, h * V_DIM:(h + 1) * V_DIM]


def _inproj_body(x_ref, w_ref, *refs):
    xb = x_ref[...].astype(BF16)

    def mm(c0):
        return _dot(xb, w_ref[:, c0:c0 + QK_WIDTH])

    k = mm(QK_WIDTH)
    if len(refs) == 4:
        q_ref, k_ref, v_ref, u_ref = refs
        q_ref[...] = mm(0) * ATTN_SCALE
    else:
        wqt_ref, wvt_ref, q_ref, k_ref, v_ref, u_ref, kb_ref, vt_ref = refs
        q_ref[...] = _dot_nt(wqt_ref[...], xb) * ATTN_SCALE
        vt_ref[...] = _dot_nt(wvt_ref[...], xb).astype(BF16)
        kb_ref[...] = k.astype(BF16)
    _store_head_rows(k_ref, k)
    _store_head_rows(v_ref, mm(2 * QK_WIDTH))
    a = mm(2 * QK_WIDTH + ATTN_WIDTH)
    b = mm(2 * QK_WIDTH + ATTN_WIDTH + CONV_CH)
    u_ref[...] = a * _sigmoid(b)


def _inproj(x, w_in_b, tm, transposed_w=()):
    n = x.shape[0]
    wide = QK_WIDTH
    blk = pl.BlockSpec((tm, wide), lambda i: (i, 0))
    hblk = pl.BlockSpec((tm * N_HEADS, V_DIM), lambda i: (i, 0))
    full = lambda a: pl.BlockSpec(a.shape, lambda i: (0, 0))
    row_major = jax.ShapeDtypeStruct((n, wide), F32)
    head_rows = jax.ShapeDtypeStruct((n * N_HEADS, V_DIM), F32)
    out_specs = [blk, hblk, hblk, blk]
    out_shape = [row_major, head_rows, head_rows, row_major]
    if transposed_w:
        t_blk = pl.BlockSpec((None, wide, tm), lambda i: (i, 0, 0))
        out_specs = [t_blk, hblk, hblk, blk, blk, t_blk]
        out_shape = [jax.ShapeDtypeStruct((n // tm, wide, tm), F32), head_rows, head_rows,
                     row_major, jax.ShapeDtypeStruct((n, wide), BF16),
                     jax.ShapeDtypeStruct((n // tm, wide, tm), BF16)]
    return pl.pallas_call(
        _inproj_body,
        name="inproj",
        grid=(n // tm,),
        in_specs=[pl.BlockSpec((tm, D_MODEL), lambda i: (i, 0)), full(w_in_b)]
        + [full(w) for w in transposed_w],
        out_specs=out_specs,
        out_shape=out_shape,
        compiler_params=_params(("parallel",)),
    )(x, w_in_b, *transposed_w)


def _attn_p_body(lam_init, q_ref, k_ref, v_ref, lq1, lk1, lq2, lk2, g_ref, o_ref, acc_ref):
    qi = pl.program_id(2)
    tq = q_ref.shape[1]
    tk = v_ref.shape[2]
    q_t = q_ref[...]
    feat = lax.broadcasted_iota(I32, q_t.shape, 0)
    q_maps = (jnp.where(feat < HEAD_DIM, q_t, 0.0).astype(BF16),
              jnp.where(feat >= HEAD_DIM, q_t, 0.0).astype(BF16))
    acc_ref[...] = jnp.zeros(acc_ref.shape, F32)

    def chunk(masked, j, carry):
        kc = k_ref[pl.ds(pl.multiple_of(j * tk, tk), tk), :]
        vc = v_ref[j]
        if masked:
            key = lax.broadcasted_iota(I32, (tk, tq), 0)
            qry = lax.broadcasted_iota(I32, (tk, tq), 1)
            keep = key <= qry
        out = []
        for a in range(2):
            m_old, l_old = carry[2 * a], carry[2 * a + 1]
            s_t = _dot(kc, q_maps[a])
            if masked:
                s_t = jnp.where(keep, s_t, NEG_INF)
            m_new = jnp.maximum(m_old, jnp.max(s_t, axis=0, keepdims=True))
            alpha = jnp.exp(m_old - m_new)
            p_t = jnp.exp(s_t - m_new)
            out.append(m_new)
            out.append(alpha * l_old + jnp.sum(p_t, axis=0, keepdims=True))
            acc_ref[a] = alpha * acc_ref[a] + _dot(vc, p_t.astype(BF16))
        return tuple(out)

    neg = jnp.full((1, tq), -jnp.inf, F32)
    zero = jnp.zeros((1, tq), F32)
    carry = lax.fori_loop(0, qi, functools.partial(chunk, False), (neg, zero, neg, zero))
    _, l0, _, l1 = chunk(True, qi, carry)

    lam = _diff_lambda(lq1[...], lk1[...], lq2[...], lk2[...], lam_init)
    o_t = acc_ref[0] / l0 - lam * (acc_ref[1] / l1)
    ms = jnp.mean(o_t * o_t, axis=0, keepdims=True)
    o_t = o_t * lax.rsqrt(ms + NORM_EPS) * g_ref[...] * (1.0 - lam_init)
    o_ref[...] = o_t.T


def _attn_prompt(q_t, kb, v_t, lam_vecs, subln_g_col, lam_init, batch, seq):
    t = q_t.shape[2]
    n = kb.shape[0]
    nq = seq // t
    small = lambda w: pl.BlockSpec((1, w), lambda b, h, i: (0, 0))
    return pl.pallas_call(
        functools.partial(_attn_p_body, lam_init),
        name="attn_prompt",
        grid=(batch, N_HEADS, nq),
        in_specs=[pl.BlockSpec((None, V_DIM, t), lambda b, h, i: (b * nq + i, h, 0)),
                  pl.BlockSpec((seq, V_DIM), lambda b, h, i: (b, h)),
                  pl.BlockSpec((nq, V_DIM, t), lambda b, h, i: (b, h, 0))]
        + [small(HEAD_DIM)] * 4 + [pl.BlockSpec((V_DIM, 1), lambda b, h, i: (0, 0))],
        out_specs=pl.BlockSpec((t, V_DIM), lambda b, h, i: (b * nq + i, h)),
        out_shape=jax.ShapeDtypeStruct((n, ATTN_WIDTH), F32),
        scratch_shapes=[pltpu.VMEM((2, V_DIM, t), F32)],
        compiler_params=_params(("parallel", "parallel", "parallel")),
    )(q_t, kb, v_t, *lam_vecs, subln_g_col)


def _attn_s_body(lam_init, n_steps, pt_ref, q_ref, kn_ref, vn_ref, *rest):
    npg = PAGES_PER_STEP
    k_refs = rest[:npg]
    v_refs = rest[npg:2 * npg]
    (lq1, lk1, lq2, lk2, g_ref, o_ref, qall, knew, vnew, m_ref, l_ref,
     acc_ref) = rest[2 * npg:]
    j = pl.program_id(1)
    ds = q_ref.shape[0]
    hr = 2 * ds

    def head_rows(ref, h):
        return ref[pl.ds(h, PAGE_SIZE, stride=N_HEADS), :].astype(BF16)

    @pl.when(j == 0)
    def _():
        pieces = []
        for h in range(N_HEADS):
            qh = q_ref[:, h * V_DIM:(h + 1) * V_DIM]
            lane = lax.broadcasted_iota(I32, qh.shape, 1)
            pieces.append(jnp.where(lane < HEAD_DIM, qh, 0.0))
            pieces.append(jnp.where(lane >= HEAD_DIM, qh, 0.0))
        qall[...] = jnp.concatenate(pieces, axis=0).astype(BF16)
        knew[...] = jnp.zeros(knew.shape, F32)
        vnew[...] = jnp.zeros(vnew.shape, F32)
        knew[0:ds * N_HEADS] = kn_ref[...]
        vnew[0:ds * N_HEADS] = vn_ref[...]
        row = lax.broadcasted_iota(I32, (hr, PAGE_SIZE), 0)
        key = lax.broadcasted_iota(I32, (hr, PAGE_SIZE), 1)
        keep = key <= (row % ds)
        for h in range(N_HEADS):
            rows = slice(h * hr, (h + 1) * hr)
            s = jnp.where(keep, _dot_nt(qall[rows], head_rows(knew, h)), NEG_INF)
            m = jnp.max(s, axis=-1, keepdims=True)
            p = jnp.exp(s - m)
            m_ref[rows] = m
            l_ref[rows] = jnp.sum(p, axis=-1, keepdims=True)
            acc_ref[rows] = _dot(p.astype(BF16), head_rows(vnew, h))

    q_all, m_all, l_all, acc_all = qall[...], m_ref[...], l_ref[...], acc_ref[...]
    heads = range(N_HEADS)
    rows = [slice(h * hr, (h + 1) * hr) for h in heads]
    s = [_dot_nt(q_all[rows[h]],
                 jnp.concatenate([head_rows(k_refs[i], h) for i in range(npg)], axis=0))
         for h in heads]
    s = jnp.concatenate(s, axis=0)
    m_new = jnp.maximum(m_all, jnp.max(s, axis=-1, keepdims=True))
    alpha = jnp.exp(m_all - m_new)
    p = jnp.exp(s - m_new)
    l_new = alpha * l_all + jnp.sum(p, axis=-1, keepdims=True)
    pb = p.astype(BF16)
    pv = [_dot(pb[rows[h]],
               jnp.concatenate([head_rows(v_refs[i], h) for i in range(npg)], axis=0))
          for h in heads]
    acc_new = alpha * acc_all + jnp.concatenate(pv, axis=0)
    m_ref[...] = m_new
    l_ref[...] = l_new
    acc_ref[...] = acc_new

    @pl.when(j == n_steps - 1)
    def _():
        lam = _diff_lambda(lq1[...], lk1[...], lq2[...], lk2[...], lam_init)
        on = acc_new / l_new
        for h in range(N_HEADS):
            r0 = h * hr
            o = on[r0:r0 + ds] - lam * on[r0 + ds:r0 + 2 * ds]
            ms = jnp.mean(o * o, axis=-1, keepdims=True)
            o_ref[:, h * V_DIM:(h + 1) * V_DIM] = (
                o * lax.rsqrt(ms + NORM_EPS) * g_ref[...] * (1.0 - lam_init))


def _attn_sample(q, k_new, v_new, cache_k, cache_v, page_table, lam_vecs, subln_g, lam_init):
    b, ds, _ = q.shape
    n_pages = page_table.shape[1]
    n_steps = n_pages // PAGES_PER_STEP
    nrow = N_HEADS * 2 * ds
    pt = page_table.reshape(-1)

    def page_map(i, bi, j, pt_ref):
        return (pt_ref[bi * n_pages + j * PAGES_PER_STEP + i], 0, 0)

    page_specs = [pl.BlockSpec((None, PAGE_ROWS, V_DIM), functools.partial(page_map, i))
                  for i in range(PAGES_PER_STEP)]
    small = lambda w: pl.BlockSpec((1, w), lambda bi, j, pt_ref: (0, 0))
    new_spec = pl.BlockSpec((None, ds * N_HEADS, V_DIM), lambda bi, j, pt_ref: (bi, 0, 0))
    qo_spec = pl.BlockSpec((None, ds, ATTN_WIDTH), lambda bi, j, pt_ref: (bi, 0, 0))
    grid_spec = pltpu.PrefetchScalarGridSpec(
        num_scalar_prefetch=1,
        grid=(b, n_steps),
        in_specs=[qo_spec, new_spec, new_spec] + page_specs + page_specs
        + [small(HEAD_DIM)] * 4 + [small(V_DIM)],
        out_specs=qo_spec,
        scratch_shapes=[pltpu.VMEM((nrow, V_DIM), BF16),
                        pltpu.VMEM((PAGE_ROWS, V_DIM), F32),
                        pltpu.VMEM((PAGE_ROWS, V_DIM), F32),
                        pltpu.VMEM((nrow, 1), F32),
                        pltpu.VMEM((nrow, 1), F32),
                        pltpu.VMEM((nrow, V_DIM), F32)],
    )
    return pl.pallas_call(
        functools.partial(_attn_s_body, lam_init, n_steps),
        name="attn_sample",
        grid_spec=grid_spec,
        out_shape=jax.ShapeDtypeStruct((b, ds, ATTN_WIDTH), F32),
        compiler_params=_params(("parallel", "arbitrary")),
    )(pt, q, k_new, v_new, *([cache_k] * PAGES_PER_STEP), *([cache_v] * PAGES_PER_STEP),
      *lam_vecs, subln_g)


def _conv_taps(ext_ref, w_ref, bias, rows):
    off = HIST - (CONV_WIDTH - 1)
    acc = jnp.broadcast_to(bias, (rows, CONV_CH))
    for jt in range(CONV_WIDTH):
        acc = acc + w_ref[jt:jt + 1, :] * ext_ref[pl.ds(jt + off, rows), :]
    return acc


CONV_ROWS = 64
SUBLANES = 8


def _conv_p_body(u_ref, h_ref, w_ref, b_ref, g_ref, beta_ref, c_ref, ext_ref, sh_ref):
    t = pl.program_id(1)
    tm = u_ref.shape[0]
    hist = h_ref[...]
    ext_ref[0:HIST] = jnp.where(t == 0, jnp.zeros_like(hist), hist)
    ext_ref[HIST:HIST + tm] = u_ref[...]
    span = sh_ref.shape[1]
    for r in range(1, SUBLANES):
        sh_ref[r - 1] = ext_ref[pl.ds(r, span), :]
    off = HIST - (CONV_WIDTH - 1)
    bias = b_ref[...]
    for c0 in range(0, tm, CONV_ROWS):
        acc = jnp.broadcast_to(bias, (CONV_ROWS, CONV_CH))
        for jt in range(CONV_WIDTH):
            r = (jt + off) % SUBLANES
            base = c0 + jt + off - r
            src = ext_ref if r == 0 else sh_ref.at[r - 1]
            acc = acc + w_ref[jt:jt + 1, :] * src[pl.ds(base, CONV_ROWS), :]
        c_ref[c0:c0 + CONV_ROWS, :] = _silu(_layer_norm(acc, g_ref[...], beta_ref[...]))


def _conv_prompt(u, w_dw, b_dw, g, beta, batch, seq, tm):
    n = u.shape[0]
    nt = seq // tm
    per = tm // HIST
    cur = lambda b, t: (b * nt + t, 0)
    prev = lambda b, t: (jnp.maximum((b * nt + t) * per - 1, 0), 0)
    small = lambda r: pl.BlockSpec((r, CONV_CH), lambda b, t: (0, 0))
    return pl.pallas_call(
        _conv_p_body,
        name="conv_prompt",
        grid=(batch, nt),
        in_specs=[pl.BlockSpec((tm, CONV_CH), cur), pl.BlockSpec((HIST, CONV_CH), prev),
                  small(CONV_WIDTH), small(1), small(1), small(1)],
        out_specs=pl.BlockSpec((tm, CONV_CH), cur),
        out_shape=jax.ShapeDtypeStruct((n, CONV_CH), F32),
        scratch_shapes=[pltpu.VMEM((HIST + tm, CONV_CH), F32),
                        pltpu.VMEM((SUBLANES - 1, HIST + tm - SUBLANES, CONV_CH), F32)],
        compiler_params=_params(("parallel", "parallel")),
    )(u, u, w_dw, b_dw, g, beta)


def _conv_s_body(u_ref, st_ref, w_ref, b_ref, g_ref, beta_ref, c_ref, ns_ref, ext_ref):
    ds = u_ref.shape[0]
    keep = CONV_WIDTH - 1
    off = HIST - keep
    ext_ref[0:off] = jnp.zeros((off, CONV_CH), F32)
    ext_ref[off:HIST] = st_ref[...]
    ext_ref[HIST:HIST + ds] = u_ref[...]
    y = _conv_taps(ext_ref, w_ref, b_ref[...], ds)
    c_ref[...] = _silu(_layer_norm(y, g_ref[...], beta_ref[...]))
    ns_ref[...] = ext_ref[HIST + ds - keep:HIST + ds]


def _conv_sample(u, state, w_dw, b_dw, g, beta):
    b, ds, _ = u.shape
    keep = CONV_WIDTH - 1
    per_b = lambda r: pl.BlockSpec((None, r, CONV_CH), lambda i: (i, 0, 0))
    small = lambda r: pl.BlockSpec((r, CONV_CH), lambda i: (0, 0))
    return pl.pallas_call(
        _conv_s_body,
        name="conv_sample",
        grid=(b,),
        in_specs=[per_b(ds), per_b(keep), small(CONV_WIDTH), small(1), small(1), small(1)],
        out_specs=[per_b(ds), per_b(keep)],
        out_shape=[jax.ShapeDtypeStruct((b, ds, CONV_CH), F32),
                   jax.ShapeDtypeStruct((b, keep, CONV_CH), F32)],
        scratch_shapes=[pltpu.VMEM((HIST + ds, CONV_CH), F32)],
        compiler_params=_params(("parallel",)),
    )(u, state, w_dw, b_dw, g, beta)


def _first_index(hit, iota, limit, axis):
    return jnp.min(jnp.where(hit, iota, limit), axis=axis, keepdims=True)


def _route(x1, wr_hi, wr_lo, bias):
    tm = x1.shape[0]
    x_hi = x1.astype(BF16)
    x_lo = (x1 - x_hi.astype(F32)).astype(BF16)
    logits = _dot_nt(wr_hi, x_hi) + (_dot_nt(wr_lo, x_hi) + _dot_nt(wr_hi, x_lo))
    scores = _sigmoid(logits)
    sel = scores + bias
    sel3 = sel.reshape(N_GROUPS, GROUP_SIZE, tm)
    member = lax.broadcasted_iota(I32, sel3.shape, 1)
    m1 = jnp.max(sel3, axis=1, keepdims=True)
    i1 = _first_index(sel3 == m1, member, GROUP_SIZE, 1)
    m2 = jnp.max(jnp.where(member == i1, -jnp.inf, sel3), axis=1, keepdims=True)
    gs = jnp.broadcast_to(m1 + m2, sel3.shape).reshape(N_EXPERTS, tm)
    eiota = lax.broadcasted_iota(I32, (N_EXPERTS, tm), 0)
    giota = eiota // GROUP_SIZE
    gmask = jnp.zeros((N_EXPERTS, tm), jnp.bool_)
    for _ in range(TOPK_GROUPS):
        m = jnp.max(gs, axis=0, keepdims=True)
        gi = _first_index(gs == m, giota, N_GROUPS, 0)
        pick = giota == gi
        gmask = jnp.logical_or(gmask, pick)
        gs = jnp.where(pick, -jnp.inf, gs)
    selm = jnp.where(gmask, sel, NEG_INF)
    idx_rows, w_rows = [], []
    for _ in range(TOP_K):
        m = jnp.max(selm, axis=0, keepdims=True)
        ei = _first_index(selm == m, eiota, N_EXPERTS, 0)
        pick = eiota == ei
        idx_rows.append(ei)
        w_rows.append(jnp.sum(jnp.where(pick, scores, 0.0), axis=0, keepdims=True))
        selm = jnp.where(pick, -jnp.inf, selm)
    idx = jnp.concatenate(idx_rows, axis=0)
    w = jnp.concatenate(w_rows, axis=0)
    w = w / jnp.sum(w, axis=0, keepdims=True) * ROUTED_SCALE
    return idx, w


def _mix_body(o_ref, c_ref, x_ref, wo_a, wo_c, g_ref, b_ref, wrh_ref, wrl_ref, br_ref,
              x1_ref, xrt_ref, idx_ref, wts_ref, rank_ref, cnt_ref, base_ref):
    i = pl.program_id(0)
    tm = x_ref.shape[0]

    @pl.when(i == 0)
    def _():
        base_ref[...] = jnp.zeros(base_ref.shape, F32)

    mix = _dot(o_ref[...].astype(BF16), wo_a[...]) + _dot(c_ref[...].astype(BF16), wo_c[...])
    x1 = _layer_norm(DEEPNORM_ALPHA * x_ref[...] + mix, g_ref[...], b_ref[...])
    x1_ref[...] = x1
    _store_packed_tiles(xrt_ref, x1)

    idx, w = _route(x1, wrh_ref[...], wrl_ref[...], br_ref[...])
    idx_ref[...] = idx
    w_rows = jnp.concatenate([w, jnp.zeros((LANES - TOP_K, tm), F32)], axis=0)
    wts_ref[...] = w_rows.T

    eiota = lax.broadcasted_iota(I32, (N_EXPERTS, tm), 0)
    chosen = jnp.zeros((N_EXPERTS, tm), F32)
    for k in range(TOP_K):
        chosen = chosen + jnp.where(eiota == idx[k:k + 1, :], 1.0, 0.0)
    r = lax.broadcasted_iota(I32, (tm, tm), 0)
    c = lax.broadcasted_iota(I32, (tm, tm), 1)
    before = jnp.where(r < c, 1.0, 0.0).astype(BF16)
    base = base_ref[:, 0:1]
    rank_full = _dot(chosen.astype(BF16), before) + base
    rows = [jnp.sum(jnp.where(eiota == idx[k:k + 1, :], rank_full, 0.0), axis=0, keepdims=True)
            for k in range(TOP_K)]
    rank_ref[...] = jnp.concatenate(rows, axis=0).astype(I32)
    total = base + jnp.sum(chosen, axis=1, keepdims=True)
    base_ref[...] = jnp.broadcast_to(total, base_ref.shape)
    cnt_ref[...] = jnp.broadcast_to(total, cnt_ref.shape)


def _mix(o, c, x, wo_a, wo_c, ln_g, ln_b, wr_hi, wr_lo, b_router, tm):
    n = x.shape[0]
    rows = lambda w: pl.BlockSpec((tm, w), lambda i: (i, 0))
    full = lambda a: pl.BlockSpec(a.shape, lambda i: (0, 0))
    cols = pl.BlockSpec((TOP_K, tm), lambda i: (0, i))
    cnt = pl.BlockSpec((N_EXPERTS, LANES), lambda i: (0, 0))
    return pl.pallas_call(
        _mix_body,
        name="mix",
        grid=(n // tm,),
        in_specs=[rows(ATTN_WIDTH), rows(CONV_CH), rows(D_MODEL), full(wo_a), full(wo_c),
                  full(ln_g), full(ln_b), full(wr_hi), full(wr_lo), full(b_router)],
        out_specs=[rows(D_MODEL), pl.BlockSpec((tm * PACK_TILE, LANES), lambda i: (i, 0)), cols,
                   rows(LANES), cols, cnt],
        out_shape=[jax.ShapeDtypeStruct((n, D_MODEL), F32),
                   jax.ShapeDtypeStruct((n * PACK_TILE, LANES), U32),
                   jax.ShapeDtypeStruct((TOP_K, n), I32),
                   jax.ShapeDtypeStruct((n, LANES), F32),
                   jax.ShapeDtypeStruct((TOP_K, n), I32),
                   jax.ShapeDtypeStruct((N_EXPERTS, LANES), F32)],
        scratch_shapes=[pltpu.VMEM((N_EXPERTS, LANES), F32)],
        compiler_params=_params(("arbitrary",)),
    )(o, c, x, wo_a, wo_c, ln_g, ln_b, wr_hi, wr_lo, b_router)


def _dest_body(idx_ref, rank_ref, ps_ref, dest_ref):
    idx = idx_ref[...]
    tm = idx.shape[1]
    eiota = lax.broadcasted_iota(I32, (N_EXPERTS, tm), 0)
    ps = ps_ref[...]
    rows = [jnp.sum(jnp.where(eiota == idx[k:k + 1, :], ps, 0.0), axis=0, keepdims=True)
            for k in range(TOP_K)]
    dest_ref[...] = jnp.concatenate(rows, axis=0).astype(I32) + rank_ref[...]


def _dest(idx, rank, pstart_col, tm):
    n = idx.shape[1]
    cols = pl.BlockSpec((TOP_K, tm), lambda i: (0, i))
    return pl.pallas_call(
        _dest_body,
        name="dest",
        grid=(n // tm,),
        in_specs=[cols, cols, pl.BlockSpec((N_EXPERTS, 1), lambda i: (0, 0))],
        out_specs=cols,
        out_shape=jax.ShapeDtypeStruct((TOP_K, n), I32),
        compiler_params=_params(("parallel",)),
    )(idx, rank, pstart_col)


PAD_PIECES = tuple(1 << s for s in reversed(range(EXPERT_BLOCK.bit_length() - 1)))
ROW_UNROLL = 4


def _dispatch_body(ps_ref, cnt_ref, dest_ref, x_ref, xs_ref, zbuf, sem, zsem):
    i = pl.program_id(0)
    ts = x_ref.shape[0]

    @pl.when(i == 0)
    def _():
        zbuf[...] = jnp.zeros(zbuf.shape, U32)

        def pad_rows(wait, e, carry):
            cnt = cnt_ref[e]
            base = ps_ref[e] + cnt
            pad = (-cnt) & (EXPERT_BLOCK - 1)
            for p in PAD_PIECES:
                @pl.when((pad & p) != 0)
                def _():
                    cp = pltpu.make_async_copy(zbuf.at[pl.ds(0, p)], xs_ref.at[pl.ds(base, p)],
                                               zsem)
                    if wait:
                        cp.wait()
                    else:
                        cp.start()
                base = base + (pad & p)
            return carry

        lax.fori_loop(0, N_EXPERTS, functools.partial(pad_rows, False), 0)
        lax.fori_loop(0, N_EXPERTS, functools.partial(pad_rows, True), 0)

    def row_copy(t, k):
        return pltpu.make_async_copy(x_ref.at[t], xs_ref.at[dest_ref[k, t]], sem)

    def issue(g, carry):
        for u in range(ROW_UNROLL):
            for k in range(TOP_K):
                row_copy(g * ROW_UNROLL + u, k).start(priority=k % 2)
        return carry

    lax.fori_loop(0, ts // ROW_UNROLL, issue, 0)
    for k in range(TOP_K):
        pltpu.make_async_copy(x_ref, xs_ref.at[pl.ds(0, ts)], sem).wait()


def _dispatch(x_tiles, dest, pstart, counts, n_rows, ts):
    n = x_tiles.shape[0]
    tile = (PACK_TILE, LANES)
    grid_spec = pltpu.PrefetchScalarGridSpec(
        num_scalar_prefetch=2,
        grid=(n // ts,),
        in_specs=[pl.BlockSpec((TOP_K, ts), lambda i, ps, cn: (0, i), memory_space=pltpu.SMEM),
                  pl.BlockSpec((ts,) + tile, lambda i, ps, cn: (i, 0, 0))],
        out_specs=pl.BlockSpec(memory_space=pl.ANY),
        scratch_shapes=[pltpu.VMEM((PAD_PIECES[0],) + tile, U32),
                        pltpu.SemaphoreType.DMA, pltpu.SemaphoreType.DMA],
    )
    return pl.pallas_call(
        _dispatch_body,
        name="dispatch",
        grid_spec=grid_spec,
        out_shape=jax.ShapeDtypeStruct((n_rows,) + tile, U32),
        compiler_params=_params(("arbitrary",)),
    )(pstart, counts, dest, x_tiles)


def _expert_body(be_ref, nb_ref, xs_ref, wg_ref, wu_ref, wd_ref, ys_ref):
    del be_ref
    i = pl.program_id(0)

    @pl.when(i < nb_ref[0])
    def _():
        x = _load_packed_tiles(xs_ref, EXPERT_BLOCK)
        h = (_silu(_dot(x, wg_ref[...])) * _dot(x, wu_ref[...])).astype(BF16)
        _store_row_tiles(ys_ref, _dot(h, wd_ref[...]))

    @pl.when(i >= nb_ref[0])
    def _():
        ys_ref[...] = jnp.zeros(ys_ref.shape, F32)


def _experts(xs, block_e, nb_used, wg, wu, wd):
    rows = xs.shape[0] // PACK_TILE
    n_blocks = rows // EXPERT_BLOCK
    in_rows = pl.BlockSpec((EXPERT_BLOCK * PACK_TILE, LANES),
                           lambda i, be, nb: (jnp.minimum(i, nb[0] - 1), 0))
    grid_spec = pltpu.PrefetchScalarGridSpec(
        num_scalar_prefetch=2,
        grid=(n_blocks,),
        in_specs=[in_rows,
                  pl.BlockSpec((None, D_MODEL, D_EXPERT), lambda i, be, nb: (be[i], 0, 0)),
                  pl.BlockSpec((None, D_MODEL, D_EXPERT), lambda i, be, nb: (be[i], 0, 0)),
                  pl.BlockSpec((None, D_EXPERT, D_MODEL), lambda i, be, nb: (be[i], 0, 0))],
        out_specs=pl.BlockSpec((EXPERT_BLOCK * ROW_TILE, LANES), lambda i, be, nb: (i, 0)),
    )
    return pl.pallas_call(
        _expert_body,
        name="experts",
        grid_spec=grid_spec,
        out_shape=jax.ShapeDtypeStruct((rows * ROW_TILE, LANES), F32),
        compiler_params=_params(("arbitrary",)),
    )(block_e, nb_used, xs, wg, wu, wd)


def _combine_body(dest_ref, dnext_ref, wts_ref, x1_ref, ys_ref, wsg, wsu, wsd, g_ref, b_ref,
                  y_ref, buf, sems):
    i = pl.program_id(0)
    n_tiles = pl.num_programs(0)
    tc = x1_ref.shape[0]
    slot = i % 2

    def issue(d_ref, s, g, carry):
        for u in range(ROW_UNROLL):
            t = g * ROW_UNROLL + u
            for k in range(TOP_K):
                pltpu.make_async_copy(_token_tile(ys_ref, d_ref[k, t]),
                                      _token_tile(buf.at[s, k], t),
                                      sems.at[s]).start(priority=k % 2)
        return carry

    n_groups = tc // ROW_UNROLL

    @pl.when(i == 0)
    def _():
        lax.fori_loop(0, n_groups, functools.partial(issue, dest_ref, slot), 0)

    @pl.when(i + 1 < n_tiles)
    def _():
        lax.fori_loop(0, n_groups, functools.partial(issue, dnext_ref, 1 - slot), 0)

    x1 = x1_ref[...]
    xb = x1.astype(BF16)
    hs = (_silu(_dot(xb, wsg[...])) * _dot(xb, wsu[...])).astype(BF16)
    shared = _dot(hs, wsd[...])

    for k in range(TOP_K):
        pltpu.make_async_copy(ys_ref.at[pl.ds(0, tc * ROW_TILE)], buf.at[slot, k],
                              sems.at[slot]).wait()

    w = wts_ref[...]
    routed = w[:, 0:1] * _load_row_tiles(buf.at[slot, 0], tc)
    for k in range(1, TOP_K):
        routed = routed + w[:, k:k + 1] * _load_row_tiles(buf.at[slot, k], tc)
    y_ref[...] = _layer_norm(DEEPNORM_ALPHA * x1 + (routed + shared), g_ref[...], b_ref[...])


def _combine(dest, wts_t, x1, ys, wsg, wsu, wsd, ln_g, ln_b, tc):
    n = x1.shape[0]
    last = n // tc - 1
    dest_cur = pl.BlockSpec((TOP_K, tc), lambda i: (0, i), memory_space=pltpu.SMEM)
    dest_next = pl.BlockSpec((TOP_K, tc), lambda i: (0, jnp.minimum(i + 1, last)),
                             memory_space=pltpu.SMEM)
    full = lambda a: pl.BlockSpec(a.shape, lambda i: (0, 0))
    rows = lambda w: pl.BlockSpec((tc, w), lambda i: (i, 0))
    return pl.pallas_call(
        _combine_body,
        name="combine",
        grid=(n // tc,),
        in_specs=[dest_cur, dest_next, rows(LANES), rows(D_MODEL),
                  pl.BlockSpec(memory_space=pl.ANY),
                  full(wsg), full(wsu), full(wsd), full(ln_g), full(ln_b)],
        out_specs=rows(D_MODEL),
        out_shape=jax.ShapeDtypeStruct((n, D_MODEL), F32),
        scratch_shapes=[pltpu.VMEM((2, TOP_K, tc * ROW_TILE, LANES), F32),
                        pltpu.SemaphoreType.DMA((2,))],
        compiler_params=_params(("arbitrary",)),
    )(dest, dest, wts_t, x1, ys, wsg, wsu, wsd, ln_g, ln_b)


def _moe(x1, x_tiles, idx, wts_t, rank, cnt, moe_w, ln_g, ln_b, tile):
    n = x1.shape[0]
    wg, wu, wd, wsg, wsu, wsd = moe_w
    blk = EXPERT_BLOCK
    n_blocks = -(-(n * TOP_K) // blk) + N_EXPERTS
    counts = cnt[:, 0].astype(I32)
    padded = (counts + blk - 1) // blk * blk
    ends = jnp.cumsum(padded)
    pstart = ends - padded
    nb_used = (ends[-1] // blk).astype(I32).reshape(1)
    block_row0 = jnp.arange(n_blocks, dtype=I32) * blk
    block_e = jnp.minimum(jnp.sum((ends[None, :] <= block_row0[:, None]).astype(I32), axis=1),
                          N_EXPERTS - 1).astype(I32)
    dest = _dest(idx, rank, pstart.astype(F32).reshape(N_EXPERTS, 1), tile)
    xs = _dispatch(x_tiles.reshape(n, PACK_TILE, LANES), dest, pstart, counts, n_blocks * blk,
                   tile)
    ys = _experts(xs.reshape(n_blocks * blk * PACK_TILE, LANES), block_e, nb_used, wg, wu, wd)
    return _combine(dest, wts_t, x1, ys, wsg, wsu, wsd, ln_g, ln_b, min(tile, 128))


def kernel(x_prompt, x_sample, cache_k, cache_v, state_conv, page_table, w_in, lam_q1, lam_k1,
           lam_q2, lam_k2, subln_g, w_dw, b_dw, conv_ln_g, conv_ln_b, w_o, ln1_g, ln1_b,
           w_router, b_router, w_gate, w_up, w_down, w_sh_gate, w_sh_up, w_sh_down, ln2_g,
           ln2_b):
    batch, seq, _ = x_prompt.shape
    dec_b, dec_s, _ = x_sample.shape
    n_p, n_s = batch * seq, dec_b * dec_s
    layer = 0
    lam_init = _lambda_init(layer)
    row = lambda a: a[layer].reshape(1, -1)

    w_in_b = w_in[layer].astype(BF16)
    wo = w_o[layer].astype(BF16)
    wo_a, wo_c = wo[:ATTN_WIDTH], wo[ATTN_WIDTH:]
    wr_t = w_router[layer].T
    wr_hi = wr_t.astype(BF16)
    wr_lo = (wr_t - wr_hi.astype(F32)).astype(BF16)
    br = b_router[layer].reshape(-1, 1)
    moe_w = tuple(w[layer].astype(BF16)
                  for w in (w_gate, w_up, w_down, w_sh_gate, w_sh_up, w_sh_down))
    lam_vecs = (row(lam_q1), row(lam_k1), row(lam_q2), row(lam_k2))
    g_sub = row(subln_g)
    conv_w = (w_dw[layer], row(b_dw), row(conv_ln_g), row(conv_ln_b))
    ln1 = (row(ln1_g), row(ln1_b))
    ln2 = (row(ln2_g), row(ln2_b))

    xp = x_prompt.reshape(n_p, D_MODEL)
    wq_t = w_in_b[:, :QK_WIDTH].T
    wv_t = w_in_b[:, 2 * QK_WIDTH:2 * QK_WIDTH + ATTN_WIDTH].T
    qt_p, k_p, v_p, u_p, kb_p, vt_p = _inproj(xp, w_in_b, 512, (wq_t, wv_t))
    o_p = _attn_prompt(qt_p, kb_p, vt_p, lam_vecs, g_sub.reshape(V_DIM, 1), lam_init, batch, seq)
    c_p = _conv_prompt(u_p, *conv_w, batch, seq, 512)
    routed_p = _mix(o_p, c_p, xp, wo_a, wo_c, *ln1, wr_hi, wr_lo, br, 512)
    y_p = _moe(*routed_p, moe_w, *ln2, 256)

    xs = x_sample.reshape(n_s, D_MODEL)
    q_s, k_s, v_s, u_s = _inproj(xs, w_in_b, n_s)
    pool = cache_k.shape[1]
    page_rows = lambda a: a[layer].reshape(pool, PAGE_ROWS, V_DIM)
    new_rows = lambda a: a.reshape(dec_b, dec_s * N_HEADS, V_DIM)
    o_s = _attn_sample(q_s.reshape(dec_b, dec_s, QK_WIDTH), new_rows(k_s), new_rows(v_s),
                       page_rows(cache_k), page_rows(cache_v), page_table, lam_vecs, g_sub,
                       lam_init)
    c_s, st_s = _conv_sample(u_s.reshape(dec_b, dec_s, CONV_CH), state_conv[layer], *conv_w)
    routed_s = _mix(o_s.reshape(n_s, ATTN_WIDTH), c_s.reshape(n_s, CONV_CH), xs, wo_a, wo_c,
                    *ln1, wr_hi, wr_lo, br, n_s)
    y_s = _moe(*routed_s, moe_w, *ln2, 128)

    keep = CONV_WIDTH - 1
    u_p3 = u_p.reshape(batch, seq, CONV_CH)
    return (y_p.reshape(batch, seq, D_MODEL),
            y_s.reshape(dec_b, dec_s, D_MODEL),
            k_p.reshape(1, batch, seq, N_HEADS, V_DIM),
            v_p.reshape(1, batch, seq, N_HEADS, V_DIM),
            u_p3[:, seq - keep:, :][None],
            k_s.reshape(1, dec_b, dec_s, N_HEADS, V_DIM),
            v_s.reshape(1, dec_b, dec_s, N_HEADS, V_DIM),
            st_s[None])
```

```python
import functools
import math

import jax
import jax.numpy as jnp
from jax import lax
from jax.experimental import pallas as pl
from jax.experimental.pallas import tpu as pltpu

F32 = jnp.float32
BF16 = jnp.bfloat16
I32 = jnp.int32
U32 = jnp.uint32

D_MODEL = 1024
N_HEADS = 4
HEAD_DIM = 64
V_DIM = 128
ATTN_WIDTH = N_HEADS * V_DIM
QK_WIDTH = N_HEADS * 2 * HEAD_DIM
ATTN_SCALE = HEAD_DIM ** -0.5
CONV_CH = D_MODEL - ATTN_WIDTH
CONV_WIDTH = 31
N_EXPERTS = 64
N_GROUPS = 8
GROUP_SIZE = N_EXPERTS // N_GROUPS
TOPK_GROUPS = 4
TOP_K = 8
D_EXPERT = D_MODEL // 4
ROUTED_SCALE = 2.5
DEPTH = 1
DEEPNORM_ALPHA = (2 * DEPTH) ** 0.25
NORM_EPS = 1e-5
NEG_INF = -1e30
PAGE_SIZE = 128

LANES = 128
ROW_TILE = D_MODEL // LANES
PACK_TILE = ROW_TILE // 2
VMEM_LIMIT = 48 * 1024 * 1024
PAGES_PER_STEP = 16
PAGE_ROWS = PAGE_SIZE * N_HEADS
EXPERT_BLOCK = 512
HIST = 32


def _lambda_init(layer):
    return 0.8 - 0.6 * math.exp(-0.3 * layer)


def _sigmoid(x):
    return 1.0 / (1.0 + jnp.exp(-x))


def _silu(x):
    return x * _sigmoid(x)


def _layer_norm(x, g, b):
    mu = jnp.mean(x, axis=-1, keepdims=True)
    xc = x - mu
    var = jnp.mean(xc * xc, axis=-1, keepdims=True)
    return xc * lax.rsqrt(var + NORM_EPS) * g + b


def _dot(a, b):
    return jnp.dot(a, b, preferred_element_type=F32)


def _dot_nt(a, b):
    return lax.dot_general(a, b, (((1,), (1,)), ((), ())), preferred_element_type=F32)


def _diff_lambda(lq1, lk1, lq2, lk2, lam_init):
    a = jnp.exp(jnp.sum(lq1 * lk1, axis=-1, keepdims=True))
    b = jnp.exp(jnp.sum(lq2 * lk2, axis=-1, keepdims=True))
    return a - b + lam_init


def _params(dims):
    return pltpu.CompilerParams(dimension_semantics=dims, vmem_limit_bytes=VMEM_LIMIT)


def _store_row_tiles(ref, x):
    m = x.shape[0]
    for j in range(ROW_TILE):
        ref[pl.ds(j, m, stride=ROW_TILE), :] = x[:, j * LANES:(j + 1) * LANES]


def _load_row_tiles(ref, m):
    return jnp.concatenate([ref[pl.ds(j, m, stride=ROW_TILE), :] for j in range(ROW_TILE)],
                           axis=1)


def _token_tile(ref, t):
    return ref.at[pl.ds(pl.multiple_of(t * ROW_TILE, ROW_TILE), ROW_TILE)]


def _store_packed_tiles(ref, x):
    m = x.shape[0]
    bits = pltpu.bitcast(x.astype(BF16).astype(F32), U32)
    half = D_MODEL // 2
    for j in range(PACK_TILE):
        lo = bits[:, j * LANES:(j + 1) * LANES] >> 16
        hi = bits[:, half + j * LANES:half + (j + 1) * LANES]
        ref[pl.ds(j, m, stride=PACK_TILE), :] = lo | hi


def _load_packed_tiles(ref, m):
    words = [ref[pl.ds(j, m, stride=PACK_TILE), :] for j in range(PACK_TILE)]
    lo = [pltpu.bitcast(w << 16, F32) for w in words]
    hi = [pltpu.bitcast(w & jnp.uint32(0xFFFF0000), F32) for w in words]
    return jnp.concatenate(lo + hi, axis=1).astype(BF16)


def _store_head_rows(ref, x):
    m = x.shape[0]
    for h in range(N_HEADS):
        ref[pl.ds(h, m, stride=N_HEADS), :] = x[:, h * V_DIM:(h + 1) * V_DIM]


def _inproj_body(x_ref, w_ref, *refs):
    xb = x_ref[...].astype(BF16)

    def mm(c0):
        return _dot(xb, w_ref[:, c0:c0 + QK_WIDTH])

    k = mm(QK_WIDTH)
    if len(refs) == 4:
        q_ref, k_ref, v_ref, u_ref = refs
        q_ref[...] = mm(0) * ATTN_SCALE
    else:
        wqt_ref, wvt_ref, q_ref, k_ref, v_ref, u_ref, kb_ref, vt_ref = refs
        q_ref[...] = _dot_nt(wqt_ref[...], xb) * ATTN_SCALE
        vt_ref[...] = _dot_nt(wvt_ref[...], xb).astype(BF16)
        kb_ref[...] = k.astype(BF16)
    _store_head_rows(k_ref, k)
    _store_head_rows(v_ref, mm(2 * QK_WIDTH))
    a = mm(2 * QK_WIDTH + ATTN_WIDTH)
    b = mm(2 * QK_WIDTH + ATTN_WIDTH + CONV_CH)
    u_ref[...] = a * _sigmoid(b)


def _inproj(x, w_in_b, tm, transposed_w=()):
    n = x.shape[0]
    wide = QK_WIDTH
    blk = pl.BlockSpec((tm, wide), lambda i: (i, 0))
    hblk = pl.BlockSpec((tm * N_HEADS, V_DIM), lambda i: (i, 0))
    full = lambda a: pl.BlockSpec(a.shape, lambda i: (0, 0))
    row_major = jax.ShapeDtypeStruct((n, wide), F32)
    head_rows = jax.ShapeDtypeStruct((n * N_HEADS, V_DIM), F32)
    out_specs = [blk, hblk, hblk, blk]
    out_shape = [row_major, head_rows, head_rows, row_major]
    if transposed_w:
        t_blk = pl.BlockSpec((None, wide, tm), lambda i: (i, 0, 0))
        out_specs = [t_blk, hblk, hblk, blk, blk, t_blk]
        out_shape = [jax.ShapeDtypeStruct((n // tm, wide, tm), F32), head_rows, head_rows,
                     row_major, jax.ShapeDtypeStruct((n, wide), BF16),
                     jax.ShapeDtypeStruct((n // tm, wide, tm), BF16)]
    return pl.pallas_call(
        _inproj_body,
        name="inproj",
        grid=(n // tm,),
        in_specs=[pl.BlockSpec((tm, D_MODEL), lambda i: (i, 0)), full(w_in_b)]
        + [full(w) for w in transposed_w],
        out_specs=out_specs,
        out_shape=out_shape,
        compiler_params=_params(("parallel",)),
    )(x, w_in_b, *transposed_w)


def _attn_p_body(lam_init, q_ref, k_ref, v_ref, lq1, lk1, lq2, lk2, g_ref, o_ref, acc_ref):
    qi = pl.program_id(2)
    tq = q_ref.shape[1]
    tk = v_ref.shape[2]
    q_t = q_ref[...]
    feat = lax.broadcasted_iota(I32, q_t.shape, 0)
    q_maps = (jnp.where(feat < HEAD_DIM, q_t, 0.0).astype(BF16),
              jnp.where(feat >= HEAD_DIM, q_t, 0.0).astype(BF16))
    acc_ref[...] = jnp.zeros(acc_ref.shape, F32)

    def chunk(masked, j, carry):
        kc = k_ref[pl.ds(pl.multiple_of(j * tk, tk), tk), :]
        vc = v_ref[j]
        if masked:
            key = lax.broadcasted_iota(I32, (tk, tq), 0)
            qry = lax.broadcasted_iota(I32, (tk, tq), 1)
            keep = key <= qry
        out = []
        for a in range(2):
            m_old, l_old = carry[2 * a], carry[2 * a + 1]
            s_t = _dot(kc, q_maps[a])
            if masked:
                s_t = jnp.where(keep, s_t, NEG_INF)
            m_new = jnp.maximum(m_old, jnp.max(s_t, axis=0, keepdims=True))
            alpha = jnp.exp(m_old - m_new)
            p_t = jnp.exp(s_t - m_new)
            out.append(m_new)
            out.append(alpha * l_old + jnp.sum(p_t, axis=0, keepdims=True))
            acc_ref[a] = alpha * acc_ref[a] + _dot(vc, p_t.astype(BF16))
        return tuple(out)

    neg = jnp.full((1, tq), -jnp.inf, F32)
    zero = jnp.zeros((1, tq), F32)
    carry = lax.fori_loop(0, qi, functools.partial(chunk, False), (neg, zero, neg, zero))
    _, l0, _, l1 = chunk(True, qi, carry)

    lam = _diff_lambda(lq1[...], lk1[...], lq2[...], lk2[...], lam_init)
    o_t = acc_ref[0] / l0 - lam * (acc_ref[1] / l1)
    ms = jnp.mean(o_t * o_t, axis=0, keepdims=True)
    o_t = o_t * lax.rsqrt(ms + NORM_EPS) * g_ref[...] * (1.0 - lam_init)
    o_ref[...] = o_t.T


def _attn_prompt(q_t, kb, v_t, lam_vecs, subln_g_col, lam_init, batch, seq):
    t = q_t.shape[2]
    n = kb.shape[0]
    nq = seq // t
    small = lambda w: pl.BlockSpec((1, w), lambda b, h, i: (0, 0))
    return pl.pallas_call(
        functools.partial(_attn_p_body, lam_init),
        name="attn_prompt",
        grid=(batch, N_HEADS, nq),
        in_specs=[pl.BlockSpec((None, V_DIM, t), lambda b, h, i: (b * nq + i, h, 0)),
                  pl.BlockSpec((seq, V_DIM), lambda b, h, i: (b, h)),
                  pl.BlockSpec((nq, V_DIM, t), lambda b, h, i: (b, h, 0))]
        + [small(HEAD_DIM)] * 4 + [pl.BlockSpec((V_DIM, 1), lambda b, h, i: (0, 0))],
        out_specs=pl.BlockSpec((t, V_DIM), lambda b, h, i: (b * nq + i, h)),
        out_shape=jax.ShapeDtypeStruct((n, ATTN_WIDTH), F32),
        scratch_shapes=[pltpu.VMEM((2, V_DIM, t), F32)],
        compiler_params=_params(("parallel", "parallel", "parallel")),
    )(q_t, kb, v_t, *lam_vecs, subln_g_col)


def _attn_s_body(lam_init, n_steps, pt_ref, q_ref, kn_ref, vn_ref, *rest):
    npg = PAGES_PER_STEP
    k_refs = rest[:npg]
    v_refs = rest[npg:2 * npg]
    (lq1, lk1, lq2, lk2, g_ref, o_ref, qall, knew, vnew, m_ref, l_ref,
     acc_ref) = rest[2 * npg:]
    j = pl.program_id(1)
    ds = q_ref.shape[0]
    hr = 2 * ds

    def head_rows(ref, h):
        return ref[pl.ds(h, PAGE_SIZE, stride=N_HEADS), :].astype(BF16)

    @pl.when(j == 0)
    def _():
        pieces = []
        for h in range(N_HEADS):
            qh = q_ref[:, h * V_DIM:(h + 1) * V_DIM]
            lane = lax.broadcasted_iota(I32, qh.shape, 1)
            pieces.append(jnp.where(lane < HEAD_DIM, qh, 0.0))
            pieces.append(jnp.where(lane >= HEAD_DIM, qh, 0.0))
        qall[...] = jnp.concatenate(pieces, axis=0).astype(BF16)
        knew[...] = jnp.zeros(knew.shape, F32)
        vnew[...] = jnp.zeros(vnew.shape, F32)
        knew[0:ds * N_HEADS] = kn_ref[...]
        vnew[0:ds * N_HEADS] = vn_ref[...]
        row = lax.broadcasted_iota(I32, (hr, PAGE_SIZE), 0)
        key = lax.broadcasted_iota(I32, (hr, PAGE_SIZE), 1)
        keep = key <= (row % ds)
        for h in range(N_HEADS):
            rows = slice(h * hr, (h + 1) * hr)
            s = jnp.where(keep, _dot_nt(qall[rows], head_rows(knew, h)), NEG_INF)
            m = jnp.max(s, axis=-1, keepdims=True)
            p = jnp.exp(s - m)
            m_ref[rows] = m
            l_ref[rows] = jnp.sum(p, axis=-1, keepdims=True)
            acc_ref[rows] = _dot(p.astype(BF16), head_rows(vnew, h))

    q_all, m_all, l_all, acc_all = qall[...], m_ref[...], l_ref[...], acc_ref[...]
    heads = range(N_HEADS)
    rows = [slice(h * hr, (h + 1) * hr) for h in heads]
    s = [_dot_nt(q_all[rows[h]],
                 jnp.concatenate([head_rows(k_refs[i], h) for i in range(npg)], axis=0))
         for h in heads]
    s = jnp.concatenate(s, axis=0)
    m_new = jnp.maximum(m_all, jnp.max(s, axis=-1, keepdims=True))
    alpha = jnp.exp(m_all - m_new)
    p = jnp.exp(s - m_new)
    l_new = alpha * l_all + jnp.sum(p, axis=-1, keepdims=True)
    pb = p.astype(BF16)
    pv = [_dot(pb[rows[h]],
               jnp.concatenate([head_rows(v_refs[i], h) for i in range(npg)], axis=0))
          for h in heads]
    acc_new = alpha * acc_all + jnp.concatenate(pv, axis=0)
    m_ref[...] = m_new
    l_ref[...] = l_new
    acc_ref[...] = acc_new

    @pl.when(j == n_steps - 1)
    def _():
        lam = _diff_lambda(lq1[...], lk1[...], lq2[...], lk2[...], lam_init)
        on = acc_new / l_new
        for h in range(N_HEADS):
            r0 = h * hr
            o = on[r0:r0 + ds] - lam * on[r0 + ds:r0 + 2 * ds]
            ms = jnp.mean(o * o, axis=-1, keepdims=True)
            o_ref[:, h * V_DIM:(h + 1) * V_DIM] = (
                o * lax.rsqrt(ms + NORM_EPS) * g_ref[...] * (1.0 - lam_init))


def _attn_sample(q, k_new, v_new, cache_k, cache_v, page_table, lam_vecs, subln_g, lam_init):
    b, ds, _ = q.shape
    n_pages = page_table.shape[1]
    n_steps = n_pages // PAGES_PER_STEP
    nrow = N_HEADS * 2 * ds
    pt = page_table.reshape(-1)

    def page_map(i, bi, j, pt_ref):
        return (pt_ref[bi * n_pages + j * PAGES_PER_STEP + i], 0, 0)

    page_specs = [pl.BlockSpec((None, PAGE_ROWS, V_DIM), functools.partial(page_map, i))
                  for i in range(PAGES_PER_STEP)]
    small = lambda w: pl.BlockSpec((1, w), lambda bi, j, pt_ref: (0, 0))
    new_spec = pl.BlockSpec((None, ds * N_HEADS, V_DIM), lambda bi, j, pt_ref: (bi, 0, 0))
    qo_spec = pl.BlockSpec((None, ds, ATTN_WIDTH), lambda bi, j, pt_ref: (bi, 0, 0))
    grid_spec = pltpu.PrefetchScalarGridSpec(
        num_scalar_prefetch=1,
        grid=(b, n_steps),
        in_specs=[qo_spec, new_spec, new_spec] + page_specs + page_specs
        + [small(HEAD_DIM)] * 4 + [small(V_DIM)],
        out_specs=qo_spec,
        scratch_shapes=[pltpu.VMEM((nrow, V_DIM), BF16),
                        pltpu.VMEM((PAGE_ROWS, V_DIM), F32),
                        pltpu.VMEM((PAGE_ROWS, V_DIM), F32),
                        pltpu.VMEM((nrow, 1), F32),
                        pltpu.VMEM((nrow, 1), F32),
                        pltpu.VMEM((nrow, V_DIM), F32)],
    )
    return pl.pallas_call(
        functools.partial(_attn_s_body, lam_init, n_steps),
        name="attn_sample",
        grid_spec=grid_spec,
        out_shape=jax.ShapeDtypeStruct((b, ds, ATTN_WIDTH), F32),
        compiler_params=_params(("parallel", "arbitrary")),
    )(pt, q, k_new, v_new, *([cache_k] * PAGES_PER_STEP), *([cache_v] * PAGES_PER_STEP),
      *lam_vecs, subln_g)


def _conv_taps(ext_ref, w_ref, bias, rows):
    off = HIST - (CONV_WIDTH - 1)
    acc = jnp.broadcast_to(bias, (rows, CONV_CH))
    for jt in range(CONV_WIDTH):
        acc = acc + w_ref[jt:jt + 1, :] * ext_ref[pl.ds(jt + off, rows), :]
    return acc


CONV_ROWS = 64
SUBLANES = 8


def _conv_p_body(u_ref, h_ref, w_ref, b_ref, g_ref, beta_ref, c_ref, ext_ref, sh_ref):
    t = pl.program_id(1)
    tm = u_ref.shape[0]
    hist = h_ref[...]
    ext_ref[0:HIST] = jnp.where(t == 0, jnp.zeros_like(hist), hist)
    ext_ref[HIST:HIST + tm] = u_ref[...]
    span = sh_ref.shape[1]
    for r in range(1, SUBLANES):
        sh_ref[r - 1] = ext_ref[pl.ds(r, span), :]
    off = HIST - (CONV_WIDTH - 1)
    bias = b_ref[...]
    for c0 in range(0, tm, CONV_ROWS):
        acc = jnp.broadcast_to(bias, (CONV_ROWS, CONV_CH))
        for jt in range(CONV_WIDTH):
            r = (jt + off) % SUBLANES
            base = c0 + jt + off - r
            src = ext_ref if r == 0 else sh_ref.at[r - 1]
            acc = acc + w_ref[jt:jt + 1, :] * src[pl.ds(base, CONV_ROWS), :]
        c_ref[c0:c0 + CONV_ROWS, :] = _silu(_layer_norm(acc, g_ref[...], beta_ref[...]))


def _conv_prompt(u, w_dw, b_dw, g, beta, batch, seq, tm):
    n = u.shape[0]
    nt = seq // tm
    per = tm // HIST
    cur = lambda b, t: (b * nt + t, 0)
    prev = lambda b, t: (jnp.maximum((b * nt + t) * per - 1, 0), 0)
    small = lambda r: pl.BlockSpec((r, CONV_CH), lambda b, t: (0, 0))
    return pl.pallas_call(
        _conv_p_body,
        name="conv_prompt",
        grid=(batch, nt),
        in_specs=[pl.BlockSpec((tm, CONV_CH), cur), pl.BlockSpec((HIST, CONV_CH), prev),
                  small(CONV_WIDTH), small(1), small(1), small(1)],
        out_specs=pl.BlockSpec((tm, CONV_CH), cur),
        out_shape=jax.ShapeDtypeStruct((n, CONV_CH), F32),
        scratch_shapes=[pltpu.VMEM((HIST + tm, CONV_CH), F32),
                        pltpu.VMEM((SUBLANES - 1, HIST + tm - SUBLANES, CONV_CH), F32)],
        compiler_params=_params(("parallel", "parallel")),
    )(u, u, w_dw, b_dw, g, beta)


def _conv_s_body(u_ref, st_ref, w_ref, b_ref, g_ref, beta_ref, c_ref, ns_ref, ext_ref):
    ds = u_ref.shape[0]
    keep = CONV_WIDTH - 1
    off = HIST - keep
    ext_ref[0:off] = jnp.zeros((off, CONV_CH), F32)
    ext_ref[off:HIST] = st_ref[...]
    ext_ref[HIST:HIST + ds] = u_ref[...]
    y = _conv_taps(ext_ref, w_ref, b_ref[...], ds)
    c_ref[...] = _silu(_layer_norm(y, g_ref[...], beta_ref[...]))
    ns_ref[...] = ext_ref[HIST + ds - keep:HIST + ds]


def _conv_sample(u, state, w_dw, b_dw, g, beta):
    b, ds, _ = u.shape
    keep = CONV_WIDTH - 1
    per_b = lambda r: pl.BlockSpec((None, r, CONV_CH), lambda i: (i, 0, 0))
    small = lambda r: pl.BlockSpec((r, CONV_CH), lambda i: (0, 0))
    return pl.pallas_call(
        _conv_s_body,
        name="conv_sample",
        grid=(b,),
        in_specs=[per_b(ds), per_b(keep), small(CONV_WIDTH), small(1), small(1), small(1)],
        out_specs=[per_b(ds), per_b(keep)],
        out_shape=[jax.ShapeDtypeStruct((b, ds, CONV_CH), F32),
                   jax.ShapeDtypeStruct((b, keep, CONV_CH), F32)],
        scratch_shapes=[pltpu.VMEM((HIST + ds, CONV_CH), F32)],
        compiler_params=_params(("parallel",)),
    )(u, state, w_dw, b_dw, g, beta)


def _first_index(hit, iota, limit, axis):
    return jnp.min(jnp.where(hit, iota, limit), axis=axis, keepdims=True)


def _route(x1, wr_hi, wr_lo, bias):
    tm = x1.shape[0]
    x_hi = x1.astype(BF16)
    x_lo = (x1 - x_hi.astype(F32)).astype(BF16)
    logits = _dot_nt(wr_hi, x_hi) + (_dot_nt(wr_lo, x_hi) + _dot_nt(wr_hi, x_lo))
    scores = _sigmoid(logits)
    sel = scores + bias
    sel3 = sel.reshape(N_GROUPS, GROUP_SIZE, tm)
    member = lax.broadcasted_iota(I32, sel3.shape, 1)
    m1 = jnp.max(sel3, axis=1, keepdims=True)
    i1 = _first_index(sel3 == m1, member, GROUP_SIZE, 1)
    m2 = jnp.max(jnp.where(member == i1, -jnp.inf, sel3), axis=1, keepdims=True)
    gs = jnp.broadcast_to(m1 + m2, sel3.shape).reshape(N_EXPERTS, tm)
    eiota = lax.broadcasted_iota(I32, (N_EXPERTS, tm), 0)
    giota = eiota // GROUP_SIZE
    gmask = jnp.zeros((N_EXPERTS, tm), jnp.bool_)
    for _ in range(TOPK_GROUPS):
        m = jnp.max(gs, axis=0, keepdims=True)
        gi = _first_index(gs == m, giota, N_GROUPS, 0)
        pick = giota == gi
        gmask = jnp.logical_or(gmask, pick)
        gs = jnp.where(pick, -jnp.inf, gs)
    selm = jnp.where(gmask, sel, NEG_INF)
    idx_rows, w_rows = [], []
    for _ in range(TOP_K):
        m = jnp.max(selm, axis=0, keepdims=True)
        ei = _first_index(selm == m, eiota, N_EXPERTS, 0)
        pick = eiota == ei
        idx_rows.append(ei)
        w_rows.append(jnp.sum(jnp.where(pick, scores, 0.0), axis=0, keepdims=True))
        selm = jnp.where(pick, -jnp.inf, selm)
    idx = jnp.concatenate(idx_rows, axis=0)
    w = jnp.concatenate(w_rows, axis=0)
    w = w / jnp.sum(w, axis=0, keepdims=True) * ROUTED_SCALE
    return idx, w


def _mix_body(o_ref, c_ref, x_ref, wo_a, wo_c, g_ref, b_ref, wrh_ref, wrl_ref, br_ref,
              x1_ref, xrt_ref, idx_ref, wts_ref, rank_ref, cnt_ref, base_ref):
    i = pl.program_id(0)
    tm = x_ref.shape[0]

    @pl.when(i == 0)
    def _():
        base_ref[...] = jnp.zeros(base_ref.shape, F32)

    mix = _dot(o_ref[...].astype(BF16), wo_a[...]) + _dot(c_ref[...].astype(BF16), wo_c[...])
    x1 = _layer_norm(DEEPNORM_ALPHA * x_ref[...] + mix, g_ref[...], b_ref[...])
    x1_ref[...] = x1
    _store_packed_tiles(xrt_ref, x1)

    idx, w = _route(x1, wrh_ref[...], wrl_ref[...], br_ref[...])
    idx_ref[...] = idx
    w_rows = jnp.concatenate([w, jnp.zeros((LANES - TOP_K, tm), F32)], axis=0)
    wts_ref[...] = w_rows.T

    eiota = lax.broadcasted_iota(I32, (N_EXPERTS, tm), 0)
    chosen = jnp.zeros((N_EXPERTS, tm), F32)
    for k in range(TOP_K):
        chosen = chosen + jnp.where(eiota == idx[k:k + 1, :], 1.0, 0.0)
    r = lax.broadcasted_iota(I32, (tm, tm), 0)
    c = lax.broadcasted_iota(I32, (tm, tm), 1)
    before = jnp.where(r < c, 1.0, 0.0).astype(BF16)
    base = base_ref[:, 0:1]
    rank_full = _dot(chosen.astype(BF16), before) + base
    rows = [jnp.sum(jnp.where(eiota == idx[k:k + 1, :], rank_full, 0.0), axis=0, keepdims=True)
            for k in range(TOP_K)]
    rank_ref[...] = jnp.concatenate(rows, axis=0).astype(I32)
    total = base + jnp.sum(chosen, axis=1, keepdims=True)
    base_ref[...] = jnp.broadcast_to(total, base_ref.shape)
    cnt_ref[...] = jnp.broadcast_to(total, cnt_ref.shape)


def _mix(o, c, x, wo_a, wo_c, ln_g, ln_b, wr_hi, wr_lo, b_router, tm):
    n = x.shape[0]
    rows = lambda w: pl.BlockSpec((tm, w), lambda i: (i, 0))
    full = lambda a: pl.BlockSpec(a.shape, lambda i: (0, 0))
    cols = pl.BlockSpec((TOP_K, tm), lambda i: (0, i))
    cnt = pl.BlockSpec((N_EXPERTS, LANES), lambda i: (0, 0))
    return pl.pallas_call(
        _mix_body,
        name="mix",
        grid=(n // tm,),
        in_specs=[rows(ATTN_WIDTH), rows(CONV_CH), rows(D_MODEL), full(wo_a), full(wo_c),
                  full(ln_g), full(ln_b), full(wr_hi), full(wr_lo), full(b_router)],
        out_specs=[rows(D_MODEL), pl.BlockSpec((tm * PACK_TILE, LANES), lambda i: (i, 0)), cols,
                   rows(LANES), cols, cnt],
        out_shape=[jax.ShapeDtypeStruct((n, D_MODEL), F32),
                   jax.ShapeDtypeStruct((n * PACK_TILE, LANES), U32),
                   jax.ShapeDtypeStruct((TOP_K, n), I32),
                   jax.ShapeDtypeStruct((n, LANES), F32),
                   jax.ShapeDtypeStruct((TOP_K, n), I32),
                   jax.ShapeDtypeStruct((N_EXPERTS, LANES), F32)],
        scratch_shapes=[pltpu.VMEM((N_EXPERTS, LANES), F32)],
        compiler_params=_params(("arbitrary",)),
    )(o, c, x, wo_a, wo_c, ln_g, ln_b, wr_hi, wr_lo, b_router)


def _dest_body(idx_ref, rank_ref, ps_ref, dest_ref):
    idx = idx_ref[...]
    tm = idx.shape[1]
    eiota = lax.broadcasted_iota(I32, (N_EXPERTS, tm), 0)
    ps = ps_ref[...]
    rows = [jnp.sum(jnp.where(eiota == idx[k:k + 1, :], ps, 0.0), axis=0, keepdims=True)
            for k in range(TOP_K)]
    dest_ref[...] = jnp.concatenate(rows, axis=0).astype(I32) + rank_ref[...]


def _dest(idx, rank, pstart_col, tm):
    n = idx.shape[1]
    cols = pl.BlockSpec((TOP_K, tm), lambda i: (0, i))
    return pl.pallas_call(
        _dest_body,
        name="dest",
        grid=(n // tm,),
        in_specs=[cols, cols, pl.BlockSpec((N_EXPERTS, 1), lambda i: (0, 0))],
        out_specs=cols,
        out_shape=jax.ShapeDtypeStruct((TOP_K, n), I32),
        compiler_params=_params(("parallel",)),
    )(idx, rank, pstart_col)


PAD_PIECES = tuple(1 << s for s in reversed(range(EXPERT_BLOCK.bit_length() - 1)))
ROW_UNROLL = 4


def _dispatch_body(ps_ref, cnt_ref, dest_ref, x_ref, xs_ref, zbuf, sem, zsem):
    i = pl.program_id(0)
    ts = x_ref.shape[0]

    @pl.when(i == 0)
    def _():
        zbuf[...] = jnp.zeros(zbuf.shape, U32)

        def pad_rows(wait, e, carry):
            cnt = cnt_ref[e]
            base = ps_ref[e] + cnt
            pad = (-cnt) & (EXPERT_BLOCK - 1)
            for p in PAD_PIECES:
                @pl.when((pad & p) != 0)
                def _():
                    cp = pltpu.make_async_copy(zbuf.at[pl.ds(0, p)], xs_ref.at[pl.ds(base, p)],
                                               zsem)
                    if wait:
                        cp.wait()
                    else:
                        cp.start()
                base = base + (pad & p)
            return carry

        lax.fori_loop(0, N_EXPERTS, functools.partial(pad_rows, False), 0)
        lax.fori_loop(0, N_EXPERTS, functools.partial(pad_rows, True), 0)

    def row_copy(t, k):
        return pltpu.make_async_copy(x_ref.at[t], xs_ref.at[dest_ref[k, t]], sem)

    def issue(g, carry):
        for u in range(ROW_UNROLL):
            for k in range(TOP_K):
                row_copy(g * ROW_UNROLL + u, k).start(priority=k % 2)
        return carry

    lax.fori_loop(0, ts // ROW_UNROLL, issue, 0)
    for k in range(TOP_K):
        pltpu.make_async_copy(x_ref, xs_ref.at[pl.ds(0, ts)], sem).wait()


def _dispatch(x_tiles, dest, pstart, counts, n_rows, ts):
    n = x_tiles.shape[0]
    tile = (PACK_TILE, LANES)
    grid_spec = pltpu.PrefetchScalarGridSpec(
        num_scalar_prefetch=2,
        grid=(n // ts,),
        in_specs=[pl.BlockSpec((TOP_K, ts), lambda i, ps, cn: (0, i), memory_space=pltpu.SMEM),
                  pl.BlockSpec((ts,) + tile, lambda i, ps, cn: (i, 0, 0))],
        out_specs=pl.BlockSpec(memory_space=pl.ANY),
        scratch_shapes=[pltpu.VMEM((PAD_PIECES[0],) + tile, U32),
                        pltpu.SemaphoreType.DMA, pltpu.SemaphoreType.DMA],
    )
    return pl.pallas_call(
        _dispatch_body,
        name="dispatch",
        grid_spec=grid_spec,
        out_shape=jax.ShapeDtypeStruct((n_rows,) + tile, U32),
        compiler_params=_params(("arbitrary",)),
    )(pstart, counts, dest, x_tiles)


def _expert_body(be_ref, nb_ref, xs_ref, wg_ref, wu_ref, wd_ref, ys_ref):
    del be_ref
    i = pl.program_id(0)

    @pl.when(i < nb_ref[0])
    def _():
        x = _load_packed_tiles(xs_ref, EXPERT_BLOCK)
        h = (_silu(_dot(x, wg_ref[...])) * _dot(x, wu_ref[...])).astype(BF16)
        _store_row_tiles(ys_ref, _dot(h, wd_ref[...]))

    @pl.when(i >= nb_ref[0])
    def _():
        ys_ref[...] = jnp.zeros(ys_ref.shape, F32)


def _experts(xs, block_e, nb_used, wg, wu, wd):
    rows = xs.shape[0] // PACK_TILE
    n_blocks = rows // EXPERT_BLOCK
    in_rows = pl.BlockSpec((EXPERT_BLOCK * PACK_TILE, LANES),
                           lambda i, be, nb: (jnp.minimum(i, nb[0] - 1), 0))
    grid_spec = pltpu.PrefetchScalarGridSpec(
        num_scalar_prefetch=2,
        grid=(n_blocks,),
        in_specs=[in_rows,
                  pl.BlockSpec((None, D_MODEL, D_EXPERT), lambda i, be, nb: (be[i], 0, 0)),
                  pl.BlockSpec((None, D_MODEL, D_EXPERT), lambda i, be, nb: (be[i], 0, 0)),
                  pl.BlockSpec((None, D_EXPERT, D_MODEL), lambda i, be, nb: (be[i], 0, 0))],
        out_specs=pl.BlockSpec((EXPERT_BLOCK * ROW_TILE, LANES), lambda i, be, nb: (i, 0)),
    )
    return pl.pallas_call(
        _expert_body,
        name="experts",
        grid_spec=grid_spec,
        out_shape=jax.ShapeDtypeStruct((rows * ROW_TILE, LANES), F32),
        compiler_params=_params(("arbitrary",)),
    )(block_e, nb_used, xs, wg, wu, wd)


def _combine_body(dest_ref, dnext_ref, wts_ref, x1_ref, ys_ref, wsg, wsu, wsd, g_ref, b_ref,
                  y_ref, buf, sems):
    i = pl.program_id(0)
    n_tiles = pl.num_programs(0)
    tc = x1_ref.shape[0]
    slot = i % 2

    def issue(d_ref, s, g, carry):
        for u in range(ROW_UNROLL):
            t = g * ROW_UNROLL + u
            for k in range(TOP_K):
                pltpu.make_async_copy(_token_tile(ys_ref, d_ref[k, t]),
                                      _token_tile(buf.at[s, k], t),
                                      sems.at[s]).start(priority=k % 2)
        return carry

    n_groups = tc // ROW_UNROLL

    @pl.when(i == 0)
    def _():
        lax.fori_loop(0, n_groups, functools.partial(issue, dest_ref, slot), 0)

    @pl.when(i + 1 < n_tiles)
    def _():
        lax.fori_loop(0, n_groups, functools.partial(issue, dnext_ref, 1 - slot), 0)

    x1 = x1_ref[...]
    xb = x1.astype(BF16)
    hs = (_silu(_dot(xb, wsg[...])) * _dot(xb, wsu[...])).astype(BF16)
    shared = _dot(hs, wsd[...])

    for k in range(TOP_K):
        pltpu.make_async_copy(ys_ref.at[pl.ds(0, tc * ROW_TILE)], buf.at[slot, k],
                              sems.at[slot]).wait()

    w = wts_ref[...]
    routed = w[:, 0:1] * _load_row_tiles(buf.at[slot, 0], tc)
    for k in range(1, TOP_K):
        routed = routed + w[:, k:k + 1] * _load_row_tiles(buf.at[slot, k], tc)
    y_ref[...] = _layer_norm(DEEPNORM_ALPHA * x1 + (routed + shared), g_ref[...], b_ref[...])


def _combine(dest, wts_t, x1, ys, wsg, wsu, wsd, ln_g, ln_b, tc):
    n = x1.shape[0]
    last = n // tc - 1
    dest_cur = pl.BlockSpec((TOP_K, tc), lambda i: (0, i), memory_space=pltpu.SMEM)
    dest_next = pl.BlockSpec((TOP_K, tc), lambda i: (0, jnp.minimum(i + 1, last)),
                             memory_space=pltpu.SMEM)
    full = lambda a: pl.BlockSpec(a.shape, lambda i: (0, 0))
    rows = lambda w: pl.BlockSpec((tc, w), lambda i: (i, 0))
    return pl.pallas_call(
        _combine_body,
        name="combine",
        grid=(n // tc,),
        in_specs=[dest_cur, dest_next, rows(LANES), rows(D_MODEL),
                  pl.BlockSpec(memory_space=pl.ANY),
                  full(wsg), full(wsu), full(wsd), full(ln_g), full(ln_b)],
        out_specs=rows(D_MODEL),
        out_shape=jax.ShapeDtypeStruct((n, D_MODEL), F32),
        scratch_shapes=[pltpu.VMEM((2, TOP_K, tc * ROW_TILE, LANES), F32),
                        pltpu.SemaphoreType.DMA((2,))],
        compiler_params=_params(("arbitrary",)),
    )(dest, dest, wts_t, x1, ys, wsg, wsu, wsd, ln_g, ln_b)


def _moe(x1, x_tiles, idx, wts_t, rank, cnt, moe_w, ln_g, ln_b, tile):
    n = x1.shape[0]
    wg, wu, wd, wsg, wsu, wsd = moe_w
    blk = EXPERT_BLOCK
    n_blocks = -(-(n * TOP_K) // blk) + N_EXPERTS
    counts = cnt[:, 0].astype(I32)
    padded = (counts + blk - 1) // blk * blk
    ends = jnp.cumsum(padded)
    pstart = ends - padded
    nb_used = (ends[-1] // blk).astype(I32).reshape(1)
    block_row0 = jnp.arange(n_blocks, dtype=I32) * blk
    block_e = jnp.minimum(jnp.sum((ends[None, :] <= block_row0[:, None]).astype(I32), axis=1),
                          N_EXPERTS - 1).astype(I32)
    dest = _dest(idx, rank, pstart.astype(F32).reshape(N_EXPERTS, 1), tile)
    xs = _dispatch(x_tiles.reshape(n, PACK_TILE, LANES), dest, pstart, counts, n_blocks * blk,
                   tile)
    ys = _experts(xs.reshape(n_blocks * blk * PACK_TILE, LANES), block_e, nb_used, wg, wu, wd)
    return _combine(dest, wts_t, x1, ys, wsg, wsu, wsd, ln_g, ln_b, min(tile, 128))


def kernel(x_prompt, x_sample, cache_k, cache_v, state_conv, page_table, w_in, lam_q1, lam_k1,
           lam_q2, lam_k2, subln_g, w_dw, b_dw, conv_ln_g, conv_ln_b, w_o, ln1_g, ln1_b,
           w_router, b_router, w_gate, w_up, w_down, w_sh_gate, w_sh_up, w_sh_down, ln2_g,
           ln2_b):
    batch, seq, _ = x_prompt.shape
    dec_b, dec_s, _ = x_sample.shape
    n_p, n_s = batch * seq, dec_b * dec_s
    layer = 0
    lam_init = _lambda_init(layer)
    row = lambda a: a[layer].reshape(1, -1)

    w_in_b = w_in[layer].astype(BF16)
    wo = w_o[layer].astype(BF16)
    wo_a, wo_c = wo[:ATTN_WIDTH], wo[ATTN_WIDTH:]
    wr_t = w_router[layer].T
    wr_hi = wr_t.astype(BF16)
    wr_lo = (wr_t - wr_hi.astype(F32)).astype(BF16)
    br = b_router[layer].reshape(-1, 1)
    moe_w = tuple(w[layer].astype(BF16)
                  for w in (w_gate, w_up, w_down, w_sh_gate, w_sh_up, w_sh_down))
    lam_vecs = (row(lam_q1), row(lam_k1), row(lam_q2), row(lam_k2))
    g_sub = row(subln_g)
    conv_w = (w_dw[layer], row(b_dw), row(conv_ln_g), row(conv_ln_b))
    ln1 = (row(ln1_g), row(ln1_b))
    ln2 = (row(ln2_g), row(ln2_b))

    xp = x_prompt.reshape(n_p, D_MODEL)
    wq_t = w_in_b[:, :QK_WIDTH].T
    wv_t = w_in_b[:, 2 * QK_WIDTH:2 * QK_WIDTH + ATTN_WIDTH].T
    qt_p, k_p, v_p, u_p, kb_p, vt_p = _inproj(xp, w_in_b, 512, (wq_t, wv_t))
    o_p = _attn_prompt(qt_p, kb_p, vt_p, lam_vecs, g_sub.reshape(V_DIM, 1), lam_init, batch, seq)
    c_p = _conv_prompt(u_p, *conv_w, batch, seq, 512)
    routed_p = _mix(o_p, c_p, xp, wo_a, wo_c, *ln1, wr_hi, wr_lo, br, 512)
    y_p = _moe(*routed_p, moe_w, *ln2, 256)

    xs = x_sample.reshape(n_s, D_MODEL)
    q_s, k_s, v_s, u_s = _inproj(xs, w_in_b, n_s)
    pool = cache_k.shape[1]
    page_rows = lambda a: a[layer].reshape(pool, PAGE_ROWS, V_DIM)
    new_rows = lambda a: a.reshape(dec_b, dec_s * N_HEADS, V_DIM)
    o_s = _attn_sample(q_s.reshape(dec_b, dec_s, QK_WIDTH), new_rows(k_s), new_rows(v_s),
                       page_rows(cache_k), page_rows(cache_v), page_table, lam_vecs, g_sub,
                       lam_init)
    c_s, st_s = _conv_sample(u_s.reshape(dec_b, dec_s, CONV_CH), state_conv[layer], *conv_w)
    routed_s = _mix(o_s.reshape(n_s, ATTN_WIDTH), c_s.reshape(n_s, CONV_CH), xs, wo_a, wo_c,
                    *ln1, wr_hi, wr_lo, br, n_s)
    y_s = _moe(*routed_s, moe_w, *ln2, 128)

    keep = CONV_WIDTH - 1
    u_p3 = u_p.reshape(batch, seq, CONV_CH)
    return (y_p.reshape(batch, seq, D_MODEL),
            y_s.reshape(dec_b, dec_s, D_MODEL),
            k_p.reshape(1, batch, seq, N_HEADS, V_DIM),
            v_p.reshape(1, batch, seq, N_HEADS, V_DIM),
            u_p3[:, seq - keep:, :][None],
            k_s.reshape(1, dec_b, dec_s, N_HEADS, V_DIM),
            v_s.reshape(1, dec_b, dec_s, N_HEADS, V_DIM),
            st_s[None])
```

```python
import functools
import math

import jax
import jax.numpy as jnp
from jax import lax
from jax.experimental import pallas as pl
from jax.experimental.pallas import tpu as pltpu

F32 = jnp.float32
BF16 = jnp.bfloat16
I32 = jnp.int32
U32 = jnp.uint32

D_MODEL = 1024
N_HEADS = 4
HEAD_DIM = 64
V_DIM = 128
ATTN_WIDTH = N_HEADS * V_DIM
QK_WIDTH = N_HEADS * 2 * HEAD_DIM
ATTN_SCALE = HEAD_DIM ** -0.5
CONV_CH = D_MODEL - ATTN_WIDTH
CONV_WIDTH = 31
N_EXPERTS = 64
N_GROUPS = 8
GROUP_SIZE = N_EXPERTS // N_GROUPS
TOPK_GROUPS = 4
TOP_K = 8
D_EXPERT = D_MODEL // 4
ROUTED_SCALE = 2.5
DEPTH = 1
DEEPNORM_ALPHA = (2 * DEPTH) ** 0.25
NORM_EPS = 1e-5
NEG_INF = -1e30
PAGE_SIZE = 128

LANES = 128
ROW_TILE = D_MODEL // LANES
PACK_TILE = ROW_TILE // 2
VMEM_LIMIT = 48 * 1024 * 1024
PAGES_PER_STEP = 16
PAGE_ROWS = PAGE_SIZE * N_HEADS
EXPERT_BLOCK = 512
HIST = 32


def _lambda_init(layer):
    return 0.8 - 0.6 * math.exp(-0.3 * layer)


def _sigmoid(x):
    return 1.0 / (1.0 + jnp.exp(-x))


def _silu(x):
    return x * _sigmoid(x)


def _layer_norm(x, g, b):
    mu = jnp.mean(x, axis=-1, keepdims=True)
    xc = x - mu
    var = jnp.mean(xc * xc, axis=-1, keepdims=True)
    return xc * lax.rsqrt(var + NORM_EPS) * g + b


def _dot(a, b):
    return jnp.dot(a, b, preferred_element_type=F32)


def _dot_nt(a, b):
    return lax.dot_general(a, b, (((1,), (1,)), ((), ())), preferred_element_type=F32)


def _diff_lambda(lq1, lk1, lq2, lk2, lam_init):
    a = jnp.exp(jnp.sum(lq1 * lk1, axis=-1, keepdims=True))
    b = jnp.exp(jnp.sum(lq2 * lk2, axis=-1, keepdims=True))
    return a - b + lam_init


def _params(dims):
    return pltpu.CompilerParams(dimension_semantics=dims, vmem_limit_bytes=VMEM_LIMIT)


def _store_row_tiles(ref, x):
    m = x.shape[0]
    for j in range(ROW_TILE):
        ref[pl.ds(j, m, stride=ROW_TILE), :] = x[:, j * LANES:(j + 1) * LANES]


def _load_row_tiles(ref, m):
    return jnp.concatenate([ref[pl.ds(j, m, stride=ROW_TILE), :] for j in range(ROW_TILE)],
                           axis=1)


def _token_tile(ref, t):
    return ref.at[pl.ds(pl.multiple_of(t * ROW_TILE, ROW_TILE), ROW_TILE)]


def _store_packed_tiles(ref, x):
    m = x.shape[0]
    bits = pltpu.bitcast(x.astype(BF16).astype(F32), U32)
    half = D_MODEL // 2
    for j in range(PACK_TILE):
        lo = bits[:, j * LANES:(j + 1) * LANES] >> 16
        hi = bits[:, half + j * LANES:half + (j + 1) * LANES]
        ref[pl.ds(j, m, stride=PACK_TILE), :] = lo | hi


def _load_packed_tiles(ref, m):
    words = [ref[pl.ds(j, m, stride=PACK_TILE), :] for j in range(PACK_TILE)]
    lo = [pltpu.bitcast(w << 16, F32) for w in words]
    hi = [pltpu.bitcast(w & jnp.uint32(0xFFFF0000), F32) for w in words]
    return jnp.concatenate(lo + hi, axis=1).astype(BF16)


def _store_head_rows(ref, x):
    m = x.shape[0]
    for h in range(N_HEADS):
        ref[pl.ds(h, m, stride=N_HEADS), :] = x[:, h * V_DIM:(h + 1) * V_DIM]


def _inproj_body(x_ref, w_ref, *refs):
    xb = x_ref[...].astype(BF16)

    def mm(c0):
        return _dot(xb, w_ref[:, c0:c0 + QK_WIDTH])

    k = mm(QK_WIDTH)
    if len(refs) == 4:
        q_ref, k_ref, v_ref, u_ref = refs
        q_ref[...] = mm(0) * ATTN_SCALE
    else:
        wqt_ref, wvt_ref, q_ref, k_ref, v_ref, u_ref, kb_ref, vt_ref = refs
        q_ref[...] = _dot_nt(wqt_ref[...], xb) * ATTN_SCALE
        vt_ref[...] = _dot_nt(wvt_ref[...], xb).astype(BF16)
        kb_ref[...] = k.astype(BF16)
    _store_head_rows(k_ref, k)
    _store_head_rows(v_ref, mm(2 * QK_WIDTH))
    a = mm(2 * QK_WIDTH + ATTN_WIDTH)
    b = mm(2 * QK_WIDTH + ATTN_WIDTH + CONV_CH)
    u_ref[...] = a * _sigmoid(b)


def _inproj(x, w_in_b, tm, transposed_w=()):
    n = x.shape[0]
    wide = QK_WIDTH
    blk = pl.BlockSpec((tm, wide), lambda i: (i, 0))
    hblk = pl.BlockSpec((tm * N_HEADS, V_DIM), lambda i: (i, 0))
    full = lambda a: pl.BlockSpec(a.shape, lambda i: (0, 0))
    row_major = jax.ShapeDtypeStruct((n, wide), F32)
    head_rows = jax.ShapeDtypeStruct((n * N_HEADS, V_DIM), F32)
    out_specs = [blk, hblk, hblk, blk]
    out_shape = [row_major, head_rows, head_rows, row_major]
    if transposed_w:
        t_blk = pl.BlockSpec((None, wide, tm), lambda i: (i, 0, 0))
        out_specs = [t_blk, hblk, hblk, blk, blk, t_blk]
        out_shape = [jax.ShapeDtypeStruct((n // tm, wide, tm), F32), head_rows, head_rows,
                     row_major, jax.ShapeDtypeStruct((n, wide), BF16),
                     jax.ShapeDtypeStruct((n // tm, wide, tm), BF16)]
    return pl.pallas_call(
        _inproj_body,
        name="inproj",
        grid=(n // tm,),
        in_specs=[pl.BlockSpec((tm, D_MODEL), lambda i: (i, 0)), full(w_in_b)]
        + [full(w) for w in transposed_w],
        out_specs=out_specs,
        out_shape=out_shape,
        compiler_params=_params(("parallel",)),
    )(x, w_in_b, *transposed_w)


def _attn_p_body(lam_init, q_ref, k_ref, v_ref, lq1, lk1, lq2, lk2, g_ref, o_ref, acc_ref):
    qi = pl.program_id(2)
    tq = q_ref.shape[1]
    tk = v_ref.shape[2]
    q_t = q_ref[...]
    feat = lax.broadcasted_iota(I32, q_t.shape, 0)
    q_maps = (jnp.where(feat < HEAD_DIM, q_t, 0.0).astype(BF16),
              jnp.where(feat >= HEAD_DIM, q_t, 0.0).astype(BF16))
    acc_ref[...] = jnp.zeros(acc_ref.shape, F32)

    def chunk(masked, j, carry):
        kc = k_ref[pl.ds(pl.multiple_of(j * tk, tk), tk), :]
        vc = v_ref[j]
        if masked:
            key = lax.broadcasted_iota(I32, (tk, tq), 0)
            qry = lax.broadcasted_iota(I32, (tk, tq), 1)
            keep = key <= qry
        out = []
        for a in range(2):
            m_old, l_old = carry[2 * a], carry[2 * a + 1]
            s_t = _dot(kc, q_maps[a])
            if masked:
                s_t = jnp.where(keep, s_t, NEG_INF)
            m_new = jnp.maximum(m_old, jnp.max(s_t, axis=0, keepdims=True))
            alpha = jnp.exp(m_old - m_new)
            p_t = jnp.exp(s_t - m_new)
            out.append(m_new)
            out.append(alpha * l_old + jnp.sum(p_t, axis=0, keepdims=True))
            acc_ref[a] = alpha * acc_ref[a] + _dot(vc, p_t.astype(BF16))
        return tuple(out)

    neg = jnp.full((1, tq), -jnp.inf, F32)
    zero = jnp.zeros((1, tq), F32)
    carry = lax.fori_loop(0, qi, functools.partial(chunk, False), (neg, zero, neg, zero))
    _, l0, _, l1 = chunk(True, qi, carry)

    lam = _diff_lambda(lq1[...], lk1[...], lq2[...], lk2[...], lam_init)
    o_t = acc_ref[0] / l0 - lam * (acc_ref[1] / l1)
    ms = jnp.mean(o_t * o_t, axis=0, keepdims=True)
    o_t = o_t * lax.rsqrt(ms + NORM_EPS) * g_ref[...] * (1.0 - lam_init)
    o_ref[...] = o_t.T


def _attn_prompt(q_t, kb, v_t, lam_vecs, subln_g_col, lam_init, batch, seq):
    t = q_t.shape[2]
    n = kb.shape[0]
    nq = seq // t
    small = lambda w: pl.BlockSpec((1, w), lambda b, h, i: (0, 0))
    return pl.pallas_call(
        functools.partial(_attn_p_body, lam_init),
        name="attn_prompt",
        grid=(batch, N_HEADS, nq),
        in_specs=[pl.BlockSpec((None, V_DIM, t), lambda b, h, i: (b * nq + i, h, 0)),
                  pl.BlockSpec((seq, V_DIM), lambda b, h, i: (b, h)),
                  pl.BlockSpec((nq, V_DIM, t), lambda b, h, i: (b, h, 0))]
        + [small(HEAD_DIM)] * 4 + [pl.BlockSpec((V_DIM, 1), lambda b, h, i: (0, 0))],
        out_specs=pl.BlockSpec((t, V_DIM), lambda b, h, i: (b * nq + i, h)),
        out_shape=jax.ShapeDtypeStruct((n, ATTN_WIDTH), F32),
        scratch_shapes=[pltpu.VMEM((2, V_DIM, t), F32)],
        compiler_params=_params(("parallel", "parallel", "parallel")),
    )(q_t, kb, v_t, *lam_vecs, subln_g_col)


def _attn_s_body(lam_init, n_steps, pt_ref, q_ref, kn_ref, vn_ref, *rest):
    npg = PAGES_PER_STEP
    k_refs = rest[:npg]
    v_refs = rest[npg:2 * npg]
    (lq1, lk1, lq2, lk2, g_ref, o_ref, qall, knew, vnew, m_ref, l_ref,
     acc_ref) = rest[2 * npg:]
    j = pl.program_id(1)
    ds = q_ref.shape[0]
    hr = 2 * ds

    def head_rows(ref, h):
        return ref[pl.ds(h, PAGE_SIZE, stride=N_HEADS), :].astype(BF16)

    @pl.when(j == 0)
    def _():
        pieces = []
        for h in range(N_HEADS):
            qh = q_ref[:, h * V_DIM:(h + 1) * V_DIM]
            lane = lax.broadcasted_iota(I32, qh.shape, 1)
            pieces.append(jnp.where(lane < HEAD_DIM, qh, 0.0))
            pieces.append(jnp.where(lane >= HEAD_DIM, qh, 0.0))
        qall[...] = jnp.concatenate(pieces, axis=0).astype(BF16)
        knew[...] = jnp.zeros(knew.shape, F32)
        vnew[...] = jnp.zeros(vnew.shape, F32)
        knew[0:ds * N_HEADS] = kn_ref[...]
        vnew[0:ds * N_HEADS] = vn_ref[...]
        row = lax.broadcasted_iota(I32, (hr, PAGE_SIZE), 0)
        key = lax.broadcasted_iota(I32, (hr, PAGE_SIZE), 1)
        keep = key <= (row % ds)
        for h in range(N_HEADS):
            rows = slice(h * hr, (h + 1) * hr)
            s = jnp.where(keep, _dot_nt(qall[rows], head_rows(knew, h)), NEG_INF)
            m = jnp.max(s, axis=-1, keepdims=True)
            p = jnp.exp(s - m)
            m_ref[rows] = m
            l_ref[rows] = jnp.sum(p, axis=-1, keepdims=True)
            acc_ref[rows] = _dot(p.astype(BF16), head_rows(vnew, h))

    q_all, m_all, l_all, acc_all = qall[...], m_ref[...], l_ref[...], acc_ref[...]
    heads = range(N_HEADS)
    rows = [slice(h * hr, (h + 1) * hr) for h in heads]
    s = [_dot_nt(q_all[rows[h]],
                 jnp.concatenate([head_rows(k_refs[i], h) for i in range(npg)], axis=0))
         for h in heads]
    s = jnp.concatenate(s, axis=0)
    m_new = jnp.maximum(m_all, jnp.max(s, axis=-1, keepdims=True))
    alpha = jnp.exp(m_all - m_new)
    p = jnp.exp(s - m_new)
    l_new = alpha * l_all + jnp.sum(p, axis=-1, keepdims=True)
    pb = p.astype(BF16)
    pv = [_dot(pb[rows[h]],
               jnp.concatenate([head_rows(v_refs[i], h) for i in range(npg)], axis=0))
          for h in heads]
    acc_new = alpha * acc_all + jnp.concatenate(pv, axis=0)
    m_ref[...] = m_new
    l_ref[...] = l_new
    acc_ref[...] = acc_new

    @pl.when(j == n_steps - 1)
    def _():
        lam = _diff_lambda(lq1[...], lk1[...], lq2[...], lk2[...], lam_init)
        on = acc_new / l_new
        for h in range(N_HEADS):
            r0 = h * hr
            o = on[r0:r0 + ds] - lam * on[r0 + ds:r0 + 2 * ds]
            ms = jnp.mean(o * o, axis=-1, keepdims=True)
            o_ref[:, h * V_DIM:(h + 1) * V_DIM] = (
                o * lax.rsqrt(ms + NORM_EPS) * g_ref[...] * (1.0 - lam_init))


def _attn_sample(q, k_new, v_new, cache_k, cache_v, page_table, lam_vecs, subln_g, lam_init):
    b, ds, _ = q.shape
    n_pages = page_table.shape[1]
    n_steps = n_pages // PAGES_PER_STEP
    nrow = N_HEADS * 2 * ds
    pt = page_table.reshape(-1)

    def page_map(i, bi, j, pt_ref):
        return (pt_ref[bi * n_pages + j * PAGES_PER_STEP + i], 0, 0)

    page_specs = [pl.BlockSpec((None, PAGE_ROWS, V_DIM), functools.partial(page_map, i))
                  for i in range(PAGES_PER_STEP)]
    small = lambda w: pl.BlockSpec((1, w), lambda bi, j, pt_ref: (0, 0))
    new_spec = pl.BlockSpec((None, ds * N_HEADS, V_DIM), lambda bi, j, pt_ref: (bi, 0, 0))
    qo_spec = pl.BlockSpec((None, ds, ATTN_WIDTH), lambda bi, j, pt_ref: (bi, 0, 0))
    grid_spec = pltpu.PrefetchScalarGridSpec(
        num_scalar_prefetch=1,
        grid=(b, n_steps),
        in_specs=[qo_spec, new_spec, new_spec] + page_specs + page_specs
        + [small(HEAD_DIM)] * 4 + [small(V_DIM)],
        out_specs=qo_spec,
        scratch_shapes=[pltpu.VMEM((nrow, V_DIM), BF16),
                        pltpu.VMEM((PAGE_ROWS, V_DIM), F32),
                        pltpu.VMEM((PAGE_ROWS, V_DIM), F32),
                        pltpu.VMEM((nrow, 1), F32),
                        pltpu.VMEM((nrow, 1), F32),
                        pltpu.VMEM((nrow, V_DIM), F32)],
    )
    return pl.pallas_call(
        functools.partial(_attn_s_body, lam_init, n_steps),
        name="attn_sample",
        grid_spec=grid_spec,
        out_shape=jax.ShapeDtypeStruct((b, ds, ATTN_WIDTH), F32),
        compiler_params=_params(("parallel", "arbitrary")),
    )(pt, q, k_new, v_new, *([cache_k] * PAGES_PER_STEP), *([cache_v] * PAGES_PER_STEP),
      *lam_vecs, subln_g)


def _conv_taps(ext_ref, w_ref, bias, rows):
    off = HIST - (CONV_WIDTH - 1)
    acc = jnp.broadcast_to(bias, (rows, CONV_CH))
    for jt in range(CONV_WIDTH):
        acc = acc + w_ref[jt:jt + 1, :] * ext_ref[pl.ds(jt + off, rows), :]
    return acc


CONV_ROWS = 64
SUBLANES = 8


def _conv_p_body(u_ref, h_ref, w_ref, b_ref, g_ref, beta_ref, c_ref, ext_ref, sh_ref):
    t = pl.program_id(1)
    tm = u_ref.shape[0]
    hist = h_ref[...]
    ext_ref[0:HIST] = jnp.where(t == 0, jnp.zeros_like(hist), hist)
    ext_ref[HIST:HIST + tm] = u_ref[...]
    span = sh_ref.shape[1]
    for r in range(1, SUBLANES):
        sh_ref[r - 1] = ext_ref[pl.ds(r, span), :]
    off = HIST - (CONV_WIDTH - 1)
    bias = b_ref[...]
    for c0 in range(0, tm, CONV_ROWS):
        acc = jnp.broadcast_to(bias, (CONV_ROWS, CONV_CH))
        for jt in range(CONV_WIDTH):
            r = (jt + off) % SUBLANES
            base = c0 + jt + off - r
            src = ext_ref if r == 0 else sh_ref.at[r - 1]
            acc = acc + w_ref[jt:jt + 1, :] * src[pl.ds(base, CONV_ROWS), :]
        c_ref[c0:c0 + CONV_ROWS, :] = _silu(_layer_norm(acc, g_ref[...], beta_ref[...]))


def _conv_prompt(u, w_dw, b_dw, g, beta, batch, seq, tm):
    n = u.shape[0]
    nt = seq // tm
    per = tm // HIST
    cur = lambda b, t: (b * nt + t, 0)
    prev = lambda b, t: (jnp.maximum((b * nt + t) * per - 1, 0), 0)
    small = lambda r: pl.BlockSpec((r, CONV_CH), lambda b, t: (0, 0))
    return pl.pallas_call(
        _conv_p_body,
        name="conv_prompt",
        grid=(batch, nt),
        in_specs=[pl.BlockSpec((tm, CONV_CH), cur), pl.BlockSpec((HIST, CONV_CH), prev),
                  small(CONV_WIDTH), small(1), small(1), small(1)],
        out_specs=pl.BlockSpec((tm, CONV_CH), cur),
        out_shape=jax.ShapeDtypeStruct((n, CONV_CH), F32),
        scratch_shapes=[pltpu.VMEM((HIST + tm, CONV_CH), F32),
                        pltpu.VMEM((SUBLANES - 1, HIST + tm - SUBLANES, CONV_CH), F32)],
        compiler_params=_params(("parallel", "parallel")),
    )(u, u, w_dw, b_dw, g, beta)


def _conv_s_body(u_ref, st_ref, w_ref, b_ref, g_ref, beta_ref, c_ref, ns_ref, ext_ref):
    ds = u_ref.shape[0]
    keep = CONV_WIDTH - 1
    off = HIST - keep
    ext_ref[0:off] = jnp.zeros((off, CONV_CH), F32)
    ext_ref[off:HIST] = st_ref[...]
    ext_ref[HIST:HIST + ds] = u_ref[...]
    y = _conv_taps(ext_ref, w_ref, b_ref[...], ds)
    c_ref[...] = _silu(_layer_norm(y, g_ref[...], beta_ref[...]))
    ns_ref[...] = ext_ref[HIST + ds - keep:HIST + ds]


def _conv_sample(u, state, w_dw, b_dw, g, beta):
    b, ds, _ = u.shape
    keep = CONV_WIDTH - 1
    per_b = lambda r: pl.BlockSpec((None, r, CONV_CH), lambda i: (i, 0, 0))
    small = lambda r: pl.BlockSpec((r, CONV_CH), lambda i: (0, 0))
    return pl.pallas_call(
        _conv_s_body,
        name="conv_sample",
        grid=(b,),
        in_specs=[per_b(ds), per_b(keep), small(CONV_WIDTH), small(1), small(1), small(1)],
        out_specs=[per_b(ds), per_b(keep)],
        out_shape=[jax.ShapeDtypeStruct((b, ds, CONV_CH), F32),
                   jax.ShapeDtypeStruct((b, keep, CONV_CH), F32)],
        scratch_shapes=[pltpu.VMEM((HIST + ds, CONV_CH), F32)],
        compiler_params=_params(("parallel",)),
    )(u, state, w_dw, b_dw, g, beta)


def _first_index(hit, iota, limit, axis):
    return jnp.min(jnp.where(hit, iota, limit), axis=axis, keepdims=True)


def _route(x1, wr_hi, wr_lo, bias):
    tm = x1.shape[0]
    x_hi = x1.astype(BF16)
    x_lo = (x1 - x_hi.astype(F32)).astype(BF16)
    logits = _dot_nt(wr_hi, x_hi) + (_dot_nt(wr_lo, x_hi) + _dot_nt(wr_hi, x_lo))
    scores = _sigmoid(logits)
    sel = scores + bias
    sel3 = sel.reshape(N_GROUPS, GROUP_SIZE, tm)
    member = lax.broadcasted_iota(I32, sel3.shape, 1)
    m1 = jnp.max(sel3, axis=1, keepdims=True)
    i1 = _first_index(sel3 == m1, member, GROUP_SIZE, 1)
    m2 = jnp.max(jnp.where(member == i1, -jnp.inf, sel3), axis=1, keepdims=True)
    gs = jnp.broadcast_to(m1 + m2, sel3.shape).reshape(N_EXPERTS, tm)
    eiota = lax.broadcasted_iota(I32, (N_EXPERTS, tm), 0)
    giota = eiota // GROUP_SIZE
    gmask = jnp.zeros((N_EXPERTS, tm), jnp.bool_)
    for _ in range(TOPK_GROUPS):
        m = jnp.max(gs, axis=0, keepdims=True)
        gi = _first_index(gs == m, giota, N_GROUPS, 0)
        pick = giota == gi
        gmask = jnp.logical_or(gmask, pick)
        gs = jnp.where(pick, -jnp.inf, gs)
    selm = jnp.where(gmask, sel, NEG_INF)
    idx_rows, w_rows = [], []
    for _ in range(TOP_K):
        m = jnp.max(selm, axis=0, keepdims=True)
        ei = _first_index(selm == m, eiota, N_EXPERTS, 0)
        pick = eiota == ei
        idx_rows.append(ei)
        w_rows.append(jnp.sum(jnp.where(pick, scores, 0.0), axis=0, keepdims=True))
        selm = jnp.where(pick, -jnp.inf, selm)
    idx = jnp.concatenate(idx_rows, axis=0)
    w = jnp.concatenate(w_rows, axis=0)
    w = w / jnp.sum(w, axis=0, keepdims=True) * ROUTED_SCALE
    return idx, w


def _mix_body(o_ref, c_ref, x_ref, wo_a, wo_c, g_ref, b_ref, wrh_ref, wrl_ref, br_ref, cnt0_ref,
              x1_ref, xrt_ref, idx_ref, wts_ref, rank_ref, cnt_ref, base_ref):
    i = pl.program_id(0)
    tm = x_ref.shape[0]

    @pl.when(i == 0)
    def _():
        base_ref[...] = cnt0_ref[...]

    mix = _dot(o_ref[...].astype(BF16), wo_a[...]) + _dot(c_ref[...].astype(BF16), wo_c[...])
    x1 = _layer_norm(DEEPNORM_ALPHA * x_ref[...] + mix, g_ref[...], b_ref[...])
    x1_ref[...] = x1
    _store_packed_tiles(xrt_ref, x1)

    idx, w = _route(x1, wrh_ref[...], wrl_ref[...], br_ref[...])
    idx_ref[...] = idx
    w_rows = jnp.concatenate([w, jnp.zeros((LANES - TOP_K, tm), F32)], axis=0)
    wts_ref[...] = w_rows.T

    eiota = lax.broadcasted_iota(I32, (N_EXPERTS, tm), 0)
    chosen = jnp.zeros((N_EXPERTS, tm), F32)
    for k in range(TOP_K):
        chosen = chosen + jnp.where(eiota == idx[k:k + 1, :], 1.0, 0.0)
    r = lax.broadcasted_iota(I32, (tm, tm), 0)
    c = lax.broadcasted_iota(I32, (tm, tm), 1)
    before = jnp.where(r < c, 1.0, 0.0).astype(BF16)
    base = base_ref[:, 0:1]
    rank_full = _dot(chosen.astype(BF16), before) + base
    rows = [jnp.sum(jnp.where(eiota == idx[k:k + 1, :], rank_full, 0.0), axis=0, keepdims=True)
            for k in range(TOP_K)]
    rank_ref[...] = jnp.concatenate(rows, axis=0).astype(I32)
    total = base + jnp.sum(chosen, axis=1, keepdims=True)
    base_ref[...] = jnp.broadcast_to(total, base_ref.shape)
    cnt_ref[...] = jnp.broadcast_to(total, cnt_ref.shape)


def _mix(o, c, x, wo_a, wo_c, ln_g, ln_b, wr_hi, wr_lo, b_router, cnt0, tm):
    n = x.shape[0]
    rows = lambda w: pl.BlockSpec((tm, w), lambda i: (i, 0))
    full = lambda a: pl.BlockSpec(a.shape, lambda i: (0, 0))
    cols = pl.BlockSpec((TOP_K, tm), lambda i: (0, i))
    cnt = pl.BlockSpec((N_EXPERTS, LANES), lambda i: (0, 0))
    return pl.pallas_call(
        _mix_body,
        name="mix",
        grid=(n // tm,),
        in_specs=[rows(ATTN_WIDTH), rows(CONV_CH), rows(D_MODEL), full(wo_a), full(wo_c),
                  full(ln_g), full(ln_b), full(wr_hi), full(wr_lo), full(b_router), cnt],
        out_specs=[rows(D_MODEL), pl.BlockSpec((tm * PACK_TILE, LANES), lambda i: (i, 0)), cols,
                   rows(LANES), cols, cnt],
        out_shape=[jax.ShapeDtypeStruct((n, D_MODEL), F32),
                   jax.ShapeDtypeStruct((n * PACK_TILE, LANES), U32),
                   jax.ShapeDtypeStruct((TOP_K, n), I32),
                   jax.ShapeDtypeStruct((n, LANES), F32),
                   jax.ShapeDtypeStruct((TOP_K, n), I32),
                   jax.ShapeDtypeStruct((N_EXPERTS, LANES), F32)],
        scratch_shapes=[pltpu.VMEM((N_EXPERTS, LANES), F32)],
        compiler_params=_params(("arbitrary",)),
    )(o, c, x, wo_a, wo_c, ln_g, ln_b, wr_hi, wr_lo, b_router, cnt0)


def _dest_body(idx_ref, rank_ref, ps_ref, dest_ref):
    idx = idx_ref[...]
    tm = idx.shape[1]
    eiota = lax.broadcasted_iota(I32, (N_EXPERTS, tm), 0)
    ps = ps_ref[...]
    rows = [jnp.sum(jnp.where(eiota == idx[k:k + 1, :], ps, 0.0), axis=0, keepdims=True)
            for k in range(TOP_K)]
    dest_ref[...] = jnp.concatenate(rows, axis=0).astype(I32) + rank_ref[...]


def _dest(idx, rank, pstart_col, tm):
    n = idx.shape[1]
    cols = pl.BlockSpec((TOP_K, tm), lambda i: (0, i))
    return pl.pallas_call(
        _dest_body,
        name="dest",
        grid=(n // tm,),
        in_specs=[cols, cols, pl.BlockSpec((N_EXPERTS, 1), lambda i: (0, 0))],
        out_specs=cols,
        out_shape=jax.ShapeDtypeStruct((TOP_K, n), I32),
        compiler_params=_params(("parallel",)),
    )(idx, rank, pstart_col)


PAD_PIECES = tuple(1 << s for s in reversed(range(EXPERT_BLOCK.bit_length() - 1)))
ROW_UNROLL = 4


def _dispatch_body(first_call, ps_ref, cnt_ref, dest_ref, x_ref, *refs):
    xs_ref, zbuf, sem, zsem = refs[-4:]
    i = pl.program_id(0)
    ts = x_ref.shape[0]

    @pl.when(jnp.logical_and(i == 0, first_call))
    def _():
        zbuf[...] = jnp.zeros(zbuf.shape, U32)

        def pad_rows(wait, e, carry):
            cnt = cnt_ref[e]
            base = ps_ref[e] + cnt
            pad = (-cnt) & (EXPERT_BLOCK - 1)
            for p in PAD_PIECES:
                @pl.when((pad & p) != 0)
                def _():
                    cp = pltpu.make_async_copy(zbuf.at[pl.ds(0, p)], xs_ref.at[pl.ds(base, p)],
                                               zsem)
                    if wait:
                        cp.wait()
                    else:
                        cp.start()
                base = base + (pad & p)
            return carry

        lax.fori_loop(0, N_EXPERTS, functools.partial(pad_rows, False), 0)
        lax.fori_loop(0, N_EXPERTS, functools.partial(pad_rows, True), 0)

    def row_copy(t, k):
        return pltpu.make_async_copy(x_ref.at[t], xs_ref.at[dest_ref[k, t]], sem)

    def issue(g, carry):
        for u in range(ROW_UNROLL):
            for k in range(TOP_K):
                row_copy(g * ROW_UNROLL + u, k).start(priority=k % 2)
        return carry

    lax.fori_loop(0, ts // ROW_UNROLL, issue, 0)
    for k in range(TOP_K):
        pltpu.make_async_copy(x_ref, xs_ref.at[pl.ds(0, ts)], sem).wait()


def _dispatch(x_tiles, dest, pstart, counts, n_rows, ts, xs_prev=None):
    n = x_tiles.shape[0]
    tile = (PACK_TILE, LANES)
    first_call = xs_prev is None
    in_specs = [pl.BlockSpec((TOP_K, ts), lambda i, ps, cn: (0, i), memory_space=pltpu.SMEM),
                pl.BlockSpec((ts,) + tile, lambda i, ps, cn: (i, 0, 0))]
    args = [pstart, counts, dest, x_tiles]
    aliases = {}
    if not first_call:
        in_specs.append(pl.BlockSpec(memory_space=pl.ANY))
        args.append(xs_prev)
        aliases = {len(args) - 1: 0}
    grid_spec = pltpu.PrefetchScalarGridSpec(
        num_scalar_prefetch=2,
        grid=(n // ts,),
        in_specs=in_specs,
        out_specs=pl.BlockSpec(memory_space=pl.ANY),
        scratch_shapes=[pltpu.VMEM((PAD_PIECES[0],) + tile, U32),
                        pltpu.SemaphoreType.DMA, pltpu.SemaphoreType.DMA],
    )
    return pl.pallas_call(
        functools.partial(_dispatch_body, first_call),
        name="dispatch",
        grid_spec=grid_spec,
        out_shape=jax.ShapeDtypeStruct((n_rows,) + tile, U32),
        input_output_aliases=aliases,
        compiler_params=_params(("arbitrary",)),
    )(*args)


def _expert_body(be_ref, nb_ref, xs_ref, wg_ref, wu_ref, wd_ref, ys_ref, wg_b, wu_b, wd_b):
    i = pl.program_id(0)

    @pl.when(jnp.logical_or(i == 0, be_ref[i] != be_ref[jnp.maximum(i - 1, 0)]))
    def _():
        wg_b[...] = wg_ref[...].astype(BF16)
        wu_b[...] = wu_ref[...].astype(BF16)
        wd_b[...] = wd_ref[...].astype(BF16)

    @pl.when(i < nb_ref[0])
    def _():
        x = _load_packed_tiles(xs_ref, EXPERT_BLOCK)
        h = (_silu(_dot(x, wg_b[...])) * _dot(x, wu_b[...])).astype(BF16)
        _store_row_tiles(ys_ref, _dot(h, wd_b[...]))

    @pl.when(i >= nb_ref[0])
    def _():
        ys_ref[...] = jnp.zeros(ys_ref.shape, F32)


def _experts(xs, block_e, nb_used, wg, wu, wd):
    rows = xs.shape[0] // PACK_TILE
    n_blocks = rows // EXPERT_BLOCK
    in_rows = pl.BlockSpec((EXPERT_BLOCK * PACK_TILE, LANES),
                           lambda i, be, nb: (jnp.minimum(i, nb[0] - 1), 0))
    grid_spec = pltpu.PrefetchScalarGridSpec(
        num_scalar_prefetch=2,
        grid=(n_blocks,),
        in_specs=[in_rows,
                  pl.BlockSpec((None, D_MODEL, D_EXPERT), lambda i, be, nb: (be[i], 0, 0)),
                  pl.BlockSpec((None, D_MODEL, D_EXPERT), lambda i, be, nb: (be[i], 0, 0)),
                  pl.BlockSpec((None, D_EXPERT, D_MODEL), lambda i, be, nb: (be[i], 0, 0))],
        out_specs=pl.BlockSpec((EXPERT_BLOCK * ROW_TILE, LANES), lambda i, be, nb: (i, 0)),
        scratch_shapes=[pltpu.VMEM((D_MODEL, D_EXPERT), BF16), pltpu.VMEM((D_MODEL, D_EXPERT), BF16),
                        pltpu.VMEM((D_EXPERT, D_MODEL), BF16)],
    )
    return pl.pallas_call(
        _expert_body,
        name="experts",
        grid_spec=grid_spec,
        out_shape=jax.ShapeDtypeStruct((rows * ROW_TILE, LANES), F32),
        compiler_params=_params(("arbitrary",)),
    )(block_e, nb_used, xs, wg, wu, wd)


def _combine_body(dest_ref, dnext_ref, wts_ref, x1_ref, ys_ref, wsg, wsu, wsd, g_ref, b_ref,
                  y_ref, buf, sems):
    i = pl.program_id(0)
    n_tiles = pl.num_programs(0)
    tc = x1_ref.shape[0]
    slot = i % 2

    def issue(d_ref, s, g, carry):
        for u in range(ROW_UNROLL):
            t = g * ROW_UNROLL + u
            for k in range(TOP_K):
                pltpu.make_async_copy(_token_tile(ys_ref, d_ref[k, t]),
                                      _token_tile(buf.at[s, k], t),
                                      sems.at[s]).start(priority=k % 2)
        return carry

    n_groups = tc // ROW_UNROLL

    @pl.when(i == 0)
    def _():
        lax.fori_loop(0, n_groups, functools.partial(issue, dest_ref, slot), 0)

    @pl.when(i + 1 < n_tiles)
    def _():
        lax.fori_loop(0, n_groups, functools.partial(issue, dnext_ref, 1 - slot), 0)

    x1 = x1_ref[...]
    xb = x1.astype(BF16)
    hs = (_silu(_dot(xb, wsg[...])) * _dot(xb, wsu[...])).astype(BF16)
    shared = _dot(hs, wsd[...])

    for k in range(TOP_K):
        pltpu.make_async_copy(ys_ref.at[pl.ds(0, tc * ROW_TILE)], buf.at[slot, k],
                              sems.at[slot]).wait()

    w = wts_ref[...]
    routed = w[:, 0:1] * _load_row_tiles(buf.at[slot, 0], tc)
    for k in range(1, TOP_K):
        routed = routed + w[:, k:k + 1] * _load_row_tiles(buf.at[slot, k], tc)
    y_ref[...] = _layer_norm(DEEPNORM_ALPHA * x1 + (routed + shared), g_ref[...], b_ref[...])


def _combine(dest, wts_t, x1, ys, wsg, wsu, wsd, ln_g, ln_b, tc):
    n = x1.shape[0]
    last = n // tc - 1
    dest_cur = pl.BlockSpec((TOP_K, tc), lambda i: (0, i), memory_space=pltpu.SMEM)
    dest_next = pl.BlockSpec((TOP_K, tc), lambda i: (0, jnp.minimum(i + 1, last)),
                             memory_space=pltpu.SMEM)
    full = lambda a: pl.BlockSpec(a.shape, lambda i: (0, 0))
    rows = lambda w: pl.BlockSpec((tc, w), lambda i: (i, 0))
    return pl.pallas_call(
        _combine_body,
        name="combine",
        grid=(n // tc,),
        in_specs=[dest_cur, dest_next, rows(LANES), rows(D_MODEL),
                  pl.BlockSpec(memory_space=pl.ANY),
                  full(wsg), full(wsu), full(wsd), full(ln_g), full(ln_b)],
        out_specs=rows(D_MODEL),
        out_shape=jax.ShapeDtypeStruct((n, D_MODEL), F32),
        scratch_shapes=[pltpu.VMEM((2, TOP_K, tc * ROW_TILE, LANES), F32),
                        pltpu.SemaphoreType.DMA((2,))],
        compiler_params=_params(("arbitrary",)),
    )(dest, dest, wts_t, x1, ys, wsg, wsu, wsd, ln_g, ln_b)


def _moe(groups, cnt, moe_w, ln_g, ln_b):
    wg, wu, wd, wsg, wsu, wsd = moe_w
    blk = EXPERT_BLOCK
    n_total = sum(g[0].shape[0] for g in groups)
    n_blocks = -(-(n_total * TOP_K) // blk) + N_EXPERTS
    counts = cnt[:, 0].astype(I32)
    padded = (counts + blk - 1) // blk * blk
    ends = jnp.cumsum(padded)
    pstart = ends - padded
    nb_used = (ends[-1] // blk).astype(I32).reshape(1)
    block_row0 = jnp.arange(n_blocks, dtype=I32) * blk
    block_e = jnp.minimum(jnp.sum((ends[None, :] <= block_row0[:, None]).astype(I32), axis=1),
                          N_EXPERTS - 1).astype(I32)
    pstart_col = pstart.astype(F32).reshape(N_EXPERTS, 1)
    xs, dests = None, []
    for x1, x_tiles, idx, _, rank, tile in groups:
        dest = _dest(idx, rank, pstart_col, tile)
        xs = _dispatch(x_tiles.reshape(x1.shape[0], PACK_TILE, LANES), dest, pstart, counts,
                       n_blocks * blk, tile, xs)
        dests.append(dest)
    ys = _experts(xs.reshape(n_blocks * blk * PACK_TILE, LANES), block_e, nb_used, wg, wu, wd)
    return [_combine(dest, wts_t, x1, ys, wsg, wsu, wsd, ln_g, ln_b, min(tile, 128))
            for dest, (x1, _, _, wts_t, _, tile) in zip(dests, groups)]


def kernel(x_prompt, x_sample, cache_k, cache_v, state_conv, page_table, w_in, lam_q1, lam_k1,
           lam_q2, lam_k2, subln_g, w_dw, b_dw, conv_ln_g, conv_ln_b, w_o, ln1_g, ln1_b,
           w_router, b_router, w_gate, w_up, w_down, w_sh_gate, w_sh_up, w_sh_down, ln2_g,
           ln2_b):
    batch, seq, _ = x_prompt.shape
    dec_b, dec_s, _ = x_sample.shape
    n_p, n_s = batch * seq, dec_b * dec_s
    layer = 0
    lam_init = _lambda_init(layer)
    row = lambda a: a[layer].reshape(1, -1)

    w_in_b = w_in[layer].astype(BF16)
    wo = w_o[layer].astype(BF16)
    wo_a, wo_c = wo[:ATTN_WIDTH], wo[ATTN_WIDTH:]
    wr_t = w_router[layer].T
    wr_hi = wr_t.astype(BF16)
    wr_lo = (wr_t - wr_hi.astype(F32)).astype(BF16)
    br = b_router[layer].reshape(-1, 1)
    moe_w = (w_gate[layer], w_up[layer], w_down[layer]) + tuple(
        w[layer].astype(BF16) for w in (w_sh_gate, w_sh_up, w_sh_down))
    lam_vecs = (row(lam_q1), row(lam_k1), row(lam_q2), row(lam_k2))
    g_sub = row(subln_g)
    conv_w = (w_dw[layer], row(b_dw), row(conv_ln_g), row(conv_ln_b))
    ln1 = (row(ln1_g), row(ln1_b))
    ln2 = (row(ln2_g), row(ln2_b))

    xp = x_prompt.reshape(n_p, D_MODEL)
    wq_t = w_in_b[:, :QK_WIDTH].T
    wv_t = w_in_b[:, 2 * QK_WIDTH:2 * QK_WIDTH + ATTN_WIDTH].T
    qt_p, k_p, v_p, u_p, kb_p, vt_p = _inproj(xp, w_in_b, 512, (wq_t, wv_t))
    o_p = _attn_prompt(qt_p, kb_p, vt_p, lam_vecs, g_sub.reshape(V_DIM, 1), lam_init, batch, seq)
    c_p = _conv_prompt(u_p, *conv_w, batch, seq, 512)
    cnt0 = jnp.zeros((N_EXPERTS, LANES), F32)
    *routed_p, cnt_p = _mix(o_p, c_p, xp, wo_a, wo_c, *ln1, wr_hi, wr_lo, br, cnt0, 512)

    xs = x_sample.reshape(n_s, D_MODEL)
    q_s, k_s, v_s, u_s = _inproj(xs, w_in_b, n_s)
    pool = cache_k.shape[1]
    page_rows = lambda a: a[layer].reshape(pool, PAGE_ROWS, V_DIM)
    new_rows = lambda a: a.reshape(dec_b, dec_s * N_HEADS, V_DIM)
    o_s = _attn_sample(q_s.reshape(dec_b, dec_s, QK_WIDTH), new_rows(k_s), new_rows(v_s),
                       page_rows(cache_k), page_rows(cache_v), page_table, lam_vecs, g_sub,
                       lam_init)
    c_s, st_s = _conv_sample(u_s.reshape(dec_b, dec_s, CONV_CH), state_conv[layer], *conv_w)
    *routed_s, cnt_all = _mix(o_s.reshape(n_s, ATTN_WIDTH), c_s.reshape(n_s, CONV_CH), xs, wo_a,
                              wo_c, *ln1, wr_hi, wr_lo, br, cnt_p, n_s)
    y_p, y_s = _moe([(*routed_p, 256), (*routed_s, 128)], cnt_all, moe_w, *ln2)

    keep = CONV_WIDTH - 1
    u_p3 = u_p.reshape(batch, seq, CONV_CH)
    return (y_p.reshape(batch, seq, D_MODEL),
            y_s.reshape(dec_b, dec_s, D_MODEL),
            k_p.reshape(1, batch, seq, N_HEADS, V_DIM),
            v_p.reshape(1, batch, seq, N_HEADS, V_DIM),
            u_p3[:, seq - keep:, :][None],
            k_s.reshape(1, dec_b, dec_s, N_HEADS, V_DIM),
            v_s.reshape(1, dec_b, dec_s, N_HEADS, V_DIM),
            st_s[None])
```

```python
import functools
import math

import jax
import jax.numpy as jnp
from jax import lax
from jax.experimental import pallas as pl
from jax.experimental.pallas import tpu as pltpu

F32 = jnp.float32
BF16 = jnp.bfloat16
I32 = jnp.int32
U32 = jnp.uint32

D_MODEL = 1024
N_HEADS = 4
HEAD_DIM = 64
V_DIM = 128
ATTN_WIDTH = N_HEADS * V_DIM
QK_WIDTH = N_HEADS * 2 * HEAD_DIM
ATTN_SCALE = HEAD_DIM ** -0.5
CONV_CH = D_MODEL - ATTN_WIDTH
CONV_WIDTH = 31
N_EXPERTS = 64
N_GROUPS = 8
GROUP_SIZE = N_EXPERTS // N_GROUPS
TOPK_GROUPS = 4
TOP_K = 8
D_EXPERT = D_MODEL // 4
ROUTED_SCALE = 2.5
DEPTH = 1
DEEPNORM_ALPHA = (2 * DEPTH) ** 0.25
NORM_EPS = 1e-5
NEG_INF = -1e30
PAGE_SIZE = 128

LANES = 128
ROW_TILE = D_MODEL // LANES
PACK_TILE = ROW_TILE // 2
VMEM_LIMIT = 48 * 1024 * 1024
PAGES_PER_STEP = 16
PAGE_ROWS = PAGE_SIZE * N_HEADS
EXPERT_BLOCK = 512
HIST = 32


def _lambda_init(layer):
    return 0.8 - 0.6 * math.exp(-0.3 * layer)


def _sigmoid(x):
    return 1.0 / (1.0 + jnp.exp(-x))


def _silu(x):
    return x * _sigmoid(x)


def _layer_norm(x, g, b):
    mu = jnp.mean(x, axis=-1, keepdims=True)
    xc = x - mu
    var = jnp.mean(xc * xc, axis=-1, keepdims=True)
    return xc * lax.rsqrt(var + NORM_EPS) * g + b


def _dot(a, b):
    return jnp.dot(a, b, preferred_element_type=F32)


def _dot_nt(a, b):
    return lax.dot_general(a, b, (((1,), (1,)), ((), ())), preferred_element_type=F32)


def _diff_lambda(lq1, lk1, lq2, lk2, lam_init):
    a = jnp.exp(jnp.sum(lq1 * lk1, axis=-1, keepdims=True))
    b = jnp.exp(jnp.sum(lq2 * lk2, axis=-1, keepdims=True))
    return a - b + lam_init


def _params(dims):
    return pltpu.CompilerParams(dimension_semantics=dims, vmem_limit_bytes=VMEM_LIMIT)


def _store_row_tiles(ref, x):
    m = x.shape[0]
    for j in range(ROW_TILE):
        ref[pl.ds(j, m, stride=ROW_TILE), :] = x[:, j * LANES:(j + 1) * LANES]


def _load_row_tiles(ref, m):
    return jnp.concatenate([ref[pl.ds(j, m, stride=ROW_TILE), :] for j in range(ROW_TILE)],
                           axis=1)


def _token_tile(ref, t):
    return ref.at[pl.ds(pl.multiple_of(t * ROW_TILE, ROW_TILE), ROW_TILE)]


def _store_packed_tiles(ref, x):
    m = x.shape[0]
    bits = pltpu.bitcast(x.astype(BF16).astype(F32), U32)
    half = D_MODEL // 2
    for j in range(PACK_TILE):
        lo = bits[:, j * LANES:(j + 1) * LANES] >> 16
        hi = bits[:, half + j * LANES:half + (j + 1) * LANES]
        ref[pl.ds(j, m, stride=PACK_TILE), :] = lo | hi


def _load_packed_tiles(ref, m):
    words = [ref[pl.ds(j, m, stride=PACK_TILE), :] for j in range(PACK_TILE)]
    lo = [pltpu.bitcast(w << 16, F32) for w in words]
    hi = [pltpu.bitcast(w & jnp.uint32(0xFFFF0000), F32) for w in words]
    return jnp.concatenate(lo + hi, axis=1).astype(BF16)


def _store_head_rows(ref, x):
    m = x.shape[0]
    for h in range(N_HEADS):
        ref[pl.ds(h, m, stride=N_HEADS), :] = x[:, h * V_DIM:(h + 1) * V_DIM]


def _inproj_body(x_ref, w_ref, *refs):
    xb = x_ref[...].astype(BF16)

    def mm(c0):
        return _dot(xb, w_ref[:, c0:c0 + QK_WIDTH])

    k = mm(QK_WIDTH)
    if len(refs) == 4:
        q_ref, k_ref, v_ref, u_ref = refs
        q_ref[...] = mm(0) * ATTN_SCALE
    else:
        wqt_ref, wvt_ref, q_ref, k_ref, v_ref, u_ref, kb_ref, vt_ref = refs
        q_ref[...] = _dot_nt(wqt_ref[...], xb) * ATTN_SCALE
        vt_ref[...] = _dot_nt(wvt_ref[...], xb).astype(BF16)
        kb_ref[...] = k.astype(BF16)
    _store_head_rows(k_ref, k)
    _store_head_rows(v_ref, mm(2 * QK_WIDTH))
    a = mm(2 * QK_WIDTH + ATTN_WIDTH)
    b = mm(2 * QK_WIDTH + ATTN_WIDTH + CONV_CH)
    u_ref[...] = a * _sigmoid(b)


def _inproj(x, w_in_b, tm, transposed_w=()):
    n = x.shape[0]
    wide = QK_WIDTH
    blk = pl.BlockSpec((tm, wide), lambda i: (i, 0))
    hblk = pl.BlockSpec((tm * N_HEADS, V_DIM), lambda i: (i, 0))
    full = lambda a: pl.BlockSpec(a.shape, lambda i: (0, 0))
    row_major = jax.ShapeDtypeStruct((n, wide), F32)
    head_rows = jax.ShapeDtypeStruct((n * N_HEADS, V_DIM), F32)
    out_specs = [blk, hblk, hblk, blk]
    out_shape = [row_major, head_rows, head_rows, row_major]
    if transposed_w:
        t_blk = pl.BlockSpec((None, wide, tm), lambda i: (i, 0, 0))
        out_specs = [t_blk, hblk, hblk, blk, blk, t_blk]
        out_shape = [jax.ShapeDtypeStruct((n // tm, wide, tm), F32), head_rows, head_rows,
                     row_major, jax.ShapeDtypeStruct((n, wide), BF16),
                     jax.ShapeDtypeStruct((n // tm, wide, tm), BF16)]
    return pl.pallas_call(
        _inproj_body,
        name="inproj",
        grid=(n // tm,),
        in_specs=[pl.BlockSpec((tm, D_MODEL), lambda i: (i, 0)), full(w_in_b)]
        + [full(w) for w in transposed_w],
        out_specs=out_specs,
        out_shape=out_shape,
        compiler_params=_params(("parallel",)),
    )(x, w_in_b, *transposed_w)


def _attn_p_body(lam_init, q_ref, k_ref, v_ref, lq1, lk1, lq2, lk2, g_ref, o_ref, acc_ref):
    qi = pl.program_id(2)
    tq = q_ref.shape[1]
    tk = v_ref.shape[2]
    q_t = q_ref[...]
    feat = lax.broadcasted_iota(I32, q_t.shape, 0)
    q_both = jnp.concatenate([jnp.where(feat < HEAD_DIM, q_t, 0.0),
                              jnp.where(feat >= HEAD_DIM, q_t, 0.0)], axis=1).astype(BF16)
    acc_ref[...] = jnp.zeros(acc_ref.shape, F32)

    def chunk(masked, j, carry):
        m_old, l_old = carry
        kc = k_ref[pl.ds(pl.multiple_of(j * tk, tk), tk), :]
        s_t = _dot(kc, q_both)
        if masked:
            key = lax.broadcasted_iota(I32, s_t.shape, 0)
            qry = lax.broadcasted_iota(I32, s_t.shape, 1) % tq
            s_t = jnp.where(key <= qry, s_t, NEG_INF)
        m_new = jnp.maximum(m_old, jnp.max(s_t, axis=0, keepdims=True))
        alpha = jnp.exp(m_old - m_new)
        p_t = jnp.exp(s_t - m_new)
        l_new = alpha * l_old + jnp.sum(p_t, axis=0, keepdims=True)
        acc_ref[...] = alpha * acc_ref[...] + _dot(v_ref[j], p_t.astype(BF16))
        return m_new, l_new

    init = (jnp.full((1, 2 * tq), -jnp.inf, F32), jnp.zeros((1, 2 * tq), F32))
    carry = lax.fori_loop(0, qi, functools.partial(chunk, False), init)
    _, l_all = chunk(True, qi, carry)

    lam = _diff_lambda(lq1[...], lk1[...], lq2[...], lk2[...], lam_init)
    on = acc_ref[...] / l_all
    o_t = on[:, :tq] - lam * on[:, tq:]
    ms = jnp.mean(o_t * o_t, axis=0, keepdims=True)
    o_t = o_t * lax.rsqrt(ms + NORM_EPS) * g_ref[...] * (1.0 - lam_init)
    o_ref[...] = o_t.T


def _attn_prompt(q_t, kb, v_t, lam_vecs, subln_g_col, lam_init, batch, seq):
    t = q_t.shape[2]
    n = kb.shape[0]
    nq = seq // t
    small = lambda w: pl.BlockSpec((1, w), lambda b, h, i: (0, 0))
    return pl.pallas_call(
        functools.partial(_attn_p_body, lam_init),
        name="attn_prompt",
        grid=(batch, N_HEADS, nq),
        in_specs=[pl.BlockSpec((None, V_DIM, t), lambda b, h, i: (b * nq + i, h, 0)),
                  pl.BlockSpec((seq, V_DIM), lambda b, h, i: (b, h)),
                  pl.BlockSpec((nq, V_DIM, t), lambda b, h, i: (b, h, 0))]
        + [small(HEAD_DIM)] * 4 + [pl.BlockSpec((V_DIM, 1), lambda b, h, i: (0, 0))],
        out_specs=pl.BlockSpec((t, V_DIM), lambda b, h, i: (b * nq + i, h)),
        out_shape=jax.ShapeDtypeStruct((n, ATTN_WIDTH), F32),
        scratch_shapes=[pltpu.VMEM((V_DIM, 2 * t), F32)],
        compiler_params=_params(("parallel", "parallel", "parallel")),
    )(q_t, kb, v_t, *lam_vecs, subln_g_col)


def _attn_s_body(lam_init, n_steps, pt_ref, q_ref, kn_ref, vn_ref, *rest):
    npg = PAGES_PER_STEP
    k_refs = rest[:npg]
    v_refs = rest[npg:2 * npg]
    (lq1, lk1, lq2, lk2, g_ref, o_ref, qall, knew, vnew, m_ref, l_ref,
     acc_ref) = rest[2 * npg:]
    j = pl.program_id(1)
    ds = q_ref.shape[0]
    hr = 2 * ds

    def head_rows(ref, h):
        return ref[pl.ds(h, PAGE_SIZE, stride=N_HEADS), :].astype(BF16)

    @pl.when(j == 0)
    def _():
        pieces = []
        for h in range(N_HEADS):
            qh = q_ref[:, h * V_DIM:(h + 1) * V_DIM]
            lane = lax.broadcasted_iota(I32, qh.shape, 1)
            pieces.append(jnp.where(lane < HEAD_DIM, qh, 0.0))
            pieces.append(jnp.where(lane >= HEAD_DIM, qh, 0.0))
        qall[...] = jnp.concatenate(pieces, axis=0).astype(BF16)
        knew[...] = jnp.zeros(knew.shape, F32)
        vnew[...] = jnp.zeros(vnew.shape, F32)
        knew[0:ds * N_HEADS] = kn_ref[...]
        vnew[0:ds * N_HEADS] = vn_ref[...]
        row = lax.broadcasted_iota(I32, (hr, PAGE_SIZE), 0)
        key = lax.broadcasted_iota(I32, (hr, PAGE_SIZE), 1)
        keep = key <= (row % ds)
        for h in range(N_HEADS):
            rows = slice(h * hr, (h + 1) * hr)
            s = jnp.where(keep, _dot_nt(qall[rows], head_rows(knew, h)), NEG_INF)
            m = jnp.max(s, axis=-1, keepdims=True)
            p = jnp.exp(s - m)
            m_ref[rows] = m
            l_ref[rows] = jnp.sum(p, axis=-1, keepdims=True)
            acc_ref[rows] = _dot(p.astype(BF16), head_rows(vnew, h))

    q_all, m_all, l_all, acc_all = qall[...], m_ref[...], l_ref[...], acc_ref[...]
    heads = range(N_HEADS)
    rows = [slice(h * hr, (h + 1) * hr) for h in heads]
    s = [_dot_nt(q_all[rows[h]],
                 jnp.concatenate([head_rows(k_refs[i], h) for i in range(npg)], axis=0))
         for h in heads]
    s = jnp.concatenate(s, axis=0)
    m_new = jnp.maximum(m_all, jnp.max(s, axis=-1, keepdims=True))
    alpha = jnp.exp(m_all - m_new)
    p = jnp.exp(s - m_new)
    l_new = alpha * l_all + jnp.sum(p, axis=-1, keepdims=True)
    pb = p.astype(BF16)
    pv = [_dot(pb[rows[h]],
               jnp.concatenate([head_rows(v_refs[i], h) for i in range(npg)], axis=0))
          for h in heads]
    acc_new = alpha * acc_all + jnp.concatenate(pv, axis=0)
    m_ref[...] = m_new
    l_ref[...] = l_new
    acc_ref[...] = acc_new

    @pl.when(j == n_steps - 1)
    def _():
        lam = _diff_lambda(lq1[...], lk1[...], lq2[...], lk2[...], lam_init)
        on = acc_new / l_new
        for h in range(N_HEADS):
            r0 = h * hr
            o = on[r0:r0 + ds] - lam * on[r0 + ds:r0 + 2 * ds]
            ms = jnp.mean(o * o, axis=-1, keepdims=True)
            o_ref[:, h * V_DIM:(h + 1) * V_DIM] = (
                o * lax.rsqrt(ms + NORM_EPS) * g_ref[...] * (1.0 - lam_init))


def _attn_sample(q, k_new, v_new, cache_k, cache_v, page_table, lam_vecs, subln_g, lam_init):
    b, ds, _ = q.shape
    n_pages = page_table.shape[1]
    n_steps = n_pages // PAGES_PER_STEP
    nrow = N_HEADS * 2 * ds
    pt = page_table.reshape(-1)

    def page_map(i, bi, j, pt_ref):
        return (pt_ref[bi * n_pages + j * PAGES_PER_STEP + i], 0, 0)

    page_specs = [pl.BlockSpec((None, PAGE_ROWS, V_DIM), functools.partial(page_map, i))
                  for i in range(PAGES_PER_STEP)]
    small = lambda w: pl.BlockSpec((1, w), lambda bi, j, pt_ref: (0, 0))
    new_spec = pl.BlockSpec((None, ds * N_HEADS, V_DIM), lambda bi, j, pt_ref: (bi, 0, 0))
    qo_spec = pl.BlockSpec((None, ds, ATTN_WIDTH), lambda bi, j, pt_ref: (bi, 0, 0))
    grid_spec = pltpu.PrefetchScalarGridSpec(
        num_scalar_prefetch=1,
        grid=(b, n_steps),
        in_specs=[qo_spec, new_spec, new_spec] + page_specs + page_specs
        + [small(HEAD_DIM)] * 4 + [small(V_DIM)],
        out_specs=qo_spec,
        scratch_shapes=[pltpu.VMEM((nrow, V_DIM), BF16),
                        pltpu.VMEM((PAGE_ROWS, V_DIM), F32),
                        pltpu.VMEM((PAGE_ROWS, V_DIM), F32),
                        pltpu.VMEM((nrow, 1), F32),
                        pltpu.VMEM((nrow, 1), F32),
                        pltpu.VMEM((nrow, V_DIM), F32)],
    )
    return pl.pallas_call(
        functools.partial(_attn_s_body, lam_init, n_steps),
        name="attn_sample",
        grid_spec=grid_spec,
        out_shape=jax.ShapeDtypeStruct((b, ds, ATTN_WIDTH), F32),
        compiler_params=_params(("parallel", "arbitrary")),
    )(pt, q, k_new, v_new, *([cache_k] * PAGES_PER_STEP), *([cache_v] * PAGES_PER_STEP),
      *lam_vecs, subln_g)


def _conv_taps(ext_ref, w_ref, bias, rows):
    off = HIST - (CONV_WIDTH - 1)
    acc = jnp.broadcast_to(bias, (rows, CONV_CH))
    for jt in range(CONV_WIDTH):
        acc = acc + w_ref[jt:jt + 1, :] * ext_ref[pl.ds(jt + off, rows), :]
    return acc


CONV_ROWS = 64
SUBLANES = 8


def _conv_p_body(u_ref, h_ref, w_ref, b_ref, g_ref, beta_ref, c_ref, ext_ref, sh_ref):
    t = pl.program_id(1)
    tm = u_ref.shape[0]
    hist = h_ref[...]
    ext_ref[0:HIST] = jnp.where(t == 0, jnp.zeros_like(hist), hist)
    ext_ref[HIST:HIST + tm] = u_ref[...]
    span = sh_ref.shape[1]
    for r in range(1, SUBLANES):
        sh_ref[r - 1] = ext_ref[pl.ds(r, span), :]
    off = HIST - (CONV_WIDTH - 1)
    bias = b_ref[...]
    for c0 in range(0, tm, CONV_ROWS):
        acc = jnp.broadcast_to(bias, (CONV_ROWS, CONV_CH))
        for jt in range(CONV_WIDTH):
            r = (jt + off) % SUBLANES
            base = c0 + jt + off - r
            src = ext_ref if r == 0 else sh_ref.at[r - 1]
            acc = acc + w_ref[jt:jt + 1, :] * src[pl.ds(base, CONV_ROWS), :]
        c_ref[c0:c0 + CONV_ROWS, :] = _silu(_layer_norm(acc, g_ref[...], beta_ref[...]))


def _conv_prompt(u, w_dw, b_dw, g, beta, batch, seq, tm):
    n = u.shape[0]
    nt = seq // tm
    per = tm // HIST
    cur = lambda b, t: (b * nt + t, 0)
    prev = lambda b, t: (jnp.maximum((b * nt + t) * per - 1, 0), 0)
    small = lambda r: pl.BlockSpec((r, CONV_CH), lambda b, t: (0, 0))
    return pl.pallas_call(
        _conv_p_body,
        name="conv_prompt",
        grid=(batch, nt),
        in_specs=[pl.BlockSpec((tm, CONV_CH), cur), pl.BlockSpec((HIST, CONV_CH), prev),
                  small(CONV_WIDTH), small(1), small(1), small(1)],
        out_specs=pl.BlockSpec((tm, CONV_CH), cur),
        out_shape=jax.ShapeDtypeStruct((n, CONV_CH), F32),
        scratch_shapes=[pltpu.VMEM((HIST + tm, CONV_CH), F32),
                        pltpu.VMEM((SUBLANES - 1, HIST + tm - SUBLANES, CONV_CH), F32)],
        compiler_params=_params(("parallel", "parallel")),
    )(u, u, w_dw, b_dw, g, beta)


def _conv_s_body(u_ref, st_ref, w_ref, b_ref, g_ref, beta_ref, c_ref, ns_ref, ext_ref):
    ds = u_ref.shape[0]
    keep = CONV_WIDTH - 1
    off = HIST - keep
    ext_ref[0:off] = jnp.zeros((off, CONV_CH), F32)
    ext_ref[off:HIST] = st_ref[...]
    ext_ref[HIST:HIST + ds] = u_ref[...]
    y = _conv_taps(ext_ref, w_ref, b_ref[...], ds)
    c_ref[...] = _silu(_layer_norm(y, g_ref[...], beta_ref[...]))
    ns_ref[...] = ext_ref[HIST + ds - keep:HIST + ds]


def _conv_sample(u, state, w_dw, b_dw, g, beta):
    b, ds, _ = u.shape
    keep = CONV_WIDTH - 1
    per_b = lambda r: pl.BlockSpec((None, r, CONV_CH), lambda i: (i, 0, 0))
    small = lambda r: pl.BlockSpec((r, CONV_CH), lambda i: (0, 0))
    return pl.pallas_call(
        _conv_s_body,
        name="conv_sample",
        grid=(b,),
        in_specs=[per_b(ds), per_b(keep), small(CONV_WIDTH), small(1), small(1), small(1)],
        out_specs=[per_b(ds), per_b(keep)],
        out_shape=[jax.ShapeDtypeStruct((b, ds, CONV_CH), F32),
                   jax.ShapeDtypeStruct((b, keep, CONV_CH), F32)],
        scratch_shapes=[pltpu.VMEM((HIST + ds, CONV_CH), F32)],
        compiler_params=_params(("parallel",)),
    )(u, state, w_dw, b_dw, g, beta)


def _first_index(hit, iota, limit, axis):
    return jnp.min(jnp.where(hit, iota, limit), axis=axis, keepdims=True)


def _route(x1, wr_hi, wr_lo, bias):
    tm = x1.shape[0]
    x_hi = x1.astype(BF16)
    x_lo = (x1 - x_hi.astype(F32)).astype(BF16)
    logits = _dot_nt(wr_hi, x_hi) + (_dot_nt(wr_lo, x_hi) + _dot_nt(wr_hi, x_lo))
    scores = _sigmoid(logits)
    sel = scores + bias
    sel3 = sel.reshape(N_GROUPS, GROUP_SIZE, tm)
    member = lax.broadcasted_iota(I32, sel3.shape, 1)
    m1 = jnp.max(sel3, axis=1, keepdims=True)
    i1 = _first_index(sel3 == m1, member, GROUP_SIZE, 1)
    m2 = jnp.max(jnp.where(member == i1, -jnp.inf, sel3), axis=1, keepdims=True)
    gs = jnp.broadcast_to(m1 + m2, sel3.shape).reshape(N_EXPERTS, tm)
    eiota = lax.broadcasted_iota(I32, (N_EXPERTS, tm), 0)
    giota = eiota // GROUP_SIZE
    gmask = jnp.zeros((N_EXPERTS, tm), jnp.bool_)
    for _ in range(TOPK_GROUPS):
        m = jnp.max(gs, axis=0, keepdims=True)
        gi = _first_index(gs == m, giota, N_GROUPS, 0)
        pick = giota == gi
        gmask = jnp.logical_or(gmask, pick)
        gs = jnp.where(pick, -jnp.inf, gs)
    selm = jnp.where(gmask, sel, NEG_INF)
    idx_rows, w_rows = [], []
    for _ in range(TOP_K):
        m = jnp.max(selm, axis=0, keepdims=True)
        ei = _first_index(selm == m, eiota, N_EXPERTS, 0)
        pick = eiota == ei
        idx_rows.append(ei)
        w_rows.append(jnp.sum(jnp.where(pick, scores, 0.0), axis=0, keepdims=True))
        selm = jnp.where(pick, -jnp.inf, selm)
    idx = jnp.concatenate(idx_rows, axis=0)
    w = jnp.concatenate(w_rows, axis=0)
    w = w / jnp.sum(w, axis=0, keepdims=True) * ROUTED_SCALE
    return idx, w


def _mix_body(o_ref, c_ref, x_ref, wo_a, wo_c, g_ref, b_ref, wrh_ref, wrl_ref, br_ref, cnt0_ref,
              x1_ref, xrt_ref, idx_ref, wts_ref, rank_ref, cnt_ref, base_ref):
    i = pl.program_id(0)
    tm = x_ref.shape[0]

    @pl.when(i == 0)
    def _():
        base_ref[...] = cnt0_ref[...]

    mix = _dot(o_ref[...].astype(BF16), wo_a[...]) + _dot(c_ref[...].astype(BF16), wo_c[...])
    x1 = _layer_norm(DEEPNORM_ALPHA * x_ref[...] + mix, g_ref[...], b_ref[...])
    x1_ref[...] = x1
    _store_packed_tiles(xrt_ref, x1)

    idx, w = _route(x1, wrh_ref[...], wrl_ref[...], br_ref[...])
    idx_ref[...] = idx
    w_rows = jnp.concatenate([w, jnp.zeros((LANES - TOP_K, tm), F32)], axis=0)
    wts_ref[...] = w_rows.T

    eiota = lax.broadcasted_iota(I32, (N_EXPERTS, tm), 0)
    chosen = jnp.zeros((N_EXPERTS, tm), F32)
    for k in range(TOP_K):
        chosen = chosen + jnp.where(eiota == idx[k:k + 1, :], 1.0, 0.0)
    r = lax.broadcasted_iota(I32, (tm, tm), 0)
    c = lax.broadcasted_iota(I32, (tm, tm), 1)
    before = jnp.where(r < c, 1.0, 0.0).astype(BF16)
    base = base_ref[:, 0:1]
    rank_full = _dot(chosen.astype(BF16), before) + base
    rows = [jnp.sum(jnp.where(eiota == idx[k:k + 1, :], rank_full, 0.0), axis=0, keepdims=True)
            for k in range(TOP_K)]
    rank_ref[...] = jnp.concatenate(rows, axis=0).astype(I32)
    total = base + jnp.sum(chosen, axis=1, keepdims=True)
    base_ref[...] = jnp.broadcast_to(total, base_ref.shape)
    cnt_ref[...] = jnp.broadcast_to(total, cnt_ref.shape)


def _mix(o, c, x, wo_a, wo_c, ln_g, ln_b, wr_hi, wr_lo, b_router, cnt0, tm):
    n = x.shape[0]
    rows = lambda w: pl.BlockSpec((tm, w), lambda i: (i, 0))
    full = lambda a: pl.BlockSpec(a.shape, lambda i: (0, 0))
    cols = pl.BlockSpec((TOP_K, tm), lambda i: (0, i))
    cnt = pl.BlockSpec((N_EXPERTS, LANES), lambda i: (0, 0))
    return pl.pallas_call(
        _mix_body,
        name="mix",
        grid=(n // tm,),
        in_specs=[rows(ATTN_WIDTH), rows(CONV_CH), rows(D_MODEL), full(wo_a), full(wo_c),
                  full(ln_g), full(ln_b), full(wr_hi), full(wr_lo), full(b_router), cnt],
        out_specs=[rows(D_MODEL), pl.BlockSpec((tm * PACK_TILE, LANES), lambda i: (i, 0)), cols,
                   rows(LANES), cols, cnt],
        out_shape=[jax.ShapeDtypeStruct((n, D_MODEL), F32),
                   jax.ShapeDtypeStruct((n * PACK_TILE, LANES), U32),
                   jax.ShapeDtypeStruct((TOP_K, n), I32),
                   jax.ShapeDtypeStruct((n, LANES), F32),
                   jax.ShapeDtypeStruct((TOP_K, n), I32),
                   jax.ShapeDtypeStruct((N_EXPERTS, LANES), F32)],
        scratch_shapes=[pltpu.VMEM((N_EXPERTS, LANES), F32)],
        compiler_params=_params(("arbitrary",)),
    )(o, c, x, wo_a, wo_c, ln_g, ln_b, wr_hi, wr_lo, b_router, cnt0)


def _dest_body(idx_ref, rank_ref, ps_ref, dest_ref):
    idx = idx_ref[...]
    tm = idx.shape[1]
    eiota = lax.broadcasted_iota(I32, (N_EXPERTS, tm), 0)
    ps = ps_ref[...]
    rows = [jnp.sum(jnp.where(eiota == idx[k:k + 1, :], ps, 0.0), axis=0, keepdims=True)
            for k in range(TOP_K)]
    dest_ref[...] = jnp.concatenate(rows, axis=0).astype(I32) + rank_ref[...]


def _dest(idx, rank, pstart_col, tm):
    n = idx.shape[1]
    cols = pl.BlockSpec((TOP_K, tm), lambda i: (0, i))
    return pl.pallas_call(
        _dest_body,
        name="dest",
        grid=(n // tm,),
        in_specs=[cols, cols, pl.BlockSpec((N_EXPERTS, 1), lambda i: (0, 0))],
        out_specs=cols,
        out_shape=jax.ShapeDtypeStruct((TOP_K, n), I32),
        compiler_params=_params(("parallel",)),
    )(idx, rank, pstart_col)


PAD_PIECES = tuple(1 << s for s in reversed(range(EXPERT_BLOCK.bit_length() - 1)))
ROW_UNROLL = 4


def _dispatch_body(first_call, ps_ref, cnt_ref, dest_ref, x_ref, *refs):
    xs_ref, zbuf, sem, zsem = refs[-4:]
    i = pl.program_id(0)
    ts = x_ref.shape[0]

    @pl.when(jnp.logical_and(i == 0, first_call))
    def _():
        zbuf[...] = jnp.zeros(zbuf.shape, U32)

        def pad_rows(wait, e, carry):
            cnt = cnt_ref[e]
            base = ps_ref[e] + cnt
            pad = (-cnt) & (EXPERT_BLOCK - 1)
            for p in PAD_PIECES:
                @pl.when((pad & p) != 0)
                def _():
                    cp = pltpu.make_async_copy(zbuf.at[pl.ds(0, p)], xs_ref.at[pl.ds(base, p)],
                                               zsem)
                    if wait:
                        cp.wait()
                    else:
                        cp.start()
                base = base + (pad & p)
            return carry

        lax.fori_loop(0, N_EXPERTS, functools.partial(pad_rows, False), 0)
        lax.fori_loop(0, N_EXPERTS, functools.partial(pad_rows, True), 0)

    def row_copy(t, k):
        return pltpu.make_async_copy(x_ref.at[t], xs_ref.at[dest_ref[k, t]], sem)

    def issue(g, carry):
        for u in range(ROW_UNROLL):
            for k in range(TOP_K):
                row_copy(g * ROW_UNROLL + u, k).start(priority=k % 2)
        return carry

    lax.fori_loop(0, ts // ROW_UNROLL, issue, 0)
    for k in range(TOP_K):
        pltpu.make_async_copy(x_ref, xs_ref.at[pl.ds(0, ts)], sem).wait()


def _dispatch(x_tiles, dest, pstart, counts, n_rows, ts, xs_prev=None):
    n = x_tiles.shape[0]
    tile = (PACK_TILE, LANES)
    first_call = xs_prev is None
    in_specs = [pl.BlockSpec((TOP_K, ts), lambda i, ps, cn: (0, i), memory_space=pltpu.SMEM),
                pl.BlockSpec((ts,) + tile, lambda i, ps, cn: (i, 0, 0))]
    args = [pstart, counts, dest, x_tiles]
    aliases = {}
    if not first_call:
        in_specs.append(pl.BlockSpec(memory_space=pl.ANY))
        args.append(xs_prev)
        aliases = {len(args) - 1: 0}
    grid_spec = pltpu.PrefetchScalarGridSpec(
        num_scalar_prefetch=2,
        grid=(n // ts,),
        in_specs=in_specs,
        out_specs=pl.BlockSpec(memory_space=pl.ANY),
        scratch_shapes=[pltpu.VMEM((PAD_PIECES[0],) + tile, U32),
                        pltpu.SemaphoreType.DMA, pltpu.SemaphoreType.DMA],
    )
    return pl.pallas_call(
        functools.partial(_dispatch_body, first_call),
        name="dispatch",
        grid_spec=grid_spec,
        out_shape=jax.ShapeDtypeStruct((n_rows,) + tile, U32),
        input_output_aliases=aliases,
        compiler_params=_params(("arbitrary",)),
    )(*args)


def _expert_body(be_ref, nb_ref, xs_ref, wg_ref, wu_ref, wd_ref, ys_ref, wg_b, wu_b, wd_b):
    i = pl.program_id(0)

    @pl.when(jnp.logical_or(i == 0, be_ref[i] != be_ref[jnp.maximum(i - 1, 0)]))
    def _():
        wg_b[...] = wg_ref[...].astype(BF16)
        wu_b[...] = wu_ref[...].astype(BF16)
        wd_b[...] = wd_ref[...].astype(BF16)

    @pl.when(i < nb_ref[0])
    def _():
        x = _load_packed_tiles(xs_ref, EXPERT_BLOCK)
        h = (_silu(_dot(x, wg_b[...])) * _dot(x, wu_b[...])).astype(BF16)
        _store_row_tiles(ys_ref, _dot(h, wd_b[...]))

    @pl.when(i >= nb_ref[0])
    def _():
        ys_ref[...] = jnp.zeros(ys_ref.shape, F32)


def _experts(xs, block_e, nb_used, wg, wu, wd):
    rows = xs.shape[0] // PACK_TILE
    n_blocks = rows // EXPERT_BLOCK
    in_rows = pl.BlockSpec((EXPERT_BLOCK * PACK_TILE, LANES),
                           lambda i, be, nb: (jnp.minimum(i, nb[0] - 1), 0))
    grid_spec = pltpu.PrefetchScalarGridSpec(
        num_scalar_prefetch=2,
        grid=(n_blocks,),
        in_specs=[in_rows,
                  pl.BlockSpec((None, D_MODEL, D_EXPERT), lambda i, be, nb: (be[i], 0, 0)),
                  pl.BlockSpec((None, D_MODEL, D_EXPERT), lambda i, be, nb: (be[i], 0, 0)),
                  pl.BlockSpec((None, D_EXPERT, D_MODEL), lambda i, be, nb: (be[i], 0, 0))],
        out_specs=pl.BlockSpec((EXPERT_BLOCK * ROW_TILE, LANES), lambda i, be, nb: (i, 0)),
        scratch_shapes=[pltpu.VMEM((D_MODEL, D_EXPERT), BF16), pltpu.VMEM((D_MODEL, D_EXPERT), BF16),
                        pltpu.VMEM((D_EXPERT, D_MODEL), BF16)],
    )
    return pl.pallas_call(
        _expert_body,
        name="experts",
        grid_spec=grid_spec,
        out_shape=jax.ShapeDtypeStruct((rows * ROW_TILE, LANES), F32),
        compiler_params=_params(("arbitrary",)),
    )(block_e, nb_used, xs, wg, wu, wd)


def _combine_body(dest_ref, dnext_ref, wts_ref, x1_ref, ys_ref, wsg, wsu, wsd, g_ref, b_ref,
                  y_ref, buf, sems):
    i = pl.program_id(0)
    n_tiles = pl.num_programs(0)
    tc = x1_ref.shape[0]
    slot = i % 2

    def issue(d_ref, s, g, carry):
        for u in range(ROW_UNROLL):
            t = g * ROW_UNROLL + u
            for k in range(TOP_K):
                pltpu.make_async_copy(_token_tile(ys_ref, d_ref[k, t]),
                                      _token_tile(buf.at[s, k], t),
                                      sems.at[s]).start(priority=k % 2)
        return carry

    n_groups = tc // ROW_UNROLL

    @pl.when(i == 0)
    def _():
        lax.fori_loop(0, n_groups, functools.partial(issue, dest_ref, slot), 0)

    @pl.when(i + 1 < n_tiles)
    def _():
        lax.fori_loop(0, n_groups, functools.partial(issue, dnext_ref, 1 - slot), 0)

    x1 = x1_ref[...]
    xb = x1.astype(BF16)
    hs = (_silu(_dot(xb, wsg[...])) * _dot(xb, wsu[...])).astype(BF16)
    shared = _dot(hs, wsd[...])

    for k in range(TOP_K):
        pltpu.make_async_copy(ys_ref.at[pl.ds(0, tc * ROW_TILE)], buf.at[slot, k],
                              sems.at[slot]).wait()

    w = wts_ref[...]
    routed = w[:, 0:1] * _load_row_tiles(buf.at[slot, 0], tc)
    for k in range(1, TOP_K):
        routed = routed + w[:, k:k + 1] * _load_row_tiles(buf.at[slot, k], tc)
    y_ref[...] = _layer_norm(DEEPNORM_ALPHA * x1 + (routed + shared), g_ref[...], b_ref[...])


def _combine(dest, wts_t, x1, ys, wsg, wsu, wsd, ln_g, ln_b, tc):
    n = x1.shape[0]
    last = n // tc - 1
    dest_cur = pl.BlockSpec((TOP_K, tc), lambda i: (0, i), memory_space=pltpu.SMEM)
    dest_next = pl.BlockSpec((TOP_K, tc), lambda i: (0, jnp.minimum(i + 1, last)),
                             memory_space=pltpu.SMEM)
    full = lambda a: pl.BlockSpec(a.shape, lambda i: (0, 0))
    rows = lambda w: pl.BlockSpec((tc, w), lambda i: (i, 0))
    return pl.pallas_call(
        _combine_body,
        name="combine",
        grid=(n // tc,),
        in_specs=[dest_cur, dest_next, rows(LANES), rows(D_MODEL),
                  pl.BlockSpec(memory_space=pl.ANY),
                  full(wsg), full(wsu), full(wsd), full(ln_g), full(ln_b)],
        out_specs=rows(D_MODEL),
        out_shape=jax.ShapeDtypeStruct((n, D_MODEL), F32),
        scratch_shapes=[pltpu.VMEM((2, TOP_K, tc * ROW_TILE, LANES), F32),
                        pltpu.SemaphoreType.DMA((2,))],
        compiler_params=_params(("arbitrary",)),
    )(dest, dest, wts_t, x1, ys, wsg, wsu, wsd, ln_g, ln_b)


def _moe(groups, cnt, moe_w, ln_g, ln_b):
    wg, wu, wd, wsg, wsu, wsd = moe_w
    blk = EXPERT_BLOCK
    n_total = sum(g[0].shape[0] for g in groups)
    n_blocks = -(-(n_total * TOP_K) // blk) + N_EXPERTS
    counts = cnt[:, 0].astype(I32)
    padded = (counts + blk - 1) // blk * blk
    ends = jnp.cumsum(padded)
    pstart = ends - padded
    nb_used = (ends[-1] // blk).astype(I32).reshape(1)
    block_row0 = jnp.arange(n_blocks, dtype=I32) * blk
    block_e = jnp.minimum(jnp.sum((ends[None, :] <= block_row0[:, None]).astype(I32), axis=1),
                          N_EXPERTS - 1).astype(I32)
    pstart_col = pstart.astype(F32).reshape(N_EXPERTS, 1)
    xs, dests = None, []
    for x1, x_tiles, idx, _, rank, tile in groups:
        dest = _dest(idx, rank, pstart_col, tile)
        xs = _dispatch(x_tiles.reshape(x1.shape[0], PACK_TILE, LANES), dest, pstart, counts,
                       n_blocks * blk, tile, xs)
        dests.append(dest)
    ys = _experts(xs.reshape(n_blocks * blk * PACK_TILE, LANES), block_e, nb_used, wg, wu, wd)
    return [_combine(dest, wts_t, x1, ys, wsg, wsu, wsd, ln_g, ln_b, min(tile, 128))
            for dest, (x1, _, _, wts_t, _, tile) in zip(dests, groups)]


def kernel(x_prompt, x_sample, cache_k, cache_v, state_conv, page_table, w_in, lam_q1, lam_k1,
           lam_q2, lam_k2, subln_g, w_dw, b_dw, conv_ln_g, conv_ln_b, w_o, ln1_g, ln1_b,
           w_router, b_router, w_gate, w_up, w_down, w_sh_gate, w_sh_up, w_sh_down, ln2_g,
           ln2_b):
    batch, seq, _ = x_prompt.shape
    dec_b, dec_s, _ = x_sample.shape
    n_p, n_s = batch * seq, dec_b * dec_s
    layer = 0
    lam_init = _lambda_init(layer)
    row = lambda a: a[layer].reshape(1, -1)

    w_in_b = w_in[layer].astype(BF16)
    wo = w_o[layer].astype(BF16)
    wo_a, wo_c = wo[:ATTN_WIDTH], wo[ATTN_WIDTH:]
    wr_t = w_router[layer].T
    wr_hi = wr_t.astype(BF16)
    wr_lo = (wr_t - wr_hi.astype(F32)).astype(BF16)
    br = b_router[layer].reshape(-1, 1)
    moe_w = (w_gate[layer], w_up[layer], w_down[layer]) + tuple(
        w[layer].astype(BF16) for w in (w_sh_gate, w_sh_up, w_sh_down))
    lam_vecs = (row(lam_q1), row(lam_k1), row(lam_q2), row(lam_k2))
    g_sub = row(subln_g)
    conv_w = (w_dw[layer], row(b_dw), row(conv_ln_g), row(conv_ln_b))
    ln1 = (row(ln1_g), row(ln1_b))
    ln2 = (row(ln2_g), row(ln2_b))

    xp = x_prompt.reshape(n_p, D_MODEL)
    wq_t = w_in_b[:, :QK_WIDTH].T
    wv_t = w_in_b[:, 2 * QK_WIDTH:2 * QK_WIDTH + ATTN_WIDTH].T
    qt_p, k_p, v_p, u_p, kb_p, vt_p = _inproj(xp, w_in_b, 512, (wq_t, wv_t))
    o_p = _attn_prompt(qt_p, kb_p, vt_p, lam_vecs, g_sub.reshape(V_DIM, 1), lam_init, batch, seq)
    c_p = _conv_prompt(u_p, *conv_w, batch, seq, 512)
    cnt0 = jnp.zeros((N_EXPERTS, LANES), F32)
    *routed_p, cnt_p = _mix(o_p, c_p, xp, wo_a, wo_c, *ln1, wr_hi, wr_lo, br, cnt0, 512)

    xs = x_sample.reshape(n_s, D_MODEL)
    q_s, k_s, v_s, u_s = _inproj(xs, w_in_b, n_s)
    pool = cache_k.shape[1]
    page_rows = lambda a: a[layer].reshape(pool, PAGE_ROWS, V_DIM)
    new_rows = lambda a: a.reshape(dec_b, dec_s * N_HEADS, V_DIM)
    o_s = _attn_sample(q_s.reshape(dec_b, dec_s, QK_WIDTH), new_rows(k_s), new_rows(v_s),
                       page_rows(cache_k), page_rows(cache_v), page_table, lam_vecs, g_sub,
                       lam_init)
    c_s, st_s = _conv_sample(u_s.reshape(dec_b, dec_s, CONV_CH), state_conv[layer], *conv_w)
    *routed_s, cnt_all = _mix(o_s.reshape(n_s, ATTN_WIDTH), c_s.reshape(n_s, CONV_CH), xs, wo_a,
                              wo_c, *ln1, wr_hi, wr_lo, br, cnt_p, n_s)
    y_p, y_s = _moe([(*routed_p, 256), (*routed_s, 128)], cnt_all, moe_w, *ln2)

    keep = CONV_WIDTH - 1
    u_p3 = u_p.reshape(batch, seq, CONV_CH)
    return (y_p.reshape(batch, seq, D_MODEL),
            y_s.reshape(dec_b, dec_s, D_MODEL),
            k_p.reshape(1, batch, seq, N_HEADS, V_DIM),
            v_p.reshape(1, batch, seq, N_HEADS, V_DIM),
            u_p3[:, seq - keep:, :][None],
            k_s.reshape(1, dec_b, dec_s, N_HEADS, V_DIM),
            v_s.reshape(1, dec_b, dec_s, N_HEADS, V_DIM),
            st_s[None])
```

```python
import functools
import math

import jax
import jax.numpy as jnp
from jax import lax
from jax.experimental import pallas as pl
from jax.experimental.pallas import tpu as pltpu

F32 = jnp.float32
BF16 = jnp.bfloat16
I32 = jnp.int32
U32 = jnp.uint32

D_MODEL = 1024
N_HEADS = 4
HEAD_DIM = 64
V_DIM = 128
ATTN_WIDTH = N_HEADS * V_DIM
QK_WIDTH = N_HEADS * 2 * HEAD_DIM
ATTN_SCALE = HEAD_DIM ** -0.5
CONV_CH = D_MODEL - ATTN_WIDTH
CONV_WIDTH = 31
N_EXPERTS = 64
N_GROUPS = 8
GROUP_SIZE = N_EXPERTS // N_GROUPS
TOPK_GROUPS = 4
TOP_K = 8
D_EXPERT = D_MODEL // 4
ROUTED_SCALE = 2.5
DEPTH = 1
DEEPNORM_ALPHA = (2 * DEPTH) ** 0.25
NORM_EPS = 1e-5
NEG_INF = -1e30
PAGE_SIZE = 128

LANES = 128
ROW_TILE = D_MODEL // LANES
PACK_TILE = ROW_TILE // 2
VMEM_LIMIT = 48 * 1024 * 1024
PAGES_PER_STEP = 16
PAGE_ROWS = PAGE_SIZE * N_HEADS
EXPERT_BLOCK = 512
HIST = 32


def _lambda_init(layer):
    return 0.8 - 0.6 * math.exp(-0.3 * layer)


def _sigmoid(x):
    return 1.0 / (1.0 + jnp.exp(-x))


def _silu(x):
    return x * _sigmoid(x)


def _layer_norm(x, g, b):
    mu = jnp.mean(x, axis=-1, keepdims=True)
    xc = x - mu
    var = jnp.mean(xc * xc, axis=-1, keepdims=True)
    return xc * lax.rsqrt(var + NORM_EPS) * g + b


def _dot(a, b):
    return jnp.dot(a, b, preferred_element_type=F32)


def _dot_nt(a, b):
    return lax.dot_general(a, b, (((1,), (1,)), ((), ())), preferred_element_type=F32)


def _diff_lambda(lq1, lk1, lq2, lk2, lam_init):
    a = jnp.exp(jnp.sum(lq1 * lk1, axis=-1, keepdims=True))
    b = jnp.exp(jnp.sum(lq2 * lk2, axis=-1, keepdims=True))
    return a - b + lam_init


def _params(dims):
    return pltpu.CompilerParams(dimension_semantics=dims, vmem_limit_bytes=VMEM_LIMIT)


def _store_row_tiles(ref, x):
    m = x.shape[0]
    for j in range(ROW_TILE):
        ref[pl.ds(j, m, stride=ROW_TILE), :] = x[:, j * LANES:(j + 1) * LANES]


def _load_row_tiles(ref, m):
    return jnp.concatenate([ref[pl.ds(j, m, stride=ROW_TILE), :] for j in range(ROW_TILE)],
                           axis=1)


def _token_tile(ref, t):
    return ref.at[pl.ds(pl.multiple_of(t * ROW_TILE, ROW_TILE), ROW_TILE)]


def _store_packed_tiles(ref, x):
    m = x.shape[0]
    bits = pltpu.bitcast(x.astype(BF16).astype(F32), U32)
    half = D_MODEL // 2
    for j in range(PACK_TILE):
        lo = bits[:, j * LANES:(j + 1) * LANES] >> 16
        hi = bits[:, half + j * LANES:half + (j + 1) * LANES]
        ref[pl.ds(j, m, stride=PACK_TILE), :] = lo | hi


def _load_packed_tiles(ref, m):
    words = [ref[pl.ds(j, m, stride=PACK_TILE), :] for j in range(PACK_TILE)]
    lo = [pltpu.bitcast(w << 16, F32) for w in words]
    hi = [pltpu.bitcast(w & jnp.uint32(0xFFFF0000), F32) for w in words]
    return jnp.concatenate(lo + hi, axis=1).astype(BF16)


def _store_head_rows(ref, x):
    m = x.shape[0]
    for h in range(N_HEADS):
        ref[pl.ds(h, m, stride=N_HEADS), :] = x[:, h * V_DIM:(h + 1) * V_DIM]


def _inproj_body(x_ref, w_ref, *refs):
    xb = x_ref[...].astype(BF16)

    def mm(c0):
        return _dot(xb, w_ref[:, c0:c0 + QK_WIDTH])

    k = mm(QK_WIDTH)
    if len(refs) == 4:
        q_ref, k_ref, v_ref, u_ref = refs
        q_ref[...] = mm(0) * ATTN_SCALE
    else:
        wqt_ref, wvt_ref, q_ref, k_ref, v_ref, u_ref, kb_ref, vt_ref = refs
        q_ref[...] = _dot_nt(wqt_ref[...], xb) * ATTN_SCALE
        vt_ref[...] = _dot_nt(wvt_ref[...], xb).astype(BF16)
        kb_ref[...] = k.astype(BF16)
    _store_head_rows(k_ref, k)
    _store_head_rows(v_ref, mm(2 * QK_WIDTH))
    a = mm(2 * QK_WIDTH + ATTN_WIDTH)
    b = mm(2 * QK_WIDTH + ATTN_WIDTH + CONV_CH)
    u_ref[...] = a * _sigmoid(b)


def _inproj(x, w_in_b, tm, transposed_w=()):
    n = x.shape[0]
    wide = QK_WIDTH
    blk = pl.BlockSpec((tm, wide), lambda i: (i, 0))
    hblk = pl.BlockSpec((tm * N_HEADS, V_DIM), lambda i: (i, 0))
    full = lambda a: pl.BlockSpec(a.shape, lambda i: (0, 0))
    row_major = jax.ShapeDtypeStruct((n, wide), F32)
    head_rows = jax.ShapeDtypeStruct((n * N_HEADS, V_DIM), F32)
    out_specs = [blk, hblk, hblk, blk]
    out_shape = [row_major, head_rows, head_rows, row_major]
    if transposed_w:
        t_blk = pl.BlockSpec((None, wide, tm), lambda i: (i, 0, 0))
        out_specs = [t_blk, hblk, hblk, blk, blk, t_blk]
        out_shape = [jax.ShapeDtypeStruct((n // tm, wide, tm), F32), head_rows, head_rows,
                     row_major, jax.ShapeDtypeStruct((n, wide), BF16),
                     jax.ShapeDtypeStruct((n // tm, wide, tm), BF16)]
    return pl.pallas_call(
        _inproj_body,
        name="inproj",
        grid=(n // tm,),
        in_specs=[pl.BlockSpec((tm, D_MODEL), lambda i: (i, 0)), full(w_in_b)]
        + [full(w) for w in transposed_w],
        out_specs=out_specs,
        out_shape=out_shape,
        compiler_params=_params(("parallel",)),
    )(x, w_in_b, *transposed_w)


def _attn_p_body(lam_init, q_ref, k_ref, v_ref, lq1, lk1, lq2, lk2, g_ref, o_ref, acc_ref):
    qi = pl.program_id(2)
    tq = q_ref.shape[1]
    tk = v_ref.shape[2]
    q_t = q_ref[...]
    feat = lax.broadcasted_iota(I32, q_t.shape, 0)
    q_both = jnp.concatenate([jnp.where(feat < HEAD_DIM, q_t, 0.0),
                              jnp.where(feat >= HEAD_DIM, q_t, 0.0)], axis=1).astype(BF16)
    acc_ref[...] = jnp.zeros(acc_ref.shape, F32)

    def chunk(masked, j, carry):
        m_old, l_old = carry
        kc = k_ref[pl.ds(pl.multiple_of(j * tk, tk), tk), :]
        s_t = _dot(kc, q_both)
        if masked:
            key = lax.broadcasted_iota(I32, s_t.shape, 0)
            qry = lax.broadcasted_iota(I32, s_t.shape, 1) % tq
            s_t = jnp.where(key <= qry, s_t, NEG_INF)
        m_new = jnp.maximum(m_old, jnp.max(s_t, axis=0, keepdims=True))
        alpha = jnp.exp(m_old - m_new)
        p_t = jnp.exp(s_t - m_new)
        l_new = alpha * l_old + jnp.sum(p_t, axis=0, keepdims=True)
        acc_ref[...] = alpha * acc_ref[...] + _dot(v_ref[j], p_t.astype(BF16))
        return m_new, l_new

    init = (jnp.full((1, 2 * tq), -jnp.inf, F32), jnp.zeros((1, 2 * tq), F32))
    carry = lax.fori_loop(0, qi, functools.partial(chunk, False), init)
    _, l_all = chunk(True, qi, carry)

    lam = _diff_lambda(lq1[...], lk1[...], lq2[...], lk2[...], lam_init)
    on = acc_ref[...] / l_all
    o_t = on[:, :tq] - lam * on[:, tq:]
    ms = jnp.mean(o_t * o_t, axis=0, keepdims=True)
    o_t = o_t * lax.rsqrt(ms + NORM_EPS) * g_ref[...] * (1.0 - lam_init)
    o_ref[...] = o_t.T


def _attn_prompt(q_t, kb, v_t, lam_vecs, subln_g_col, lam_init, batch, seq):
    t = q_t.shape[2]
    n = kb.shape[0]
    nq = seq // t
    small = lambda w: pl.BlockSpec((1, w), lambda b, h, i: (0, 0))
    return pl.pallas_call(
        functools.partial(_attn_p_body, lam_init),
        name="attn_prompt",
        grid=(batch, N_HEADS, nq),
        in_specs=[pl.BlockSpec((None, V_DIM, t), lambda b, h, i: (b * nq + i, h, 0)),
                  pl.BlockSpec((seq, V_DIM), lambda b, h, i: (b, h)),
                  pl.BlockSpec((nq, V_DIM, t), lambda b, h, i: (b, h, 0))]
        + [small(HEAD_DIM)] * 4 + [pl.BlockSpec((V_DIM, 1), lambda b, h, i: (0, 0))],
        out_specs=pl.BlockSpec((t, V_DIM), lambda b, h, i: (b * nq + i, h)),
        out_shape=jax.ShapeDtypeStruct((n, ATTN_WIDTH), F32),
        scratch_shapes=[pltpu.VMEM((V_DIM, 2 * t), F32)],
        compiler_params=_params(("parallel", "parallel", "parallel")),
    )(q_t, kb, v_t, *lam_vecs, subln_g_col)


def _attn_s_body(lam_init, n_steps, tokens_per_step, reserve, pt_ref, ps_ref, cnt_ref, q_ref,
                 kn_ref, vn_ref, *rest):
    del pt_ref
    npg = PAGES_PER_STEP
    k_refs = rest[:npg]
    v_refs = rest[npg:2 * npg]
    (lq1, lk1, lq2, lk2, g_ref, dest_ref, x_ref, o_ref, xs_ref, qall, knew, vnew, m_ref, l_ref,
     acc_ref, zbuf, sem, zsem) = rest[2 * npg:]
    j = pl.program_id(1)
    ds = q_ref.shape[0]
    hr = 2 * ds
    step = pl.program_id(0) * n_steps + j

    @pl.when(step == 0)
    def _():
        _zero_segment_tails(ps_ref, cnt_ref, xs_ref, zbuf, zsem, reserve)

    first_token = (step % (x_ref.shape[0] // tokens_per_step)) * tokens_per_step
    _start_row_copies(dest_ref, x_ref, xs_ref, sem, first_token, tokens_per_step)

    def head_rows(ref, h):
        return ref[pl.ds(h, PAGE_SIZE, stride=N_HEADS), :].astype(BF16)

    @pl.when(j == 0)
    def _():
        pieces = []
        for h in range(N_HEADS):
            qh = q_ref[:, h * V_DIM:(h + 1) * V_DIM]
            lane = lax.broadcasted_iota(I32, qh.shape, 1)
            pieces.append(jnp.where(lane < HEAD_DIM, qh, 0.0))
            pieces.append(jnp.where(lane >= HEAD_DIM, qh, 0.0))
        qall[...] = jnp.concatenate(pieces, axis=0).astype(BF16)
        knew[...] = jnp.zeros(knew.shape, F32)
        vnew[...] = jnp.zeros(vnew.shape, F32)
        knew[0:ds * N_HEADS] = kn_ref[...]
        vnew[0:ds * N_HEADS] = vn_ref[...]
        row = lax.broadcasted_iota(I32, (hr, PAGE_SIZE), 0)
        key = lax.broadcasted_iota(I32, (hr, PAGE_SIZE), 1)
        keep = key <= (row % ds)
        for h in range(N_HEADS):
            rows = slice(h * hr, (h + 1) * hr)
            s = jnp.where(keep, _dot_nt(qall[rows], head_rows(knew, h)), NEG_INF)
            m = jnp.max(s, axis=-1, keepdims=True)
            p = jnp.exp(s - m)
            m_ref[rows] = m
            l_ref[rows] = jnp.sum(p, axis=-1, keepdims=True)
            acc_ref[rows] = _dot(p.astype(BF16), head_rows(vnew, h))

    q_all, m_all, l_all, acc_all = qall[...], m_ref[...], l_ref[...], acc_ref[...]
    heads = range(N_HEADS)
    rows = [slice(h * hr, (h + 1) * hr) for h in heads]
    s = [_dot_nt(q_all[rows[h]],
                 jnp.concatenate([head_rows(k_refs[i], h) for i in range(npg)], axis=0))
         for h in heads]
    s = jnp.concatenate(s, axis=0)
    m_new = jnp.maximum(m_all, jnp.max(s, axis=-1, keepdims=True))
    alpha = jnp.exp(m_all - m_new)
    p = jnp.exp(s - m_new)
    l_new = alpha * l_all + jnp.sum(p, axis=-1, keepdims=True)
    pb = p.astype(BF16)
    pv = [_dot(pb[rows[h]],
               jnp.concatenate([head_rows(v_refs[i], h) for i in range(npg)], axis=0))
          for h in heads]
    acc_new = alpha * acc_all + jnp.concatenate(pv, axis=0)
    m_ref[...] = m_new
    l_ref[...] = l_new
    acc_ref[...] = acc_new
    _wait_row_copies(x_ref, xs_ref, sem, tokens_per_step)

    @pl.when(j == n_steps - 1)
    def _():
        lam = _diff_lambda(lq1[...], lk1[...], lq2[...], lk2[...], lam_init)
        on = acc_new / l_new
        for h in range(N_HEADS):
            r0 = h * hr
            o = on[r0:r0 + ds] - lam * on[r0 + ds:r0 + 2 * ds]
            ms = jnp.mean(o * o, axis=-1, keepdims=True)
            o_ref[:, h * V_DIM:(h + 1) * V_DIM] = (
                o * lax.rsqrt(ms + NORM_EPS) * g_ref[...] * (1.0 - lam_init))


def _attn_sample(q, k_new, v_new, cache_k, cache_v, page_table, lam_vecs, subln_g, lam_init,
                 x_tiles, dest, pstart, counts, n_rows, reserve):
    b, ds, _ = q.shape
    n_pages = page_table.shape[1]
    n_steps = n_pages // PAGES_PER_STEP
    nrow = N_HEADS * 2 * ds
    pt = page_table.reshape(-1)
    total_steps = b * n_steps
    n = x_tiles.shape[0]
    tokens_per_step = n // total_steps
    assert tokens_per_step * total_steps == n and LANES % tokens_per_step == 0
    per_block = LANES // tokens_per_step
    tile = (PACK_TILE, LANES)

    def page_map(i, bi, j, pt_ref, *_):
        return (pt_ref[bi * n_pages + j * PAGES_PER_STEP + i], 0, 0)

    page_specs = [pl.BlockSpec((None, PAGE_ROWS, V_DIM), functools.partial(page_map, i))
                  for i in range(PAGES_PER_STEP)]
    token_block = lambda bi, j, *_: (bi * n_steps + j) // per_block
    small = lambda w: pl.BlockSpec((1, w), lambda bi, j, *_: (0, 0))
    new_spec = pl.BlockSpec((None, ds * N_HEADS, V_DIM), lambda bi, j, *_: (bi, 0, 0))
    qo_spec = pl.BlockSpec((None, ds, ATTN_WIDTH), lambda bi, j, *_: (bi, 0, 0))
    grid_spec = pltpu.PrefetchScalarGridSpec(
        num_scalar_prefetch=3,
        grid=(b, n_steps),
        in_specs=[qo_spec, new_spec, new_spec] + page_specs + page_specs
        + [small(HEAD_DIM)] * 4 + [small(V_DIM)]
        + [pl.BlockSpec((TOP_K, LANES), lambda bi, j, *_: (0, token_block(bi, j)),
                        memory_space=pltpu.SMEM),
           pl.BlockSpec((LANES,) + tile, lambda bi, j, *_: (token_block(bi, j), 0, 0))],
        out_specs=[qo_spec, pl.BlockSpec(memory_space=pl.ANY)],
        scratch_shapes=[pltpu.VMEM((nrow, V_DIM), BF16),
                        pltpu.VMEM((PAGE_ROWS, V_DIM), F32),
                        pltpu.VMEM((PAGE_ROWS, V_DIM), F32),
                        pltpu.VMEM((nrow, 1), F32),
                        pltpu.VMEM((nrow, 1), F32),
                        pltpu.VMEM((nrow, V_DIM), F32),
                        pltpu.VMEM((TAIL_PIECES[0],) + tile, U32),
                        pltpu.SemaphoreType.DMA, pltpu.SemaphoreType.DMA],
    )
    return pl.pallas_call(
        functools.partial(_attn_s_body, lam_init, n_steps, tokens_per_step, reserve),
        name="attn_sample",
        grid_spec=grid_spec,
        out_shape=[jax.ShapeDtypeStruct((b, ds, ATTN_WIDTH), F32),
                   jax.ShapeDtypeStruct((n_rows,) + tile, U32)],
        compiler_params=_params(("arbitrary", "arbitrary")),
    )(pt, pstart, counts, q, k_new, v_new, *([cache_k] * PAGES_PER_STEP),
      *([cache_v] * PAGES_PER_STEP), *lam_vecs, subln_g, dest, x_tiles)


def _conv_taps(ext_ref, w_ref, bias, rows):
    off = HIST - (CONV_WIDTH - 1)
    acc = jnp.broadcast_to(bias, (rows, CONV_CH))
    for jt in range(CONV_WIDTH):
        acc = acc + w_ref[jt:jt + 1, :] * ext_ref[pl.ds(jt + off, rows), :]
    return acc


CONV_ROWS = 64
SUBLANES = 8


def _conv_p_body(u_ref, h_ref, w_ref, b_ref, g_ref, beta_ref, c_ref, ext_ref, sh_ref):
    t = pl.program_id(1)
    tm = u_ref.shape[0]
    hist = h_ref[...]
    ext_ref[0:HIST] = jnp.where(t == 0, jnp.zeros_like(hist), hist)
    ext_ref[HIST:HIST + tm] = u_ref[...]
    span = sh_ref.shape[1]
    for r in range(1, SUBLANES):
        sh_ref[r - 1] = ext_ref[pl.ds(r, span), :]
    off = HIST - (CONV_WIDTH - 1)
    bias = b_ref[...]
    for c0 in range(0, tm, CONV_ROWS):
        acc = jnp.broadcast_to(bias, (CONV_ROWS, CONV_CH))
        for jt in range(CONV_WIDTH):
            r = (jt + off) % SUBLANES
            base = c0 + jt + off - r
            src = ext_ref if r == 0 else sh_ref.at[r - 1]
            acc = acc + w_ref[jt:jt + 1, :] * src[pl.ds(base, CONV_ROWS), :]
        c_ref[c0:c0 + CONV_ROWS, :] = _silu(_layer_norm(acc, g_ref[...], beta_ref[...]))


def _conv_prompt(u, w_dw, b_dw, g, beta, batch, seq, tm):
    n = u.shape[0]
    nt = seq // tm
    per = tm // HIST
    cur = lambda b, t: (b * nt + t, 0)
    prev = lambda b, t: (jnp.maximum((b * nt + t) * per - 1, 0), 0)
    small = lambda r: pl.BlockSpec((r, CONV_CH), lambda b, t: (0, 0))
    return pl.pallas_call(
        _conv_p_body,
        name="conv_prompt",
        grid=(batch, nt),
        in_specs=[pl.BlockSpec((tm, CONV_CH), cur), pl.BlockSpec((HIST, CONV_CH), prev),
                  small(CONV_WIDTH), small(1), small(1), small(1)],
        out_specs=pl.BlockSpec((tm, CONV_CH), cur),
        out_shape=jax.ShapeDtypeStruct((n, CONV_CH), F32),
        scratch_shapes=[pltpu.VMEM((HIST + tm, CONV_CH), F32),
                        pltpu.VMEM((SUBLANES - 1, HIST + tm - SUBLANES, CONV_CH), F32)],
        compiler_params=_params(("parallel", "parallel")),
    )(u, u, w_dw, b_dw, g, beta)


def _conv_s_body(u_ref, st_ref, w_ref, b_ref, g_ref, beta_ref, c_ref, ns_ref, ext_ref):
    ds = u_ref.shape[0]
    keep = CONV_WIDTH - 1
    off = HIST - keep
    ext_ref[0:off] = jnp.zeros((off, CONV_CH), F32)
    ext_ref[off:HIST] = st_ref[...]
    ext_ref[HIST:HIST + ds] = u_ref[...]
    y = _conv_taps(ext_ref, w_ref, b_ref[...], ds)
    c_ref[...] = _silu(_layer_norm(y, g_ref[...], beta_ref[...]))
    ns_ref[...] = ext_ref[HIST + ds - keep:HIST + ds]


def _conv_sample(u, state, w_dw, b_dw, g, beta):
    b, ds, _ = u.shape
    keep = CONV_WIDTH - 1
    per_b = lambda r: pl.BlockSpec((None, r, CONV_CH), lambda i: (i, 0, 0))
    small = lambda r: pl.BlockSpec((r, CONV_CH), lambda i: (0, 0))
    return pl.pallas_call(
        _conv_s_body,
        name="conv_sample",
        grid=(b,),
        in_specs=[per_b(ds), per_b(keep), small(CONV_WIDTH), small(1), small(1), small(1)],
        out_specs=[per_b(ds), per_b(keep)],
        out_shape=[jax.ShapeDtypeStruct((b, ds, CONV_CH), F32),
                   jax.ShapeDtypeStruct((b, keep, CONV_CH), F32)],
        scratch_shapes=[pltpu.VMEM((HIST + ds, CONV_CH), F32)],
        compiler_params=_params(("parallel",)),
    )(u, state, w_dw, b_dw, g, beta)


def _first_index(hit, iota, limit, axis):
    return jnp.min(jnp.where(hit, iota, limit), axis=axis, keepdims=True)


def _route(x1, wr_hi, wr_lo, bias):
    tm = x1.shape[0]
    x_hi = x1.astype(BF16)
    x_lo = (x1 - x_hi.astype(F32)).astype(BF16)
    logits = _dot_nt(wr_hi, x_hi) + (_dot_nt(wr_lo, x_hi) + _dot_nt(wr_hi, x_lo))
    scores = _sigmoid(logits)
    sel = scores + bias
    sel3 = sel.reshape(N_GROUPS, GROUP_SIZE, tm)
    member = lax.broadcasted_iota(I32, sel3.shape, 1)
    m1 = jnp.max(sel3, axis=1, keepdims=True)
    i1 = _first_index(sel3 == m1, member, GROUP_SIZE, 1)
    m2 = jnp.max(jnp.where(member == i1, -jnp.inf, sel3), axis=1, keepdims=True)
    gs = jnp.broadcast_to(m1 + m2, sel3.shape).reshape(N_EXPERTS, tm)
    eiota = lax.broadcasted_iota(I32, (N_EXPERTS, tm), 0)
    giota = eiota // GROUP_SIZE
    gmask = jnp.zeros((N_EXPERTS, tm), jnp.bool_)
    for _ in range(TOPK_GROUPS):
        m = jnp.max(gs, axis=0, keepdims=True)
        gi = _first_index(gs == m, giota, N_GROUPS, 0)
        pick = giota == gi
        gmask = jnp.logical_or(gmask, pick)
        gs = jnp.where(pick, -jnp.inf, gs)
    selm = jnp.where(gmask, sel, NEG_INF)
    idx_rows, w_rows = [], []
    for _ in range(TOP_K):
        m = jnp.max(selm, axis=0, keepdims=True)
        ei = _first_index(selm == m, eiota, N_EXPERTS, 0)
        pick = eiota == ei
        idx_rows.append(ei)
        w_rows.append(jnp.sum(jnp.where(pick, scores, 0.0), axis=0, keepdims=True))
        selm = jnp.where(pick, -jnp.inf, selm)
    idx = jnp.concatenate(idx_rows, axis=0)
    w = jnp.concatenate(w_rows, axis=0)
    w = w / jnp.sum(w, axis=0, keepdims=True) * ROUTED_SCALE
    return idx, w


def _mix_body(o_ref, c_ref, x_ref, wo_a, wo_c, g_ref, b_ref, wrh_ref, wrl_ref, br_ref, cnt0_ref,
              x1_ref, xrt_ref, idx_ref, wts_ref, rank_ref, cnt_ref, base_ref):
    i = pl.program_id(0)
    tm = x_ref.shape[0]

    @pl.when(i == 0)
    def _():
        base_ref[...] = cnt0_ref[...]

    mix = _dot(o_ref[...].astype(BF16), wo_a[...]) + _dot(c_ref[...].astype(BF16), wo_c[...])
    x1 = _layer_norm(DEEPNORM_ALPHA * x_ref[...] + mix, g_ref[...], b_ref[...])
    x1_ref[...] = x1
    _store_packed_tiles(xrt_ref, x1)

    idx, w = _route(x1, wrh_ref[...], wrl_ref[...], br_ref[...])
    idx_ref[...] = idx
    w_rows = jnp.concatenate([w, jnp.zeros((LANES - TOP_K, tm), F32)], axis=0)
    wts_ref[...] = w_rows.T

    eiota = lax.broadcasted_iota(I32, (N_EXPERTS, tm), 0)
    chosen = jnp.zeros((N_EXPERTS, tm), F32)
    for k in range(TOP_K):
        chosen = chosen + jnp.where(eiota == idx[k:k + 1, :], 1.0, 0.0)
    r = lax.broadcasted_iota(I32, (tm, tm), 0)
    c = lax.broadcasted_iota(I32, (tm, tm), 1)
    before = jnp.where(r < c, 1.0, 0.0).astype(BF16)
    base = base_ref[:, 0:1]
    rank_full = _dot(chosen.astype(BF16), before) + base
    rows = [jnp.sum(jnp.where(eiota == idx[k:k + 1, :], rank_full, 0.0), axis=0, keepdims=True)
            for k in range(TOP_K)]
    rank_ref[...] = jnp.concatenate(rows, axis=0).astype(I32)
    total = base + jnp.sum(chosen, axis=1, keepdims=True)
    base_ref[...] = jnp.broadcast_to(total, base_ref.shape)
    cnt_ref[...] = jnp.broadcast_to(total, cnt_ref.shape)


def _mix(o, c, x, wo_a, wo_c, ln_g, ln_b, wr_hi, wr_lo, b_router, cnt0, tm):
    n = x.shape[0]
    rows = lambda w: pl.BlockSpec((tm, w), lambda i: (i, 0))
    full = lambda a: pl.BlockSpec(a.shape, lambda i: (0, 0))
    cols = pl.BlockSpec((TOP_K, tm), lambda i: (0, i))
    cnt = pl.BlockSpec((N_EXPERTS, LANES), lambda i: (0, 0))
    return pl.pallas_call(
        _mix_body,
        name="mix",
        grid=(n // tm,),
        in_specs=[rows(ATTN_WIDTH), rows(CONV_CH), rows(D_MODEL), full(wo_a), full(wo_c),
                  full(ln_g), full(ln_b), full(wr_hi), full(wr_lo), full(b_router), cnt],
        out_specs=[rows(D_MODEL), pl.BlockSpec((tm * PACK_TILE, LANES), lambda i: (i, 0)), cols,
                   rows(LANES), cols, cnt],
        out_shape=[jax.ShapeDtypeStruct((n, D_MODEL), F32),
                   jax.ShapeDtypeStruct((n * PACK_TILE, LANES), U32),
                   jax.ShapeDtypeStruct((TOP_K, n), I32),
                   jax.ShapeDtypeStruct((n, LANES), F32),
                   jax.ShapeDtypeStruct((TOP_K, n), I32),
                   jax.ShapeDtypeStruct((N_EXPERTS, LANES), F32)],
        scratch_shapes=[pltpu.VMEM((N_EXPERTS, LANES), F32)],
        compiler_params=_params(("arbitrary",)),
    )(o, c, x, wo_a, wo_c, ln_g, ln_b, wr_hi, wr_lo, b_router, cnt0)


def _dest_body(idx_ref, rank_ref, ps_ref, dest_ref):
    idx = idx_ref[...]
    tm = idx.shape[1]
    eiota = lax.broadcasted_iota(I32, (N_EXPERTS, tm), 0)
    ps = ps_ref[...]
    rows = [jnp.sum(jnp.where(eiota == idx[k:k + 1, :], ps, 0.0), axis=0, keepdims=True)
            for k in range(TOP_K)]
    dest_ref[...] = jnp.concatenate(rows, axis=0).astype(I32) + rank_ref[...]


def _dest(idx, rank, pstart_col, tm):
    n = idx.shape[1]
    cols = pl.BlockSpec((TOP_K, tm), lambda i: (0, i))
    return pl.pallas_call(
        _dest_body,
        name="dest",
        grid=(n // tm,),
        in_specs=[cols, cols, pl.BlockSpec((N_EXPERTS, 1), lambda i: (0, 0))],
        out_specs=cols,
        out_shape=jax.ShapeDtypeStruct((TOP_K, n), I32),
        compiler_params=_params(("parallel",)),
    )(idx, rank, pstart_col)


TAIL_PIECES = tuple(1 << s for s in reversed(range(EXPERT_BLOCK.bit_length())))
ROW_UNROLL = 4


def _zero_segment_tails(ps_ref, cnt_ref, xs_ref, zbuf, zsem, reserve):
    zbuf[...] = jnp.zeros(zbuf.shape, U32)

    def tail(wait, e, carry):
        cnt = cnt_ref[e]
        base = ps_ref[e] + cnt
        seg = (cnt + reserve + EXPERT_BLOCK - 1) // EXPERT_BLOCK * EXPERT_BLOCK
        pad = seg - cnt
        for p in TAIL_PIECES:
            @pl.when((pad & p) != 0)
            def _():
                cp = pltpu.make_async_copy(zbuf.at[pl.ds(0, p)], xs_ref.at[pl.ds(base, p)], zsem)
                if wait:
                    cp.wait()
                else:
                    cp.start()
            base = base + (pad & p)
        return carry

    lax.fori_loop(0, N_EXPERTS, functools.partial(tail, False), 0)
    lax.fori_loop(0, N_EXPERTS, functools.partial(tail, True), 0)


def _start_row_copies(dest_ref, x_ref, xs_ref, sem, first, count):
    def issue(g, carry):
        for u in range(ROW_UNROLL):
            t = first + g * ROW_UNROLL + u
            for k in range(TOP_K):
                pltpu.make_async_copy(x_ref.at[t], xs_ref.at[dest_ref[k, t]],
                                      sem).start(priority=k % 2)
        return carry

    lax.fori_loop(0, count // ROW_UNROLL, issue, 0)


def _wait_row_copies(x_ref, xs_ref, sem, count):
    for _ in range(TOP_K):
        pltpu.make_async_copy(x_ref.at[pl.ds(0, count)], xs_ref.at[pl.ds(0, count)], sem).wait()


def _dispatch_body(dest_ref, x_ref, xs_prev_ref, xs_ref, sem):
    del xs_prev_ref
    ts = x_ref.shape[0]
    _start_row_copies(dest_ref, x_ref, xs_ref, sem, 0, ts)
    _wait_row_copies(x_ref, xs_ref, sem, ts)


def _dispatch(x_tiles, dest, xs_prev, ts):
    n = x_tiles.shape[0]
    return pl.pallas_call(
        _dispatch_body,
        name="dispatch",
        grid=(n // ts,),
        in_specs=[pl.BlockSpec((TOP_K, ts), lambda i: (0, i), memory_space=pltpu.SMEM),
                  pl.BlockSpec((ts, PACK_TILE, LANES), lambda i: (i, 0, 0)),
                  pl.BlockSpec(memory_space=pl.ANY)],
        out_specs=pl.BlockSpec(memory_space=pl.ANY),
        out_shape=jax.ShapeDtypeStruct(xs_prev.shape, xs_prev.dtype),
        scratch_shapes=[pltpu.SemaphoreType.DMA],
        input_output_aliases={2: 0},
        compiler_params=_params(("arbitrary",)),
    )(dest, x_tiles, xs_prev)


def _expert_body(be_ref, nb_ref, xs_ref, wg_ref, wu_ref, wd_ref, ys_ref, wg_b, wu_b, wd_b):
    i = pl.program_id(0)

    @pl.when(jnp.logical_or(i == 0, be_ref[i] != be_ref[jnp.maximum(i - 1, 0)]))
    def _():
        wg_b[...] = wg_ref[...].astype(BF16)
        wu_b[...] = wu_ref[...].astype(BF16)
        wd_b[...] = wd_ref[...].astype(BF16)

    @pl.when(i < nb_ref[0])
    def _():
        x = _load_packed_tiles(xs_ref, EXPERT_BLOCK)
        h = (_silu(_dot(x, wg_b[...])) * _dot(x, wu_b[...])).astype(BF16)
        _store_row_tiles(ys_ref, _dot(h, wd_b[...]))


def _experts(xs, block_e, nb_used, wg, wu, wd):
    rows = xs.shape[0] // PACK_TILE
    n_blocks = rows // EXPERT_BLOCK
    used = lambda i, be, nb: (jnp.minimum(i, nb[0] - 1), 0)
    in_rows = pl.BlockSpec((EXPERT_BLOCK * PACK_TILE, LANES), used)
    grid_spec = pltpu.PrefetchScalarGridSpec(
        num_scalar_prefetch=2,
        grid=(n_blocks,),
        in_specs=[in_rows,
                  pl.BlockSpec((None, D_MODEL, D_EXPERT), lambda i, be, nb: (be[i], 0, 0)),
                  pl.BlockSpec((None, D_MODEL, D_EXPERT), lambda i, be, nb: (be[i], 0, 0)),
                  pl.BlockSpec((None, D_EXPERT, D_MODEL), lambda i, be, nb: (be[i], 0, 0))],
        out_specs=pl.BlockSpec((EXPERT_BLOCK * ROW_TILE, LANES), used),
        scratch_shapes=[pltpu.VMEM((D_MODEL, D_EXPERT), BF16), pltpu.VMEM((D_MODEL, D_EXPERT), BF16),
                        pltpu.VMEM((D_EXPERT, D_MODEL), BF16)],
    )
    return pl.pallas_call(
        _expert_body,
        name="experts",
        grid_spec=grid_spec,
        out_shape=jax.ShapeDtypeStruct((rows * ROW_TILE, LANES), F32),
        compiler_params=_params(("arbitrary",)),
    )(block_e, nb_used, xs, wg, wu, wd)


def _combine_body(dest_ref, dnext_ref, wts_ref, x1_ref, ys_ref, wsg, wsu, wsd, g_ref, b_ref,
                  y_ref, buf, sems):
    i = pl.program_id(0)
    n_tiles = pl.num_programs(0)
    tc = x1_ref.shape[0]
    slot = i % 2

    def issue(d_ref, s, g, carry):
        for u in range(ROW_UNROLL):
            t = g * ROW_UNROLL + u
            for k in range(TOP_K):
                pltpu.make_async_copy(_token_tile(ys_ref, d_ref[k, t]),
                                      _token_tile(buf.at[s, k], t),
                                      sems.at[s]).start(priority=k % 2)
        return carry

    n_groups = tc // ROW_UNROLL

    @pl.when(i == 0)
    def _():
        lax.fori_loop(0, n_groups, functools.partial(issue, dest_ref, slot), 0)

    @pl.when(i + 1 < n_tiles)
    def _():
        lax.fori_loop(0, n_groups, functools.partial(issue, dnext_ref, 1 - slot), 0)

    x1 = x1_ref[...]
    xb = x1.astype(BF16)
    hs = (_silu(_dot(xb, wsg[...])) * _dot(xb, wsu[...])).astype(BF16)
    shared = _dot(hs, wsd[...])

    for k in range(TOP_K):
        pltpu.make_async_copy(ys_ref.at[pl.ds(0, tc * ROW_TILE)], buf.at[slot, k],
                              sems.at[slot]).wait()

    w = wts_ref[...]
    routed = w[:, 0:1] * _load_row_tiles(buf.at[slot, 0], tc)
    for k in range(1, TOP_K):
        routed = routed + w[:, k:k + 1] * _load_row_tiles(buf.at[slot, k], tc)
    y_ref[...] = _layer_norm(DEEPNORM_ALPHA * x1 + (routed + shared), g_ref[...], b_ref[...])


def _combine(dest, wts_t, x1, ys, wsg, wsu, wsd, ln_g, ln_b, tc):
    n = x1.shape[0]
    last = n // tc - 1
    dest_cur = pl.BlockSpec((TOP_K, tc), lambda i: (0, i), memory_space=pltpu.SMEM)
    dest_next = pl.BlockSpec((TOP_K, tc), lambda i: (0, jnp.minimum(i + 1, last)),
                             memory_space=pltpu.SMEM)
    full = lambda a: pl.BlockSpec(a.shape, lambda i: (0, 0))
    rows = lambda w: pl.BlockSpec((tc, w), lambda i: (i, 0))
    return pl.pallas_call(
        _combine_body,
        name="combine",
        grid=(n // tc,),
        in_specs=[dest_cur, dest_next, rows(LANES), rows(D_MODEL),
                  pl.BlockSpec(memory_space=pl.ANY),
                  full(wsg), full(wsu), full(wsd), full(ln_g), full(ln_b)],
        out_specs=rows(D_MODEL),
        out_shape=jax.ShapeDtypeStruct((n, D_MODEL), F32),
        scratch_shapes=[pltpu.VMEM((2, TOP_K, tc * ROW_TILE, LANES), F32),
                        pltpu.SemaphoreType.DMA((2,))],
        compiler_params=_params(("arbitrary",)),
    )(dest, dest, wts_t, x1, ys, wsg, wsu, wsd, ln_g, ln_b)


def _segments(cnt_first, n_first, reserve):
    blk = EXPERT_BLOCK
    assert reserve <= blk
    n_blocks = -(-(n_first * TOP_K + N_EXPERTS * reserve) // blk) + N_EXPERTS
    counts = cnt_first[:, 0].astype(I32)
    padded = (counts + reserve + blk - 1) // blk * blk
    ends = jnp.cumsum(padded)
    pstart = ends - padded
    nb_used = (ends[-1] // blk).astype(I32).reshape(1)
    block_row0 = jnp.arange(n_blocks, dtype=I32) * blk
    block_e = jnp.minimum(jnp.sum((ends[None, :] <= block_row0[:, None]).astype(I32), axis=1),
                          N_EXPERTS - 1).astype(I32)
    return pstart, counts, block_e, nb_used, n_blocks * blk


def kernel(x_prompt, x_sample, cache_k, cache_v, state_conv, page_table, w_in, lam_q1, lam_k1,
           lam_q2, lam_k2, subln_g, w_dw, b_dw, conv_ln_g, conv_ln_b, w_o, ln1_g, ln1_b,
           w_router, b_router, w_gate, w_up, w_down, w_sh_gate, w_sh_up, w_sh_down, ln2_g,
           ln2_b):
    batch, seq, _ = x_prompt.shape
    dec_b, dec_s, _ = x_sample.shape
    n_p, n_s = batch * seq, dec_b * dec_s
    layer = 0
    lam_init = _lambda_init(layer)
    row = lambda a: a[layer].reshape(1, -1)

    w_in_b = w_in[layer].astype(BF16)
    wo = w_o[layer].astype(BF16)
    wo_a, wo_c = wo[:ATTN_WIDTH], wo[ATTN_WIDTH:]
    wr_t = w_router[layer].T
    wr_hi = wr_t.astype(BF16)
    wr_lo = (wr_t - wr_hi.astype(F32)).astype(BF16)
    br = b_router[layer].reshape(-1, 1)
    moe_w = (w_gate[layer], w_up[layer], w_down[layer]) + tuple(
        w[layer].astype(BF16) for w in (w_sh_gate, w_sh_up, w_sh_down))
    lam_vecs = (row(lam_q1), row(lam_k1), row(lam_q2), row(lam_k2))
    g_sub = row(subln_g)
    conv_w = (w_dw[layer], row(b_dw), row(conv_ln_g), row(conv_ln_b))
    ln1 = (row(ln1_g), row(ln1_b))
    ln2 = (row(ln2_g), row(ln2_b))

    xp = x_prompt.reshape(n_p, D_MODEL)
    wq_t = w_in_b[:, :QK_WIDTH].T
    wv_t = w_in_b[:, 2 * QK_WIDTH:2 * QK_WIDTH + ATTN_WIDTH].T
    qt_p, k_p, v_p, u_p, kb_p, vt_p = _inproj(xp, w_in_b, 512, (wq_t, wv_t))
    o_p = _attn_prompt(qt_p, kb_p, vt_p, lam_vecs, g_sub.reshape(V_DIM, 1), lam_init, batch, seq)
    c_p = _conv_prompt(u_p, *conv_w, batch, seq, 512)
    cnt0 = jnp.zeros((N_EXPERTS, LANES), F32)
    x1_p, xt_p, idx_p, wts_p, rank_p, cnt_p = _mix(o_p, c_p, xp, wo_a, wo_c, *ln1, wr_hi, wr_lo,
                                                   br, cnt0, 512)
    pstart, counts_p, block_e, nb_used, n_rows = _segments(cnt_p, n_p, n_s)
    pstart_col = pstart.astype(F32).reshape(N_EXPERTS, 1)
    dest_p = _dest(idx_p, rank_p, pstart_col, 256)

    xs = x_sample.reshape(n_s, D_MODEL)
    q_s, k_s, v_s, u_s = _inproj(xs, w_in_b, n_s)
    pool = cache_k.shape[1]
    page_rows = lambda a: a[layer].reshape(pool, PAGE_ROWS, V_DIM)
    new_rows = lambda a: a.reshape(dec_b, dec_s * N_HEADS, V_DIM)
    o_s, rows_in = _attn_sample(
        q_s.reshape(dec_b, dec_s, QK_WIDTH), new_rows(k_s), new_rows(v_s), page_rows(cache_k),
        page_rows(cache_v), page_table, lam_vecs, g_sub, lam_init,
        xt_p.reshape(n_p, PACK_TILE, LANES), dest_p, pstart, counts_p, n_rows, n_s)
    c_s, st_s = _conv_sample(u_s.reshape(dec_b, dec_s, CONV_CH), state_conv[layer], *conv_w)
    x1_s, xt_s, idx_s, wts_s, rank_s, _ = _mix(
        o_s.reshape(n_s, ATTN_WIDTH), c_s.reshape(n_s, CONV_CH), xs, wo_a, wo_c, *ln1, wr_hi,
        wr_lo, br, cnt_p, n_s)
    dest_s = _dest(idx_s, rank_s, pstart_col, 128)
    rows_in = _dispatch(xt_s.reshape(n_s, PACK_TILE, LANES), dest_s, rows_in, 128)

    wg, wu, wd, wsg, wsu, wsd = moe_w
    rows_out = _experts(rows_in.reshape(n_rows * PACK_TILE, LANES), block_e, nb_used, wg, wu, wd)
    y_p = _combine(dest_p, wts_p, x1_p, rows_out, wsg, wsu, wsd, *ln2, 128)
    y_s = _combine(dest_s, wts_s, x1_s, rows_out, wsg, wsu, wsd, *ln2, 128)

    keep = CONV_WIDTH - 1
    u_p3 = u_p.reshape(batch, seq, CONV_CH)
    return (y_p.reshape(batch, seq, D_MODEL),
            y_s.reshape(dec_b, dec_s, D_MODEL),
            k_p.reshape(1, batch, seq, N_HEADS, V_DIM),
            v_p.reshape(1, batch, seq, N_HEADS, V_DIM),
            u_p3[:, seq - keep:, :][None],
            k_s.reshape(1, dec_b, dec_s, N_HEADS, V_DIM),
            v_s.reshape(1, dec_b, dec_s, N_HEADS, V_DIM),
            st_s[None])
```

```python
import functools
import math

import jax
import jax.numpy as jnp
from jax import lax
from jax.experimental import pallas as pl
from jax.experimental.pallas import tpu as pltpu

F32 = jnp.float32
BF16 = jnp.bfloat16
I32 = jnp.int32
U32 = jnp.uint32

D_MODEL = 1024
N_HEADS = 4
HEAD_DIM = 64
V_DIM = 128
ATTN_WIDTH = N_HEADS * V_DIM
QK_WIDTH = N_HEADS * 2 * HEAD_DIM
ATTN_SCALE = HEAD_DIM ** -0.5
CONV_CH = D_MODEL - ATTN_WIDTH
CONV_WIDTH = 31
N_EXPERTS = 64
N_GROUPS = 8
GROUP_SIZE = N_EXPERTS // N_GROUPS
TOPK_GROUPS = 4
TOP_K = 8
D_EXPERT = D_MODEL // 4
ROUTED_SCALE = 2.5
DEPTH = 1
DEEPNORM_ALPHA = (2 * DEPTH) ** 0.25
NORM_EPS = 1e-5
NEG_INF = -1e30
PAGE_SIZE = 128

LANES = 128
ROW_TILE = D_MODEL // LANES
PACK_TILE = ROW_TILE // 2
VMEM_LIMIT = 48 * 1024 * 1024
PAGES_PER_STEP = 16
ATTN_TILE = 512
PAGE_ROWS = PAGE_SIZE * N_HEADS
EXPERT_BLOCK = 512
HIST = 32


def _lambda_init(layer):
    return 0.8 - 0.6 * math.exp(-0.3 * layer)


def _sigmoid(x):
    return 1.0 / (1.0 + jnp.exp(-x))


def _silu(x):
    return x * _sigmoid(x)


def _layer_norm(x, g, b):
    mu = jnp.mean(x, axis=-1, keepdims=True)
    xc = x - mu
    var = jnp.mean(xc * xc, axis=-1, keepdims=True)
    return xc * lax.rsqrt(var + NORM_EPS) * g + b


def _dot(a, b):
    return jnp.dot(a, b, preferred_element_type=F32)


def _dot_nt(a, b):
    return lax.dot_general(a, b, (((1,), (1,)), ((), ())), preferred_element_type=F32)


def _diff_lambda(lq1, lk1, lq2, lk2, lam_init):
    a = jnp.exp(jnp.sum(lq1 * lk1, axis=-1, keepdims=True))
    b = jnp.exp(jnp.sum(lq2 * lk2, axis=-1, keepdims=True))
    return a - b + lam_init


def _params(dims):
    return pltpu.CompilerParams(dimension_semantics=dims, vmem_limit_bytes=VMEM_LIMIT)


def _store_row_tiles(ref, x):
    m = x.shape[0]
    for j in range(ROW_TILE):
        ref[pl.ds(j, m, stride=ROW_TILE), :] = x[:, j * LANES:(j + 1) * LANES]


def _load_row_tiles(ref, m):
    return jnp.concatenate([ref[pl.ds(j, m, stride=ROW_TILE), :] for j in range(ROW_TILE)],
                           axis=1)


def _token_tile(ref, t):
    return ref.at[pl.ds(pl.multiple_of(t * ROW_TILE, ROW_TILE), ROW_TILE)]


def _store_packed_tiles(ref, x):
    m = x.shape[0]
    bits = pltpu.bitcast(x.astype(BF16).astype(F32), U32)
    half = D_MODEL // 2
    for j in range(PACK_TILE):
        lo = bits[:, j * LANES:(j + 1) * LANES] >> 16
        hi = bits[:, half + j * LANES:half + (j + 1) * LANES]
        ref[pl.ds(j, m, stride=PACK_TILE), :] = lo | hi


def _load_packed_tiles(ref, m):
    words = [ref[pl.ds(j, m, stride=PACK_TILE), :] for j in range(PACK_TILE)]
    lo = [pltpu.bitcast(w << 16, F32) for w in words]
    hi = [pltpu.bitcast(w & jnp.uint32(0xFFFF0000), F32) for w in words]
    return jnp.concatenate(lo + hi, axis=1).astype(BF16)


def _store_head_rows(ref, x):
    m = x.shape[0]
    for h in range(N_HEADS):
        ref[pl.ds(h, m, stride=N_HEADS), :] = x[:, h * V_DIM:(h + 1) * V_DIM]


def _inproj_body(x_ref, w_ref, *refs):
    xb = x_ref[...].astype(BF16)

    def mm(c0):
        return _dot(xb, w_ref[:, c0:c0 + QK_WIDTH])

    k = mm(QK_WIDTH)
    if len(refs) == 4:
        q_ref, k_ref, v_ref, u_ref = refs
        q_ref[...] = mm(0) * ATTN_SCALE
    else:
        wqt_ref, wvt_ref, q_ref, k_ref, v_ref, u_ref, kb_ref, vt_ref = refs
        q_t = _dot_nt(wqt_ref[...], xb) * ATTN_SCALE
        v_t = _dot_nt(wvt_ref[...], xb).astype(BF16)
        for c in range(q_ref.shape[0]):
            cols = slice(c * ATTN_TILE, (c + 1) * ATTN_TILE)
            q_ref[c] = q_t[:, cols]
            vt_ref[c] = v_t[:, cols]
        kb_ref[...] = k.astype(BF16)
    _store_head_rows(k_ref, k)
    _store_head_rows(v_ref, mm(2 * QK_WIDTH))
    a = mm(2 * QK_WIDTH + ATTN_WIDTH)
    b = mm(2 * QK_WIDTH + ATTN_WIDTH + CONV_CH)
    u_ref[...] = a * _sigmoid(b)


def _inproj(x, w_in_b, tm, transposed_w=()):
    n = x.shape[0]
    wide = QK_WIDTH
    blk = pl.BlockSpec((tm, wide), lambda i: (i, 0))
    hblk = pl.BlockSpec((tm * N_HEADS, V_DIM), lambda i: (i, 0))
    full = lambda a: pl.BlockSpec(a.shape, lambda i: (0, 0))
    row_major = jax.ShapeDtypeStruct((n, wide), F32)
    head_rows = jax.ShapeDtypeStruct((n * N_HEADS, V_DIM), F32)
    out_specs = [blk, hblk, hblk, blk]
    out_shape = [row_major, head_rows, head_rows, row_major]
    if transposed_w:
        t_blk = pl.BlockSpec((tm // ATTN_TILE, wide, ATTN_TILE), lambda i: (i, 0, 0))
        t_shape = (n // ATTN_TILE, wide, ATTN_TILE)
        out_specs = [t_blk, hblk, hblk, blk, blk, t_blk]
        out_shape = [jax.ShapeDtypeStruct(t_shape, F32), head_rows, head_rows,
                     row_major, jax.ShapeDtypeStruct((n, wide), BF16),
                     jax.ShapeDtypeStruct(t_shape, BF16)]
    return pl.pallas_call(
        _inproj_body,
        name="inproj",
        grid=(n // tm,),
        in_specs=[pl.BlockSpec((tm, D_MODEL), lambda i: (i, 0)), full(w_in_b)]
        + [full(w) for w in transposed_w],
        out_specs=out_specs,
        out_shape=out_shape,
        compiler_params=_params(("parallel",)),
    )(x, w_in_b, *transposed_w)


def _attn_p_body(lam_init, q_ref, k_ref, v_ref, lq1, lk1, lq2, lk2, g_ref, o_ref, acc_ref):
    qi = pl.program_id(2)
    tq = q_ref.shape[1]
    tk = v_ref.shape[2]
    q_t = q_ref[...]
    feat = lax.broadcasted_iota(I32, q_t.shape, 0)
    q_both = jnp.concatenate([jnp.where(feat < HEAD_DIM, q_t, 0.0),
                              jnp.where(feat >= HEAD_DIM, q_t, 0.0)], axis=1).astype(BF16)
    acc_ref[...] = jnp.zeros(acc_ref.shape, F32)

    def chunk(masked, j, carry):
        m_old, l_old = carry
        kc = k_ref[pl.ds(pl.multiple_of(j * tk, tk), tk), :]
        s_t = _dot(kc, q_both)
        if masked:
            key = lax.broadcasted_iota(I32, s_t.shape, 0)
            qry = lax.broadcasted_iota(I32, s_t.shape, 1) % tq
            s_t = jnp.where(key <= qry, s_t, NEG_INF)
        m_new = jnp.maximum(m_old, jnp.max(s_t, axis=0, keepdims=True))
        alpha = jnp.exp(m_old - m_new)
        p_t = jnp.exp(s_t - m_new)
        l_new = alpha * l_old + jnp.sum(p_t, axis=0, keepdims=True)
        acc_ref[...] = alpha * acc_ref[...] + _dot(v_ref[j], p_t.astype(BF16))
        return m_new, l_new

    init = (jnp.full((1, 2 * tq), -jnp.inf, F32), jnp.zeros((1, 2 * tq), F32))
    carry = lax.fori_loop(0, qi, functools.partial(chunk, False), init)
    _, l_all = chunk(True, qi, carry)

    lam = _diff_lambda(lq1[...], lk1[...], lq2[...], lk2[...], lam_init)
    on = acc_ref[...] / l_all
    o_t = on[:, :tq] - lam * on[:, tq:]
    ms = jnp.mean(o_t * o_t, axis=0, keepdims=True)
    o_t = o_t * lax.rsqrt(ms + NORM_EPS) * g_ref[...] * (1.0 - lam_init)
    o_ref[...] = o_t.T


def _attn_prompt(q_t, kb, v_t, lam_vecs, subln_g_col, lam_init, batch, seq):
    t = q_t.shape[2]
    n = kb.shape[0]
    nq = seq // t
    small = lambda w: pl.BlockSpec((1, w), lambda b, h, i: (0, 0))
    return pl.pallas_call(
        functools.partial(_attn_p_body, lam_init),
        name="attn_prompt",
        grid=(batch, N_HEADS, nq),
        in_specs=[pl.BlockSpec((None, V_DIM, t), lambda b, h, i: (b * nq + i, h, 0)),
                  pl.BlockSpec((seq, V_DIM), lambda b, h, i: (b, h)),
                  pl.BlockSpec((nq, V_DIM, t), lambda b, h, i: (b, h, 0))]
        + [small(HEAD_DIM)] * 4 + [pl.BlockSpec((V_DIM, 1), lambda b, h, i: (0, 0))],
        out_specs=pl.BlockSpec((t, V_DIM), lambda b, h, i: (b * nq + i, h)),
        out_shape=jax.ShapeDtypeStruct((n, ATTN_WIDTH), F32),
        scratch_shapes=[pltpu.VMEM((V_DIM, 2 * t), F32)],
        compiler_params=_params(("parallel", "parallel", "parallel")),
    )(q_t, kb, v_t, *lam_vecs, subln_g_col)


def _attn_s_body(lam_init, n_steps, tokens_per_step, reserve, pt_ref, ps_ref, cnt_ref, q_ref,
                 kn_ref, vn_ref, *rest):
    del pt_ref
    npg = PAGES_PER_STEP
    k_refs = rest[:npg]
    v_refs = rest[npg:2 * npg]
    (lq1, lk1, lq2, lk2, g_ref, dest_ref, x_ref, o_ref, xs_ref, qall, knew, vnew, m_ref, l_ref,
     acc_ref, zbuf, sem, zsem) = rest[2 * npg:]
    j = pl.program_id(1)
    ds = q_ref.shape[0]
    hr = 2 * ds
    step = pl.program_id(0) * n_steps + j

    @pl.when(step == 0)
    def _():
        _zero_segment_tails(ps_ref, cnt_ref, xs_ref, zbuf, zsem, reserve)

    first_token = (step % (x_ref.shape[0] // tokens_per_step)) * tokens_per_step
    _start_row_copies(dest_ref, x_ref, xs_ref, sem, first_token, tokens_per_step)

    def head_rows(ref, h):
        return ref[pl.ds(h, PAGE_SIZE, stride=N_HEADS), :].astype(BF16)

    @pl.when(j == 0)
    def _():
        pieces = []
        for h in range(N_HEADS):
            qh = q_ref[:, h * V_DIM:(h + 1) * V_DIM]
            lane = lax.broadcasted_iota(I32, qh.shape, 1)
            pieces.append(jnp.where(lane < HEAD_DIM, qh, 0.0))
            pieces.append(jnp.where(lane >= HEAD_DIM, qh, 0.0))
        qall[...] = jnp.concatenate(pieces, axis=0).astype(BF16)
        knew[...] = jnp.zeros(knew.shape, F32)
        vnew[...] = jnp.zeros(vnew.shape, F32)
        knew[0:ds * N_HEADS] = kn_ref[...]
        vnew[0:ds * N_HEADS] = vn_ref[...]
        row = lax.broadcasted_iota(I32, (hr, PAGE_SIZE), 0)
        key = lax.broadcasted_iota(I32, (hr, PAGE_SIZE), 1)
        keep = key <= (row % ds)
        for h in range(N_HEADS):
            rows = slice(h * hr, (h + 1) * hr)
            s = jnp.where(keep, _dot_nt(qall[rows], head_rows(knew, h)), NEG_INF)
            m = jnp.max(s, axis=-1, keepdims=True)
            p = jnp.exp(s - m)
            m_ref[rows] = m
            l_ref[rows] = jnp.sum(p, axis=-1, keepdims=True)
            acc_ref[rows] = _dot(p.astype(BF16), head_rows(vnew, h))

    q_all, m_all, l_all, acc_all = qall[...], m_ref[...], l_ref[...], acc_ref[...]
    heads = range(N_HEADS)
    rows = [slice(h * hr, (h + 1) * hr) for h in heads]
    s = [_dot_nt(q_all[rows[h]],
                 jnp.concatenate([head_rows(k_refs[i], h) for i in range(npg)], axis=0))
         for h in heads]
    s = jnp.concatenate(s, axis=0)
    m_new = jnp.maximum(m_all, jnp.max(s, axis=-1, keepdims=True))
    alpha = jnp.exp(m_all - m_new)
    p = jnp.exp(s - m_new)
    l_new = alpha * l_all + jnp.sum(p, axis=-1, keepdims=True)
    pb = p.astype(BF16)
    pv = [_dot(pb[rows[h]],
               jnp.concatenate([head_rows(v_refs[i], h) for i in range(npg)], axis=0))
          for h in heads]
    acc_new = alpha * acc_all + jnp.concatenate(pv, axis=0)
    m_ref[...] = m_new
    l_ref[...] = l_new
    acc_ref[...] = acc_new
    _wait_row_copies(x_ref, xs_ref, sem, tokens_per_step)

    @pl.when(j == n_steps - 1)
    def _():
        lam = _diff_lambda(lq1[...], lk1[...], lq2[...], lk2[...], lam_init)
        on = acc_new / l_new
        for h in range(N_HEADS):
            r0 = h * hr
            o = on[r0:r0 + ds] - lam * on[r0 + ds:r0 + 2 * ds]
            ms = jnp.mean(o * o, axis=-1, keepdims=True)
            o_ref[:, h * V_DIM:(h + 1) * V_DIM] = (
                o * lax.rsqrt(ms + NORM_EPS) * g_ref[...] * (1.0 - lam_init))


def _attn_sample(q, k_new, v_new, cache_k, cache_v, page_table, lam_vecs, subln_g, lam_init,
                 x_tiles, dest, pstart, counts, n_rows, reserve):
    b, ds, _ = q.shape
    n_pages = page_table.shape[1]
    n_steps = n_pages // PAGES_PER_STEP
    nrow = N_HEADS * 2 * ds
    pt = page_table.reshape(-1)
    total_steps = b * n_steps
    n = x_tiles.shape[0]
    tokens_per_step = n // total_steps
    assert tokens_per_step * total_steps == n and LANES % tokens_per_step == 0
    per_block = LANES // tokens_per_step
    tile = (PACK_TILE, LANES)

    def page_map(i, bi, j, pt_ref, *_):
        return (pt_ref[bi * n_pages + j * PAGES_PER_STEP + i], 0, 0)

    page_specs = [pl.BlockSpec((None, PAGE_ROWS, V_DIM), functools.partial(page_map, i))
                  for i in range(PAGES_PER_STEP)]
    token_block = lambda bi, j, *_: (bi * n_steps + j) // per_block
    small = lambda w: pl.BlockSpec((1, w), lambda bi, j, *_: (0, 0))
    new_spec = pl.BlockSpec((None, ds * N_HEADS, V_DIM), lambda bi, j, *_: (bi, 0, 0))
    qo_spec = pl.BlockSpec((None, ds, ATTN_WIDTH), lambda bi, j, *_: (bi, 0, 0))
    grid_spec = pltpu.PrefetchScalarGridSpec(
        num_scalar_prefetch=3,
        grid=(b, n_steps),
        in_specs=[qo_spec, new_spec, new_spec] + page_specs + page_specs
        + [small(HEAD_DIM)] * 4 + [small(V_DIM)]
        + [pl.BlockSpec((TOP_K, LANES), lambda bi, j, *_: (0, token_block(bi, j)),
                        memory_space=pltpu.SMEM),
           pl.BlockSpec((LANES,) + tile, lambda bi, j, *_: (token_block(bi, j), 0, 0))],
        out_specs=[qo_spec, pl.BlockSpec(memory_space=pl.ANY)],
        scratch_shapes=[pltpu.VMEM((nrow, V_DIM), BF16),
                        pltpu.VMEM((PAGE_ROWS, V_DIM), F32),
                        pltpu.VMEM((PAGE_ROWS, V_DIM), F32),
                        pltpu.VMEM((nrow, 1), F32),
                        pltpu.VMEM((nrow, 1), F32),
                        pltpu.VMEM((nrow, V_DIM), F32),
                        pltpu.VMEM((TAIL_PIECES[0],) + tile, U32),
                        pltpu.SemaphoreType.DMA, pltpu.SemaphoreType.DMA],
    )
    return pl.pallas_call(
        functools.partial(_attn_s_body, lam_init, n_steps, tokens_per_step, reserve),
        name="attn_sample",
        grid_spec=grid_spec,
        out_shape=[jax.ShapeDtypeStruct((b, ds, ATTN_WIDTH), F32),
                   jax.ShapeDtypeStruct((n_rows,) + tile, U32)],
        compiler_params=_params(("arbitrary", "arbitrary")),
    )(pt, pstart, counts, q, k_new, v_new, *([cache_k] * PAGES_PER_STEP),
      *([cache_v] * PAGES_PER_STEP), *lam_vecs, subln_g, dest, x_tiles)


def _conv_taps(ext_ref, w_ref, bias, rows):
    off = HIST - (CONV_WIDTH - 1)
    acc = jnp.broadcast_to(bias, (rows, CONV_CH))
    for jt in range(CONV_WIDTH):
        acc = acc + w_ref[jt:jt + 1, :] * ext_ref[pl.ds(jt + off, rows), :]
    return acc


CONV_ROWS = 64
SUBLANES = 8


def _conv_p_body(u_ref, h_ref, w_ref, b_ref, g_ref, beta_ref, c_ref, ext_ref, sh_ref):
    t = pl.program_id(1)
    tm = u_ref.shape[0]
    hist = h_ref[...]
    ext_ref[0:HIST] = jnp.where(t == 0, jnp.zeros_like(hist), hist)
    ext_ref[HIST:HIST + tm] = u_ref[...]
    span = sh_ref.shape[1]
    for r in range(1, SUBLANES):
        sh_ref[r - 1] = ext_ref[pl.ds(r, span), :]
    off = HIST - (CONV_WIDTH - 1)
    bias = b_ref[...]
    for c0 in range(0, tm, CONV_ROWS):
        acc = jnp.broadcast_to(bias, (CONV_ROWS, CONV_CH))
        for jt in range(CONV_WIDTH):
            r = (jt + off) % SUBLANES
            base = c0 + jt + off - r
            src = ext_ref if r == 0 else sh_ref.at[r - 1]
            acc = acc + w_ref[jt:jt + 1, :] * src[pl.ds(base, CONV_ROWS), :]
        c_ref[c0:c0 + CONV_ROWS, :] = _silu(_layer_norm(acc, g_ref[...], beta_ref[...]))


def _conv_prompt(u, w_dw, b_dw, g, beta, batch, seq, tm):
    n = u.shape[0]
    nt = seq // tm
    per = tm // HIST
    cur = lambda b, t: (b * nt + t, 0)
    prev = lambda b, t: (jnp.maximum((b * nt + t) * per - 1, 0), 0)
    small = lambda r: pl.BlockSpec((r, CONV_CH), lambda b, t: (0, 0))
    return pl.pallas_call(
        _conv_p_body,
        name="conv_prompt",
        grid=(batch, nt),
        in_specs=[pl.BlockSpec((tm, CONV_CH), cur), pl.BlockSpec((HIST, CONV_CH), prev),
                  small(CONV_WIDTH), small(1), small(1), small(1)],
        out_specs=pl.BlockSpec((tm, CONV_CH), cur),
        out_shape=jax.ShapeDtypeStruct((n, CONV_CH), F32),
        scratch_shapes=[pltpu.VMEM((HIST + tm, CONV_CH), F32),
                        pltpu.VMEM((SUBLANES - 1, HIST + tm - SUBLANES, CONV_CH), F32)],
        compiler_params=_params(("parallel", "parallel")),
    )(u, u, w_dw, b_dw, g, beta)


def _conv_s_body(u_ref, st_ref, w_ref, b_ref, g_ref, beta_ref, c_ref, ns_ref, ext_ref):
    ds = u_ref.shape[0]
    keep = CONV_WIDTH - 1
    off = HIST - keep
    ext_ref[0:off] = jnp.zeros((off, CONV_CH), F32)
    ext_ref[off:HIST] = st_ref[...]
    ext_ref[HIST:HIST + ds] = u_ref[...]
    y = _conv_taps(ext_ref, w_ref, b_ref[...], ds)
    c_ref[...] = _silu(_layer_norm(y, g_ref[...], beta_ref[...]))
    ns_ref[...] = ext_ref[HIST + ds - keep:HIST + ds]


def _conv_sample(u, state, w_dw, b_dw, g, beta):
    b, ds, _ = u.shape
    keep = CONV_WIDTH - 1
    per_b = lambda r: pl.BlockSpec((None, r, CONV_CH), lambda i: (i, 0, 0))
    small = lambda r: pl.BlockSpec((r, CONV_CH), lambda i: (0, 0))
    return pl.pallas_call(
        _conv_s_body,
        name="conv_sample",
        grid=(b,),
        in_specs=[per_b(ds), per_b(keep), small(CONV_WIDTH), small(1), small(1), small(1)],
        out_specs=[per_b(ds), per_b(keep)],
        out_shape=[jax.ShapeDtypeStruct((b, ds, CONV_CH), F32),
                   jax.ShapeDtypeStruct((b, keep, CONV_CH), F32)],
        scratch_shapes=[pltpu.VMEM((HIST + ds, CONV_CH), F32)],
        compiler_params=_params(("parallel",)),
    )(u, state, w_dw, b_dw, g, beta)


def _first_index(hit, iota, limit, axis):
    return jnp.min(jnp.where(hit, iota, limit), axis=axis, keepdims=True)


def _route(x1, wr_hi, wr_lo, bias):
    tm = x1.shape[0]
    x_hi = x1.astype(BF16)
    x_lo = (x1 - x_hi.astype(F32)).astype(BF16)
    logits = _dot_nt(wr_hi, x_hi) + (_dot_nt(wr_lo, x_hi) + _dot_nt(wr_hi, x_lo))
    scores = _sigmoid(logits)
    sel = scores + bias
    sel3 = sel.reshape(N_GROUPS, GROUP_SIZE, tm)
    member = lax.broadcasted_iota(I32, sel3.shape, 1)
    m1 = jnp.max(sel3, axis=1, keepdims=True)
    i1 = _first_index(sel3 == m1, member, GROUP_SIZE, 1)
    m2 = jnp.max(jnp.where(member == i1, -jnp.inf, sel3), axis=1, keepdims=True)
    gs = jnp.broadcast_to(m1 + m2, sel3.shape).reshape(N_EXPERTS, tm)
    eiota = lax.broadcasted_iota(I32, (N_EXPERTS, tm), 0)
    giota = eiota // GROUP_SIZE
    gmask = jnp.zeros((N_EXPERTS, tm), jnp.bool_)
    for _ in range(TOPK_GROUPS):
        m = jnp.max(gs, axis=0, keepdims=True)
        gi = _first_index(gs == m, giota, N_GROUPS, 0)
        pick = giota == gi
        gmask = jnp.logical_or(gmask, pick)
        gs = jnp.where(pick, -jnp.inf, gs)
    selm = jnp.where(gmask, sel, NEG_INF)
    idx_rows, w_rows = [], []
    for _ in range(TOP_K):
        m = jnp.max(selm, axis=0, keepdims=True)
        ei = _first_index(selm == m, eiota, N_EXPERTS, 0)
        pick = eiota == ei
        idx_rows.append(ei)
        w_rows.append(jnp.sum(jnp.where(pick, scores, 0.0), axis=0, keepdims=True))
        selm = jnp.where(pick, -jnp.inf, selm)
    idx = jnp.concatenate(idx_rows, axis=0)
    w = jnp.concatenate(w_rows, axis=0)
    w = w / jnp.sum(w, axis=0, keepdims=True) * ROUTED_SCALE
    return idx, w


def _mix_body(o_ref, c_ref, x_ref, wo_a, wo_c, g_ref, b_ref, wrh_ref, wrl_ref, br_ref, cnt0_ref,
              x1_ref, xrt_ref, idx_ref, wts_ref, rank_ref, cnt_ref, base_ref):
    i = pl.program_id(0)
    tm = x_ref.shape[0]

    @pl.when(i == 0)
    def _():
        base_ref[...] = cnt0_ref[...]

    mix = _dot(o_ref[...].astype(BF16), wo_a[...]) + _dot(c_ref[...].astype(BF16), wo_c[...])
    x1 = _layer_norm(DEEPNORM_ALPHA * x_ref[...] + mix, g_ref[...], b_ref[...])
    x1_ref[...] = x1
    _store_packed_tiles(xrt_ref, x1)

    idx, w = _route(x1, wrh_ref[...], wrl_ref[...], br_ref[...])
    idx_ref[...] = idx
    w_rows = jnp.concatenate([w, jnp.zeros((LANES - TOP_K, tm), F32)], axis=0)
    wts_ref[...] = w_rows.T

    eiota = lax.broadcasted_iota(I32, (N_EXPERTS, tm), 0)
    chosen = jnp.zeros((N_EXPERTS, tm), F32)
    for k in range(TOP_K):
        chosen = chosen + jnp.where(eiota == idx[k:k + 1, :], 1.0, 0.0)
    r = lax.broadcasted_iota(I32, (tm, tm), 0)
    c = lax.broadcasted_iota(I32, (tm, tm), 1)
    before = jnp.where(r < c, 1.0, 0.0).astype(BF16)
    base = base_ref[:, 0:1]
    rank_full = _dot(chosen.astype(BF16), before) + base
    rows = [jnp.sum(jnp.where(eiota == idx[k:k + 1, :], rank_full, 0.0), axis=0, keepdims=True)
            for k in range(TOP_K)]
    rank_ref[...] = jnp.concatenate(rows, axis=0).astype(I32)
    total = base + jnp.sum(chosen, axis=1, keepdims=True)
    base_ref[...] = jnp.broadcast_to(total, base_ref.shape)
    cnt_ref[...] = jnp.broadcast_to(total, cnt_ref.shape)


def _mix(o, c, x, wo_a, wo_c, ln_g, ln_b, wr_hi, wr_lo, b_router, cnt0, tm):
    n = x.shape[0]
    rows = lambda w: pl.BlockSpec((tm, w), lambda i: (i, 0))
    full = lambda a: pl.BlockSpec(a.shape, lambda i: (0, 0))
    cols = pl.BlockSpec((TOP_K, tm), lambda i: (0, i))
    cnt = pl.BlockSpec((N_EXPERTS, LANES), lambda i: (0, 0))
    return pl.pallas_call(
        _mix_body,
        name="mix",
        grid=(n // tm,),
        in_specs=[rows(ATTN_WIDTH), rows(CONV_CH), rows(D_MODEL), full(wo_a), full(wo_c),
                  full(ln_g), full(ln_b), full(wr_hi), full(wr_lo), full(b_router), cnt],
        out_specs=[rows(D_MODEL), pl.BlockSpec((tm * PACK_TILE, LANES), lambda i: (i, 0)), cols,
                   rows(LANES), cols, cnt],
        out_shape=[jax.ShapeDtypeStruct((n, D_MODEL), F32),
                   jax.ShapeDtypeStruct((n * PACK_TILE, LANES), U32),
                   jax.ShapeDtypeStruct((TOP_K, n), I32),
                   jax.ShapeDtypeStruct((n, LANES), F32),
                   jax.ShapeDtypeStruct((TOP_K, n), I32),
                   jax.ShapeDtypeStruct((N_EXPERTS, LANES), F32)],
        scratch_shapes=[pltpu.VMEM((N_EXPERTS, LANES), F32)],
        compiler_params=_params(("arbitrary",)),
    )(o, c, x, wo_a, wo_c, ln_g, ln_b, wr_hi, wr_lo, b_router, cnt0)


def _dest_body(idx_ref, rank_ref, ps_ref, dest_ref):
    idx = idx_ref[...]
    tm = idx.shape[1]
    eiota = lax.broadcasted_iota(I32, (N_EXPERTS, tm), 0)
    ps = ps_ref[...]
    rows = [jnp.sum(jnp.where(eiota == idx[k:k + 1, :], ps, 0.0), axis=0, keepdims=True)
            for k in range(TOP_K)]
    dest_ref[...] = jnp.concatenate(rows, axis=0).astype(I32) + rank_ref[...]


def _dest(idx, rank, pstart_col, tm):
    n = idx.shape[1]
    cols = pl.BlockSpec((TOP_K, tm), lambda i: (0, i))
    return pl.pallas_call(
        _dest_body,
        name="dest",
        grid=(n // tm,),
        in_specs=[cols, cols, pl.BlockSpec((N_EXPERTS, 1), lambda i: (0, 0))],
        out_specs=cols,
        out_shape=jax.ShapeDtypeStruct((TOP_K, n), I32),
        compiler_params=_params(("parallel",)),
    )(idx, rank, pstart_col)


TAIL_PIECES = tuple(1 << s for s in reversed(range(EXPERT_BLOCK.bit_length())))
ROW_UNROLL = 8


def _zero_segment_tails(ps_ref, cnt_ref, xs_ref, zbuf, zsem, reserve):
    zbuf[...] = jnp.zeros(zbuf.shape, U32)

    def tail(wait, e, carry):
        cnt = cnt_ref[e]
        base = ps_ref[e] + cnt
        seg = (cnt + reserve + EXPERT_BLOCK - 1) // EXPERT_BLOCK * EXPERT_BLOCK
        pad = seg - cnt
        for p in TAIL_PIECES:
            @pl.when((pad & p) != 0)
            def _():
                cp = pltpu.make_async_copy(zbuf.at[pl.ds(0, p)], xs_ref.at[pl.ds(base, p)], zsem)
                if wait:
                    cp.wait()
                else:
                    cp.start()
            base = base + (pad & p)
        return carry

    lax.fori_loop(0, N_EXPERTS, functools.partial(tail, False), 0)
    lax.fori_loop(0, N_EXPERTS, functools.partial(tail, True), 0)


def _start_row_copies(dest_ref, x_ref, xs_ref, sem, first, count):
    def issue(g, carry):
        for u in range(ROW_UNROLL):
            t = first + g * ROW_UNROLL + u
            for k in range(TOP_K):
                pltpu.make_async_copy(x_ref.at[t], xs_ref.at[dest_ref[k, t]],
                                      sem).start(priority=k % 2)
        return carry

    lax.fori_loop(0, count // ROW_UNROLL, issue, 0)


def _wait_row_copies(x_ref, xs_ref, sem, count):
    for _ in range(TOP_K):
        pltpu.make_async_copy(x_ref.at[pl.ds(0, count)], xs_ref.at[pl.ds(0, count)], sem).wait()


def _dispatch_body(dest_ref, x_ref, xs_prev_ref, xs_ref, sem):
    del xs_prev_ref
    ts = x_ref.shape[0]
    _start_row_copies(dest_ref, x_ref, xs_ref, sem, 0, ts)
    _wait_row_copies(x_ref, xs_ref, sem, ts)


def _dispatch(x_tiles, dest, xs_prev, ts):
    n = x_tiles.shape[0]
    return pl.pallas_call(
        _dispatch_body,
        name="dispatch",
        grid=(n // ts,),
        in_specs=[pl.BlockSpec((TOP_K, ts), lambda i: (0, i), memory_space=pltpu.SMEM),
                  pl.BlockSpec((ts, PACK_TILE, LANES), lambda i: (i, 0, 0)),
                  pl.BlockSpec(memory_space=pl.ANY)],
        out_specs=pl.BlockSpec(memory_space=pl.ANY),
        out_shape=jax.ShapeDtypeStruct(xs_prev.shape, xs_prev.dtype),
        scratch_shapes=[pltpu.SemaphoreType.DMA],
        input_output_aliases={2: 0},
        compiler_params=_params(("arbitrary",)),
    )(dest, x_tiles, xs_prev)


def _expert_body(be_ref, nb_ref, xs_ref, wg_ref, wu_ref, wd_ref, ys_ref, wg_b, wu_b, wd_b):
    i = pl.program_id(0)

    @pl.when(jnp.logical_or(i == 0, be_ref[i] != be_ref[jnp.maximum(i - 1, 0)]))
    def _():
        wg_b[...] = wg_ref[...].astype(BF16)
        wu_b[...] = wu_ref[...].astype(BF16)
        wd_b[...] = wd_ref[...].astype(BF16)

    @pl.when(i < nb_ref[0])
    def _():
        x = _load_packed_tiles(xs_ref, EXPERT_BLOCK)
        h = (_silu(_dot(x, wg_b[...])) * _dot(x, wu_b[...])).astype(BF16)
        _store_row_tiles(ys_ref, _dot(h, wd_b[...]))


def _experts(xs, block_e, nb_used, wg, wu, wd):
    rows = xs.shape[0] // PACK_TILE
    n_blocks = rows // EXPERT_BLOCK
    used = lambda i, be, nb: (jnp.minimum(i, nb[0] - 1), 0)
    in_rows = pl.BlockSpec((EXPERT_BLOCK * PACK_TILE, LANES), used)
    grid_spec = pltpu.PrefetchScalarGridSpec(
        num_scalar_prefetch=2,
        grid=(n_blocks,),
        in_specs=[in_rows,
                  pl.BlockSpec((None, D_MODEL, D_EXPERT), lambda i, be, nb: (be[i], 0, 0)),
                  pl.BlockSpec((None, D_MODEL, D_EXPERT), lambda i, be, nb: (be[i], 0, 0)),
                  pl.BlockSpec((None, D_EXPERT, D_MODEL), lambda i, be, nb: (be[i], 0, 0))],
        out_specs=pl.BlockSpec((EXPERT_BLOCK * ROW_TILE, LANES), used),
        scratch_shapes=[pltpu.VMEM((D_MODEL, D_EXPERT), BF16), pltpu.VMEM((D_MODEL, D_EXPERT), BF16),
                        pltpu.VMEM((D_EXPERT, D_MODEL), BF16)],
    )
    return pl.pallas_call(
        _expert_body,
        name="experts",
        grid_spec=grid_spec,
        out_shape=jax.ShapeDtypeStruct((rows * ROW_TILE, LANES), F32),
        compiler_params=_params(("arbitrary",)),
    )(block_e, nb_used, xs, wg, wu, wd)


def _combine_body(dest_ref, dnext_ref, wts_ref, x1_ref, ys_ref, wsg, wsu, wsd, g_ref, b_ref,
                  y_ref, buf, sems):
    i = pl.program_id(0)
    n_tiles = pl.num_programs(0)
    tc = x1_ref.shape[0]
    slot = i % 2

    def issue(d_ref, s, g, carry):
        for u in range(ROW_UNROLL):
            t = g * ROW_UNROLL + u
            for k in range(TOP_K):
                pltpu.make_async_copy(_token_tile(ys_ref, d_ref[k, t]),
                                      _token_tile(buf.at[s, k], t),
                                      sems.at[s]).start(priority=k % 2)
        return carry

    n_groups = tc // ROW_UNROLL

    @pl.when(i == 0)
    def _():
        lax.fori_loop(0, n_groups, functools.partial(issue, dest_ref, slot), 0)

    @pl.when(i + 1 < n_tiles)
    def _():
        lax.fori_loop(0, n_groups, functools.partial(issue, dnext_ref, 1 - slot), 0)

    x1 = x1_ref[...]
    xb = x1.astype(BF16)
    hs = (_silu(_dot(xb, wsg[...])) * _dot(xb, wsu[...])).astype(BF16)
    shared = _dot(hs, wsd[...])

    for k in range(TOP_K):
        pltpu.make_async_copy(ys_ref.at[pl.ds(0, tc * ROW_TILE)], buf.at[slot, k],
                              sems.at[slot]).wait()

    w = wts_ref[...]
    routed = w[:, 0:1] * _load_row_tiles(buf.at[slot, 0], tc)
    for k in range(1, TOP_K):
        routed = routed + w[:, k:k + 1] * _load_row_tiles(buf.at[slot, k], tc)
    y_ref[...] = _layer_norm(DEEPNORM_ALPHA * x1 + (routed + shared), g_ref[...], b_ref[...])


def _combine(dest, wts_t, x1, ys, wsg, wsu, wsd, ln_g, ln_b, tc):
    n = x1.shape[0]
    last = n // tc - 1
    dest_cur = pl.BlockSpec((TOP_K, tc), lambda i: (0, i), memory_space=pltpu.SMEM)
    dest_next = pl.BlockSpec((TOP_K, tc), lambda i: (0, jnp.minimum(i + 1, last)),
                             memory_space=pltpu.SMEM)
    full = lambda a: pl.BlockSpec(a.shape, lambda i: (0, 0))
    rows = lambda w: pl.BlockSpec((tc, w), lambda i: (i, 0))
    return pl.pallas_call(
        _combine_body,
        name="combine",
        grid=(n // tc,),
        in_specs=[dest_cur, dest_next, rows(LANES), rows(D_MODEL),
                  pl.BlockSpec(memory_space=pl.ANY),
                  full(wsg), full(wsu), full(wsd), full(ln_g), full(ln_b)],
        out_specs=rows(D_MODEL),
        out_shape=jax.ShapeDtypeStruct((n, D_MODEL), F32),
        scratch_shapes=[pltpu.VMEM((2, TOP_K, tc * ROW_TILE, LANES), F32),
                        pltpu.SemaphoreType.DMA((2,))],
        compiler_params=_params(("arbitrary",)),
    )(dest, dest, wts_t, x1, ys, wsg, wsu, wsd, ln_g, ln_b)


def _segments(cnt_first, n_first, reserve):
    blk = EXPERT_BLOCK
    assert reserve <= blk
    n_blocks = -(-(n_first * TOP_K + N_EXPERTS * reserve) // blk) + N_EXPERTS
    counts = cnt_first[:, 0].astype(I32)
    padded = (counts + reserve + blk - 1) // blk * blk
    ends = jnp.cumsum(padded)
    pstart = ends - padded
    nb_used = (ends[-1] // blk).astype(I32).reshape(1)
    block_row0 = jnp.arange(n_blocks, dtype=I32) * blk
    block_e = jnp.minimum(jnp.sum((ends[None, :] <= block_row0[:, None]).astype(I32), axis=1),
                          N_EXPERTS - 1).astype(I32)
    return pstart, counts, block_e, nb_used, n_blocks * blk


def kernel(x_prompt, x_sample, cache_k, cache_v, state_conv, page_table, w_in, lam_q1, lam_k1,
           lam_q2, lam_k2, subln_g, w_dw, b_dw, conv_ln_g, conv_ln_b, w_o, ln1_g, ln1_b,
           w_router, b_router, w_gate, w_up, w_down, w_sh_gate, w_sh_up, w_sh_down, ln2_g,
           ln2_b):
    batch, seq, _ = x_prompt.shape
    dec_b, dec_s, _ = x_sample.shape
    n_p, n_s = batch * seq, dec_b * dec_s
    layer = 0
    lam_init = _lambda_init(layer)
    row = lambda a: a[layer].reshape(1, -1)

    w_in_b = w_in[layer].astype(BF16)
    wo = w_o[layer].astype(BF16)
    wo_a, wo_c = wo[:ATTN_WIDTH], wo[ATTN_WIDTH:]
    wr_t = w_router[layer].T
    wr_hi = wr_t.astype(BF16)
    wr_lo = (wr_t - wr_hi.astype(F32)).astype(BF16)
    br = b_router[layer].reshape(-1, 1)
    moe_w = (w_gate[layer], w_up[layer], w_down[layer]) + tuple(
        w[layer].astype(BF16) for w in (w_sh_gate, w_sh_up, w_sh_down))
    lam_vecs = (row(lam_q1), row(lam_k1), row(lam_q2), row(lam_k2))
    g_sub = row(subln_g)
    conv_w = (w_dw[layer], row(b_dw), row(conv_ln_g), row(conv_ln_b))
    ln1 = (row(ln1_g), row(ln1_b))
    ln2 = (row(ln2_g), row(ln2_b))

    xp = x_prompt.reshape(n_p, D_MODEL)
    wq_t = w_in_b[:, :QK_WIDTH].T
    wv_t = w_in_b[:, 2 * QK_WIDTH:2 * QK_WIDTH + ATTN_WIDTH].T
    qt_p, k_p, v_p, u_p, kb_p, vt_p = _inproj(xp, w_in_b, ATTN_TILE, (wq_t, wv_t))
    o_p = _attn_prompt(qt_p, kb_p, vt_p, lam_vecs, g_sub.reshape(V_DIM, 1), lam_init, batch, seq)
    c_p = _conv_prompt(u_p, *conv_w, batch, seq, 512)
    cnt0 = jnp.zeros((N_EXPERTS, LANES), F32)
    x1_p, xt_p, idx_p, wts_p, rank_p, cnt_p = _mix(o_p, c_p, xp, wo_a, wo_c, *ln1, wr_hi, wr_lo,
                                                   br, cnt0, 512)
    pstart, counts_p, block_e, nb_used, n_rows = _segments(cnt_p, n_p, n_s)
    pstart_col = pstart.astype(F32).reshape(N_EXPERTS, 1)
    dest_p = _dest(idx_p, rank_p, pstart_col, 256)

    xs = x_sample.reshape(n_s, D_MODEL)
    q_s, k_s, v_s, u_s = _inproj(xs, w_in_b, n_s)
    pool = cache_k.shape[1]
    page_rows = lambda a: a[layer].reshape(pool, PAGE_ROWS, V_DIM)
    new_rows = lambda a: a.reshape(dec_b, dec_s * N_HEADS, V_DIM)
    o_s, rows_in = _attn_sample(
        q_s.reshape(dec_b, dec_s, QK_WIDTH), new_rows(k_s), new_rows(v_s), page_rows(cache_k),
        page_rows(cache_v), page_table, lam_vecs, g_sub, lam_init,
        xt_p.reshape(n_p, PACK_TILE, LANES), dest_p, pstart, counts_p, n_rows, n_s)
    c_s, st_s = _conv_sample(u_s.reshape(dec_b, dec_s, CONV_CH), state_conv[layer], *conv_w)
    x1_s, xt_s, idx_s, wts_s, rank_s, _ = _mix(
        o_s.reshape(n_s, ATTN_WIDTH), c_s.reshape(n_s, CONV_CH), xs, wo_a, wo_c, *ln1, wr_hi,
        wr_lo, br, cnt_p, n_s)
    dest_s = _dest(idx_s, rank_s, pstart_col, 128)
    rows_in = _dispatch(xt_s.reshape(n_s, PACK_TILE, LANES), dest_s, rows_in, 128)

    wg, wu, wd, wsg, wsu, wsd = moe_w
    rows_out = _experts(rows_in.reshape(n_rows * PACK_TILE, LANES), block_e, nb_used, wg, wu, wd)
    y_p = _combine(dest_p, wts_p, x1_p, rows_out, wsg, wsu, wsd, *ln2, 128)
    y_s = _combine(dest_s, wts_s, x1_s, rows_out, wsg, wsu, wsd, *ln2, 128)

    keep = CONV_WIDTH - 1
    u_p3 = u_p.reshape(batch, seq, CONV_CH)
    return (y_p.reshape(batch, seq, D_MODEL),
            y_s.reshape(dec_b, dec_s, D_MODEL),
            k_p.reshape(1, batch, seq, N_HEADS, V_DIM),
            v_p.reshape(1, batch, seq, N_HEADS, V_DIM),
            u_p3[:, seq - keep:, :][None],
            k_s.reshape(1, dec_b, dec_s, N_HEADS, V_DIM),
            v_s.reshape(1, dec_b, dec_s, N_HEADS, V_DIM),
            st_s[None])
```

```python
import functools
import math

import jax
import jax.numpy as jnp
from jax import lax
from jax.experimental import pallas as pl
from jax.experimental.pallas import tpu as pltpu

F32 = jnp.float32
BF16 = jnp.bfloat16
I32 = jnp.int32
U32 = jnp.uint32

D_MODEL = 1024
N_HEADS = 4
HEAD_DIM = 64
V_DIM = 128
ATTN_WIDTH = N_HEADS * V_DIM
QK_WIDTH = N_HEADS * 2 * HEAD_DIM
ATTN_SCALE = HEAD_DIM ** -0.5
CONV_CH = D_MODEL - ATTN_WIDTH
CONV_WIDTH = 31
N_EXPERTS = 64
N_GROUPS = 8
GROUP_SIZE = N_EXPERTS // N_GROUPS
TOPK_GROUPS = 4
TOP_K = 8
D_EXPERT = D_MODEL // 4
ROUTED_SCALE = 2.5
DEPTH = 1
DEEPNORM_ALPHA = (2 * DEPTH) ** 0.25
NORM_EPS = 1e-5
NEG_INF = -1e30
PAGE_SIZE = 128

LANES = 128
ROW_TILE = D_MODEL // LANES
PACK_TILE = ROW_TILE // 2
VMEM_LIMIT = 48 * 1024 * 1024
PAGES_PER_STEP = 32
ATTN_TILE = 512
PAGE_ROWS = PAGE_SIZE * N_HEADS
EXPERT_BLOCK = 512
HIST = 32


def _lambda_init(layer):
    return 0.8 - 0.6 * math.exp(-0.3 * layer)


def _sigmoid(x):
    return 1.0 / (1.0 + jnp.exp(-x))


def _silu(x):
    return x * _sigmoid(x)


def _layer_norm(x, g, b):
    mu = jnp.mean(x, axis=-1, keepdims=True)
    xc = x - mu
    var = jnp.mean(xc * xc, axis=-1, keepdims=True)
    return xc * lax.rsqrt(var + NORM_EPS) * g + b


def _dot(a, b):
    return jnp.dot(a, b, preferred_element_type=F32)


def _dot_nt(a, b):
    return lax.dot_general(a, b, (((1,), (1,)), ((), ())), preferred_element_type=F32)


def _diff_lambda(lq1, lk1, lq2, lk2, lam_init):
    a = jnp.exp(jnp.sum(lq1 * lk1, axis=-1, keepdims=True))
    b = jnp.exp(jnp.sum(lq2 * lk2, axis=-1, keepdims=True))
    return a - b + lam_init


def _params(dims):
    return pltpu.CompilerParams(dimension_semantics=dims, vmem_limit_bytes=VMEM_LIMIT)


def _store_row_tiles(ref, x):
    m = x.shape[0]
    for j in range(ROW_TILE):
        ref[pl.ds(j, m, stride=ROW_TILE), :] = x[:, j * LANES:(j + 1) * LANES]


def _load_row_tiles(ref, m):
    return jnp.concatenate([ref[pl.ds(j, m, stride=ROW_TILE), :] for j in range(ROW_TILE)],
                           axis=1)


def _token_tile(ref, t):
    return ref.at[pl.ds(pl.multiple_of(t * ROW_TILE, ROW_TILE), ROW_TILE)]


def _store_packed_tiles(ref, x):
    m = x.shape[0]
    bits = pltpu.bitcast(x.astype(BF16).astype(F32), U32)
    half = D_MODEL // 2
    for j in range(PACK_TILE):
        lo = bits[:, j * LANES:(j + 1) * LANES] >> 16
        hi = bits[:, half + j * LANES:half + (j + 1) * LANES]
        ref[pl.ds(j, m, stride=PACK_TILE), :] = lo | hi


def _load_packed_tiles(ref, m):
    words = [ref[pl.ds(j, m, stride=PACK_TILE), :] for j in range(PACK_TILE)]
    lo = [pltpu.bitcast(w << 16, F32) for w in words]
    hi = [pltpu.bitcast(w & jnp.uint32(0xFFFF0000), F32) for w in words]
    return jnp.concatenate(lo + hi, axis=1).astype(BF16)


def _store_head_rows(ref, x):
    m = x.shape[0]
    for h in range(N_HEADS):
        ref[pl.ds(h, m, stride=N_HEADS), :] = x[:, h * V_DIM:(h + 1) * V_DIM]


def _inproj_body(x_ref, w_ref, *refs):
    xb = x_ref[...].astype(BF16)

    def mm(c0):
        return _dot(xb, w_ref[:, c0:c0 + QK_WIDTH])

    k = mm(QK_WIDTH)
    if len(refs) == 4:
        q_ref, k_ref, v_ref, u_ref = refs
        q_ref[...] = mm(0) * ATTN_SCALE
    else:
        wqt_ref, wvt_ref, q_ref, k_ref, v_ref, u_ref, kb_ref, vt_ref = refs
        q_t = _dot_nt(wqt_ref[...], xb) * ATTN_SCALE
        v_t = _dot_nt(wvt_ref[...], xb).astype(BF16)
        for c in range(q_ref.shape[0]):
            cols = slice(c * ATTN_TILE, (c + 1) * ATTN_TILE)
            q_ref[c] = q_t[:, cols]
            vt_ref[c] = v_t[:, cols]
        kb_ref[...] = k.astype(BF16)
    _store_head_rows(k_ref, k)
    _store_head_rows(v_ref, mm(2 * QK_WIDTH))
    a = mm(2 * QK_WIDTH + ATTN_WIDTH)
    b = mm(2 * QK_WIDTH + ATTN_WIDTH + CONV_CH)
    u_ref[...] = a * _sigmoid(b)


def _inproj(x, w_in_b, tm, transposed_w=()):
    n = x.shape[0]
    wide = QK_WIDTH
    blk = pl.BlockSpec((tm, wide), lambda i: (i, 0))
    hblk = pl.BlockSpec((tm * N_HEADS, V_DIM), lambda i: (i, 0))
    full = lambda a: pl.BlockSpec(a.shape, lambda i: (0, 0))
    row_major = jax.ShapeDtypeStruct((n, wide), F32)
    head_rows = jax.ShapeDtypeStruct((n * N_HEADS, V_DIM), F32)
    out_specs = [blk, hblk, hblk, blk]
    out_shape = [row_major, head_rows, head_rows, row_major]
    if transposed_w:
        t_blk = pl.BlockSpec((tm // ATTN_TILE, wide, ATTN_TILE), lambda i: (i, 0, 0))
        t_shape = (n // ATTN_TILE, wide, ATTN_TILE)
        out_specs = [t_blk, hblk, hblk, blk, blk, t_blk]
        out_shape = [jax.ShapeDtypeStruct(t_shape, F32), head_rows, head_rows,
                     row_major, jax.ShapeDtypeStruct((n, wide), BF16),
                     jax.ShapeDtypeStruct(t_shape, BF16)]
    return pl.pallas_call(
        _inproj_body,
        name="inproj",
        grid=(n // tm,),
        in_specs=[pl.BlockSpec((tm, D_MODEL), lambda i: (i, 0)), full(w_in_b)]
        + [full(w) for w in transposed_w],
        out_specs=out_specs,
        out_shape=out_shape,
        compiler_params=_params(("parallel",)),
    )(x, w_in_b, *transposed_w)


def _attn_p_body(lam_init, q_ref, k_ref, v_ref, lq1, lk1, lq2, lk2, g_ref, o_ref, acc_ref):
    qi = pl.program_id(2)
    tq = q_ref.shape[1]
    tk = v_ref.shape[2]
    q_t = q_ref[...]
    feat = lax.broadcasted_iota(I32, q_t.shape, 0)
    q_both = jnp.concatenate([jnp.where(feat < HEAD_DIM, q_t, 0.0),
                              jnp.where(feat >= HEAD_DIM, q_t, 0.0)], axis=1).astype(BF16)
    acc_ref[...] = jnp.zeros(acc_ref.shape, F32)

    def chunk(masked, j, carry):
        m_old, l_old = carry
        kc = k_ref[pl.ds(pl.multiple_of(j * tk, tk), tk), :]
        s_t = _dot(kc, q_both)
        if masked:
            key = lax.broadcasted_iota(I32, s_t.shape, 0)
            qry = lax.broadcasted_iota(I32, s_t.shape, 1) % tq
            s_t = jnp.where(key <= qry, s_t, NEG_INF)
        m_new = jnp.maximum(m_old, jnp.max(s_t, axis=0, keepdims=True))
        alpha = jnp.exp(m_old - m_new)
        p_t = jnp.exp(s_t - m_new)
        l_new = alpha * l_old + jnp.sum(p_t, axis=0, keepdims=True)
        acc_ref[...] = alpha * acc_ref[...] + _dot(v_ref[j], p_t.astype(BF16))
        return m_new, l_new

    init = (jnp.full((1, 2 * tq), -jnp.inf, F32), jnp.zeros((1, 2 * tq), F32))
    carry = lax.fori_loop(0, qi, functools.partial(chunk, False), init)
    _, l_all = chunk(True, qi, carry)

    lam = _diff_lambda(lq1[...], lk1[...], lq2[...], lk2[...], lam_init)
    on = acc_ref[...] / l_all
    o_t = on[:, :tq] - lam * on[:, tq:]
    ms = jnp.mean(o_t * o_t, axis=0, keepdims=True)
    o_t = o_t * lax.rsqrt(ms + NORM_EPS) * g_ref[...] * (1.0 - lam_init)
    o_ref[...] = o_t.T


def _attn_prompt(q_t, kb, v_t, lam_vecs, subln_g_col, lam_init, batch, seq):
    t = q_t.shape[2]
    n = kb.shape[0]
    nq = seq // t
    small = lambda w: pl.BlockSpec((1, w), lambda b, h, i: (0, 0))
    return pl.pallas_call(
        functools.partial(_attn_p_body, lam_init),
        name="attn_prompt",
        grid=(batch, N_HEADS, nq),
        in_specs=[pl.BlockSpec((None, V_DIM, t), lambda b, h, i: (b * nq + i, h, 0)),
                  pl.BlockSpec((seq, V_DIM), lambda b, h, i: (b, h)),
                  pl.BlockSpec((nq, V_DIM, t), lambda b, h, i: (b, h, 0))]
        + [small(HEAD_DIM)] * 4 + [pl.BlockSpec((V_DIM, 1), lambda b, h, i: (0, 0))],
        out_specs=pl.BlockSpec((t, V_DIM), lambda b, h, i: (b * nq + i, h)),
        out_shape=jax.ShapeDtypeStruct((n, ATTN_WIDTH), F32),
        scratch_shapes=[pltpu.VMEM((V_DIM, 2 * t), F32)],
        compiler_params=_params(("parallel", "parallel", "parallel")),
    )(q_t, kb, v_t, *lam_vecs, subln_g_col)


def _attn_s_body(lam_init, n_steps, tokens_per_step, reserve, pt_ref, ps_ref, cnt_ref, q_ref,
                 kn_ref, vn_ref, *rest):
    del pt_ref
    npg = PAGES_PER_STEP
    k_refs = rest[:npg]
    v_refs = rest[npg:2 * npg]
    (lq1, lk1, lq2, lk2, g_ref, dest_ref, x_ref, o_ref, xs_ref, qall, knew, vnew, m_ref, l_ref,
     acc_ref, zbuf, sem, zsem) = rest[2 * npg:]
    j = pl.program_id(1)
    ds = q_ref.shape[0]
    hr = 2 * ds
    step = pl.program_id(0) * n_steps + j

    @pl.when(step == 0)
    def _():
        _zero_segment_tails(ps_ref, cnt_ref, xs_ref, zbuf, zsem, reserve)

    first_token = (step % (x_ref.shape[0] // tokens_per_step)) * tokens_per_step
    _start_row_copies(dest_ref, x_ref, xs_ref, sem, first_token, tokens_per_step)

    def head_rows(ref, h):
        return ref[pl.ds(h, PAGE_SIZE, stride=N_HEADS), :].astype(BF16)

    @pl.when(j == 0)
    def _():
        pieces = []
        for h in range(N_HEADS):
            qh = q_ref[:, h * V_DIM:(h + 1) * V_DIM]
            lane = lax.broadcasted_iota(I32, qh.shape, 1)
            pieces.append(jnp.where(lane < HEAD_DIM, qh, 0.0))
            pieces.append(jnp.where(lane >= HEAD_DIM, qh, 0.0))
        qall[...] = jnp.concatenate(pieces, axis=0).astype(BF16)
        knew[...] = jnp.zeros(knew.shape, F32)
        vnew[...] = jnp.zeros(vnew.shape, F32)
        knew[0:ds * N_HEADS] = kn_ref[...]
        vnew[0:ds * N_HEADS] = vn_ref[...]
        row = lax.broadcasted_iota(I32, (hr, PAGE_SIZE), 0)
        key = lax.broadcasted_iota(I32, (hr, PAGE_SIZE), 1)
        keep = key <= (row % ds)
        for h in range(N_HEADS):
            rows = slice(h * hr, (h + 1) * hr)
            s = jnp.where(keep, _dot_nt(qall[rows], head_rows(knew, h)), NEG_INF)
            m = jnp.max(s, axis=-1, keepdims=True)
            p = jnp.exp(s - m)
            m_ref[rows] = m
            l_ref[rows] = jnp.sum(p, axis=-1, keepdims=True)
            acc_ref[rows] = _dot(p.astype(BF16), head_rows(vnew, h))

    q_all, m_all, l_all, acc_all = qall[...], m_ref[...], l_ref[...], acc_ref[...]
    heads = range(N_HEADS)
    rows = [slice(h * hr, (h + 1) * hr) for h in heads]
    s = [_dot_nt(q_all[rows[h]],
                 jnp.concatenate([head_rows(k_refs[i], h) for i in range(npg)], axis=0))
         for h in heads]
    s = jnp.concatenate(s, axis=0)
    m_new = jnp.maximum(m_all, jnp.max(s, axis=-1, keepdims=True))
    alpha = jnp.exp(m_all - m_new)
    p = jnp.exp(s - m_new)
    l_new = alpha * l_all + jnp.sum(p, axis=-1, keepdims=True)
    pb = p.astype(BF16)
    pv = [_dot(pb[rows[h]],
               jnp.concatenate([head_rows(v_refs[i], h) for i in range(npg)], axis=0))
          for h in heads]
    acc_new = alpha * acc_all + jnp.concatenate(pv, axis=0)
    m_ref[...] = m_new
    l_ref[...] = l_new
    acc_ref[...] = acc_new
    _wait_row_copies(x_ref, xs_ref, sem, tokens_per_step)

    @pl.when(j == n_steps - 1)
    def _():
        lam = _diff_lambda(lq1[...], lk1[...], lq2[...], lk2[...], lam_init)
        on = acc_new / l_new
        for h in range(N_HEADS):
            r0 = h * hr
            o = on[r0:r0 + ds] - lam * on[r0 + ds:r0 + 2 * ds]
            ms = jnp.mean(o * o, axis=-1, keepdims=True)
            o_ref[:, h * V_DIM:(h + 1) * V_DIM] = (
                o * lax.rsqrt(ms + NORM_EPS) * g_ref[...] * (1.0 - lam_init))


def _attn_sample(q, k_new, v_new, cache_k, cache_v, page_table, lam_vecs, subln_g, lam_init,
                 x_tiles, dest, pstart, counts, n_rows, reserve):
    b, ds, _ = q.shape
    n_pages = page_table.shape[1]
    n_steps = n_pages // PAGES_PER_STEP
    nrow = N_HEADS * 2 * ds
    pt = page_table.reshape(-1)
    total_steps = b * n_steps
    n = x_tiles.shape[0]
    tokens_per_step = n // total_steps
    assert tokens_per_step * total_steps == n and LANES % tokens_per_step == 0
    per_block = LANES // tokens_per_step
    tile = (PACK_TILE, LANES)

    def page_map(i, bi, j, pt_ref, *_):
        return (pt_ref[bi * n_pages + j * PAGES_PER_STEP + i], 0, 0)

    page_specs = [pl.BlockSpec((None, PAGE_ROWS, V_DIM), functools.partial(page_map, i))
                  for i in range(PAGES_PER_STEP)]
    token_block = lambda bi, j, *_: (bi * n_steps + j) // per_block
    small = lambda w: pl.BlockSpec((1, w), lambda bi, j, *_: (0, 0))
    new_spec = pl.BlockSpec((None, ds * N_HEADS, V_DIM), lambda bi, j, *_: (bi, 0, 0))
    qo_spec = pl.BlockSpec((None, ds, ATTN_WIDTH), lambda bi, j, *_: (bi, 0, 0))
    grid_spec = pltpu.PrefetchScalarGridSpec(
        num_scalar_prefetch=3,
        grid=(b, n_steps),
        in_specs=[qo_spec, new_spec, new_spec] + page_specs + page_specs
        + [small(HEAD_DIM)] * 4 + [small(V_DIM)]
        + [pl.BlockSpec((TOP_K, LANES), lambda bi, j, *_: (0, token_block(bi, j)),
                        memory_space=pltpu.SMEM),
           pl.BlockSpec((LANES,) + tile, lambda bi, j, *_: (token_block(bi, j), 0, 0))],
        out_specs=[qo_spec, pl.BlockSpec(memory_space=pl.ANY)],
        scratch_shapes=[pltpu.VMEM((nrow, V_DIM), BF16),
                        pltpu.VMEM((PAGE_ROWS, V_DIM), F32),
                        pltpu.VMEM((PAGE_ROWS, V_DIM), F32),
                        pltpu.VMEM((nrow, 1), F32),
                        pltpu.VMEM((nrow, 1), F32),
                        pltpu.VMEM((nrow, V_DIM), F32),
                        pltpu.VMEM((TAIL_PIECES[0],) + tile, U32),
                        pltpu.SemaphoreType.DMA, pltpu.SemaphoreType.DMA],
    )
    return pl.pallas_call(
        functools.partial(_attn_s_body, lam_init, n_steps, tokens_per_step, reserve),
        name="attn_sample",
        grid_spec=grid_spec,
        out_shape=[jax.ShapeDtypeStruct((b, ds, ATTN_WIDTH), F32),
                   jax.ShapeDtypeStruct((n_rows,) + tile, U32)],
        compiler_params=_params(("arbitrary", "arbitrary")),
    )(pt, pstart, counts, q, k_new, v_new, *([cache_k] * PAGES_PER_STEP),
      *([cache_v] * PAGES_PER_STEP), *lam_vecs, subln_g, dest, x_tiles)


def _conv_taps(ext_ref, w_ref, bias, rows):
    off = HIST - (CONV_WIDTH - 1)
    acc = jnp.broadcast_to(bias, (rows, CONV_CH))
    for jt in range(CONV_WIDTH):
        acc = acc + w_ref[jt:jt + 1, :] * ext_ref[pl.ds(jt + off, rows), :]
    return acc


CONV_ROWS = 64
SUBLANES = 8


def _conv_p_body(u_ref, h_ref, w_ref, b_ref, g_ref, beta_ref, c_ref, ext_ref, sh_ref):
    t = pl.program_id(1)
    tm = u_ref.shape[0]
    hist = h_ref[...]
    ext_ref[0:HIST] = jnp.where(t == 0, jnp.zeros_like(hist), hist)
    ext_ref[HIST:HIST + tm] = u_ref[...]
    span = sh_ref.shape[1]
    for r in range(1, SUBLANES):
        sh_ref[r - 1] = ext_ref[pl.ds(r, span), :]
    off = HIST - (CONV_WIDTH - 1)
    bias = b_ref[...]
    for c0 in range(0, tm, CONV_ROWS):
        acc = jnp.broadcast_to(bias, (CONV_ROWS, CONV_CH))
        for jt in range(CONV_WIDTH):
            r = (jt + off) % SUBLANES
            base = c0 + jt + off - r
            src = ext_ref if r == 0 else sh_ref.at[r - 1]
            acc = acc + w_ref[jt:jt + 1, :] * src[pl.ds(base, CONV_ROWS), :]
        c_ref[c0:c0 + CONV_ROWS, :] = _silu(_layer_norm(acc, g_ref[...], beta_ref[...]))


def _conv_prompt(u, w_dw, b_dw, g, beta, batch, seq, tm):
    n = u.shape[0]
    nt = seq // tm
    per = tm // HIST
    cur = lambda b, t: (b * nt + t, 0)
    prev = lambda b, t: (jnp.maximum((b * nt + t) * per - 1, 0), 0)
    small = lambda r: pl.BlockSpec((r, CONV_CH), lambda b, t: (0, 0))
    return pl.pallas_call(
        _conv_p_body,
        name="conv_prompt",
        grid=(batch, nt),
        in_specs=[pl.BlockSpec((tm, CONV_CH), cur), pl.BlockSpec((HIST, CONV_CH), prev),
                  small(CONV_WIDTH), small(1), small(1), small(1)],
        out_specs=pl.BlockSpec((tm, CONV_CH), cur),
        out_shape=jax.ShapeDtypeStruct((n, CONV_CH), F32),
        scratch_shapes=[pltpu.VMEM((HIST + tm, CONV_CH), F32),
                        pltpu.VMEM((SUBLANES - 1, HIST + tm - SUBLANES, CONV_CH), F32)],
        compiler_params=_params(("parallel", "parallel")),
    )(u, u, w_dw, b_dw, g, beta)


def _conv_s_body(u_ref, st_ref, w_ref, b_ref, g_ref, beta_ref, c_ref, ns_ref, ext_ref):
    ds = u_ref.shape[0]
    keep = CONV_WIDTH - 1
    off = HIST - keep
    ext_ref[0:off] = jnp.zeros((off, CONV_CH), F32)
    ext_ref[off:HIST] = st_ref[...]
    ext_ref[HIST:HIST + ds] = u_ref[...]
    y = _conv_taps(ext_ref, w_ref, b_ref[...], ds)
    c_ref[...] = _silu(_layer_norm(y, g_ref[...], beta_ref[...]))
    ns_ref[...] = ext_ref[HIST + ds - keep:HIST + ds]


def _conv_sample(u, state, w_dw, b_dw, g, beta):
    b, ds, _ = u.shape
    keep = CONV_WIDTH - 1
    per_b = lambda r: pl.BlockSpec((None, r, CONV_CH), lambda i: (i, 0, 0))
    small = lambda r: pl.BlockSpec((r, CONV_CH), lambda i: (0, 0))
    return pl.pallas_call(
        _conv_s_body,
        name="conv_sample",
        grid=(b,),
        in_specs=[per_b(ds), per_b(keep), small(CONV_WIDTH), small(1), small(1), small(1)],
        out_specs=[per_b(ds), per_b(keep)],
        out_shape=[jax.ShapeDtypeStruct((b, ds, CONV_CH), F32),
                   jax.ShapeDtypeStruct((b, keep, CONV_CH), F32)],
        scratch_shapes=[pltpu.VMEM((HIST + ds, CONV_CH), F32)],
        compiler_params=_params(("parallel",)),
    )(u, state, w_dw, b_dw, g, beta)


def _first_index(hit, iota, limit, axis):
    return jnp.min(jnp.where(hit, iota, limit), axis=axis, keepdims=True)


def _route(x1, wr_hi, wr_lo, bias):
    tm = x1.shape[0]
    x_hi = x1.astype(BF16)
    x_lo = (x1 - x_hi.astype(F32)).astype(BF16)
    logits = _dot_nt(wr_hi, x_hi) + (_dot_nt(wr_lo, x_hi) + _dot_nt(wr_hi, x_lo))
    scores = _sigmoid(logits)
    sel = scores + bias
    sel3 = sel.reshape(N_GROUPS, GROUP_SIZE, tm)
    member = lax.broadcasted_iota(I32, sel3.shape, 1)
    m1 = jnp.max(sel3, axis=1, keepdims=True)
    i1 = _first_index(sel3 == m1, member, GROUP_SIZE, 1)
    m2 = jnp.max(jnp.where(member == i1, -jnp.inf, sel3), axis=1, keepdims=True)
    gs = jnp.broadcast_to(m1 + m2, sel3.shape).reshape(N_EXPERTS, tm)
    eiota = lax.broadcasted_iota(I32, (N_EXPERTS, tm), 0)
    giota = eiota // GROUP_SIZE
    gmask = jnp.zeros((N_EXPERTS, tm), jnp.bool_)
    for _ in range(TOPK_GROUPS):
        m = jnp.max(gs, axis=0, keepdims=True)
        gi = _first_index(gs == m, giota, N_GROUPS, 0)
        pick = giota == gi
        gmask = jnp.logical_or(gmask, pick)
        gs = jnp.where(pick, -jnp.inf, gs)
    selm = jnp.where(gmask, sel, NEG_INF)
    idx_rows, w_rows = [], []
    for _ in range(TOP_K):
        m = jnp.max(selm, axis=0, keepdims=True)
        ei = _first_index(selm == m, eiota, N_EXPERTS, 0)
        pick = eiota == ei
        idx_rows.append(ei)
        w_rows.append(jnp.sum(jnp.where(pick, scores, 0.0), axis=0, keepdims=True))
        selm = jnp.where(pick, -jnp.inf, selm)
    idx = jnp.concatenate(idx_rows, axis=0)
    w = jnp.concatenate(w_rows, axis=0)
    w = w / jnp.sum(w, axis=0, keepdims=True) * ROUTED_SCALE
    return idx, w


def _mix_body(o_ref, c_ref, x_ref, wo_a, wo_c, g_ref, b_ref, wrh_ref, wrl_ref, br_ref, cnt0_ref,
              x1_ref, xrt_ref, idx_ref, wts_ref, rank_ref, cnt_ref, base_ref):
    i = pl.program_id(0)
    tm = x_ref.shape[0]

    @pl.when(i == 0)
    def _():
        base_ref[...] = cnt0_ref[...]

    mix = _dot(o_ref[...].astype(BF16), wo_a[...]) + _dot(c_ref[...].astype(BF16), wo_c[...])
    x1 = _layer_norm(DEEPNORM_ALPHA * x_ref[...] + mix, g_ref[...], b_ref[...])
    x1_ref[...] = x1
    _store_packed_tiles(xrt_ref, x1)

    idx, w = _route(x1, wrh_ref[...], wrl_ref[...], br_ref[...])
    idx_ref[...] = idx
    w_rows = jnp.concatenate([w, jnp.zeros((LANES - TOP_K, tm), F32)], axis=0)
    wts_ref[...] = w_rows.T

    eiota = lax.broadcasted_iota(I32, (N_EXPERTS, tm), 0)
    chosen = jnp.zeros((N_EXPERTS, tm), F32)
    for k in range(TOP_K):
        chosen = chosen + jnp.where(eiota == idx[k:k + 1, :], 1.0, 0.0)
    r = lax.broadcasted_iota(I32, (tm, tm), 0)
    c = lax.broadcasted_iota(I32, (tm, tm), 1)
    before = jnp.where(r < c, 1.0, 0.0).astype(BF16)
    base = base_ref[:, 0:1]
    rank_full = _dot(chosen.astype(BF16), before) + base
    rows = [jnp.sum(jnp.where(eiota == idx[k:k + 1, :], rank_full, 0.0), axis=0, keepdims=True)
            for k in range(TOP_K)]
    rank_ref[...] = jnp.concatenate(rows, axis=0).astype(I32)
    total = base + jnp.sum(chosen, axis=1, keepdims=True)
    base_ref[...] = jnp.broadcast_to(total, base_ref.shape)
    cnt_ref[...] = jnp.broadcast_to(total, cnt_ref.shape)


def _mix(o, c, x, wo_a, wo_c, ln_g, ln_b, wr_hi, wr_lo, b_router, cnt0, tm):
    n = x.shape[0]
    rows = lambda w: pl.BlockSpec((tm, w), lambda i: (i, 0))
    full = lambda a: pl.BlockSpec(a.shape, lambda i: (0, 0))
    cols = pl.BlockSpec((TOP_K, tm), lambda i: (0, i))
    cnt = pl.BlockSpec((N_EXPERTS, LANES), lambda i: (0, 0))
    return pl.pallas_call(
        _mix_body,
        name="mix",
        grid=(n // tm,),
        in_specs=[rows(ATTN_WIDTH), rows(CONV_CH), rows(D_MODEL), full(wo_a), full(wo_c),
                  full(ln_g), full(ln_b), full(wr_hi), full(wr_lo), full(b_router), cnt],
        out_specs=[rows(D_MODEL), pl.BlockSpec((tm * PACK_TILE, LANES), lambda i: (i, 0)), cols,
                   rows(LANES), cols, cnt],
        out_shape=[jax.ShapeDtypeStruct((n, D_MODEL), F32),
                   jax.ShapeDtypeStruct((n * PACK_TILE, LANES), U32),
                   jax.ShapeDtypeStruct((TOP_K, n), I32),
                   jax.ShapeDtypeStruct((n, LANES), F32),
                   jax.ShapeDtypeStruct((TOP_K, n), I32),
                   jax.ShapeDtypeStruct((N_EXPERTS, LANES), F32)],
        scratch_shapes=[pltpu.VMEM((N_EXPERTS, LANES), F32)],
        compiler_params=_params(("arbitrary",)),
    )(o, c, x, wo_a, wo_c, ln_g, ln_b, wr_hi, wr_lo, b_router, cnt0)


def _dest_body(idx_ref, rank_ref, ps_ref, dest_ref):
    idx = idx_ref[...]
    tm = idx.shape[1]
    eiota = lax.broadcasted_iota(I32, (N_EXPERTS, tm), 0)
    ps = ps_ref[...]
    rows = [jnp.sum(jnp.where(eiota == idx[k:k + 1, :], ps, 0.0), axis=0, keepdims=True)
            for k in range(TOP_K)]
    dest_ref[...] = jnp.concatenate(rows, axis=0).astype(I32) + rank_ref[...]


def _dest(idx, rank, pstart_col, tm):
    n = idx.shape[1]
    cols = pl.BlockSpec((TOP_K, tm), lambda i: (0, i))
    return pl.pallas_call(
        _dest_body,
        name="dest",
        grid=(n // tm,),
        in_specs=[cols, cols, pl.BlockSpec((N_EXPERTS, 1), lambda i: (0, 0))],
        out_specs=cols,
        out_shape=jax.ShapeDtypeStruct((TOP_K, n), I32),
        compiler_params=_params(("parallel",)),
    )(idx, rank, pstart_col)


TAIL_PIECES = tuple(1 << s for s in reversed(range(EXPERT_BLOCK.bit_length())))
ROW_UNROLL = 8


def _zero_segment_tails(ps_ref, cnt_ref, xs_ref, zbuf, zsem, reserve):
    zbuf[...] = jnp.zeros(zbuf.shape, U32)

    def tail(wait, e, carry):
        cnt = cnt_ref[e]
        base = ps_ref[e] + cnt
        seg = (cnt + reserve + EXPERT_BLOCK - 1) // EXPERT_BLOCK * EXPERT_BLOCK
        pad = seg - cnt
        for p in TAIL_PIECES:
            @pl.when((pad & p) != 0)
            def _():
                cp = pltpu.make_async_copy(zbuf.at[pl.ds(0, p)], xs_ref.at[pl.ds(base, p)], zsem)
                if wait:
                    cp.wait()
                else:
                    cp.start()
            base = base + (pad & p)
        return carry

    lax.fori_loop(0, N_EXPERTS, functools.partial(tail, False), 0)
    lax.fori_loop(0, N_EXPERTS, functools.partial(tail, True), 0)


def _start_row_copies(dest_ref, x_ref, xs_ref, sem, first, count):
    def issue(g, carry):
        for u in range(ROW_UNROLL):
            t = first + g * ROW_UNROLL + u
            for k in range(TOP_K):
                pltpu.make_async_copy(x_ref.at[t], xs_ref.at[dest_ref[k, t]],
                                      sem).start(priority=k % 2)
        return carry

    lax.fori_loop(0, count // ROW_UNROLL, issue, 0)


def _wait_row_copies(x_ref, xs_ref, sem, count):
    for _ in range(TOP_K):
        pltpu.make_async_copy(x_ref.at[pl.ds(0, count)], xs_ref.at[pl.ds(0, count)], sem).wait()


def _dispatch_body(dest_ref, x_ref, xs_prev_ref, xs_ref, sem):
    del xs_prev_ref
    ts = x_ref.shape[0]
    _start_row_copies(dest_ref, x_ref, xs_ref, sem, 0, ts)
    _wait_row_copies(x_ref, xs_ref, sem, ts)


def _dispatch(x_tiles, dest, xs_prev, ts):
    n = x_tiles.shape[0]
    return pl.pallas_call(
        _dispatch_body,
        name="dispatch",
        grid=(n // ts,),
        in_specs=[pl.BlockSpec((TOP_K, ts), lambda i: (0, i), memory_space=pltpu.SMEM),
                  pl.BlockSpec((ts, PACK_TILE, LANES), lambda i: (i, 0, 0)),
                  pl.BlockSpec(memory_space=pl.ANY)],
        out_specs=pl.BlockSpec(memory_space=pl.ANY),
        out_shape=jax.ShapeDtypeStruct(xs_prev.shape, xs_prev.dtype),
        scratch_shapes=[pltpu.SemaphoreType.DMA],
        input_output_aliases={2: 0},
        compiler_params=_params(("arbitrary",)),
    )(dest, x_tiles, xs_prev)


def _expert_body(be_ref, nb_ref, xs_ref, wg_ref, wu_ref, wd_ref, ys_ref, wg_b, wu_b, wd_b):
    i = pl.program_id(0)

    @pl.when(jnp.logical_or(i == 0, be_ref[i] != be_ref[jnp.maximum(i - 1, 0)]))
    def _():
        wg_b[...] = wg_ref[...].astype(BF16)
        wu_b[...] = wu_ref[...].astype(BF16)
        wd_b[...] = wd_ref[...].astype(BF16)

    @pl.when(i < nb_ref[0])
    def _():
        x = _load_packed_tiles(xs_ref, EXPERT_BLOCK)
        h = (_silu(_dot(x, wg_b[...])) * _dot(x, wu_b[...])).astype(BF16)
        _store_row_tiles(ys_ref, _dot(h, wd_b[...]))


def _experts(xs, block_e, nb_used, wg, wu, wd):
    rows = xs.shape[0] // PACK_TILE
    n_blocks = rows // EXPERT_BLOCK
    used = lambda i, be, nb: (jnp.minimum(i, nb[0] - 1), 0)
    in_rows = pl.BlockSpec((EXPERT_BLOCK * PACK_TILE, LANES), used)
    grid_spec = pltpu.PrefetchScalarGridSpec(
        num_scalar_prefetch=2,
        grid=(n_blocks,),
        in_specs=[in_rows,
                  pl.BlockSpec((None, D_MODEL, D_EXPERT), lambda i, be, nb: (be[i], 0, 0)),
                  pl.BlockSpec((None, D_MODEL, D_EXPERT), lambda i, be, nb: (be[i], 0, 0)),
                  pl.BlockSpec((None, D_EXPERT, D_MODEL), lambda i, be, nb: (be[i], 0, 0))],
        out_specs=pl.BlockSpec((EXPERT_BLOCK * ROW_TILE, LANES), used),
        scratch_shapes=[pltpu.VMEM((D_MODEL, D_EXPERT), BF16), pltpu.VMEM((D_MODEL, D_EXPERT), BF16),
                        pltpu.VMEM((D_EXPERT, D_MODEL), BF16)],
    )
    return pl.pallas_call(
        _expert_body,
        name="experts",
        grid_spec=grid_spec,
        out_shape=jax.ShapeDtypeStruct((rows * ROW_TILE, LANES), F32),
        compiler_params=_params(("arbitrary",)),
    )(block_e, nb_used, xs, wg, wu, wd)


def _combine_body(dest_ref, dnext_ref, wts_ref, x1_ref, ys_ref, wsg, wsu, wsd, g_ref, b_ref,
                  y_ref, buf, sems):
    i = pl.program_id(0)
    n_tiles = pl.num_programs(0)
    tc = x1_ref.shape[0]
    slot = i % 2

    def issue(d_ref, s, g, carry):
        for u in range(ROW_UNROLL):
            t = g * ROW_UNROLL + u
            for k in range(TOP_K):
                pltpu.make_async_copy(_token_tile(ys_ref, d_ref[k, t]),
                                      _token_tile(buf.at[s, k], t),
                                      sems.at[s]).start(priority=k % 2)
        return carry

    n_groups = tc // ROW_UNROLL

    @pl.when(i == 0)
    def _():
        lax.fori_loop(0, n_groups, functools.partial(issue, dest_ref, slot), 0)

    @pl.when(i + 1 < n_tiles)
    def _():
        lax.fori_loop(0, n_groups, functools.partial(issue, dnext_ref, 1 - slot), 0)

    x1 = x1_ref[...]
    xb = x1.astype(BF16)
    hs = (_silu(_dot(xb, wsg[...])) * _dot(xb, wsu[...])).astype(BF16)
    shared = _dot(hs, wsd[...])

    for k in range(TOP_K):
        pltpu.make_async_copy(ys_ref.at[pl.ds(0, tc * ROW_TILE)], buf.at[slot, k],
                              sems.at[slot]).wait()

    w = wts_ref[...]
    routed = w[:, 0:1] * _load_row_tiles(buf.at[slot, 0], tc)
    for k in range(1, TOP_K):
        routed = routed + w[:, k:k + 1] * _load_row_tiles(buf.at[slot, k], tc)
    y_ref[...] = _layer_norm(DEEPNORM_ALPHA * x1 + (routed + shared), g_ref[...], b_ref[...])


def _combine(dest, wts_t, x1, ys, wsg, wsu, wsd, ln_g, ln_b, tc):
    n = x1.shape[0]
    last = n // tc - 1
    dest_cur = pl.BlockSpec((TOP_K, tc), lambda i: (0, i), memory_space=pltpu.SMEM)
    dest_next = pl.BlockSpec((TOP_K, tc), lambda i: (0, jnp.minimum(i + 1, last)),
                             memory_space=pltpu.SMEM)
    full = lambda a: pl.BlockSpec(a.shape, lambda i: (0, 0))
    rows = lambda w: pl.BlockSpec((tc, w), lambda i: (i, 0))
    return pl.pallas_call(
        _combine_body,
        name="combine",
        grid=(n // tc,),
        in_specs=[dest_cur, dest_next, rows(LANES), rows(D_MODEL),
                  pl.BlockSpec(memory_space=pl.ANY),
                  full(wsg), full(wsu), full(wsd), full(ln_g), full(ln_b)],
        out_specs=rows(D_MODEL),
        out_shape=jax.ShapeDtypeStruct((n, D_MODEL), F32),
        scratch_shapes=[pltpu.VMEM((2, TOP_K, tc * ROW_TILE, LANES), F32),
                        pltpu.SemaphoreType.DMA((2,))],
        compiler_params=_params(("arbitrary",)),
    )(dest, dest, wts_t, x1, ys, wsg, wsu, wsd, ln_g, ln_b)


def _segments(cnt_first, n_first, reserve):
    blk = EXPERT_BLOCK
    assert reserve <= blk
    n_blocks = -(-(n_first * TOP_K + N_EXPERTS * reserve) // blk) + N_EXPERTS
    counts = cnt_first[:, 0].astype(I32)
    padded = (counts + reserve + blk - 1) // blk * blk
    ends = jnp.cumsum(padded)
    pstart = ends - padded
    nb_used = (ends[-1] // blk).astype(I32).reshape(1)
    block_row0 = jnp.arange(n_blocks, dtype=I32) * blk
    block_e = jnp.minimum(jnp.sum((ends[None, :] <= block_row0[:, None]).astype(I32), axis=1),
                          N_EXPERTS - 1).astype(I32)
    return pstart, counts, block_e, nb_used, n_blocks * blk


def kernel(x_prompt, x_sample, cache_k, cache_v, state_conv, page_table, w_in, lam_q1, lam_k1,
           lam_q2, lam_k2, subln_g, w_dw, b_dw, conv_ln_g, conv_ln_b, w_o, ln1_g, ln1_b,
           w_router, b_router, w_gate, w_up, w_down, w_sh_gate, w_sh_up, w_sh_down, ln2_g,
           ln2_b):
    batch, seq, _ = x_prompt.shape
    dec_b, dec_s, _ = x_sample.shape
    n_p, n_s = batch * seq, dec_b * dec_s
    layer = 0
    lam_init = _lambda_init(layer)
    row = lambda a: a[layer].reshape(1, -1)

    w_in_b = w_in[layer].astype(BF16)
    wo = w_o[layer].astype(BF16)
    wo_a, wo_c = wo[:ATTN_WIDTH], wo[ATTN_WIDTH:]
    wr_t = w_router[layer].T
    wr_hi = wr_t.astype(BF16)
    wr_lo = (wr_t - wr_hi.astype(F32)).astype(BF16)
    br = b_router[layer].reshape(-1, 1)
    moe_w = (w_gate[layer], w_up[layer], w_down[layer]) + tuple(
        w[layer].astype(BF16) for w in (w_sh_gate, w_sh_up, w_sh_down))
    lam_vecs = (row(lam_q1), row(lam_k1), row(lam_q2), row(lam_k2))
    g_sub = row(subln_g)
    conv_w = (w_dw[layer], row(b_dw), row(conv_ln_g), row(conv_ln_b))
    ln1 = (row(ln1_g), row(ln1_b))
    ln2 = (row(ln2_g), row(ln2_b))

    xp = x_prompt.reshape(n_p, D_MODEL)
    wq_t = w_in_b[:, :QK_WIDTH].T
    wv_t = w_in_b[:, 2 * QK_WIDTH:2 * QK_WIDTH + ATTN_WIDTH].T
    qt_p, k_p, v_p, u_p, kb_p, vt_p = _inproj(xp, w_in_b, ATTN_TILE, (wq_t, wv_t))
    o_p = _attn_prompt(qt_p, kb_p, vt_p, lam_vecs, g_sub.reshape(V_DIM, 1), lam_init, batch, seq)
    c_p = _conv_prompt(u_p, *conv_w, batch, seq, 512)
    cnt0 = jnp.zeros((N_EXPERTS, LANES), F32)
    x1_p, xt_p, idx_p, wts_p, rank_p, cnt_p = _mix(o_p, c_p, xp, wo_a, wo_c, *ln1, wr_hi, wr_lo,
                                                   br, cnt0, 512)
    pstart, counts_p, block_e, nb_used, n_rows = _segments(cnt_p, n_p, n_s)
    pstart_col = pstart.astype(F32).reshape(N_EXPERTS, 1)
    dest_p = _dest(idx_p, rank_p, pstart_col, 256)

    xs = x_sample.reshape(n_s, D_MODEL)
    q_s, k_s, v_s, u_s = _inproj(xs, w_in_b, n_s)
    pool = cache_k.shape[1]
    page_rows = lambda a: a[layer].reshape(pool, PAGE_ROWS, V_DIM)
    new_rows = lambda a: a.reshape(dec_b, dec_s * N_HEADS, V_DIM)
    o_s, rows_in = _attn_sample(
        q_s.reshape(dec_b, dec_s, QK_WIDTH), new_rows(k_s), new_rows(v_s), page_rows(cache_k),
        page_rows(cache_v), page_table, lam_vecs, g_sub, lam_init,
        xt_p.reshape(n_p, PACK_TILE, LANES), dest_p, pstart, counts_p, n_rows, n_s)
    c_s, st_s = _conv_sample(u_s.reshape(dec_b, dec_s, CONV_CH), state_conv[layer], *conv_w)
    x1_s, xt_s, idx_s, wts_s, rank_s, _ = _mix(
        o_s.reshape(n_s, ATTN_WIDTH), c_s.reshape(n_s, CONV_CH), xs, wo_a, wo_c, *ln1, wr_hi,
        wr_lo, br, cnt_p, n_s)
    dest_s = _dest(idx_s, rank_s, pstart_col, 128)
    rows_in = _dispatch(xt_s.reshape(n_s, PACK_TILE, LANES), dest_s, rows_in, 128)

    wg, wu, wd, wsg, wsu, wsd = moe_w
    rows_out = _experts(rows_in.reshape(n_rows * PACK_TILE, LANES), block_e, nb_used, wg, wu, wd)
    y_p = _combine(dest_p, wts_p, x1_p, rows_out, wsg, wsu, wsd, *ln2, 256)
    y_s = _combine(dest_s, wts_s, x1_s, rows_out, wsg, wsu, wsd, *ln2, 128)

    keep = CONV_WIDTH - 1
    u_p3 = u_p.reshape(batch, seq, CONV_CH)
    return (y_p.reshape(batch, seq, D_MODEL),
            y_s.reshape(dec_b, dec_s, D_MODEL),
            k_p.reshape(1, batch, seq, N_HEADS, V_DIM),
            v_p.reshape(1, batch, seq, N_HEADS, V_DIM),
            u_p3[:, seq - keep:, :][None],
            k_s.reshape(1, dec_b, dec_s, N_HEADS, V_DIM),
            v_s.reshape(1, dec_b, dec_s, N_HEADS, V_DIM),
            st_s[None])
```

```python
import functools
import math

import jax
import jax.numpy as jnp
from jax import lax
from jax.experimental import pallas as pl
from jax.experimental.pallas import tpu as pltpu

F32 = jnp.float32
BF16 = jnp.bfloat16
I32 = jnp.int32
U32 = jnp.uint32

D_MODEL = 1024
N_HEADS = 4
HEAD_DIM = 64
V_DIM = 128
ATTN_WIDTH = N_HEADS * V_DIM
QK_WIDTH = N_HEADS * 2 * HEAD_DIM
ATTN_SCALE = HEAD_DIM ** -0.5
LOG2_E = math.log2(math.e)
CONV_CH = D_MODEL - ATTN_WIDTH
CONV_WIDTH = 31
N_EXPERTS = 64
N_GROUPS = 8
GROUP_SIZE = N_EXPERTS // N_GROUPS
TOPK_GROUPS = 4
TOP_K = 8
D_EXPERT = D_MODEL // 4
ROUTED_SCALE = 2.5
DEPTH = 1
DEEPNORM_ALPHA = (2 * DEPTH) ** 0.25
NORM_EPS = 1e-5
NEG_INF = -1e30
PAGE_SIZE = 128

LANES = 128
ROW_TILE = D_MODEL // LANES
PACK_TILE = ROW_TILE // 2
VMEM_LIMIT = 48 * 1024 * 1024
PAGES_PER_STEP = 32
ATTN_TILE = 512
PAGE_ROWS = PAGE_SIZE * N_HEADS
EXPERT_BLOCK = 512
HIST = 32


def _lambda_init(layer):
    return 0.8 - 0.6 * math.exp(-0.3 * layer)


def _sigmoid(x):
    return 1.0 / (1.0 + jnp.exp(-x))


def _silu(x):
    return x * _sigmoid(x)


def _layer_norm(x, g, b):
    mu = jnp.mean(x, axis=-1, keepdims=True)
    xc = x - mu
    var = jnp.mean(xc * xc, axis=-1, keepdims=True)
    return xc * lax.rsqrt(var + NORM_EPS) * g + b


def _dot(a, b):
    return jnp.dot(a, b, preferred_element_type=F32)


def _dot_nt(a, b):
    return lax.dot_general(a, b, (((1,), (1,)), ((), ())), preferred_element_type=F32)


def _diff_lambda(lq1, lk1, lq2, lk2, lam_init):
    a = jnp.exp(jnp.sum(lq1 * lk1, axis=-1, keepdims=True))
    b = jnp.exp(jnp.sum(lq2 * lk2, axis=-1, keepdims=True))
    return a - b + lam_init


def _params(dims, vmem_limit=VMEM_LIMIT):
    return pltpu.CompilerParams(dimension_semantics=dims, vmem_limit_bytes=vmem_limit)


def _store_row_tiles(ref, x):
    m = x.shape[0]
    for j in range(ROW_TILE):
        ref[pl.ds(j, m, stride=ROW_TILE), :] = x[:, j * LANES:(j + 1) * LANES]


def _load_row_tiles(ref, m):
    return jnp.concatenate([ref[pl.ds(j, m, stride=ROW_TILE), :] for j in range(ROW_TILE)],
                           axis=1)


def _token_tile(ref, t):
    return ref.at[pl.ds(pl.multiple_of(t * ROW_TILE, ROW_TILE), ROW_TILE)]


def _store_packed_tiles(ref, x):
    m = x.shape[0]
    bits = pltpu.bitcast(x.astype(BF16).astype(F32), U32)
    half = D_MODEL // 2
    for j in range(PACK_TILE):
        lo = bits[:, j * LANES:(j + 1) * LANES] >> 16
        hi = bits[:, half + j * LANES:half + (j + 1) * LANES]
        ref[pl.ds(j, m, stride=PACK_TILE), :] = lo | hi


def _load_packed_tiles(ref, m):
    words = [ref[pl.ds(j, m, stride=PACK_TILE), :] for j in range(PACK_TILE)]
    lo = [pltpu.bitcast(w << 16, F32) for w in words]
    hi = [pltpu.bitcast(w & jnp.uint32(0xFFFF0000), F32) for w in words]
    return jnp.concatenate(lo + hi, axis=1).astype(BF16)


def _store_head_rows(ref, x):
    m = x.shape[0]
    for h in range(N_HEADS):
        ref[pl.ds(h, m, stride=N_HEADS), :] = x[:, h * V_DIM:(h + 1) * V_DIM]


def _inproj_body(x_ref, w_ref, *refs):
    xb = x_ref[...].astype(BF16)

    def mm(c0):
        return _dot(xb, w_ref[:, c0:c0 + QK_WIDTH])

    k = mm(QK_WIDTH)
    if len(refs) == 4:
        q_ref, k_ref, v_ref, u_ref = refs
        q_ref[...] = mm(0) * ATTN_SCALE
    else:
        wqt_ref, wvt_ref, q_ref, k_ref, v_ref, u_ref, kb_ref, vt_ref = refs
        q_t = _dot_nt(wqt_ref[...], xb) * (ATTN_SCALE * LOG2_E)
        v_t = _dot_nt(wvt_ref[...], xb).astype(BF16)
        for c in range(q_ref.shape[0]):
            cols = slice(c * ATTN_TILE, (c + 1) * ATTN_TILE)
            q_ref[c] = q_t[:, cols]
            vt_ref[c] = v_t[:, cols]
        kb_ref[...] = k.astype(BF16)
    _store_head_rows(k_ref, k)
    _store_head_rows(v_ref, mm(2 * QK_WIDTH))
    a = mm(2 * QK_WIDTH + ATTN_WIDTH)
    b = mm(2 * QK_WIDTH + ATTN_WIDTH + CONV_CH)
    u_ref[...] = a * _sigmoid(b)


def _inproj(x, w_in_b, tm, transposed_w=()):
    n = x.shape[0]
    wide = QK_WIDTH
    blk = pl.BlockSpec((tm, wide), lambda i: (i, 0))
    hblk = pl.BlockSpec((tm * N_HEADS, V_DIM), lambda i: (i, 0))
    full = lambda a: pl.BlockSpec(a.shape, lambda i: (0, 0))
    row_major = jax.ShapeDtypeStruct((n, wide), F32)
    head_rows = jax.ShapeDtypeStruct((n * N_HEADS, V_DIM), F32)
    out_specs = [blk, hblk, hblk, blk]
    out_shape = [row_major, head_rows, head_rows, row_major]
    if transposed_w:
        t_blk = pl.BlockSpec((tm // ATTN_TILE, wide, ATTN_TILE), lambda i: (i, 0, 0))
        t_shape = (n // ATTN_TILE, wide, ATTN_TILE)
        out_specs = [t_blk, hblk, hblk, blk, blk, t_blk]
        out_shape = [jax.ShapeDtypeStruct(t_shape, F32), head_rows, head_rows,
                     row_major, jax.ShapeDtypeStruct((n, wide), BF16),
                     jax.ShapeDtypeStruct(t_shape, BF16)]
    return pl.pallas_call(
        _inproj_body,
        name="inproj",
        grid=(n // tm,),
        in_specs=[pl.BlockSpec((tm, D_MODEL), lambda i: (i, 0)), full(w_in_b)]
        + [full(w) for w in transposed_w],
        out_specs=out_specs,
        out_shape=out_shape,
        compiler_params=_params(("parallel",)),
    )(x, w_in_b, *transposed_w)


def _attn_p_body(lam_init, q_ref, k_ref, v_ref, lq1, lk1, lq2, lk2, g_ref, o_ref, acc_ref):
    qi = pl.program_id(2)
    tq = q_ref.shape[1]
    tk = v_ref.shape[2]
    q_t = q_ref[...]
    feat = lax.broadcasted_iota(I32, q_t.shape, 0)
    q_both = jnp.concatenate([jnp.where(feat < HEAD_DIM, q_t, 0.0),
                              jnp.where(feat >= HEAD_DIM, q_t, 0.0)], axis=1).astype(BF16)
    acc_ref[...] = jnp.zeros(acc_ref.shape, F32)

    def chunk(masked, j, carry):
        m_old, l_old = carry
        kc = k_ref[pl.ds(pl.multiple_of(j * tk, tk), tk), :]
        s_t = _dot(kc, q_both)
        if masked:
            key = lax.broadcasted_iota(I32, s_t.shape, 0)
            qry = lax.broadcasted_iota(I32, s_t.shape, 1) % tq
            s_t = jnp.where(key <= qry, s_t, NEG_INF)
        m_new = jnp.maximum(m_old, jnp.max(s_t, axis=0, keepdims=True))
        alpha = jnp.exp2(m_old - m_new)
        p_t = jnp.exp2(s_t - m_new)
        l_new = alpha * l_old + jnp.sum(p_t, axis=0, keepdims=True)
        acc_ref[...] = alpha * acc_ref[...] + _dot(v_ref[j], p_t.astype(BF16))
        return m_new, l_new

    init = (jnp.full((1, 2 * tq), -jnp.inf, F32), jnp.zeros((1, 2 * tq), F32))
    carry = lax.fori_loop(0, qi, functools.partial(chunk, False), init)
    _, l_all = chunk(True, qi, carry)

    lam = _diff_lambda(lq1[...], lk1[...], lq2[...], lk2[...], lam_init)
    on = acc_ref[...] / l_all
    o_t = on[:, :tq] - lam * on[:, tq:]
    ms = jnp.mean(o_t * o_t, axis=0, keepdims=True)
    o_t = o_t * lax.rsqrt(ms + NORM_EPS) * g_ref[...] * (1.0 - lam_init)
    o_ref[...] = o_t.T


def _attn_prompt(q_t, kb, v_t, lam_vecs, subln_g_col, lam_init, batch, seq):
    t = q_t.shape[2]
    n = kb.shape[0]
    nq = seq // t
    small = lambda w: pl.BlockSpec((1, w), lambda b, h, i: (0, 0))
    return pl.pallas_call(
        functools.partial(_attn_p_body, lam_init),
        name="attn_prompt",
        grid=(batch, N_HEADS, nq),
        in_specs=[pl.BlockSpec((None, V_DIM, t), lambda b, h, i: (b * nq + i, h, 0)),
                  pl.BlockSpec((seq, V_DIM), lambda b, h, i: (b, h)),
                  pl.BlockSpec((nq, V_DIM, t), lambda b, h, i: (b, h, 0))]
        + [small(HEAD_DIM)] * 4 + [pl.BlockSpec((V_DIM, 1), lambda b, h, i: (0, 0))],
        out_specs=pl.BlockSpec((t, V_DIM), lambda b, h, i: (b * nq + i, h)),
        out_shape=jax.ShapeDtypeStruct((n, ATTN_WIDTH), F32),
        scratch_shapes=[pltpu.VMEM((V_DIM, 2 * t), F32)],
        compiler_params=_params(("parallel", "parallel", "parallel")),
    )(q_t, kb, v_t, *lam_vecs, subln_g_col)


def _attn_s_body(lam_init, n_steps, tokens_per_step, reserve, pt_ref, ps_ref, cnt_ref, q_ref,
                 kn_ref, vn_ref, *rest):
    del pt_ref
    npg = PAGES_PER_STEP
    k_refs = rest[:npg]
    v_refs = rest[npg:2 * npg]
    (lq1, lk1, lq2, lk2, g_ref, dest_ref, x_ref, o_ref, xs_ref, qall, knew, vnew, m_ref, l_ref,
     acc_ref, zbuf, sem, zsem) = rest[2 * npg:]
    j = pl.program_id(1)
    ds = q_ref.shape[0]
    hr = 2 * ds
    step = pl.program_id(0) * n_steps + j

    @pl.when(step == 0)
    def _():
        _zero_segment_tails(ps_ref, cnt_ref, xs_ref, zbuf, zsem, reserve)

    first_token = (step % (x_ref.shape[0] // tokens_per_step)) * tokens_per_step
    _start_row_copies(dest_ref, x_ref, xs_ref, sem, first_token, tokens_per_step)

    def head_rows(ref, h):
        return ref[pl.ds(h, PAGE_SIZE, stride=N_HEADS), :].astype(BF16)

    @pl.when(j == 0)
    def _():
        pieces = []
        for h in range(N_HEADS):
            qh = q_ref[:, h * V_DIM:(h + 1) * V_DIM]
            lane = lax.broadcasted_iota(I32, qh.shape, 1)
            pieces.append(jnp.where(lane < HEAD_DIM, qh, 0.0))
            pieces.append(jnp.where(lane >= HEAD_DIM, qh, 0.0))
        qall[...] = jnp.concatenate(pieces, axis=0).astype(BF16)
        knew[...] = jnp.zeros(knew.shape, F32)
        vnew[...] = jnp.zeros(vnew.shape, F32)
        knew[0:ds * N_HEADS] = kn_ref[...]
        vnew[0:ds * N_HEADS] = vn_ref[...]
        row = lax.broadcasted_iota(I32, (hr, PAGE_SIZE), 0)
        key = lax.broadcasted_iota(I32, (hr, PAGE_SIZE), 1)
        keep = key <= (row % ds)
        for h in range(N_HEADS):
            rows = slice(h * hr, (h + 1) * hr)
            s = jnp.where(keep, _dot_nt(qall[rows], head_rows(knew, h)), NEG_INF)
            m = jnp.max(s, axis=-1, keepdims=True)
            p = jnp.exp(s - m)
            m_ref[rows] = m
            l_ref[rows] = jnp.sum(p, axis=-1, keepdims=True)
            acc_ref[rows] = _dot(p.astype(BF16), head_rows(vnew, h))

    q_all, m_all, l_all, acc_all = qall[...], m_ref[...], l_ref[...], acc_ref[...]
    heads = range(N_HEADS)
    rows = [slice(h * hr, (h + 1) * hr) for h in heads]
    s = [_dot_nt(q_all[rows[h]],
                 jnp.concatenate([head_rows(k_refs[i], h) for i in range(npg)], axis=0))
         for h in heads]
    s = jnp.concatenate(s, axis=0)
    m_new = jnp.maximum(m_all, jnp.max(s, axis=-1, keepdims=True))
    alpha = jnp.exp(m_all - m_new)
    p = jnp.exp(s - m_new)
    l_new = alpha * l_all + jnp.sum(p, axis=-1, keepdims=True)
    pb = p.astype(BF16)
    pv = [_dot(pb[rows[h]],
               jnp.concatenate([head_rows(v_refs[i], h) for i in range(npg)], axis=0))
          for h in heads]
    acc_new = alpha * acc_all + jnp.concatenate(pv, axis=0)
    m_ref[...] = m_new
    l_ref[...] = l_new
    acc_ref[...] = acc_new
    _wait_row_copies(x_ref, xs_ref, sem, tokens_per_step)

    @pl.when(j == n_steps - 1)
    def _():
        lam = _diff_lambda(lq1[...], lk1[...], lq2[...], lk2[...], lam_init)
        on = acc_new / l_new
        for h in range(N_HEADS):
            r0 = h * hr
            o = on[r0:r0 + ds] - lam * on[r0 + ds:r0 + 2 * ds]
            ms = jnp.mean(o * o, axis=-1, keepdims=True)
            o_ref[:, h * V_DIM:(h + 1) * V_DIM] = (
                o * lax.rsqrt(ms + NORM_EPS) * g_ref[...] * (1.0 - lam_init))


def _attn_sample(q, k_new, v_new, cache_k, cache_v, page_table, lam_vecs, subln_g, lam_init,
                 x_tiles, dest, pstart, counts, n_rows, reserve):
    b, ds, _ = q.shape
    n_pages = page_table.shape[1]
    n_steps = n_pages // PAGES_PER_STEP
    nrow = N_HEADS * 2 * ds
    pt = page_table.reshape(-1)
    total_steps = b * n_steps
    n = x_tiles.shape[0]
    tokens_per_step = n // total_steps
    assert tokens_per_step * total_steps == n and LANES % tokens_per_step == 0
    per_block = LANES // tokens_per_step
    tile = (PACK_TILE, LANES)

    def page_map(i, bi, j, pt_ref, *_):
        return (pt_ref[bi * n_pages + j * PAGES_PER_STEP + i], 0, 0)

    page_specs = [pl.BlockSpec((None, PAGE_ROWS, V_DIM), functools.partial(page_map, i))
                  for i in range(PAGES_PER_STEP)]
    token_block = lambda bi, j, *_: (bi * n_steps + j) // per_block
    small = lambda w: pl.BlockSpec((1, w), lambda bi, j, *_: (0, 0))
    new_spec = pl.BlockSpec((None, ds * N_HEADS, V_DIM), lambda bi, j, *_: (bi, 0, 0))
    qo_spec = pl.BlockSpec((None, ds, ATTN_WIDTH), lambda bi, j, *_: (bi, 0, 0))
    grid_spec = pltpu.PrefetchScalarGridSpec(
        num_scalar_prefetch=3,
        grid=(b, n_steps),
        in_specs=[qo_spec, new_spec, new_spec] + page_specs + page_specs
        + [small(HEAD_DIM)] * 4 + [small(V_DIM)]
        + [pl.BlockSpec((TOP_K, LANES), lambda bi, j, *_: (0, token_block(bi, j)),
                        memory_space=pltpu.SMEM),
           pl.BlockSpec((LANES,) + tile, lambda bi, j, *_: (token_block(bi, j), 0, 0))],
        out_specs=[qo_spec, pl.BlockSpec(memory_space=pl.ANY)],
        scratch_shapes=[pltpu.VMEM((nrow, V_DIM), BF16),
                        pltpu.VMEM((PAGE_ROWS, V_DIM), F32),
                        pltpu.VMEM((PAGE_ROWS, V_DIM), F32),
                        pltpu.VMEM((nrow, 1), F32),
                        pltpu.VMEM((nrow, 1), F32),
                        pltpu.VMEM((nrow, V_DIM), F32),
                        pltpu.VMEM((TAIL_PIECES[0],) + tile, U32),
                        pltpu.SemaphoreType.DMA, pltpu.SemaphoreType.DMA],
    )
    return pl.pallas_call(
        functools.partial(_attn_s_body, lam_init, n_steps, tokens_per_step, reserve),
        name="attn_sample",
        grid_spec=grid_spec,
        out_shape=[jax.ShapeDtypeStruct((b, ds, ATTN_WIDTH), F32),
                   jax.ShapeDtypeStruct((n_rows,) + tile, U32)],
        compiler_params=_params(("arbitrary", "arbitrary")),
    )(pt, pstart, counts, q, k_new, v_new, *([cache_k] * PAGES_PER_STEP),
      *([cache_v] * PAGES_PER_STEP), *lam_vecs, subln_g, dest, x_tiles)


def _conv_taps(ext_ref, w_ref, bias, rows):
    off = HIST - (CONV_WIDTH - 1)
    acc = jnp.broadcast_to(bias, (rows, CONV_CH))
    for jt in range(CONV_WIDTH):
        acc = acc + w_ref[jt:jt + 1, :] * ext_ref[pl.ds(jt + off, rows), :]
    return acc


CONV_ROWS = 64
SUBLANES = 8


def _conv_p_body(u_ref, h_ref, w_ref, b_ref, g_ref, beta_ref, c_ref, ext_ref, sh_ref):
    t = pl.program_id(1)
    tm = u_ref.shape[0]
    hist = h_ref[...]
    ext_ref[0:HIST] = jnp.where(t == 0, jnp.zeros_like(hist), hist)
    ext_ref[HIST:HIST + tm] = u_ref[...]
    span = sh_ref.shape[1]
    for r in range(1, SUBLANES):
        sh_ref[r - 1] = ext_ref[pl.ds(r, span), :]
    off = HIST - (CONV_WIDTH - 1)
    bias = b_ref[...]
    for c0 in range(0, tm, CONV_ROWS):
        acc = jnp.broadcast_to(bias, (CONV_ROWS, CONV_CH))
        for jt in range(CONV_WIDTH):
            r = (jt + off) % SUBLANES
            base = c0 + jt + off - r
            src = ext_ref if r == 0 else sh_ref.at[r - 1]
            acc = acc + w_ref[jt:jt + 1, :] * src[pl.ds(base, CONV_ROWS), :]
        c_ref[c0:c0 + CONV_ROWS, :] = _silu(_layer_norm(acc, g_ref[...], beta_ref[...]))


def _conv_prompt(u, w_dw, b_dw, g, beta, batch, seq, tm):
    n = u.shape[0]
    nt = seq // tm
    per = tm // HIST
    cur = lambda b, t: (b * nt + t, 0)
    prev = lambda b, t: (jnp.maximum((b * nt + t) * per - 1, 0), 0)
    small = lambda r: pl.BlockSpec((r, CONV_CH), lambda b, t: (0, 0))
    return pl.pallas_call(
        _conv_p_body,
        name="conv_prompt",
        grid=(batch, nt),
        in_specs=[pl.BlockSpec((tm, CONV_CH), cur), pl.BlockSpec((HIST, CONV_CH), prev),
                  small(CONV_WIDTH), small(1), small(1), small(1)],
        out_specs=pl.BlockSpec((tm, CONV_CH), cur),
        out_shape=jax.ShapeDtypeStruct((n, CONV_CH), F32),
        scratch_shapes=[pltpu.VMEM((HIST + tm, CONV_CH), F32),
                        pltpu.VMEM((SUBLANES - 1, HIST + tm - SUBLANES, CONV_CH), F32)],
        compiler_params=_params(("parallel", "parallel")),
    )(u, u, w_dw, b_dw, g, beta)


def _conv_s_body(u_ref, st_ref, w_ref, b_ref, g_ref, beta_ref, c_ref, ns_ref, ext_ref):
    ds = u_ref.shape[0]
    keep = CONV_WIDTH - 1
    off = HIST - keep
    ext_ref[0:off] = jnp.zeros((off, CONV_CH), F32)
    ext_ref[off:HIST] = st_ref[...]
    ext_ref[HIST:HIST + ds] = u_ref[...]
    y = _conv_taps(ext_ref, w_ref, b_ref[...], ds)
    c_ref[...] = _silu(_layer_norm(y, g_ref[...], beta_ref[...]))
    ns_ref[...] = ext_ref[HIST + ds - keep:HIST + ds]


def _conv_sample(u, state, w_dw, b_dw, g, beta):
    b, ds, _ = u.shape
    keep = CONV_WIDTH - 1
    per_b = lambda r: pl.BlockSpec((None, r, CONV_CH), lambda i: (i, 0, 0))
    small = lambda r: pl.BlockSpec((r, CONV_CH), lambda i: (0, 0))
    return pl.pallas_call(
        _conv_s_body,
        name="conv_sample",
        grid=(b,),
        in_specs=[per_b(ds), per_b(keep), small(CONV_WIDTH), small(1), small(1), small(1)],
        out_specs=[per_b(ds), per_b(keep)],
        out_shape=[jax.ShapeDtypeStruct((b, ds, CONV_CH), F32),
                   jax.ShapeDtypeStruct((b, keep, CONV_CH), F32)],
        scratch_shapes=[pltpu.VMEM((HIST + ds, CONV_CH), F32)],
        compiler_params=_params(("parallel",)),
    )(u, state, w_dw, b_dw, g, beta)


def _first_index(hit, iota, limit, axis):
    return jnp.min(jnp.where(hit, iota, limit), axis=axis, keepdims=True)


def _route(x1, wr_hi, wr_lo, bias):
    tm = x1.shape[0]
    x_hi = x1.astype(BF16)
    x_lo = (x1 - x_hi.astype(F32)).astype(BF16)
    logits = _dot_nt(wr_hi, x_hi) + (_dot_nt(wr_lo, x_hi) + _dot_nt(wr_hi, x_lo))
    scores = _sigmoid(logits)
    sel = scores + bias
    sel3 = sel.reshape(N_GROUPS, GROUP_SIZE, tm)
    member = lax.broadcasted_iota(I32, sel3.shape, 1)
    m1 = jnp.max(sel3, axis=1, keepdims=True)
    i1 = _first_index(sel3 == m1, member, GROUP_SIZE, 1)
    m2 = jnp.max(jnp.where(member == i1, -jnp.inf, sel3), axis=1, keepdims=True)
    gs = jnp.broadcast_to(m1 + m2, sel3.shape).reshape(N_EXPERTS, tm)
    eiota = lax.broadcasted_iota(I32, (N_EXPERTS, tm), 0)
    giota = eiota // GROUP_SIZE
    gmask = jnp.zeros((N_EXPERTS, tm), jnp.bool_)
    for _ in range(TOPK_GROUPS):
        m = jnp.max(gs, axis=0, keepdims=True)
        gi = _first_index(gs == m, giota, N_GROUPS, 0)
        pick = giota == gi
        gmask = jnp.logical_or(gmask, pick)
        gs = jnp.where(pick, -jnp.inf, gs)
    selm = jnp.where(gmask, sel, NEG_INF)
    idx_rows, w_rows = [], []
    for _ in range(TOP_K):
        m = jnp.max(selm, axis=0, keepdims=True)
        ei = _first_index(selm == m, eiota, N_EXPERTS, 0)
        pick = eiota == ei
        idx_rows.append(ei)
        w_rows.append(jnp.sum(jnp.where(pick, scores, 0.0), axis=0, keepdims=True))
        selm = jnp.where(pick, -jnp.inf, selm)
    idx = jnp.concatenate(idx_rows, axis=0)
    w = jnp.concatenate(w_rows, axis=0)
    w = w / jnp.sum(w, axis=0, keepdims=True) * ROUTED_SCALE
    return idx, w


def _mix_body(o_ref, c_ref, x_ref, wo_a, wo_c, g_ref, b_ref, wrh_ref, wrl_ref, br_ref, cnt0_ref,
              x1_ref, xrt_ref, idx_ref, wts_ref, rank_ref, cnt_ref, base_ref):
    i = pl.program_id(0)
    tm = x_ref.shape[0]

    @pl.when(i == 0)
    def _():
        base_ref[...] = cnt0_ref[...]

    mix = _dot(o_ref[...].astype(BF16), wo_a[...]) + _dot(c_ref[...].astype(BF16), wo_c[...])
    x1 = _layer_norm(DEEPNORM_ALPHA * x_ref[...] + mix, g_ref[...], b_ref[...])
    x1_ref[...] = x1
    _store_packed_tiles(xrt_ref, x1)

    idx, w = _route(x1, wrh_ref[...], wrl_ref[...], br_ref[...])
    idx_ref[...] = idx
    w_rows = jnp.concatenate([w, jnp.zeros((LANES - TOP_K, tm), F32)], axis=0)
    wts_ref[...] = w_rows.T

    eiota = lax.broadcasted_iota(I32, (N_EXPERTS, tm), 0)
    chosen = jnp.zeros((N_EXPERTS, tm), F32)
    for k in range(TOP_K):
        chosen = chosen + jnp.where(eiota == idx[k:k + 1, :], 1.0, 0.0)
    r = lax.broadcasted_iota(I32, (tm, tm), 0)
    c = lax.broadcasted_iota(I32, (tm, tm), 1)
    before = jnp.where(r < c, 1.0, 0.0).astype(BF16)
    base = base_ref[:, 0:1]
    rank_full = _dot(chosen.astype(BF16), before) + base
    rows = [jnp.sum(jnp.where(eiota == idx[k:k + 1, :], rank_full, 0.0), axis=0, keepdims=True)
            for k in range(TOP_K)]
    rank_ref[...] = jnp.concatenate(rows, axis=0).astype(I32)
    total = base + jnp.sum(chosen, axis=1, keepdims=True)
    base_ref[...] = jnp.broadcast_to(total, base_ref.shape)
    cnt_ref[...] = jnp.broadcast_to(total, cnt_ref.shape)


def _mix(o, c, x, wo_a, wo_c, ln_g, ln_b, wr_hi, wr_lo, b_router, cnt0, tm):
    n = x.shape[0]
    rows = lambda w: pl.BlockSpec((tm, w), lambda i: (i, 0))
    full = lambda a: pl.BlockSpec(a.shape, lambda i: (0, 0))
    cols = pl.BlockSpec((TOP_K, tm), lambda i: (0, i))
    cnt = pl.BlockSpec((N_EXPERTS, LANES), lambda i: (0, 0))
    return pl.pallas_call(
        _mix_body,
        name="mix",
        grid=(n // tm,),
        in_specs=[rows(ATTN_WIDTH), rows(CONV_CH), rows(D_MODEL), full(wo_a), full(wo_c),
                  full(ln_g), full(ln_b), full(wr_hi), full(wr_lo), full(b_router), cnt],
        out_specs=[rows(D_MODEL), pl.BlockSpec((tm * PACK_TILE, LANES), lambda i: (i, 0)), cols,
                   rows(LANES), cols, cnt],
        out_shape=[jax.ShapeDtypeStruct((n, D_MODEL), F32),
                   jax.ShapeDtypeStruct((n * PACK_TILE, LANES), U32),
                   jax.ShapeDtypeStruct((TOP_K, n), I32),
                   jax.ShapeDtypeStruct((n, LANES), F32),
                   jax.ShapeDtypeStruct((TOP_K, n), I32),
                   jax.ShapeDtypeStruct((N_EXPERTS, LANES), F32)],
        scratch_shapes=[pltpu.VMEM((N_EXPERTS, LANES), F32)],
        compiler_params=_params(("arbitrary",)),
    )(o, c, x, wo_a, wo_c, ln_g, ln_b, wr_hi, wr_lo, b_router, cnt0)


def _dest_body(idx_ref, rank_ref, ps_ref, dest_ref):
    idx = idx_ref[...]
    tm = idx.shape[1]
    eiota = lax.broadcasted_iota(I32, (N_EXPERTS, tm), 0)
    ps = ps_ref[...]
    rows = [jnp.sum(jnp.where(eiota == idx[k:k + 1, :], ps, 0.0), axis=0, keepdims=True)
            for k in range(TOP_K)]
    dest_ref[...] = jnp.concatenate(rows, axis=0).astype(I32) + rank_ref[...]


def _dest(idx, rank, pstart_col, tm):
    n = idx.shape[1]
    cols = pl.BlockSpec((TOP_K, tm), lambda i: (0, i))
    return pl.pallas_call(
        _dest_body,
        name="dest",
        grid=(n // tm,),
        in_specs=[cols, cols, pl.BlockSpec((N_EXPERTS, 1), lambda i: (0, 0))],
        out_specs=cols,
        out_shape=jax.ShapeDtypeStruct((TOP_K, n), I32),
        compiler_params=_params(("parallel",)),
    )(idx, rank, pstart_col)


TAIL_PIECES = tuple(1 << s for s in reversed(range(EXPERT_BLOCK.bit_length())))
ROW_UNROLL = 8


def _zero_segment_tails(ps_ref, cnt_ref, xs_ref, zbuf, zsem, reserve):
    zbuf[...] = jnp.zeros(zbuf.shape, U32)

    def tail(wait, e, carry):
        cnt = cnt_ref[e]
        base = ps_ref[e] + cnt
        seg = (cnt + reserve + EXPERT_BLOCK - 1) // EXPERT_BLOCK * EXPERT_BLOCK
        pad = seg - cnt
        for p in TAIL_PIECES:
            @pl.when((pad & p) != 0)
            def _():
                cp = pltpu.make_async_copy(zbuf.at[pl.ds(0, p)], xs_ref.at[pl.ds(base, p)], zsem)
                if wait:
                    cp.wait()
                else:
                    cp.start()
            base = base + (pad & p)
        return carry

    lax.fori_loop(0, N_EXPERTS, functools.partial(tail, False), 0)
    lax.fori_loop(0, N_EXPERTS, functools.partial(tail, True), 0)


def _start_row_copies(dest_ref, x_ref, xs_ref, sem, first, count):
    def issue(g, carry):
        for u in range(ROW_UNROLL):
            t = first + g * ROW_UNROLL + u
            for k in range(TOP_K):
                pltpu.make_async_copy(x_ref.at[t], xs_ref.at[dest_ref[k, t]],
                                      sem).start(priority=k % 2)
        return carry

    lax.fori_loop(0, count // ROW_UNROLL, issue, 0)


def _wait_row_copies(x_ref, xs_ref, sem, count):
    for _ in range(TOP_K):
        pltpu.make_async_copy(x_ref.at[pl.ds(0, count)], xs_ref.at[pl.ds(0, count)], sem).wait()


def _dispatch_body(dest_ref, x_ref, xs_prev_ref, xs_ref, sem):
    del xs_prev_ref
    ts = x_ref.shape[0]
    _start_row_copies(dest_ref, x_ref, xs_ref, sem, 0, ts)
    _wait_row_copies(x_ref, xs_ref, sem, ts)


def _dispatch(x_tiles, dest, xs_prev, ts):
    n = x_tiles.shape[0]
    return pl.pallas_call(
        _dispatch_body,
        name="dispatch",
        grid=(n // ts,),
        in_specs=[pl.BlockSpec((TOP_K, ts), lambda i: (0, i), memory_space=pltpu.SMEM),
                  pl.BlockSpec((ts, PACK_TILE, LANES), lambda i: (i, 0, 0)),
                  pl.BlockSpec(memory_space=pl.ANY)],
        out_specs=pl.BlockSpec(memory_space=pl.ANY),
        out_shape=jax.ShapeDtypeStruct(xs_prev.shape, xs_prev.dtype),
        scratch_shapes=[pltpu.SemaphoreType.DMA],
        input_output_aliases={2: 0},
        compiler_params=_params(("arbitrary",)),
    )(dest, x_tiles, xs_prev)


def _expert_body(be_ref, nb_ref, xs_ref, wg_ref, wu_ref, wd_ref, ys_ref, wg_b, wu_b, wd_b):
    i = pl.program_id(0)

    @pl.when(jnp.logical_or(i == 0, be_ref[i] != be_ref[jnp.maximum(i - 1, 0)]))
    def _():
        wg_b[...] = wg_ref[...].astype(BF16)
        wu_b[...] = wu_ref[...].astype(BF16)
        wd_b[...] = wd_ref[...].astype(BF16)

    @pl.when(i < nb_ref[0])
    def _():
        x = _load_packed_tiles(xs_ref, EXPERT_BLOCK)
        h = (_silu(_dot(x, wg_b[...])) * _dot(x, wu_b[...])).astype(BF16)
        _store_row_tiles(ys_ref, _dot(h, wd_b[...]))


def _experts(xs, block_e, nb_used, wg, wu, wd):
    rows = xs.shape[0] // PACK_TILE
    n_blocks = rows // EXPERT_BLOCK
    used = lambda i, be, nb: (jnp.minimum(i, nb[0] - 1), 0)
    in_rows = pl.BlockSpec((EXPERT_BLOCK * PACK_TILE, LANES), used)
    grid_spec = pltpu.PrefetchScalarGridSpec(
        num_scalar_prefetch=2,
        grid=(n_blocks,),
        in_specs=[in_rows,
                  pl.BlockSpec((None, D_MODEL, D_EXPERT), lambda i, be, nb: (be[i], 0, 0)),
                  pl.BlockSpec((None, D_MODEL, D_EXPERT), lambda i, be, nb: (be[i], 0, 0)),
                  pl.BlockSpec((None, D_EXPERT, D_MODEL), lambda i, be, nb: (be[i], 0, 0))],
        out_specs=pl.BlockSpec((EXPERT_BLOCK * ROW_TILE, LANES), used),
        scratch_shapes=[pltpu.VMEM((D_MODEL, D_EXPERT), BF16), pltpu.VMEM((D_MODEL, D_EXPERT), BF16),
                        pltpu.VMEM((D_EXPERT, D_MODEL), BF16)],
    )
    return pl.pallas_call(
        _expert_body,
        name="experts",
        grid_spec=grid_spec,
        out_shape=jax.ShapeDtypeStruct((rows * ROW_TILE, LANES), F32),
        compiler_params=_params(("arbitrary",)),
    )(block_e, nb_used, xs, wg, wu, wd)


def _combine_body(dest_ref, dnext_ref, wts_ref, x1_ref, ys_ref, wsg, wsu, wsd, g_ref, b_ref,
                  y_ref, buf, sems):
    i = pl.program_id(0)
    n_tiles = pl.num_programs(0)
    tc = x1_ref.shape[0]
    slot = i % 2

    def issue(d_ref, s, g, carry):
        for u in range(ROW_UNROLL):
            t = g * ROW_UNROLL + u
            for k in range(TOP_K):
                pltpu.make_async_copy(_token_tile(ys_ref, d_ref[k, t]),
                                      _token_tile(buf.at[s, k], t),
                                      sems.at[s]).start(priority=k % 2)
        return carry

    n_groups = tc // ROW_UNROLL

    @pl.when(i == 0)
    def _():
        lax.fori_loop(0, n_groups, functools.partial(issue, dest_ref, slot), 0)

    @pl.when(i + 1 < n_tiles)
    def _():
        lax.fori_loop(0, n_groups, functools.partial(issue, dnext_ref, 1 - slot), 0)

    x1 = x1_ref[...]
    xb = x1.astype(BF16)
    hs = (_silu(_dot(xb, wsg[...])) * _dot(xb, wsu[...])).astype(BF16)
    shared = _dot(hs, wsd[...])

    for k in range(TOP_K):
        pltpu.make_async_copy(ys_ref.at[pl.ds(0, tc * ROW_TILE)], buf.at[slot, k],
                              sems.at[slot]).wait()

    w = wts_ref[...]
    routed = w[:, 0:1] * _load_row_tiles(buf.at[slot, 0], tc)
    for k in range(1, TOP_K):
        routed = routed + w[:, k:k + 1] * _load_row_tiles(buf.at[slot, k], tc)
    y_ref[...] = _layer_norm(DEEPNORM_ALPHA * x1 + (routed + shared), g_ref[...], b_ref[...])


def _combine(dest, wts_t, x1, ys, wsg, wsu, wsd, ln_g, ln_b, tc):
    n = x1.shape[0]
    last = n // tc - 1
    dest_cur = pl.BlockSpec((TOP_K, tc), lambda i: (0, i), memory_space=pltpu.SMEM)
    dest_next = pl.BlockSpec((TOP_K, tc), lambda i: (0, jnp.minimum(i + 1, last)),
                             memory_space=pltpu.SMEM)
    full = lambda a: pl.BlockSpec(a.shape, lambda i: (0, 0))
    rows = lambda w: pl.BlockSpec((tc, w), lambda i: (i, 0))
    row_bytes = D_MODEL * 4
    need = tc * row_bytes * (2 * TOP_K + 4) + 14 * 1024 * 1024
    return pl.pallas_call(
        _combine_body,
        name="combine",
        grid=(n // tc,),
        in_specs=[dest_cur, dest_next, rows(LANES), rows(D_MODEL),
                  pl.BlockSpec(memory_space=pl.ANY),
                  full(wsg), full(wsu), full(wsd), full(ln_g), full(ln_b)],
        out_specs=rows(D_MODEL),
        out_shape=jax.ShapeDtypeStruct((n, D_MODEL), F32),
        scratch_shapes=[pltpu.VMEM((2, TOP_K, tc * ROW_TILE, LANES), F32),
                        pltpu.SemaphoreType.DMA((2,))],
        compiler_params=_params(("arbitrary",), max(VMEM_LIMIT, need)),
    )(dest, dest, wts_t, x1, ys, wsg, wsu, wsd, ln_g, ln_b)


def _segments(cnt_first, n_first, reserve):
    blk = EXPERT_BLOCK
    assert reserve <= blk
    n_blocks = -(-(n_first * TOP_K + N_EXPERTS * reserve) // blk) + N_EXPERTS
    counts = cnt_first[:, 0].astype(I32)
    padded = (counts + reserve + blk - 1) // blk * blk
    ends = jnp.cumsum(padded)
    pstart = ends - padded
    nb_used = (ends[-1] // blk).astype(I32).reshape(1)
    block_row0 = jnp.arange(n_blocks, dtype=I32) * blk
    block_e = jnp.minimum(jnp.sum((ends[None, :] <= block_row0[:, None]).astype(I32), axis=1),
                          N_EXPERTS - 1).astype(I32)
    return pstart, counts, block_e, nb_used, n_blocks * blk


def kernel(x_prompt, x_sample, cache_k, cache_v, state_conv, page_table, w_in, lam_q1, lam_k1,
           lam_q2, lam_k2, subln_g, w_dw, b_dw, conv_ln_g, conv_ln_b, w_o, ln1_g, ln1_b,
           w_router, b_router, w_gate, w_up, w_down, w_sh_gate, w_sh_up, w_sh_down, ln2_g,
           ln2_b):
    batch, seq, _ = x_prompt.shape
    dec_b, dec_s, _ = x_sample.shape
    n_p, n_s = batch * seq, dec_b * dec_s
    layer = 0
    lam_init = _lambda_init(layer)
    row = lambda a: a[layer].reshape(1, -1)

    w_in_b = w_in[layer].astype(BF16)
    wo = w_o[layer].astype(BF16)
    wo_a, wo_c = wo[:ATTN_WIDTH], wo[ATTN_WIDTH:]
    wr_t = w_router[layer].T
    wr_hi = wr_t.astype(BF16)
    wr_lo = (wr_t - wr_hi.astype(F32)).astype(BF16)
    br = b_router[layer].reshape(-1, 1)
    moe_w = (w_gate[layer], w_up[layer], w_down[layer]) + tuple(
        w[layer].astype(BF16) for w in (w_sh_gate, w_sh_up, w_sh_down))
    lam_vecs = (row(lam_q1), row(lam_k1), row(lam_q2), row(lam_k2))
    g_sub = row(subln_g)
    conv_w = (w_dw[layer], row(b_dw), row(conv_ln_g), row(conv_ln_b))
    ln1 = (row(ln1_g), row(ln1_b))
    ln2 = (row(ln2_g), row(ln2_b))

    xp = x_prompt.reshape(n_p, D_MODEL)
    wq_t = w_in_b[:, :QK_WIDTH].T
    wv_t = w_in_b[:, 2 * QK_WIDTH:2 * QK_WIDTH + ATTN_WIDTH].T
    qt_p, k_p, v_p, u_p, kb_p, vt_p = _inproj(xp, w_in_b, ATTN_TILE, (wq_t, wv_t))
    o_p = _attn_prompt(qt_p, kb_p, vt_p, lam_vecs, g_sub.reshape(V_DIM, 1), lam_init, batch, seq)
    c_p = _conv_prompt(u_p, *conv_w, batch, seq, 512)
    cnt0 = jnp.zeros((N_EXPERTS, LANES), F32)
    x1_p, xt_p, idx_p, wts_p, rank_p, cnt_p = _mix(o_p, c_p, xp, wo_a, wo_c, *ln1, wr_hi, wr_lo,
                                                   br, cnt0, 512)
    pstart, counts_p, block_e, nb_used, n_rows = _segments(cnt_p, n_p, n_s)
    pstart_col = pstart.astype(F32).reshape(N_EXPERTS, 1)
    dest_p = _dest(idx_p, rank_p, pstart_col, 256)

    xs = x_sample.reshape(n_s, D_MODEL)
    q_s, k_s, v_s, u_s = _inproj(xs, w_in_b, n_s)
    pool = cache_k.shape[1]
    page_rows = lambda a: a[layer].reshape(pool, PAGE_ROWS, V_DIM)
    new_rows = lambda a: a.reshape(dec_b, dec_s * N_HEADS, V_DIM)
    o_s, rows_in = _attn_sample(
        q_s.reshape(dec_b, dec_s, QK_WIDTH), new_rows(k_s), new_rows(v_s), page_rows(cache_k),
        page_rows(cache_v), page_table, lam_vecs, g_sub, lam_init,
        xt_p.reshape(n_p, PACK_TILE, LANES), dest_p, pstart, counts_p, n_rows, n_s)
    c_s, st_s = _conv_sample(u_s.reshape(dec_b, dec_s, CONV_CH), state_conv[layer], *conv_w)
    x1_s, xt_s, idx_s, wts_s, rank_s, _ = _mix(
        o_s.reshape(n_s, ATTN_WIDTH), c_s.reshape(n_s, CONV_CH), xs, wo_a, wo_c, *ln1, wr_hi,
        wr_lo, br, cnt_p, n_s)
    dest_s = _dest(idx_s, rank_s, pstart_col, 128)
    rows_in = _dispatch(xt_s.reshape(n_s, PACK_TILE, LANES), dest_s, rows_in, 128)

    wg, wu, wd, wsg, wsu, wsd = moe_w
    rows_out = _experts(rows_in.reshape(n_rows * PACK_TILE, LANES), block_e, nb_used, wg, wu, wd)
    y_p = _combine(dest_p, wts_p, x1_p, rows_out, wsg, wsu, wsd, *ln2, 512)
    y_s = _combine(dest_s, wts_s, x1_s, rows_out, wsg, wsu, wsd, *ln2, 128)

    keep = CONV_WIDTH - 1
    u_p3 = u_p.reshape(batch, seq, CONV_CH)
    return (y_p.reshape(batch, seq, D_MODEL),
            y_s.reshape(dec_b, dec_s, D_MODEL),
            k_p.reshape(1, batch, seq, N_HEADS, V_DIM),
            v_p.reshape(1, batch, seq, N_HEADS, V_DIM),
            u_p3[:, seq - keep:, :][None],
            k_s.reshape(1, dec_b, dec_s, N_HEADS, V_DIM),
            v_s.reshape(1, dec_b, dec_s, N_HEADS, V_DIM),
            st_s[None])
```

```python
import functools
import math

import jax
import jax.numpy as jnp
from jax import lax
from jax.experimental import pallas as pl
from jax.experimental.pallas import tpu as pltpu

F32 = jnp.float32
BF16 = jnp.bfloat16
I32 = jnp.int32
U32 = jnp.uint32

D_MODEL = 1024
N_HEADS = 4
HEAD_DIM = 64
V_DIM = 128
ATTN_WIDTH = N_HEADS * V_DIM
QK_WIDTH = N_HEADS * 2 * HEAD_DIM
ATTN_SCALE = HEAD_DIM ** -0.5
LOG2_E = math.log2(math.e)
CONV_CH = D_MODEL - ATTN_WIDTH
CONV_WIDTH = 31
N_EXPERTS = 64
N_GROUPS = 8
GROUP_SIZE = N_EXPERTS // N_GROUPS
TOPK_GROUPS = 4
TOP_K = 8
D_EXPERT = D_MODEL // 4
ROUTED_SCALE = 2.5
DEPTH = 1
DEEPNORM_ALPHA = (2 * DEPTH) ** 0.25
NORM_EPS = 1e-5
NEG_INF = -1e30
PAGE_SIZE = 128

LANES = 128
ROW_TILE = D_MODEL // LANES
PACK_TILE = ROW_TILE // 2
VMEM_LIMIT = 48 * 1024 * 1024
PAGES_PER_STEP = 32
ATTN_TILE = 512
PAGE_ROWS = PAGE_SIZE * N_HEADS
EXPERT_BLOCK = 1024
HIST = 32


def _lambda_init(layer):
    return 0.8 - 0.6 * math.exp(-0.3 * layer)


def _sigmoid(x):
    return 1.0 / (1.0 + jnp.exp(-x))


def _silu(x):
    return x * _sigmoid(x)


def _layer_norm(x, g, b):
    mu = jnp.mean(x, axis=-1, keepdims=True)
    xc = x - mu
    var = jnp.mean(xc * xc, axis=-1, keepdims=True)
    return xc * lax.rsqrt(var + NORM_EPS) * g + b


def _dot(a, b):
    return jnp.dot(a, b, preferred_element_type=F32)


def _dot_nt(a, b):
    return lax.dot_general(a, b, (((1,), (1,)), ((), ())), preferred_element_type=F32)


def _diff_lambda(lq1, lk1, lq2, lk2, lam_init):
    a = jnp.exp(jnp.sum(lq1 * lk1, axis=-1, keepdims=True))
    b = jnp.exp(jnp.sum(lq2 * lk2, axis=-1, keepdims=True))
    return a - b + lam_init


def _params(dims, vmem_limit=VMEM_LIMIT):
    return pltpu.CompilerParams(dimension_semantics=dims, vmem_limit_bytes=vmem_limit)


def _store_row_tiles(ref, x):
    m = x.shape[0]
    for j in range(ROW_TILE):
        ref[pl.ds(j, m, stride=ROW_TILE), :] = x[:, j * LANES:(j + 1) * LANES]


def _load_row_tiles(ref, m):
    return jnp.concatenate([ref[pl.ds(j, m, stride=ROW_TILE), :] for j in range(ROW_TILE)],
                           axis=1)


def _token_tile(ref, t):
    return ref.at[pl.ds(pl.multiple_of(t * ROW_TILE, ROW_TILE), ROW_TILE)]


def _store_packed_tiles(ref, x):
    m = x.shape[0]
    bits = pltpu.bitcast(x.astype(BF16).astype(F32), U32)
    half = D_MODEL // 2
    for j in range(PACK_TILE):
        lo = bits[:, j * LANES:(j + 1) * LANES] >> 16
        hi = bits[:, half + j * LANES:half + (j + 1) * LANES]
        ref[pl.ds(j, m, stride=PACK_TILE), :] = lo | hi


def _load_packed_tiles(ref, m):
    words = [ref[pl.ds(j, m, stride=PACK_TILE), :] for j in range(PACK_TILE)]
    lo = [pltpu.bitcast(w << 16, F32) for w in words]
    hi = [pltpu.bitcast(w & jnp.uint32(0xFFFF0000), F32) for w in words]
    return jnp.concatenate(lo + hi, axis=1).astype(BF16)


def _store_head_rows(ref, x):
    m = x.shape[0]
    for h in range(N_HEADS):
        ref[pl.ds(h, m, stride=N_HEADS), :] = x[:, h * V_DIM:(h + 1) * V_DIM]


def _inproj_body(x_ref, w_ref, *refs):
    xb = x_ref[...].astype(BF16)

    def mm(c0):
        return _dot(xb, w_ref[:, c0:c0 + QK_WIDTH])

    k = mm(QK_WIDTH)
    if len(refs) == 4:
        q_ref, k_ref, v_ref, u_ref = refs
        q_ref[...] = mm(0) * ATTN_SCALE
    else:
        wqt_ref, wvt_ref, q_ref, k_ref, v_ref, u_ref, kb_ref, vt_ref = refs
        q_t = _dot_nt(wqt_ref[...], xb) * (ATTN_SCALE * LOG2_E)
        v_t = _dot_nt(wvt_ref[...], xb).astype(BF16)
        for c in range(q_ref.shape[0]):
            cols = slice(c * ATTN_TILE, (c + 1) * ATTN_TILE)
            q_ref[c] = q_t[:, cols]
            vt_ref[c] = v_t[:, cols]
        kb_ref[...] = k.astype(BF16)
    _store_head_rows(k_ref, k)
    _store_head_rows(v_ref, mm(2 * QK_WIDTH))
    a = mm(2 * QK_WIDTH + ATTN_WIDTH)
    b = mm(2 * QK_WIDTH + ATTN_WIDTH + CONV_CH)
    u_ref[...] = a * _sigmoid(b)


def _inproj(x, w_in_b, tm, transposed_w=()):
    n = x.shape[0]
    wide = QK_WIDTH
    blk = pl.BlockSpec((tm, wide), lambda i: (i, 0))
    hblk = pl.BlockSpec((tm * N_HEADS, V_DIM), lambda i: (i, 0))
    full = lambda a: pl.BlockSpec(a.shape, lambda i: (0, 0))
    row_major = jax.ShapeDtypeStruct((n, wide), F32)
    head_rows = jax.ShapeDtypeStruct((n * N_HEADS, V_DIM), F32)
    out_specs = [blk, hblk, hblk, blk]
    out_shape = [row_major, head_rows, head_rows, row_major]
    if transposed_w:
        t_blk = pl.BlockSpec((tm // ATTN_TILE, wide, ATTN_TILE), lambda i: (i, 0, 0))
        t_shape = (n // ATTN_TILE, wide, ATTN_TILE)
        out_specs = [t_blk, hblk, hblk, blk, blk, t_blk]
        out_shape = [jax.ShapeDtypeStruct(t_shape, F32), head_rows, head_rows,
                     row_major, jax.ShapeDtypeStruct((n, wide), BF16),
                     jax.ShapeDtypeStruct(t_shape, BF16)]
    return pl.pallas_call(
        _inproj_body,
        name="inproj",
        grid=(n // tm,),
        in_specs=[pl.BlockSpec((tm, D_MODEL), lambda i: (i, 0)), full(w_in_b)]
        + [full(w) for w in transposed_w],
        out_specs=out_specs,
        out_shape=out_shape,
        compiler_params=_params(("parallel",)),
    )(x, w_in_b, *transposed_w)


def _attn_p_body(lam_init, q_ref, k_ref, v_ref, lq1, lk1, lq2, lk2, g_ref, o_ref, acc_ref):
    qi = pl.program_id(2)
    tq = q_ref.shape[1]
    tk = v_ref.shape[2]
    q_t = q_ref[...]
    feat = lax.broadcasted_iota(I32, q_t.shape, 0)
    q_both = jnp.concatenate([jnp.where(feat < HEAD_DIM, q_t, 0.0),
                              jnp.where(feat >= HEAD_DIM, q_t, 0.0)], axis=1).astype(BF16)
    acc_ref[...] = jnp.zeros(acc_ref.shape, F32)

    def chunk(masked, j, carry):
        m_old, l_old = carry
        kc = k_ref[pl.ds(pl.multiple_of(j * tk, tk), tk), :]
        s_t = _dot(kc, q_both)
        if masked:
            key = lax.broadcasted_iota(I32, s_t.shape, 0)
            qry = lax.broadcasted_iota(I32, s_t.shape, 1) % tq
            s_t = jnp.where(key <= qry, s_t, NEG_INF)
        m_new = jnp.maximum(m_old, jnp.max(s_t, axis=0, keepdims=True))
        alpha = jnp.exp2(m_old - m_new)
        p_t = jnp.exp2(s_t - m_new)
        l_new = alpha * l_old + jnp.sum(p_t, axis=0, keepdims=True)
        acc_ref[...] = alpha * acc_ref[...] + _dot(v_ref[j], p_t.astype(BF16))
        return m_new, l_new

    init = (jnp.full((1, 2 * tq), -jnp.inf, F32), jnp.zeros((1, 2 * tq), F32))
    carry = lax.fori_loop(0, qi, functools.partial(chunk, False), init)
    _, l_all = chunk(True, qi, carry)

    lam = _diff_lambda(lq1[...], lk1[...], lq2[...], lk2[...], lam_init)
    on = acc_ref[...] / l_all
    o_t = on[:, :tq] - lam * on[:, tq:]
    ms = jnp.mean(o_t * o_t, axis=0, keepdims=True)
    o_t = o_t * lax.rsqrt(ms + NORM_EPS) * g_ref[...] * (1.0 - lam_init)
    o_ref[...] = o_t.T


def _attn_prompt(q_t, kb, v_t, lam_vecs, subln_g_col, lam_init, batch, seq):
    t = q_t.shape[2]
    n = kb.shape[0]
    nq = seq // t
    small = lambda w: pl.BlockSpec((1, w), lambda b, h, i: (0, 0))
    return pl.pallas_call(
        functools.partial(_attn_p_body, lam_init),
        name="attn_prompt",
        grid=(batch, N_HEADS, nq),
        in_specs=[pl.BlockSpec((None, V_DIM, t), lambda b, h, i: (b * nq + i, h, 0)),
                  pl.BlockSpec((seq, V_DIM), lambda b, h, i: (b, h)),
                  pl.BlockSpec((nq, V_DIM, t), lambda b, h, i: (b, h, 0))]
        + [small(HEAD_DIM)] * 4 + [pl.BlockSpec((V_DIM, 1), lambda b, h, i: (0, 0))],
        out_specs=pl.BlockSpec((t, V_DIM), lambda b, h, i: (b * nq + i, h)),
        out_shape=jax.ShapeDtypeStruct((n, ATTN_WIDTH), F32),
        scratch_shapes=[pltpu.VMEM((V_DIM, 2 * t), F32)],
        compiler_params=_params(("parallel", "parallel", "parallel")),
    )(q_t, kb, v_t, *lam_vecs, subln_g_col)


def _attn_s_body(lam_init, n_steps, tokens_per_step, reserve, pt_ref, ps_ref, cnt_ref, q_ref,
                 kn_ref, vn_ref, *rest):
    del pt_ref
    npg = PAGES_PER_STEP
    k_refs = rest[:npg]
    v_refs = rest[npg:2 * npg]
    (lq1, lk1, lq2, lk2, g_ref, dest_ref, x_ref, o_ref, xs_ref, qall, knew, vnew, m_ref, l_ref,
     acc_ref, zbuf, sem, zsem) = rest[2 * npg:]
    j = pl.program_id(1)
    ds = q_ref.shape[0]
    hr = 2 * ds
    step = pl.program_id(0) * n_steps + j

    @pl.when(step == 0)
    def _():
        _zero_segment_tails(ps_ref, cnt_ref, xs_ref, zbuf, zsem, reserve)

    first_token = (step % (x_ref.shape[0] // tokens_per_step)) * tokens_per_step
    _start_row_copies(dest_ref, x_ref, xs_ref, sem, first_token, tokens_per_step)

    def head_rows(ref, h):
        return ref[pl.ds(h, PAGE_SIZE, stride=N_HEADS), :].astype(BF16)

    @pl.when(j == 0)
    def _():
        pieces = []
        for h in range(N_HEADS):
            qh = q_ref[:, h * V_DIM:(h + 1) * V_DIM]
            lane = lax.broadcasted_iota(I32, qh.shape, 1)
            pieces.append(jnp.where(lane < HEAD_DIM, qh, 0.0))
            pieces.append(jnp.where(lane >= HEAD_DIM, qh, 0.0))
        qall[...] = jnp.concatenate(pieces, axis=0).astype(BF16)
        knew[...] = jnp.zeros(knew.shape, F32)
        vnew[...] = jnp.zeros(vnew.shape, F32)
        knew[0:ds * N_HEADS] = kn_ref[...]
        vnew[0:ds * N_HEADS] = vn_ref[...]
        row = lax.broadcasted_iota(I32, (hr, PAGE_SIZE), 0)
        key = lax.broadcasted_iota(I32, (hr, PAGE_SIZE), 1)
        keep = key <= (row % ds)
        for h in range(N_HEADS):
            rows = slice(h * hr, (h + 1) * hr)
            s = jnp.where(keep, _dot_nt(qall[rows], head_rows(knew, h)), NEG_INF)
            m = jnp.max(s, axis=-1, keepdims=True)
            p = jnp.exp(s - m)
            m_ref[rows] = m
            l_ref[rows] = jnp.sum(p, axis=-1, keepdims=True)
            acc_ref[rows] = _dot(p.astype(BF16), head_rows(vnew, h))

    q_all, m_all, l_all, acc_all = qall[...], m_ref[...], l_ref[...], acc_ref[...]
    heads = range(N_HEADS)
    rows = [slice(h * hr, (h + 1) * hr) for h in heads]
    s = [_dot_nt(q_all[rows[h]],
                 jnp.concatenate([head_rows(k_refs[i], h) for i in range(npg)], axis=0))
         for h in heads]
    s = jnp.concatenate(s, axis=0)
    m_new = jnp.maximum(m_all, jnp.max(s, axis=-1, keepdims=True))
    alpha = jnp.exp(m_all - m_new)
    p = jnp.exp(s - m_new)
    l_new = alpha * l_all + jnp.sum(p, axis=-1, keepdims=True)
    pb = p.astype(BF16)
    pv = [_dot(pb[rows[h]],
               jnp.concatenate([head_rows(v_refs[i], h) for i in range(npg)], axis=0))
          for h in heads]
    acc_new = alpha * acc_all + jnp.concatenate(pv, axis=0)
    m_ref[...] = m_new
    l_ref[...] = l_new
    acc_ref[...] = acc_new
    _wait_row_copies(x_ref, xs_ref, sem, tokens_per_step)

    @pl.when(j == n_steps - 1)
    def _():
        lam = _diff_lambda(lq1[...], lk1[...], lq2[...], lk2[...], lam_init)
        on = acc_new / l_new
        for h in range(N_HEADS):
            r0 = h * hr
            o = on[r0:r0 + ds] - lam * on[r0 + ds:r0 + 2 * ds]
            ms = jnp.mean(o * o, axis=-1, keepdims=True)
            o_ref[:, h * V_DIM:(h + 1) * V_DIM] = (
                o * lax.rsqrt(ms + NORM_EPS) * g_ref[...] * (1.0 - lam_init))


def _attn_sample(q, k_new, v_new, cache_k, cache_v, page_table, lam_vecs, subln_g, lam_init,
                 x_tiles, dest, pstart, counts, n_rows, reserve):
    b, ds, _ = q.shape
    n_pages = page_table.shape[1]
    n_steps = n_pages // PAGES_PER_STEP
    nrow = N_HEADS * 2 * ds
    pt = page_table.reshape(-1)
    total_steps = b * n_steps
    n = x_tiles.shape[0]
    tokens_per_step = n // total_steps
    assert tokens_per_step * total_steps == n and LANES % tokens_per_step == 0
    per_block = LANES // tokens_per_step
    tile = (PACK_TILE, LANES)

    def page_map(i, bi, j, pt_ref, *_):
        return (pt_ref[bi * n_pages + j * PAGES_PER_STEP + i], 0, 0)

    page_specs = [pl.BlockSpec((None, PAGE_ROWS, V_DIM), functools.partial(page_map, i))
                  for i in range(PAGES_PER_STEP)]
    token_block = lambda bi, j, *_: (bi * n_steps + j) // per_block
    small = lambda w: pl.BlockSpec((1, w), lambda bi, j, *_: (0, 0))
    new_spec = pl.BlockSpec((None, ds * N_HEADS, V_DIM), lambda bi, j, *_: (bi, 0, 0))
    qo_spec = pl.BlockSpec((None, ds, ATTN_WIDTH), lambda bi, j, *_: (bi, 0, 0))
    grid_spec = pltpu.PrefetchScalarGridSpec(
        num_scalar_prefetch=3,
        grid=(b, n_steps),
        in_specs=[qo_spec, new_spec, new_spec] + page_specs + page_specs
        + [small(HEAD_DIM)] * 4 + [small(V_DIM)]
        + [pl.BlockSpec((TOP_K, LANES), lambda bi, j, *_: (0, token_block(bi, j)),
                        memory_space=pltpu.SMEM),
           pl.BlockSpec((LANES,) + tile, lambda bi, j, *_: (token_block(bi, j), 0, 0))],
        out_specs=[qo_spec, pl.BlockSpec(memory_space=pl.ANY)],
        scratch_shapes=[pltpu.VMEM((nrow, V_DIM), BF16),
                        pltpu.VMEM((PAGE_ROWS, V_DIM), F32),
                        pltpu.VMEM((PAGE_ROWS, V_DIM), F32),
                        pltpu.VMEM((nrow, 1), F32),
                        pltpu.VMEM((nrow, 1), F32),
                        pltpu.VMEM((nrow, V_DIM), F32),
                        pltpu.VMEM((TAIL_PIECES[0],) + tile, U32),
                        pltpu.SemaphoreType.DMA, pltpu.SemaphoreType.DMA],
    )
    return pl.pallas_call(
        functools.partial(_attn_s_body, lam_init, n_steps, tokens_per_step, reserve),
        name="attn_sample",
        grid_spec=grid_spec,
        out_shape=[jax.ShapeDtypeStruct((b, ds, ATTN_WIDTH), F32),
                   jax.ShapeDtypeStruct((n_rows,) + tile, U32)],
        compiler_params=_params(("arbitrary", "arbitrary")),
    )(pt, pstart, counts, q, k_new, v_new, *([cache_k] * PAGES_PER_STEP),
      *([cache_v] * PAGES_PER_STEP), *lam_vecs, subln_g, dest, x_tiles)


def _conv_taps(ext_ref, w_ref, bias, rows):
    off = HIST - (CONV_WIDTH - 1)
    acc = jnp.broadcast_to(bias, (rows, CONV_CH))
    for jt in range(CONV_WIDTH):
        acc = acc + w_ref[jt:jt + 1, :] * ext_ref[pl.ds(jt + off, rows), :]
    return acc


CONV_ROWS = 64
SUBLANES = 8


def _conv_p_body(u_ref, h_ref, w_ref, b_ref, g_ref, beta_ref, c_ref, ext_ref, sh_ref):
    t = pl.program_id(1)
    tm = u_ref.shape[0]
    hist = h_ref[...]
    ext_ref[0:HIST] = jnp.where(t == 0, jnp.zeros_like(hist), hist)
    ext_ref[HIST:HIST + tm] = u_ref[...]
    span = sh_ref.shape[1]
    for r in range(1, SUBLANES):
        sh_ref[r - 1] = ext_ref[pl.ds(r, span), :]
    off = HIST - (CONV_WIDTH - 1)
    bias = b_ref[...]
    for c0 in range(0, tm, CONV_ROWS):
        acc = jnp.broadcast_to(bias, (CONV_ROWS, CONV_CH))
        for jt in range(CONV_WIDTH):
            r = (jt + off) % SUBLANES
            base = c0 + jt + off - r
            src = ext_ref if r == 0 else sh_ref.at[r - 1]
            acc = acc + w_ref[jt:jt + 1, :] * src[pl.ds(base, CONV_ROWS), :]
        c_ref[c0:c0 + CONV_ROWS, :] = _silu(_layer_norm(acc, g_ref[...], beta_ref[...]))


def _conv_prompt(u, w_dw, b_dw, g, beta, batch, seq, tm):
    n = u.shape[0]
    nt = seq // tm
    per = tm // HIST
    cur = lambda b, t: (b * nt + t, 0)
    prev = lambda b, t: (jnp.maximum((b * nt + t) * per - 1, 0), 0)
    small = lambda r: pl.BlockSpec((r, CONV_CH), lambda b, t: (0, 0))
    return pl.pallas_call(
        _conv_p_body,
        name="conv_prompt",
        grid=(batch, nt),
        in_specs=[pl.BlockSpec((tm, CONV_CH), cur), pl.BlockSpec((HIST, CONV_CH), prev),
                  small(CONV_WIDTH), small(1), small(1), small(1)],
        out_specs=pl.BlockSpec((tm, CONV_CH), cur),
        out_shape=jax.ShapeDtypeStruct((n, CONV_CH), F32),
        scratch_shapes=[pltpu.VMEM((HIST + tm, CONV_CH), F32),
                        pltpu.VMEM((SUBLANES - 1, HIST + tm - SUBLANES, CONV_CH), F32)],
        compiler_params=_params(("parallel", "parallel")),
    )(u, u, w_dw, b_dw, g, beta)


def _conv_s_body(u_ref, st_ref, w_ref, b_ref, g_ref, beta_ref, c_ref, ns_ref, ext_ref):
    ds = u_ref.shape[0]
    keep = CONV_WIDTH - 1
    off = HIST - keep
    ext_ref[0:off] = jnp.zeros((off, CONV_CH), F32)
    ext_ref[off:HIST] = st_ref[...]
    ext_ref[HIST:HIST + ds] = u_ref[...]
    y = _conv_taps(ext_ref, w_ref, b_ref[...], ds)
    c_ref[...] = _silu(_layer_norm(y, g_ref[...], beta_ref[...]))
    ns_ref[...] = ext_ref[HIST + ds - keep:HIST + ds]


def _conv_sample(u, state, w_dw, b_dw, g, beta):
    b, ds, _ = u.shape
    keep = CONV_WIDTH - 1
    per_b = lambda r: pl.BlockSpec((None, r, CONV_CH), lambda i: (i, 0, 0))
    small = lambda r: pl.BlockSpec((r, CONV_CH), lambda i: (0, 0))
    return pl.pallas_call(
        _conv_s_body,
        name="conv_sample",
        grid=(b,),
        in_specs=[per_b(ds), per_b(keep), small(CONV_WIDTH), small(1), small(1), small(1)],
        out_specs=[per_b(ds), per_b(keep)],
        out_shape=[jax.ShapeDtypeStruct((b, ds, CONV_CH), F32),
                   jax.ShapeDtypeStruct((b, keep, CONV_CH), F32)],
        scratch_shapes=[pltpu.VMEM((HIST + ds, CONV_CH), F32)],
        compiler_params=_params(("parallel",)),
    )(u, state, w_dw, b_dw, g, beta)


def _first_index(hit, iota, limit, axis):
    return jnp.min(jnp.where(hit, iota, limit), axis=axis, keepdims=True)


def _route(x1, wr_hi, wr_lo, bias):
    tm = x1.shape[0]
    x_hi = x1.astype(BF16)
    x_lo = (x1 - x_hi.astype(F32)).astype(BF16)
    logits = _dot_nt(wr_hi, x_hi) + (_dot_nt(wr_lo, x_hi) + _dot_nt(wr_hi, x_lo))
    scores = _sigmoid(logits)
    sel = scores + bias
    sel3 = sel.reshape(N_GROUPS, GROUP_SIZE, tm)
    member = lax.broadcasted_iota(I32, sel3.shape, 1)
    m1 = jnp.max(sel3, axis=1, keepdims=True)
    i1 = _first_index(sel3 == m1, member, GROUP_SIZE, 1)
    m2 = jnp.max(jnp.where(member == i1, -jnp.inf, sel3), axis=1, keepdims=True)
    gs = jnp.broadcast_to(m1 + m2, sel3.shape).reshape(N_EXPERTS, tm)
    eiota = lax.broadcasted_iota(I32, (N_EXPERTS, tm), 0)
    giota = eiota // GROUP_SIZE
    gmask = jnp.zeros((N_EXPERTS, tm), jnp.bool_)
    for _ in range(TOPK_GROUPS):
        m = jnp.max(gs, axis=0, keepdims=True)
        gi = _first_index(gs == m, giota, N_GROUPS, 0)
        pick = giota == gi
        gmask = jnp.logical_or(gmask, pick)
        gs = jnp.where(pick, -jnp.inf, gs)
    selm = jnp.where(gmask, sel, NEG_INF)
    idx_rows, w_rows = [], []
    for _ in range(TOP_K):
        m = jnp.max(selm, axis=0, keepdims=True)
        ei = _first_index(selm == m, eiota, N_EXPERTS, 0)
        pick = eiota == ei
        idx_rows.append(ei)
        w_rows.append(jnp.sum(jnp.where(pick, scores, 0.0), axis=0, keepdims=True))
        selm = jnp.where(pick, -jnp.inf, selm)
    idx = jnp.concatenate(idx_rows, axis=0)
    w = jnp.concatenate(w_rows, axis=0)
    w = w / jnp.sum(w, axis=0, keepdims=True) * ROUTED_SCALE
    return idx, w


def _mix_body(o_ref, c_ref, x_ref, wo_a, wo_c, g_ref, b_ref, wrh_ref, wrl_ref, br_ref, cnt0_ref,
              x1_ref, xrt_ref, idx_ref, wts_ref, rank_ref, cnt_ref, base_ref):
    i = pl.program_id(0)
    tm = x_ref.shape[0]

    @pl.when(i == 0)
    def _():
        base_ref[...] = cnt0_ref[...]

    mix = _dot(o_ref[...].astype(BF16), wo_a[...]) + _dot(c_ref[...].astype(BF16), wo_c[...])
    x1 = _layer_norm(DEEPNORM_ALPHA * x_ref[...] + mix, g_ref[...], b_ref[...])
    x1_ref[...] = x1
    _store_packed_tiles(xrt_ref, x1)

    idx, w = _route(x1, wrh_ref[...], wrl_ref[...], br_ref[...])
    idx_ref[...] = idx
    w_rows = jnp.concatenate([w, jnp.zeros((LANES - TOP_K, tm), F32)], axis=0)
    wts_ref[...] = w_rows.T

    eiota = lax.broadcasted_iota(I32, (N_EXPERTS, tm), 0)
    chosen = jnp.zeros((N_EXPERTS, tm), F32)
    for k in range(TOP_K):
        chosen = chosen + jnp.where(eiota == idx[k:k + 1, :], 1.0, 0.0)
    r = lax.broadcasted_iota(I32, (tm, tm), 0)
    c = lax.broadcasted_iota(I32, (tm, tm), 1)
    before = jnp.where(r < c, 1.0, 0.0).astype(BF16)
    base = base_ref[:, 0:1]
    rank_full = _dot(chosen.astype(BF16), before) + base
    rows = [jnp.sum(jnp.where(eiota == idx[k:k + 1, :], rank_full, 0.0), axis=0, keepdims=True)
            for k in range(TOP_K)]
    rank_ref[...] = jnp.concatenate(rows, axis=0).astype(I32)
    total = base + jnp.sum(chosen, axis=1, keepdims=True)
    base_ref[...] = jnp.broadcast_to(total, base_ref.shape)
    cnt_ref[...] = jnp.broadcast_to(total, cnt_ref.shape)


def _mix(o, c, x, wo_a, wo_c, ln_g, ln_b, wr_hi, wr_lo, b_router, cnt0, tm):
    n = x.shape[0]
    rows = lambda w: pl.BlockSpec((tm, w), lambda i: (i, 0))
    full = lambda a: pl.BlockSpec(a.shape, lambda i: (0, 0))
    cols = pl.BlockSpec((TOP_K, tm), lambda i: (0, i))
    cnt = pl.BlockSpec((N_EXPERTS, LANES), lambda i: (0, 0))
    return pl.pallas_call(
        _mix_body,
        name="mix",
        grid=(n // tm,),
        in_specs=[rows(ATTN_WIDTH), rows(CONV_CH), rows(D_MODEL), full(wo_a), full(wo_c),
                  full(ln_g), full(ln_b), full(wr_hi), full(wr_lo), full(b_router), cnt],
        out_specs=[rows(D_MODEL), pl.BlockSpec((tm * PACK_TILE, LANES), lambda i: (i, 0)), cols,
                   rows(LANES), cols, cnt],
        out_shape=[jax.ShapeDtypeStruct((n, D_MODEL), F32),
                   jax.ShapeDtypeStruct((n * PACK_TILE, LANES), U32),
                   jax.ShapeDtypeStruct((TOP_K, n), I32),
                   jax.ShapeDtypeStruct((n, LANES), F32),
                   jax.ShapeDtypeStruct((TOP_K, n), I32),
                   jax.ShapeDtypeStruct((N_EXPERTS, LANES), F32)],
        scratch_shapes=[pltpu.VMEM((N_EXPERTS, LANES), F32)],
        compiler_params=_params(("arbitrary",)),
    )(o, c, x, wo_a, wo_c, ln_g, ln_b, wr_hi, wr_lo, b_router, cnt0)


def _dest_body(idx_ref, rank_ref, ps_ref, dest_ref):
    idx = idx_ref[...]
    tm = idx.shape[1]
    eiota = lax.broadcasted_iota(I32, (N_EXPERTS, tm), 0)
    ps = ps_ref[...]
    rows = [jnp.sum(jnp.where(eiota == idx[k:k + 1, :], ps, 0.0), axis=0, keepdims=True)
            for k in range(TOP_K)]
    dest_ref[...] = jnp.concatenate(rows, axis=0).astype(I32) + rank_ref[...]


def _dest(idx, rank, pstart_col, tm):
    n = idx.shape[1]
    cols = pl.BlockSpec((TOP_K, tm), lambda i: (0, i))
    return pl.pallas_call(
        _dest_body,
        name="dest",
        grid=(n // tm,),
        in_specs=[cols, cols, pl.BlockSpec((N_EXPERTS, 1), lambda i: (0, 0))],
        out_specs=cols,
        out_shape=jax.ShapeDtypeStruct((TOP_K, n), I32),
        compiler_params=_params(("parallel",)),
    )(idx, rank, pstart_col)


TAIL_PIECES = tuple(1 << s for s in reversed(range(EXPERT_BLOCK.bit_length())))
ROW_UNROLL = 8


def _zero_segment_tails(ps_ref, cnt_ref, xs_ref, zbuf, zsem, reserve):
    zbuf[...] = jnp.zeros(zbuf.shape, U32)

    def tail(wait, e, carry):
        cnt = cnt_ref[e]
        base = ps_ref[e] + cnt
        seg = (cnt + reserve + EXPERT_BLOCK - 1) // EXPERT_BLOCK * EXPERT_BLOCK
        pad = seg - cnt
        for p in TAIL_PIECES:
            @pl.when((pad & p) != 0)
            def _():
                cp = pltpu.make_async_copy(zbuf.at[pl.ds(0, p)], xs_ref.at[pl.ds(base, p)], zsem)
                if wait:
                    cp.wait()
                else:
                    cp.start()
            base = base + (pad & p)
        return carry

    lax.fori_loop(0, N_EXPERTS, functools.partial(tail, False), 0)
    lax.fori_loop(0, N_EXPERTS, functools.partial(tail, True), 0)


def _start_row_copies(dest_ref, x_ref, xs_ref, sem, first, count):
    def issue(g, carry):
        for u in range(ROW_UNROLL):
            t = first + g * ROW_UNROLL + u
            for k in range(TOP_K):
                pltpu.make_async_copy(x_ref.at[t], xs_ref.at[dest_ref[k, t]],
                                      sem).start(priority=k % 2)
        return carry

    lax.fori_loop(0, count // ROW_UNROLL, issue, 0)


def _wait_row_copies(x_ref, xs_ref, sem, count):
    for _ in range(TOP_K):
        pltpu.make_async_copy(x_ref.at[pl.ds(0, count)], xs_ref.at[pl.ds(0, count)], sem).wait()


def _dispatch_body(dest_ref, x_ref, xs_prev_ref, xs_ref, sem):
    del xs_prev_ref
    ts = x_ref.shape[0]
    _start_row_copies(dest_ref, x_ref, xs_ref, sem, 0, ts)
    _wait_row_copies(x_ref, xs_ref, sem, ts)


def _dispatch(x_tiles, dest, xs_prev, ts):
    n = x_tiles.shape[0]
    return pl.pallas_call(
        _dispatch_body,
        name="dispatch",
        grid=(n // ts,),
        in_specs=[pl.BlockSpec((TOP_K, ts), lambda i: (0, i), memory_space=pltpu.SMEM),
                  pl.BlockSpec((ts, PACK_TILE, LANES), lambda i: (i, 0, 0)),
                  pl.BlockSpec(memory_space=pl.ANY)],
        out_specs=pl.BlockSpec(memory_space=pl.ANY),
        out_shape=jax.ShapeDtypeStruct(xs_prev.shape, xs_prev.dtype),
        scratch_shapes=[pltpu.SemaphoreType.DMA],
        input_output_aliases={2: 0},
        compiler_params=_params(("arbitrary",)),
    )(dest, x_tiles, xs_prev)


def _expert_body(be_ref, nb_ref, xs_ref, wg_ref, wu_ref, wd_ref, ys_ref, wg_b, wu_b, wd_b):
    i = pl.program_id(0)

    @pl.when(jnp.logical_or(i == 0, be_ref[i] != be_ref[jnp.maximum(i - 1, 0)]))
    def _():
        wg_b[...] = wg_ref[...].astype(BF16)
        wu_b[...] = wu_ref[...].astype(BF16)
        wd_b[...] = wd_ref[...].astype(BF16)

    @pl.when(i < nb_ref[0])
    def _():
        x = _load_packed_tiles(xs_ref, EXPERT_BLOCK)
        h = (_silu(_dot(x, wg_b[...])) * _dot(x, wu_b[...])).astype(BF16)
        _store_row_tiles(ys_ref, _dot(h, wd_b[...]))


def _experts(xs, block_e, nb_used, wg, wu, wd):
    rows = xs.shape[0] // PACK_TILE
    n_blocks = rows // EXPERT_BLOCK
    used = lambda i, be, nb: (jnp.minimum(i, nb[0] - 1), 0)
    in_rows = pl.BlockSpec((EXPERT_BLOCK * PACK_TILE, LANES), used)
    grid_spec = pltpu.PrefetchScalarGridSpec(
        num_scalar_prefetch=2,
        grid=(n_blocks,),
        in_specs=[in_rows,
                  pl.BlockSpec((None, D_MODEL, D_EXPERT), lambda i, be, nb: (be[i], 0, 0)),
                  pl.BlockSpec((None, D_MODEL, D_EXPERT), lambda i, be, nb: (be[i], 0, 0)),
                  pl.BlockSpec((None, D_EXPERT, D_MODEL), lambda i, be, nb: (be[i], 0, 0))],
        out_specs=pl.BlockSpec((EXPERT_BLOCK * ROW_TILE, LANES), used),
        scratch_shapes=[pltpu.VMEM((D_MODEL, D_EXPERT), BF16), pltpu.VMEM((D_MODEL, D_EXPERT), BF16),
                        pltpu.VMEM((D_EXPERT, D_MODEL), BF16)],
    )
    return pl.pallas_call(
        _expert_body,
        name="experts",
        grid_spec=grid_spec,
        out_shape=jax.ShapeDtypeStruct((rows * ROW_TILE, LANES), F32),
        compiler_params=_params(("arbitrary",)),
    )(block_e, nb_used, xs, wg, wu, wd)


def _combine_body(dest_ref, dnext_ref, wts_ref, x1_ref, ys_ref, wsg, wsu, wsd, g_ref, b_ref,
                  y_ref, buf, sems):
    i = pl.program_id(0)
    n_tiles = pl.num_programs(0)
    tc = x1_ref.shape[0]
    slot = i % 2

    def issue(d_ref, s, g, carry):
        for u in range(ROW_UNROLL):
            t = g * ROW_UNROLL + u
            for k in range(TOP_K):
                pltpu.make_async_copy(_token_tile(ys_ref, d_ref[k, t]),
                                      _token_tile(buf.at[s, k], t),
                                      sems.at[s]).start(priority=k % 2)
        return carry

    n_groups = tc // ROW_UNROLL

    @pl.when(i == 0)
    def _():
        lax.fori_loop(0, n_groups, functools.partial(issue, dest_ref, slot), 0)

    @pl.when(i + 1 < n_tiles)
    def _():
        lax.fori_loop(0, n_groups, functools.partial(issue, dnext_ref, 1 - slot), 0)

    x1 = x1_ref[...]
    xb = x1.astype(BF16)
    hs = (_silu(_dot(xb, wsg[...])) * _dot(xb, wsu[...])).astype(BF16)
    shared = _dot(hs, wsd[...])

    for k in range(TOP_K):
        pltpu.make_async_copy(ys_ref.at[pl.ds(0, tc * ROW_TILE)], buf.at[slot, k],
                              sems.at[slot]).wait()

    w = wts_ref[...]
    routed = w[:, 0:1] * _load_row_tiles(buf.at[slot, 0], tc)
    for k in range(1, TOP_K):
        routed = routed + w[:, k:k + 1] * _load_row_tiles(buf.at[slot, k], tc)
    y_ref[...] = _layer_norm(DEEPNORM_ALPHA * x1 + (routed + shared), g_ref[...], b_ref[...])


def _combine(dest, wts_t, x1, ys, wsg, wsu, wsd, ln_g, ln_b, tc):
    n = x1.shape[0]
    last = n // tc - 1
    dest_cur = pl.BlockSpec((TOP_K, tc), lambda i: (0, i), memory_space=pltpu.SMEM)
    dest_next = pl.BlockSpec((TOP_K, tc), lambda i: (0, jnp.minimum(i + 1, last)),
                             memory_space=pltpu.SMEM)
    full = lambda a: pl.BlockSpec(a.shape, lambda i: (0, 0))
    rows = lambda w: pl.BlockSpec((tc, w), lambda i: (i, 0))
    row_bytes = D_MODEL * 4
    need = tc * row_bytes * (2 * TOP_K + 4) + 14 * 1024 * 1024
    return pl.pallas_call(
        _combine_body,
        name="combine",
        grid=(n // tc,),
        in_specs=[dest_cur, dest_next, rows(LANES), rows(D_MODEL),
                  pl.BlockSpec(memory_space=pl.ANY),
                  full(wsg), full(wsu), full(wsd), full(ln_g), full(ln_b)],
        out_specs=rows(D_MODEL),
        out_shape=jax.ShapeDtypeStruct((n, D_MODEL), F32),
        scratch_shapes=[pltpu.VMEM((2, TOP_K, tc * ROW_TILE, LANES), F32),
                        pltpu.SemaphoreType.DMA((2,))],
        compiler_params=_params(("arbitrary",), max(VMEM_LIMIT, need)),
    )(dest, dest, wts_t, x1, ys, wsg, wsu, wsd, ln_g, ln_b)


def _segments(cnt_first, n_first, reserve):
    blk = EXPERT_BLOCK
    assert reserve <= blk
    n_blocks = -(-(n_first * TOP_K + N_EXPERTS * reserve) // blk) + N_EXPERTS
    counts = cnt_first[:, 0].astype(I32)
    padded = (counts + reserve + blk - 1) // blk * blk
    ends = jnp.cumsum(padded)
    pstart = ends - padded
    nb_used = (ends[-1] // blk).astype(I32).reshape(1)
    block_row0 = jnp.arange(n_blocks, dtype=I32) * blk
    block_e = jnp.minimum(jnp.sum((ends[None, :] <= block_row0[:, None]).astype(I32), axis=1),
                          N_EXPERTS - 1).astype(I32)
    return pstart, counts, block_e, nb_used, n_blocks * blk


def kernel(x_prompt, x_sample, cache_k, cache_v, state_conv, page_table, w_in, lam_q1, lam_k1,
           lam_q2, lam_k2, subln_g, w_dw, b_dw, conv_ln_g, conv_ln_b, w_o, ln1_g, ln1_b,
           w_router, b_router, w_gate, w_up, w_down, w_sh_gate, w_sh_up, w_sh_down, ln2_g,
           ln2_b):
    batch, seq, _ = x_prompt.shape
    dec_b, dec_s, _ = x_sample.shape
    n_p, n_s = batch * seq, dec_b * dec_s
    layer = 0
    lam_init = _lambda_init(layer)
    row = lambda a: a[layer].reshape(1, -1)

    w_in_b = w_in[layer].astype(BF16)
    wo = w_o[layer].astype(BF16)
    wo_a, wo_c = wo[:ATTN_WIDTH], wo[ATTN_WIDTH:]
    wr_t = w_router[layer].T
    wr_hi = wr_t.astype(BF16)
    wr_lo = (wr_t - wr_hi.astype(F32)).astype(BF16)
    br = b_router[layer].reshape(-1, 1)
    moe_w = (w_gate[layer], w_up[layer], w_down[layer]) + tuple(
        w[layer].astype(BF16) for w in (w_sh_gate, w_sh_up, w_sh_down))
    lam_vecs = (row(lam_q1), row(lam_k1), row(lam_q2), row(lam_k2))
    g_sub = row(subln_g)
    conv_w = (w_dw[layer], row(b_dw), row(conv_ln_g), row(conv_ln_b))
    ln1 = (row(ln1_g), row(ln1_b))
    ln2 = (row(ln2_g), row(ln2_b))

    xp = x_prompt.reshape(n_p, D_MODEL)
    wq_t = w_in_b[:, :QK_WIDTH].T
    wv_t = w_in_b[:, 2 * QK_WIDTH:2 * QK_WIDTH + ATTN_WIDTH].T
    qt_p, k_p, v_p, u_p, kb_p, vt_p = _inproj(xp, w_in_b, ATTN_TILE, (wq_t, wv_t))
    o_p = _attn_prompt(qt_p, kb_p, vt_p, lam_vecs, g_sub.reshape(V_DIM, 1), lam_init, batch, seq)
    c_p = _conv_prompt(u_p, *conv_w, batch, seq, 512)
    cnt0 = jnp.zeros((N_EXPERTS, LANES), F32)
    x1_p, xt_p, idx_p, wts_p, rank_p, cnt_p = _mix(o_p, c_p, xp, wo_a, wo_c, *ln1, wr_hi, wr_lo,
                                                   br, cnt0, 512)
    pstart, counts_p, block_e, nb_used, n_rows = _segments(cnt_p, n_p, n_s)
    pstart_col = pstart.astype(F32).reshape(N_EXPERTS, 1)
    dest_p = _dest(idx_p, rank_p, pstart_col, 256)

    xs = x_sample.reshape(n_s, D_MODEL)
    q_s, k_s, v_s, u_s = _inproj(xs, w_in_b, n_s)
    pool = cache_k.shape[1]
    page_rows = lambda a: a[layer].reshape(pool, PAGE_ROWS, V_DIM)
    new_rows = lambda a: a.reshape(dec_b, dec_s * N_HEADS, V_DIM)
    o_s, rows_in = _attn_sample(
        q_s.reshape(dec_b, dec_s, QK_WIDTH), new_rows(k_s), new_rows(v_s), page_rows(cache_k),
        page_rows(cache_v), page_table, lam_vecs, g_sub, lam_init,
        xt_p.reshape(n_p, PACK_TILE, LANES), dest_p, pstart, counts_p, n_rows, n_s)
    c_s, st_s = _conv_sample(u_s.reshape(dec_b, dec_s, CONV_CH), state_conv[layer], *conv_w)
    x1_s, xt_s, idx_s, wts_s, rank_s, _ = _mix(
        o_s.reshape(n_s, ATTN_WIDTH), c_s.reshape(n_s, CONV_CH), xs, wo_a, wo_c, *ln1, wr_hi,
        wr_lo, br, cnt_p, n_s)
    dest_s = _dest(idx_s, rank_s, pstart_col, 128)
    rows_in = _dispatch(xt_s.reshape(n_s, PACK_TILE, LANES), dest_s, rows_in, 128)

    wg, wu, wd, wsg, wsu, wsd = moe_w
    rows_out = _experts(rows_in.reshape(n_rows * PACK_TILE, LANES), block_e, nb_used, wg, wu, wd)
    y_p = _combine(dest_p, wts_p, x1_p, rows_out, wsg, wsu, wsd, *ln2, 512)
    y_s = _combine(dest_s, wts_s, x1_s, rows_out, wsg, wsu, wsd, *ln2, 128)

    keep = CONV_WIDTH - 1
    u_p3 = u_p.reshape(batch, seq, CONV_CH)
    return (y_p.reshape(batch, seq, D_MODEL),
            y_s.reshape(dec_b, dec_s, D_MODEL),
            k_p.reshape(1, batch, seq, N_HEADS, V_DIM),
            v_p.reshape(1, batch, seq, N_HEADS, V_DIM),
            u_p3[:, seq - keep:, :][None],
            k_s.reshape(1, dec_b, dec_s, N_HEADS, V_DIM),
            v_s.reshape(1, dec_b, dec_s, N_HEADS, V_DIM),
            st_s[None])
```

```python
import functools
import math

import jax
import jax.numpy as jnp
from jax import lax
from jax.experimental import pallas as pl
from jax.experimental.pallas import tpu as pltpu

F32 = jnp.float32
BF16 = jnp.bfloat16
I32 = jnp.int32
U32 = jnp.uint32

D_MODEL = 1024
N_HEADS = 4
HEAD_DIM = 64
V_DIM = 128
ATTN_WIDTH = N_HEADS * V_DIM
QK_WIDTH = N_HEADS * 2 * HEAD_DIM
ATTN_SCALE = HEAD_DIM ** -0.5
LOG2_E = math.log2(math.e)
CONV_CH = D_MODEL - ATTN_WIDTH
CONV_WIDTH = 31
N_EXPERTS = 64
N_GROUPS = 8
GROUP_SIZE = N_EXPERTS // N_GROUPS
TOPK_GROUPS = 4
TOP_K = 8
D_EXPERT = D_MODEL // 4
ROUTED_SCALE = 2.5
DEPTH = 1
DEEPNORM_ALPHA = (2 * DEPTH) ** 0.25
NORM_EPS = 1e-5
NEG_INF = -1e30
PAGE_SIZE = 128

LANES = 128
ROW_TILE = D_MODEL // LANES
PACK_TILE = ROW_TILE // 2
VMEM_LIMIT = 48 * 1024 * 1024
PAGES_PER_STEP = 32
ATTN_TILE = 512
PAGE_ROWS = PAGE_SIZE * N_HEADS
EXPERT_BLOCK = 1024
HIST = 32


def _lambda_init(layer):
    return 0.8 - 0.6 * math.exp(-0.3 * layer)


def _sigmoid(x):
    return 1.0 / (1.0 + jnp.exp(-x))


def _silu(x):
    return x * _sigmoid(x)


def _layer_norm(x, g, b):
    mu = jnp.mean(x, axis=-1, keepdims=True)
    xc = x - mu
    var = jnp.mean(xc * xc, axis=-1, keepdims=True)
    return xc * lax.rsqrt(var + NORM_EPS) * g + b


def _dot(a, b):
    return jnp.dot(a, b, preferred_element_type=F32)


def _dot_nt(a, b):
    return lax.dot_general(a, b, (((1,), (1,)), ((), ())), preferred_element_type=F32)


def _diff_lambda(lq1, lk1, lq2, lk2, lam_init):
    a = jnp.exp(jnp.sum(lq1 * lk1, axis=-1, keepdims=True))
    b = jnp.exp(jnp.sum(lq2 * lk2, axis=-1, keepdims=True))
    return a - b + lam_init


def _params(dims, vmem_limit=VMEM_LIMIT):
    return pltpu.CompilerParams(dimension_semantics=dims, vmem_limit_bytes=vmem_limit)


def _store_row_tiles(ref, x):
    m = x.shape[0]
    for j in range(ROW_TILE):
        ref[pl.ds(j, m, stride=ROW_TILE), :] = x[:, j * LANES:(j + 1) * LANES]


def _load_row_tiles(ref, m):
    return jnp.concatenate([ref[pl.ds(j, m, stride=ROW_TILE), :] for j in range(ROW_TILE)],
                           axis=1)


def _token_tile(ref, t):
    return ref.at[pl.ds(pl.multiple_of(t * ROW_TILE, ROW_TILE), ROW_TILE)]


def _store_packed_tiles(ref, x):
    m = x.shape[0]
    bits = pltpu.bitcast(x.astype(BF16).astype(F32), U32)
    half = D_MODEL // 2
    for j in range(PACK_TILE):
        lo = bits[:, j * LANES:(j + 1) * LANES] >> 16
        hi = bits[:, half + j * LANES:half + (j + 1) * LANES]
        ref[pl.ds(j, m, stride=PACK_TILE), :] = lo | hi


def _load_packed_tiles(ref, m):
    words = [ref[pl.ds(j, m, stride=PACK_TILE), :] for j in range(PACK_TILE)]
    lo = [pltpu.bitcast(w << 16, F32) for w in words]
    hi = [pltpu.bitcast(w & jnp.uint32(0xFFFF0000), F32) for w in words]
    return jnp.concatenate(lo + hi, axis=1).astype(BF16)


def _store_head_rows(ref, x):
    m = x.shape[0]
    for h in range(N_HEADS):
        ref[pl.ds(h, m, stride=N_HEADS), :] = x[:, h * V_DIM:(h + 1) * V_DIM]


def _inproj_body(x_ref, w_ref, *refs):
    xb = x_ref[...].astype(BF16)

    def mm(c0):
        return _dot(xb, w_ref[:, c0:c0 + QK_WIDTH])

    k = mm(QK_WIDTH)
    if len(refs) == 4:
        q_ref, k_ref, v_ref, u_ref = refs
        q_ref[...] = mm(0) * ATTN_SCALE
    else:
        wqt_ref, wvt_ref, q_ref, k_ref, v_ref, u_ref, kb_ref, vt_ref = refs
        q_t = _dot_nt(wqt_ref[...], xb) * (ATTN_SCALE * LOG2_E)
        v_t = _dot_nt(wvt_ref[...], xb).astype(BF16)
        for c in range(q_ref.shape[0]):
            cols = slice(c * ATTN_TILE, (c + 1) * ATTN_TILE)
            q_ref[c] = q_t[:, cols]
            vt_ref[c] = v_t[:, cols]
        kb_ref[...] = k.astype(BF16)
    _store_head_rows(k_ref, k)
    _store_head_rows(v_ref, mm(2 * QK_WIDTH))
    a = mm(2 * QK_WIDTH + ATTN_WIDTH)
    b = mm(2 * QK_WIDTH + ATTN_WIDTH + CONV_CH)
    u_ref[...] = a * _sigmoid(b)


def _inproj(x, w_in_b, tm, transposed_w=()):
    n = x.shape[0]
    wide = QK_WIDTH
    blk = pl.BlockSpec((tm, wide), lambda i: (i, 0))
    hblk = pl.BlockSpec((tm * N_HEADS, V_DIM), lambda i: (i, 0))
    full = lambda a: pl.BlockSpec(a.shape, lambda i: (0, 0))
    row_major = jax.ShapeDtypeStruct((n, wide), F32)
    head_rows = jax.ShapeDtypeStruct((n * N_HEADS, V_DIM), F32)
    out_specs = [blk, hblk, hblk, blk]
    out_shape = [row_major, head_rows, head_rows, row_major]
    if transposed_w:
        t_blk = pl.BlockSpec((tm // ATTN_TILE, wide, ATTN_TILE), lambda i: (i, 0, 0))
        t_shape = (n // ATTN_TILE, wide, ATTN_TILE)
        out_specs = [t_blk, hblk, hblk, blk, blk, t_blk]
        out_shape = [jax.ShapeDtypeStruct(t_shape, F32), head_rows, head_rows,
                     row_major, jax.ShapeDtypeStruct((n, wide), BF16),
                     jax.ShapeDtypeStruct(t_shape, BF16)]
    return pl.pallas_call(
        _inproj_body,
        name="inproj",
        grid=(n // tm,),
        in_specs=[pl.BlockSpec((tm, D_MODEL), lambda i: (i, 0)), full(w_in_b)]
        + [full(w) for w in transposed_w],
        out_specs=out_specs,
        out_shape=out_shape,
        compiler_params=_params(("parallel",)),
    )(x, w_in_b, *transposed_w)


def _attn_p_body(lam_init, q_ref, k_ref, v_ref, lq1, lk1, lq2, lk2, g_ref, o_ref, acc_ref):
    qi = pl.program_id(2)
    tq = q_ref.shape[1]
    tk = v_ref.shape[2]
    q_t = q_ref[...]
    feat = lax.broadcasted_iota(I32, q_t.shape, 0)
    q_both = jnp.concatenate([jnp.where(feat < HEAD_DIM, q_t, 0.0),
                              jnp.where(feat >= HEAD_DIM, q_t, 0.0)], axis=1).astype(BF16)
    acc_ref[...] = jnp.zeros(acc_ref.shape, F32)

    def chunk(masked, j, carry):
        m_old, l_old = carry
        kc = k_ref[pl.ds(pl.multiple_of(j * tk, tk), tk), :]
        s_t = _dot(kc, q_both)
        if masked:
            key = lax.broadcasted_iota(I32, s_t.shape, 0)
            qry = lax.broadcasted_iota(I32, s_t.shape, 1) % tq
            s_t = jnp.where(key <= qry, s_t, NEG_INF)
        m_new = jnp.maximum(m_old, jnp.max(s_t, axis=0, keepdims=True))
        alpha = jnp.exp2(m_old - m_new)
        p_t = jnp.exp2(s_t - m_new)
        l_new = alpha * l_old + jnp.sum(p_t, axis=0, keepdims=True)
        acc_ref[...] = alpha * acc_ref[...] + _dot(v_ref[j], p_t.astype(BF16))
        return m_new, l_new

    init = (jnp.full((1, 2 * tq), -jnp.inf, F32), jnp.zeros((1, 2 * tq), F32))
    carry = lax.fori_loop(0, qi, functools.partial(chunk, False), init)
    _, l_all = chunk(True, qi, carry)

    lam = _diff_lambda(lq1[...], lk1[...], lq2[...], lk2[...], lam_init)
    on = acc_ref[...] / l_all
    o_t = on[:, :tq] - lam * on[:, tq:]
    ms = jnp.mean(o_t * o_t, axis=0, keepdims=True)
    o_t = o_t * lax.rsqrt(ms + NORM_EPS) * g_ref[...] * (1.0 - lam_init)
    o_ref[...] = o_t.T


def _attn_prompt(q_t, kb, v_t, lam_vecs, subln_g_col, lam_init, batch, seq):
    t = q_t.shape[2]
    n = kb.shape[0]
    nq = seq // t
    small = lambda w: pl.BlockSpec((1, w), lambda b, h, i: (0, 0))
    return pl.pallas_call(
        functools.partial(_attn_p_body, lam_init),
        name="attn_prompt",
        grid=(batch, N_HEADS, nq),
        in_specs=[pl.BlockSpec((None, V_DIM, t), lambda b, h, i: (b * nq + i, h, 0)),
                  pl.BlockSpec((seq, V_DIM), lambda b, h, i: (b, h)),
                  pl.BlockSpec((nq, V_DIM, t), lambda b, h, i: (b, h, 0))]
        + [small(HEAD_DIM)] * 4 + [pl.BlockSpec((V_DIM, 1), lambda b, h, i: (0, 0))],
        out_specs=pl.BlockSpec((t, V_DIM), lambda b, h, i: (b * nq + i, h)),
        out_shape=jax.ShapeDtypeStruct((n, ATTN_WIDTH), F32),
        scratch_shapes=[pltpu.VMEM((V_DIM, 2 * t), F32)],
        compiler_params=_params(("parallel", "parallel", "parallel")),
    )(q_t, kb, v_t, *lam_vecs, subln_g_col)


def _attn_s_body(lam_init, n_steps, tokens_per_step, reserve, pt_ref, ps_ref, cnt_ref, q_ref,
                 kn_ref, vn_ref, *rest):
    del pt_ref
    npg = PAGES_PER_STEP
    k_refs = rest[:npg]
    v_refs = rest[npg:2 * npg]
    (lq1, lk1, lq2, lk2, g_ref, dest_ref, x_ref, o_ref, xs_ref, qall, knew, vnew, m_ref, l_ref,
     acc_ref, zbuf, sem, zsem) = rest[2 * npg:]
    j = pl.program_id(1)
    ds = q_ref.shape[0]
    hr = 2 * ds
    step = pl.program_id(0) * n_steps + j

    @pl.when(step == 0)
    def _():
        _zero_segment_tails(ps_ref, cnt_ref, xs_ref, zbuf, zsem, reserve)

    first_token = (step % (x_ref.shape[0] // tokens_per_step)) * tokens_per_step
    _start_row_copies(dest_ref, x_ref, xs_ref, sem, first_token, tokens_per_step)

    def head_rows(ref, h):
        return ref[pl.ds(h, PAGE_SIZE, stride=N_HEADS), :].astype(BF16)

    @pl.when(j == 0)
    def _():
        pieces = []
        for h in range(N_HEADS):
            qh = q_ref[:, h * V_DIM:(h + 1) * V_DIM]
            lane = lax.broadcasted_iota(I32, qh.shape, 1)
            pieces.append(jnp.where(lane < HEAD_DIM, qh, 0.0))
            pieces.append(jnp.where(lane >= HEAD_DIM, qh, 0.0))
        qall[...] = jnp.concatenate(pieces, axis=0).astype(BF16)
        knew[...] = jnp.zeros(knew.shape, F32)
        vnew[...] = jnp.zeros(vnew.shape, F32)
        knew[0:ds * N_HEADS] = kn_ref[...]
        vnew[0:ds * N_HEADS] = vn_ref[...]
        row = lax.broadcasted_iota(I32, (hr, PAGE_SIZE), 0)
        key = lax.broadcasted_iota(I32, (hr, PAGE_SIZE), 1)
        keep = key <= (row % ds)
        for h in range(N_HEADS):
            rows = slice(h * hr, (h + 1) * hr)
            s = jnp.where(keep, _dot_nt(qall[rows], head_rows(knew, h)), NEG_INF)
            m = jnp.max(s, axis=-1, keepdims=True)
            p = jnp.exp(s - m)
            m_ref[rows] = m
            l_ref[rows] = jnp.sum(p, axis=-1, keepdims=True)
            acc_ref[rows] = _dot(p.astype(BF16), head_rows(vnew, h))

    q_all, m_all, l_all, acc_all = qall[...], m_ref[...], l_ref[...], acc_ref[...]
    heads = range(N_HEADS)
    rows = [slice(h * hr, (h + 1) * hr) for h in heads]
    s = [_dot_nt(q_all[rows[h]],
                 jnp.concatenate([head_rows(k_refs[i], h) for i in range(npg)], axis=0))
         for h in heads]
    s = jnp.concatenate(s, axis=0)
    m_new = jnp.maximum(m_all, jnp.max(s, axis=-1, keepdims=True))
    alpha = jnp.exp(m_all - m_new)
    p = jnp.exp(s - m_new)
    l_new = alpha * l_all + jnp.sum(p, axis=-1, keepdims=True)
    pb = p.astype(BF16)
    pv = [_dot(pb[rows[h]],
               jnp.concatenate([head_rows(v_refs[i], h) for i in range(npg)], axis=0))
          for h in heads]
    acc_new = alpha * acc_all + jnp.concatenate(pv, axis=0)
    m_ref[...] = m_new
    l_ref[...] = l_new
    acc_ref[...] = acc_new
    _wait_row_copies(x_ref, xs_ref, sem, tokens_per_step)

    @pl.when(j == n_steps - 1)
    def _():
        lam = _diff_lambda(lq1[...], lk1[...], lq2[...], lk2[...], lam_init)
        on = acc_new / l_new
        for h in range(N_HEADS):
            r0 = h * hr
            o = on[r0:r0 + ds] - lam * on[r0 + ds:r0 + 2 * ds]
            ms = jnp.mean(o * o, axis=-1, keepdims=True)
            o_ref[:, h * V_DIM:(h + 1) * V_DIM] = (
                o * lax.rsqrt(ms + NORM_EPS) * g_ref[...] * (1.0 - lam_init))


def _attn_sample(q, k_new, v_new, cache_k, cache_v, page_table, lam_vecs, subln_g, lam_init,
                 x_tiles, dest, pstart, counts, n_rows, reserve):
    b, ds, _ = q.shape
    n_pages = page_table.shape[1]
    n_steps = n_pages // PAGES_PER_STEP
    nrow = N_HEADS * 2 * ds
    pt = page_table.reshape(-1)
    total_steps = b * n_steps
    n = x_tiles.shape[0]
    tokens_per_step = n // total_steps
    assert tokens_per_step * total_steps == n and LANES % tokens_per_step == 0
    per_block = LANES // tokens_per_step
    tile = (PACK_TILE, LANES)

    def page_map(i, bi, j, pt_ref, *_):
        return (pt_ref[bi * n_pages + j * PAGES_PER_STEP + i], 0, 0)

    page_specs = [pl.BlockSpec((None, PAGE_ROWS, V_DIM), functools.partial(page_map, i))
                  for i in range(PAGES_PER_STEP)]
    token_block = lambda bi, j, *_: (bi * n_steps + j) // per_block
    small = lambda w: pl.BlockSpec((1, w), lambda bi, j, *_: (0, 0))
    new_spec = pl.BlockSpec((None, ds * N_HEADS, V_DIM), lambda bi, j, *_: (bi, 0, 0))
    qo_spec = pl.BlockSpec((None, ds, ATTN_WIDTH), lambda bi, j, *_: (bi, 0, 0))
    grid_spec = pltpu.PrefetchScalarGridSpec(
        num_scalar_prefetch=3,
        grid=(b, n_steps),
        in_specs=[qo_spec, new_spec, new_spec] + page_specs + page_specs
        + [small(HEAD_DIM)] * 4 + [small(V_DIM)]
        + [pl.BlockSpec((TOP_K, LANES), lambda bi, j, *_: (0, token_block(bi, j)),
                        memory_space=pltpu.SMEM),
           pl.BlockSpec((LANES,) + tile, lambda bi, j, *_: (token_block(bi, j), 0, 0))],
        out_specs=[qo_spec, pl.BlockSpec(memory_space=pl.ANY)],
        scratch_shapes=[pltpu.VMEM((nrow, V_DIM), BF16),
                        pltpu.VMEM((PAGE_ROWS, V_DIM), F32),
                        pltpu.VMEM((PAGE_ROWS, V_DIM), F32),
                        pltpu.VMEM((nrow, 1), F32),
                        pltpu.VMEM((nrow, 1), F32),
                        pltpu.VMEM((nrow, V_DIM), F32),
                        pltpu.VMEM((TAIL_PIECES[0],) + tile, U32),
                        pltpu.SemaphoreType.DMA, pltpu.SemaphoreType.DMA],
    )
    return pl.pallas_call(
        functools.partial(_attn_s_body, lam_init, n_steps, tokens_per_step, reserve),
        name="attn_sample",
        grid_spec=grid_spec,
        out_shape=[jax.ShapeDtypeStruct((b, ds, ATTN_WIDTH), F32),
                   jax.ShapeDtypeStruct((n_rows,) + tile, U32)],
        compiler_params=_params(("arbitrary", "arbitrary")),
    )(pt, pstart, counts, q, k_new, v_new, *([cache_k] * PAGES_PER_STEP),
      *([cache_v] * PAGES_PER_STEP), *lam_vecs, subln_g, dest, x_tiles)


def _conv_taps(ext_ref, w_ref, bias, rows):
    off = HIST - (CONV_WIDTH - 1)
    acc = jnp.broadcast_to(bias, (rows, CONV_CH))
    for jt in range(CONV_WIDTH):
        acc = acc + w_ref[jt:jt + 1, :] * ext_ref[pl.ds(jt + off, rows), :]
    return acc


CONV_ROWS = 64
SUBLANES = 8


def _conv_p_body(u_ref, h_ref, w_ref, b_ref, g_ref, beta_ref, c_ref, ext_ref, sh_ref):
    t = pl.program_id(1)
    tm = u_ref.shape[0]
    hist = h_ref[...]
    ext_ref[0:HIST] = jnp.where(t == 0, jnp.zeros_like(hist), hist)
    ext_ref[HIST:HIST + tm] = u_ref[...]
    span = sh_ref.shape[1]
    for r in range(1, SUBLANES):
        sh_ref[r - 1] = ext_ref[pl.ds(r, span), :]
    off = HIST - (CONV_WIDTH - 1)
    bias = b_ref[...]
    for c0 in range(0, tm, CONV_ROWS):
        acc = jnp.broadcast_to(bias, (CONV_ROWS, CONV_CH))
        for jt in range(CONV_WIDTH):
            r = (jt + off) % SUBLANES
            base = c0 + jt + off - r
            src = ext_ref if r == 0 else sh_ref.at[r - 1]
            acc = acc + w_ref[jt:jt + 1, :] * src[pl.ds(base, CONV_ROWS), :]
        c_ref[c0:c0 + CONV_ROWS, :] = _silu(_layer_norm(acc, g_ref[...], beta_ref[...]))


def _conv_prompt(u, w_dw, b_dw, g, beta, batch, seq, tm):
    n = u.shape[0]
    nt = seq // tm
    per = tm // HIST
    cur = lambda b, t: (b * nt + t, 0)
    prev = lambda b, t: (jnp.maximum((b * nt + t) * per - 1, 0), 0)
    small = lambda r: pl.BlockSpec((r, CONV_CH), lambda b, t: (0, 0))
    return pl.pallas_call(
        _conv_p_body,
        name="conv_prompt",
        grid=(batch, nt),
        in_specs=[pl.BlockSpec((tm, CONV_CH), cur), pl.BlockSpec((HIST, CONV_CH), prev),
                  small(CONV_WIDTH), small(1), small(1), small(1)],
        out_specs=pl.BlockSpec((tm, CONV_CH), cur),
        out_shape=jax.ShapeDtypeStruct((n, CONV_CH), F32),
        scratch_shapes=[pltpu.VMEM((HIST + tm, CONV_CH), F32),
                        pltpu.VMEM((SUBLANES - 1, HIST + tm - SUBLANES, CONV_CH), F32)],
        compiler_params=_params(("parallel", "parallel")),
    )(u, u, w_dw, b_dw, g, beta)


def _conv_s_body(u_ref, st_ref, w_ref, b_ref, g_ref, beta_ref, c_ref, ns_ref, ext_ref):
    ds = u_ref.shape[0]
    keep = CONV_WIDTH - 1
    off = HIST - keep
    ext_ref[0:off] = jnp.zeros((off, CONV_CH), F32)
    ext_ref[off:HIST] = st_ref[...]
    ext_ref[HIST:HIST + ds] = u_ref[...]
    y = _conv_taps(ext_ref, w_ref, b_ref[...], ds)
    c_ref[...] = _silu(_layer_norm(y, g_ref[...], beta_ref[...]))
    ns_ref[...] = ext_ref[HIST + ds - keep:HIST + ds]


def _conv_sample(u, state, w_dw, b_dw, g, beta):
    b, ds, _ = u.shape
    keep = CONV_WIDTH - 1
    per_b = lambda r: pl.BlockSpec((None, r, CONV_CH), lambda i: (i, 0, 0))
    small = lambda r: pl.BlockSpec((r, CONV_CH), lambda i: (0, 0))
    return pl.pallas_call(
        _conv_s_body,
        name="conv_sample",
        grid=(b,),
        in_specs=[per_b(ds), per_b(keep), small(CONV_WIDTH), small(1), small(1), small(1)],
        out_specs=[per_b(ds), per_b(keep)],
        out_shape=[jax.ShapeDtypeStruct((b, ds, CONV_CH), F32),
                   jax.ShapeDtypeStruct((b, keep, CONV_CH), F32)],
        scratch_shapes=[pltpu.VMEM((HIST + ds, CONV_CH), F32)],
        compiler_params=_params(("parallel",)),
    )(u, state, w_dw, b_dw, g, beta)


def _first_index(hit, iota, limit, axis):
    return jnp.min(jnp.where(hit, iota, limit), axis=axis, keepdims=True)


def _route(x1, wr_hi, wr_lo, bias):
    tm = x1.shape[0]
    x_hi = x1.astype(BF16)
    x_lo = (x1 - x_hi.astype(F32)).astype(BF16)
    both = _dot_nt(jnp.concatenate([wr_hi, wr_lo], axis=0), x_hi)
    logits = both[:N_EXPERTS] + (both[N_EXPERTS:] + _dot_nt(wr_hi, x_lo))
    scores = _sigmoid(logits)
    sel = scores + bias
    sel3 = sel.reshape(N_GROUPS, GROUP_SIZE, tm)
    member = lax.broadcasted_iota(I32, sel3.shape, 1)
    m1 = jnp.max(sel3, axis=1, keepdims=True)
    i1 = _first_index(sel3 == m1, member, GROUP_SIZE, 1)
    m2 = jnp.max(jnp.where(member == i1, -jnp.inf, sel3), axis=1, keepdims=True)
    gs = jnp.broadcast_to(m1 + m2, sel3.shape).reshape(N_EXPERTS, tm)
    eiota = lax.broadcasted_iota(I32, (N_EXPERTS, tm), 0)
    giota = eiota // GROUP_SIZE
    gmask = jnp.zeros((N_EXPERTS, tm), jnp.bool_)
    for _ in range(TOPK_GROUPS):
        m = jnp.max(gs, axis=0, keepdims=True)
        gi = _first_index(gs == m, giota, N_GROUPS, 0)
        pick = giota == gi
        gmask = jnp.logical_or(gmask, pick)
        gs = jnp.where(pick, -jnp.inf, gs)
    selm = jnp.where(gmask, sel, NEG_INF)
    idx_rows, w_rows = [], []
    for _ in range(TOP_K):
        m = jnp.max(selm, axis=0, keepdims=True)
        ei = _first_index(selm == m, eiota, N_EXPERTS, 0)
        pick = eiota == ei
        idx_rows.append(ei)
        w_rows.append(jnp.sum(jnp.where(pick, scores, 0.0), axis=0, keepdims=True))
        selm = jnp.where(pick, -jnp.inf, selm)
    idx = jnp.concatenate(idx_rows, axis=0)
    w = jnp.concatenate(w_rows, axis=0)
    w = w / jnp.sum(w, axis=0, keepdims=True) * ROUTED_SCALE
    return idx, w


def _mix_body(o_ref, c_ref, x_ref, wo_a, wo_c, g_ref, b_ref, wrh_ref, wrl_ref, br_ref, cnt0_ref,
              x1_ref, xrt_ref, idx_ref, wts_ref, rank_ref, cnt_ref, base_ref):
    i = pl.program_id(0)
    tm = x_ref.shape[0]

    @pl.when(i == 0)
    def _():
        base_ref[...] = cnt0_ref[...]

    mix = _dot(o_ref[...].astype(BF16), wo_a[...]) + _dot(c_ref[...].astype(BF16), wo_c[...])
    x1 = _layer_norm(DEEPNORM_ALPHA * x_ref[...] + mix, g_ref[...], b_ref[...])
    x1_ref[...] = x1
    _store_packed_tiles(xrt_ref, x1)

    idx, w = _route(x1, wrh_ref[...], wrl_ref[...], br_ref[...])
    idx_ref[...] = idx
    w_rows = jnp.concatenate([w, jnp.zeros((LANES - TOP_K, tm), F32)], axis=0)
    wts_ref[...] = w_rows.T

    eiota = lax.broadcasted_iota(I32, (N_EXPERTS, tm), 0)
    chosen = jnp.zeros((N_EXPERTS, tm), F32)
    for k in range(TOP_K):
        chosen = chosen + jnp.where(eiota == idx[k:k + 1, :], 1.0, 0.0)
    r = lax.broadcasted_iota(I32, (tm, tm), 0)
    c = lax.broadcasted_iota(I32, (tm, tm), 1)
    before = jnp.where(r < c, 1.0, 0.0).astype(BF16)
    base = base_ref[:, 0:1]
    rank_full = _dot(chosen.astype(BF16), before) + base
    rows = [jnp.sum(jnp.where(eiota == idx[k:k + 1, :], rank_full, 0.0), axis=0, keepdims=True)
            for k in range(TOP_K)]
    rank_ref[...] = jnp.concatenate(rows, axis=0).astype(I32)
    total = base + jnp.sum(chosen, axis=1, keepdims=True)
    base_ref[...] = jnp.broadcast_to(total, base_ref.shape)
    cnt_ref[...] = jnp.broadcast_to(total, cnt_ref.shape)


def _mix(o, c, x, wo_a, wo_c, ln_g, ln_b, wr_hi, wr_lo, b_router, cnt0, tm):
    n = x.shape[0]
    rows = lambda w: pl.BlockSpec((tm, w), lambda i: (i, 0))
    full = lambda a: pl.BlockSpec(a.shape, lambda i: (0, 0))
    cols = pl.BlockSpec((TOP_K, tm), lambda i: (0, i))
    cnt = pl.BlockSpec((N_EXPERTS, LANES), lambda i: (0, 0))
    return pl.pallas_call(
        _mix_body,
        name="mix",
        grid=(n // tm,),
        in_specs=[rows(ATTN_WIDTH), rows(CONV_CH), rows(D_MODEL), full(wo_a), full(wo_c),
                  full(ln_g), full(ln_b), full(wr_hi), full(wr_lo), full(b_router), cnt],
        out_specs=[rows(D_MODEL), pl.BlockSpec((tm * PACK_TILE, LANES), lambda i: (i, 0)), cols,
                   rows(LANES), cols, cnt],
        out_shape=[jax.ShapeDtypeStruct((n, D_MODEL), F32),
                   jax.ShapeDtypeStruct((n * PACK_TILE, LANES), U32),
                   jax.ShapeDtypeStruct((TOP_K, n), I32),
                   jax.ShapeDtypeStruct((n, LANES), F32),
                   jax.ShapeDtypeStruct((TOP_K, n), I32),
                   jax.ShapeDtypeStruct((N_EXPERTS, LANES), F32)],
        scratch_shapes=[pltpu.VMEM((N_EXPERTS, LANES), F32)],
        compiler_params=_params(("arbitrary",)),
    )(o, c, x, wo_a, wo_c, ln_g, ln_b, wr_hi, wr_lo, b_router, cnt0)


def _dest_body(idx_ref, rank_ref, ps_ref, dest_ref):
    idx = idx_ref[...]
    tm = idx.shape[1]
    eiota = lax.broadcasted_iota(I32, (N_EXPERTS, tm), 0)
    ps = ps_ref[...]
    rows = [jnp.sum(jnp.where(eiota == idx[k:k + 1, :], ps, 0.0), axis=0, keepdims=True)
            for k in range(TOP_K)]
    dest_ref[...] = jnp.concatenate(rows, axis=0).astype(I32) + rank_ref[...]


def _dest(idx, rank, pstart_col, tm):
    n = idx.shape[1]
    cols = pl.BlockSpec((TOP_K, tm), lambda i: (0, i))
    return pl.pallas_call(
        _dest_body,
        name="dest",
        grid=(n // tm,),
        in_specs=[cols, cols, pl.BlockSpec((N_EXPERTS, 1), lambda i: (0, 0))],
        out_specs=cols,
        out_shape=jax.ShapeDtypeStruct((TOP_K, n), I32),
        compiler_params=_params(("parallel",)),
    )(idx, rank, pstart_col)


TAIL_PIECES = tuple(1 << s for s in reversed(range(EXPERT_BLOCK.bit_length())))
ROW_UNROLL = 8


def _zero_segment_tails(ps_ref, cnt_ref, xs_ref, zbuf, zsem, reserve):
    zbuf[...] = jnp.zeros(zbuf.shape, U32)

    def tail(wait, e, carry):
        cnt = cnt_ref[e]
        base = ps_ref[e] + cnt
        seg = (cnt + reserve + EXPERT_BLOCK - 1) // EXPERT_BLOCK * EXPERT_BLOCK
        pad = seg - cnt
        for p in TAIL_PIECES:
            @pl.when((pad & p) != 0)
            def _():
                cp = pltpu.make_async_copy(zbuf.at[pl.ds(0, p)], xs_ref.at[pl.ds(base, p)], zsem)
                if wait:
                    cp.wait()
                else:
                    cp.start()
            base = base + (pad & p)
        return carry

    lax.fori_loop(0, N_EXPERTS, functools.partial(tail, False), 0)
    lax.fori_loop(0, N_EXPERTS, functools.partial(tail, True), 0)


def _start_row_copies(dest_ref, x_ref, xs_ref, sem, first, count):
    def issue(g, carry):
        for u in range(ROW_UNROLL):
            t = first + g * ROW_UNROLL + u
            for k in range(TOP_K):
                pltpu.make_async_copy(x_ref.at[t], xs_ref.at[dest_ref[k, t]],
                                      sem).start(priority=k % 2)
        return carry

    lax.fori_loop(0, count // ROW_UNROLL, issue, 0)


def _wait_row_copies(x_ref, xs_ref, sem, count):
    for _ in range(TOP_K):
        pltpu.make_async_copy(x_ref.at[pl.ds(0, count)], xs_ref.at[pl.ds(0, count)], sem).wait()


def _dispatch_body(dest_ref, x_ref, xs_prev_ref, xs_ref, sem):
    del xs_prev_ref
    ts = x_ref.shape[0]
    _start_row_copies(dest_ref, x_ref, xs_ref, sem, 0, ts)
    _wait_row_copies(x_ref, xs_ref, sem, ts)


def _dispatch(x_tiles, dest, xs_prev, ts):
    n = x_tiles.shape[0]
    return pl.pallas_call(
        _dispatch_body,
        name="dispatch",
        grid=(n // ts,),
        in_specs=[pl.BlockSpec((TOP_K, ts), lambda i: (0, i), memory_space=pltpu.SMEM),
                  pl.BlockSpec((ts, PACK_TILE, LANES), lambda i: (i, 0, 0)),
                  pl.BlockSpec(memory_space=pl.ANY)],
        out_specs=pl.BlockSpec(memory_space=pl.ANY),
        out_shape=jax.ShapeDtypeStruct(xs_prev.shape, xs_prev.dtype),
        scratch_shapes=[pltpu.SemaphoreType.DMA],
        input_output_aliases={2: 0},
        compiler_params=_params(("arbitrary",)),
    )(dest, x_tiles, xs_prev)


def _expert_body(be_ref, nb_ref, xs_ref, wg_ref, wu_ref, wd_ref, ys_ref, wg_b, wu_b, wd_b):
    i = pl.program_id(0)

    @pl.when(jnp.logical_or(i == 0, be_ref[i] != be_ref[jnp.maximum(i - 1, 0)]))
    def _():
        wg_b[...] = wg_ref[...].astype(BF16)
        wu_b[...] = wu_ref[...].astype(BF16)
        wd_b[...] = wd_ref[...].astype(BF16)

    @pl.when(i < nb_ref[0])
    def _():
        x = _load_packed_tiles(xs_ref, EXPERT_BLOCK)
        h = (_silu(_dot(x, wg_b[...])) * _dot(x, wu_b[...])).astype(BF16)
        _store_row_tiles(ys_ref, _dot(h, wd_b[...]))


def _experts(xs, block_e, nb_used, wg, wu, wd):
    rows = xs.shape[0] // PACK_TILE
    n_blocks = rows // EXPERT_BLOCK
    used = lambda i, be, nb: (jnp.minimum(i, nb[0] - 1), 0)
    in_rows = pl.BlockSpec((EXPERT_BLOCK * PACK_TILE, LANES), used)
    grid_spec = pltpu.PrefetchScalarGridSpec(
        num_scalar_prefetch=2,
        grid=(n_blocks,),
        in_specs=[in_rows,
                  pl.BlockSpec((None, D_MODEL, D_EXPERT), lambda i, be, nb: (be[i], 0, 0)),
                  pl.BlockSpec((None, D_MODEL, D_EXPERT), lambda i, be, nb: (be[i], 0, 0)),
                  pl.BlockSpec((None, D_EXPERT, D_MODEL), lambda i, be, nb: (be[i], 0, 0))],
        out_specs=pl.BlockSpec((EXPERT_BLOCK * ROW_TILE, LANES), used),
        scratch_shapes=[pltpu.VMEM((D_MODEL, D_EXPERT), BF16), pltpu.VMEM((D_MODEL, D_EXPERT), BF16),
                        pltpu.VMEM((D_EXPERT, D_MODEL), BF16)],
    )
    return pl.pallas_call(
        _expert_body,
        name="experts",
        grid_spec=grid_spec,
        out_shape=jax.ShapeDtypeStruct((rows * ROW_TILE, LANES), F32),
        compiler_params=_params(("arbitrary",)),
    )(block_e, nb_used, xs, wg, wu, wd)


def _combine_body(dest_ref, dnext_ref, wts_ref, x1_ref, ys_ref, wsg, wsu, wsd, g_ref, b_ref,
                  y_ref, buf, sems):
    i = pl.program_id(0)
    n_tiles = pl.num_programs(0)
    tc = x1_ref.shape[0]
    slot = i % 2

    def issue(d_ref, s, g, carry):
        for u in range(ROW_UNROLL):
            t = g * ROW_UNROLL + u
            for k in range(TOP_K):
                pltpu.make_async_copy(_token_tile(ys_ref, d_ref[k, t]),
                                      _token_tile(buf.at[s, k], t),
                                      sems.at[s]).start(priority=k % 2)
        return carry

    n_groups = tc // ROW_UNROLL

    @pl.when(i == 0)
    def _():
        lax.fori_loop(0, n_groups, functools.partial(issue, dest_ref, slot), 0)

    @pl.when(i + 1 < n_tiles)
    def _():
        lax.fori_loop(0, n_groups, functools.partial(issue, dnext_ref, 1 - slot), 0)

    x1 = x1_ref[...]
    xb = x1.astype(BF16)
    hs = (_silu(_dot(xb, wsg[...])) * _dot(xb, wsu[...])).astype(BF16)
    shared = _dot(hs, wsd[...])

    for k in range(TOP_K):
        pltpu.make_async_copy(ys_ref.at[pl.ds(0, tc * ROW_TILE)], buf.at[slot, k],
                              sems.at[slot]).wait()

    w = wts_ref[...]
    routed = w[:, 0:1] * _load_row_tiles(buf.at[slot, 0], tc)
    for k in range(1, TOP_K):
        routed = routed + w[:, k:k + 1] * _load_row_tiles(buf.at[slot, k], tc)
    y_ref[...] = _layer_norm(DEEPNORM_ALPHA * x1 + (routed + shared), g_ref[...], b_ref[...])


def _combine(dest, wts_t, x1, ys, wsg, wsu, wsd, ln_g, ln_b, tc):
    n = x1.shape[0]
    last = n // tc - 1
    dest_cur = pl.BlockSpec((TOP_K, tc), lambda i: (0, i), memory_space=pltpu.SMEM)
    dest_next = pl.BlockSpec((TOP_K, tc), lambda i: (0, jnp.minimum(i + 1, last)),
                             memory_space=pltpu.SMEM)
    full = lambda a: pl.BlockSpec(a.shape, lambda i: (0, 0))
    rows = lambda w: pl.BlockSpec((tc, w), lambda i: (i, 0))
    row_bytes = D_MODEL * 4
    need = tc * row_bytes * (2 * TOP_K + 4) + 14 * 1024 * 1024
    return pl.pallas_call(
        _combine_body,
        name="combine",
        grid=(n // tc,),
        in_specs=[dest_cur, dest_next, rows(LANES), rows(D_MODEL),
                  pl.BlockSpec(memory_space=pl.ANY),
                  full(wsg), full(wsu), full(wsd), full(ln_g), full(ln_b)],
        out_specs=rows(D_MODEL),
        out_shape=jax.ShapeDtypeStruct((n, D_MODEL), F32),
        scratch_shapes=[pltpu.VMEM((2, TOP_K, tc * ROW_TILE, LANES), F32),
                        pltpu.SemaphoreType.DMA((2,))],
        compiler_params=_params(("arbitrary",), max(VMEM_LIMIT, need)),
    )(dest, dest, wts_t, x1, ys, wsg, wsu, wsd, ln_g, ln_b)


def _segments(cnt_first, n_first, reserve):
    blk = EXPERT_BLOCK
    assert reserve <= blk
    n_blocks = -(-(n_first * TOP_K + N_EXPERTS * reserve) // blk) + N_EXPERTS
    counts = cnt_first[:, 0].astype(I32)
    padded = (counts + reserve + blk - 1) // blk * blk
    ends = jnp.cumsum(padded)
    pstart = ends - padded
    nb_used = (ends[-1] // blk).astype(I32).reshape(1)
    block_row0 = jnp.arange(n_blocks, dtype=I32) * blk
    block_e = jnp.minimum(jnp.sum((ends[None, :] <= block_row0[:, None]).astype(I32), axis=1),
                          N_EXPERTS - 1).astype(I32)
    return pstart, counts, block_e, nb_used, n_blocks * blk


def kernel(x_prompt, x_sample, cache_k, cache_v, state_conv, page_table, w_in, lam_q1, lam_k1,
           lam_q2, lam_k2, subln_g, w_dw, b_dw, conv_ln_g, conv_ln_b, w_o, ln1_g, ln1_b,
           w_router, b_router, w_gate, w_up, w_down, w_sh_gate, w_sh_up, w_sh_down, ln2_g,
           ln2_b):
    batch, seq, _ = x_prompt.shape
    dec_b, dec_s, _ = x_sample.shape
    n_p, n_s = batch * seq, dec_b * dec_s
    layer = 0
    lam_init = _lambda_init(layer)
    row = lambda a: a[layer].reshape(1, -1)

    w_in_b = w_in[layer].astype(BF16)
    wo = w_o[layer].astype(BF16)
    wo_a, wo_c = wo[:ATTN_WIDTH], wo[ATTN_WIDTH:]
    wr_t = w_router[layer].T
    wr_hi = wr_t.astype(BF16)
    wr_lo = (wr_t - wr_hi.astype(F32)).astype(BF16)
    br = b_router[layer].reshape(-1, 1)
    moe_w = (w_gate[layer], w_up[layer], w_down[layer]) + tuple(
        w[layer].astype(BF16) for w in (w_sh_gate, w_sh_up, w_sh_down))
    lam_vecs = (row(lam_q1), row(lam_k1), row(lam_q2), row(lam_k2))
    g_sub = row(subln_g)
    conv_w = (w_dw[layer], row(b_dw), row(conv_ln_g), row(conv_ln_b))
    ln1 = (row(ln1_g), row(ln1_b))
    ln2 = (row(ln2_g), row(ln2_b))

    xp = x_prompt.reshape(n_p, D_MODEL)
    wq_t = w_in_b[:, :QK_WIDTH].T
    wv_t = w_in_b[:, 2 * QK_WIDTH:2 * QK_WIDTH + ATTN_WIDTH].T
    qt_p, k_p, v_p, u_p, kb_p, vt_p = _inproj(xp, w_in_b, ATTN_TILE, (wq_t, wv_t))
    o_p = _attn_prompt(qt_p, kb_p, vt_p, lam_vecs, g_sub.reshape(V_DIM, 1), lam_init, batch, seq)
    c_p = _conv_prompt(u_p, *conv_w, batch, seq, 512)
    cnt0 = jnp.zeros((N_EXPERTS, LANES), F32)
    x1_p, xt_p, idx_p, wts_p, rank_p, cnt_p = _mix(o_p, c_p, xp, wo_a, wo_c, *ln1, wr_hi, wr_lo,
                                                   br, cnt0, 512)
    pstart, counts_p, block_e, nb_used, n_rows = _segments(cnt_p, n_p, n_s)
    pstart_col = pstart.astype(F32).reshape(N_EXPERTS, 1)
    dest_p = _dest(idx_p, rank_p, pstart_col, 256)

    xs = x_sample.reshape(n_s, D_MODEL)
    q_s, k_s, v_s, u_s = _inproj(xs, w_in_b, n_s)
    pool = cache_k.shape[1]
    page_rows = lambda a: a[layer].reshape(pool, PAGE_ROWS, V_DIM)
    new_rows = lambda a: a.reshape(dec_b, dec_s * N_HEADS, V_DIM)
    o_s, rows_in = _attn_sample(
        q_s.reshape(dec_b, dec_s, QK_WIDTH), new_rows(k_s), new_rows(v_s), page_rows(cache_k),
        page_rows(cache_v), page_table, lam_vecs, g_sub, lam_init,
        xt_p.reshape(n_p, PACK_TILE, LANES), dest_p, pstart, counts_p, n_rows, n_s)
    c_s, st_s = _conv_sample(u_s.reshape(dec_b, dec_s, CONV_CH), state_conv[layer], *conv_w)
    x1_s, xt_s, idx_s, wts_s, rank_s, _ = _mix(
        o_s.reshape(n_s, ATTN_WIDTH), c_s.reshape(n_s, CONV_CH), xs, wo_a, wo_c, *ln1, wr_hi,
        wr_lo, br, cnt_p, n_s)
    dest_s = _dest(idx_s, rank_s, pstart_col, 128)
    rows_in = _dispatch(xt_s.reshape(n_s, PACK_TILE, LANES), dest_s, rows_in, 128)

    wg, wu, wd, wsg, wsu, wsd = moe_w
    rows_out = _experts(rows_in.reshape(n_rows * PACK_TILE, LANES), block_e, nb_used, wg, wu, wd)
    y_p = _combine(dest_p, wts_p, x1_p, rows_out, wsg, wsu, wsd, *ln2, 512)
    y_s = _combine(dest_s, wts_s, x1_s, rows_out, wsg, wsu, wsd, *ln2, 128)

    keep = CONV_WIDTH - 1
    u_p3 = u_p.reshape(batch, seq, CONV_CH)
    return (y_p.reshape(batch, seq, D_MODEL),
            y_s.reshape(dec_b, dec_s, D_MODEL),
            k_p.reshape(1, batch, seq, N_HEADS, V_DIM),
            v_p.reshape(1, batch, seq, N_HEADS, V_DIM),
            u_p3[:, seq - keep:, :][None],
            k_s.reshape(1, dec_b, dec_s, N_HEADS, V_DIM),
            v_s.reshape(1, dec_b, dec_s, N_HEADS, V_DIM),
            st_s[None])
```

```python
import functools
import math

import jax
import jax.numpy as jnp
from jax import lax
from jax.experimental import pallas as pl
from jax.experimental.pallas import tpu as pltpu

F32 = jnp.float32
BF16 = jnp.bfloat16
I32 = jnp.int32
U32 = jnp.uint32

D_MODEL = 1024
N_HEADS = 4
HEAD_DIM = 64
V_DIM = 128
ATTN_WIDTH = N_HEADS * V_DIM
QK_WIDTH = N_HEADS * 2 * HEAD_DIM
ATTN_SCALE = HEAD_DIM ** -0.5
LOG2_E = math.log2(math.e)
CONV_CH = D_MODEL - ATTN_WIDTH
CONV_WIDTH = 31
N_EXPERTS = 64
N_GROUPS = 8
GROUP_SIZE = N_EXPERTS // N_GROUPS
TOPK_GROUPS = 4
TOP_K = 8
D_EXPERT = D_MODEL // 4
ROUTED_SCALE = 2.5
DEPTH = 1
DEEPNORM_ALPHA = (2 * DEPTH) ** 0.25
NORM_EPS = 1e-5
NEG_INF = -1e30
PAGE_SIZE = 128

LANES = 128
ROW_TILE = D_MODEL // LANES
PACK_TILE = ROW_TILE // 2
VMEM_LIMIT = 48 * 1024 * 1024
PAGES_PER_STEP = 32
ATTN_TILE = 512
PAGE_ROWS = PAGE_SIZE * N_HEADS
EXPERT_BLOCK = 1024
HIST = 32


def _lambda_init(layer):
    return 0.8 - 0.6 * math.exp(-0.3 * layer)


def _sigmoid(x):
    return 1.0 / (1.0 + jnp.exp(-x))


def _silu(x):
    return x * _sigmoid(x)


def _layer_norm(x, g, b):
    mu = jnp.mean(x, axis=-1, keepdims=True)
    xc = x - mu
    var = jnp.mean(xc * xc, axis=-1, keepdims=True)
    return xc * lax.rsqrt(var + NORM_EPS) * g + b


def _dot(a, b):
    return jnp.dot(a, b, preferred_element_type=F32)


def _dot_nt(a, b):
    return lax.dot_general(a, b, (((1,), (1,)), ((), ())), preferred_element_type=F32)


def _diff_lambda(lq1, lk1, lq2, lk2, lam_init):
    a = jnp.exp(jnp.sum(lq1 * lk1, axis=-1, keepdims=True))
    b = jnp.exp(jnp.sum(lq2 * lk2, axis=-1, keepdims=True))
    return a - b + lam_init


def _params(dims, vmem_limit=VMEM_LIMIT):
    return pltpu.CompilerParams(dimension_semantics=dims, vmem_limit_bytes=vmem_limit)


def _store_row_tiles(ref, x):
    m = x.shape[0]
    for j in range(ROW_TILE):
        ref[pl.ds(j, m, stride=ROW_TILE), :] = x[:, j * LANES:(j + 1) * LANES]


def _load_row_tiles(ref, m):
    return jnp.concatenate([ref[pl.ds(j, m, stride=ROW_TILE), :] for j in range(ROW_TILE)],
                           axis=1)


def _token_tile(ref, t):
    return ref.at[pl.ds(pl.multiple_of(t * ROW_TILE, ROW_TILE), ROW_TILE)]


def _store_packed_tiles(ref, x):
    m = x.shape[0]
    bits = pltpu.bitcast(x.astype(BF16).astype(F32), U32)
    half = D_MODEL // 2
    for j in range(PACK_TILE):
        lo = bits[:, j * LANES:(j + 1) * LANES] >> 16
        hi = bits[:, half + j * LANES:half + (j + 1) * LANES]
        ref[pl.ds(j, m, stride=PACK_TILE), :] = lo | hi


def _load_packed_tiles(ref, m):
    words = [ref[pl.ds(j, m, stride=PACK_TILE), :] for j in range(PACK_TILE)]
    lo = [pltpu.bitcast(w << 16, F32) for w in words]
    hi = [pltpu.bitcast(w & jnp.uint32(0xFFFF0000), F32) for w in words]
    return jnp.concatenate(lo + hi, axis=1).astype(BF16)


def _store_head_rows(ref, x):
    m = x.shape[0]
    for h in range(N_HEADS):
        ref[pl.ds(h, m, stride=N_HEADS), :] = x[:, h * V_DIM:(h + 1) * V_DIM]


def _inproj_body(x_ref, w_ref, *refs):
    xb = x_ref[...].astype(BF16)

    def mm(c0):
        return _dot(xb, w_ref[:, c0:c0 + QK_WIDTH])

    k = mm(QK_WIDTH)
    if len(refs) == 4:
        q_ref, k_ref, v_ref, u_ref = refs
        q_ref[...] = mm(0) * ATTN_SCALE
    else:
        wqt_ref, wvt_ref, q_ref, k_ref, v_ref, u_ref, kb_ref, vt_ref = refs
        q_t = _dot_nt(wqt_ref[...], xb) * (ATTN_SCALE * LOG2_E)
        v_t = _dot_nt(wvt_ref[...], xb).astype(BF16)
        for c in range(q_ref.shape[0]):
            cols = slice(c * ATTN_TILE, (c + 1) * ATTN_TILE)
            q_ref[c] = q_t[:, cols]
            vt_ref[c] = v_t[:, cols]
        kb_ref[...] = k.astype(BF16)
    _store_head_rows(k_ref, k)
    _store_head_rows(v_ref, mm(2 * QK_WIDTH))
    a = mm(2 * QK_WIDTH + ATTN_WIDTH)
    b = mm(2 * QK_WIDTH + ATTN_WIDTH + CONV_CH)
    u_ref[...] = a * _sigmoid(b)


def _inproj(x, w_in_b, tm, transposed_w=()):
    n = x.shape[0]
    wide = QK_WIDTH
    blk = pl.BlockSpec((tm, wide), lambda i: (i, 0))
    hblk = pl.BlockSpec((tm * N_HEADS, V_DIM), lambda i: (i, 0))
    full = lambda a: pl.BlockSpec(a.shape, lambda i: (0, 0))
    row_major = jax.ShapeDtypeStruct((n, wide), F32)
    head_rows = jax.ShapeDtypeStruct((n * N_HEADS, V_DIM), F32)
    out_specs = [blk, hblk, hblk, blk]
    out_shape = [row_major, head_rows, head_rows, row_major]
    if transposed_w:
        t_blk = pl.BlockSpec((tm // ATTN_TILE, wide, ATTN_TILE), lambda i: (i, 0, 0))
        t_shape = (n // ATTN_TILE, wide, ATTN_TILE)
        out_specs = [t_blk, hblk, hblk, blk, blk, t_blk]
        out_shape = [jax.ShapeDtypeStruct(t_shape, F32), head_rows, head_rows,
                     row_major, jax.ShapeDtypeStruct((n, wide), BF16),
                     jax.ShapeDtypeStruct(t_shape, BF16)]
    return pl.pallas_call(
        _inproj_body,
        name="inproj",
        grid=(n // tm,),
        in_specs=[pl.BlockSpec((tm, D_MODEL), lambda i: (i, 0)), full(w_in_b)]
        + [full(w) for w in transposed_w],
        out_specs=out_specs,
        out_shape=out_shape,
        compiler_params=_params(("parallel",)),
    )(x, w_in_b, *transposed_w)


def _attn_p_body(lam_init, q_ref, k_ref, v_ref, lq1, lk1, lq2, lk2, g_ref, o_ref, acc_ref):
    qi = pl.program_id(2)
    tq = q_ref.shape[1]
    tk = v_ref.shape[2]
    q_t = q_ref[...]
    feat = lax.broadcasted_iota(I32, q_t.shape, 0)
    q_both = jnp.concatenate([jnp.where(feat < HEAD_DIM, q_t, 0.0),
                              jnp.where(feat >= HEAD_DIM, q_t, 0.0)], axis=1).astype(BF16)
    acc_ref[...] = jnp.zeros(acc_ref.shape, F32)

    def chunk(masked, j, carry):
        m_old, l_old = carry
        kc = k_ref[pl.ds(pl.multiple_of(j * tk, tk), tk), :]
        s_t = _dot(kc, q_both)
        if masked:
            key = lax.broadcasted_iota(I32, s_t.shape, 0)
            qry = lax.broadcasted_iota(I32, s_t.shape, 1) % tq
            s_t = jnp.where(key <= qry, s_t, NEG_INF)
        m_new = jnp.maximum(m_old, jnp.max(s_t, axis=0, keepdims=True))
        alpha = jnp.exp2(m_old - m_new)
        p_t = jnp.exp2(s_t - m_new)
        l_new = alpha * l_old + jnp.sum(p_t, axis=0, keepdims=True)
        acc_ref[...] = alpha * acc_ref[...] + _dot(v_ref[j], p_t.astype(BF16))
        return m_new, l_new

    init = (jnp.full((1, 2 * tq), -jnp.inf, F32), jnp.zeros((1, 2 * tq), F32))
    carry = lax.fori_loop(0, qi, functools.partial(chunk, False), init)
    _, l_all = chunk(True, qi, carry)

    lam = _diff_lambda(lq1[...], lk1[...], lq2[...], lk2[...], lam_init)
    on = acc_ref[...] / l_all
    o_t = on[:, :tq] - lam * on[:, tq:]
    ms = jnp.mean(o_t * o_t, axis=0, keepdims=True)
    o_t = o_t * lax.rsqrt(ms + NORM_EPS) * g_ref[...] * (1.0 - lam_init)
    o_ref[...] = o_t.T


def _attn_prompt(q_t, kb, v_t, lam_vecs, subln_g_col, lam_init, batch, seq):
    t = q_t.shape[2]
    n = kb.shape[0]
    nq = seq // t
    small = lambda w: pl.BlockSpec((1, w), lambda b, h, i: (0, 0))
    return pl.pallas_call(
        functools.partial(_attn_p_body, lam_init),
        name="attn_prompt",
        grid=(batch, N_HEADS, nq),
        in_specs=[pl.BlockSpec((None, V_DIM, t), lambda b, h, i: (b * nq + i, h, 0)),
                  pl.BlockSpec((seq, V_DIM), lambda b, h, i: (b, h)),
                  pl.BlockSpec((nq, V_DIM, t), lambda b, h, i: (b, h, 0))]
        + [small(HEAD_DIM)] * 4 + [pl.BlockSpec((V_DIM, 1), lambda b, h, i: (0, 0))],
        out_specs=pl.BlockSpec((t, V_DIM), lambda b, h, i: (b * nq + i, h)),
        out_shape=jax.ShapeDtypeStruct((n, ATTN_WIDTH), F32),
        scratch_shapes=[pltpu.VMEM((V_DIM, 2 * t), F32)],
        compiler_params=_params(("parallel", "parallel", "parallel")),
    )(q_t, kb, v_t, *lam_vecs, subln_g_col)


def _attn_s_body(lam_init, n_steps, tokens_per_step, reserve, pt_ref, ps_ref, cnt_ref, q_ref,
                 kn_ref, vn_ref, *rest):
    del pt_ref
    npg = PAGES_PER_STEP
    k_refs = rest[:npg]
    v_refs = rest[npg:2 * npg]
    (lq1, lk1, lq2, lk2, g_ref, dest_ref, x_ref, o_ref, xs_ref, qall, knew, vnew, m_ref, l_ref,
     acc_ref, zbuf, sem, zsem) = rest[2 * npg:]
    j = pl.program_id(1)
    ds = q_ref.shape[0]
    hr = 2 * ds
    step = pl.program_id(0) * n_steps + j

    @pl.when(step == 0)
    def _():
        _zero_segment_tails(ps_ref, cnt_ref, xs_ref, zbuf, zsem, reserve)

    first_token = (step % (x_ref.shape[0] // tokens_per_step)) * tokens_per_step
    _start_row_copies(dest_ref, x_ref, xs_ref, sem, first_token, tokens_per_step)

    def head_rows(ref, h):
        return ref[pl.ds(h, PAGE_SIZE, stride=N_HEADS), :].astype(BF16)

    @pl.when(j == 0)
    def _():
        pieces = []
        for h in range(N_HEADS):
            qh = q_ref[:, h * V_DIM:(h + 1) * V_DIM]
            lane = lax.broadcasted_iota(I32, qh.shape, 1)
            pieces.append(jnp.where(lane < HEAD_DIM, qh, 0.0))
            pieces.append(jnp.where(lane >= HEAD_DIM, qh, 0.0))
        qall[...] = jnp.concatenate(pieces, axis=0).astype(BF16)
        knew[...] = jnp.zeros(knew.shape, F32)
        vnew[...] = jnp.zeros(vnew.shape, F32)
        knew[0:ds * N_HEADS] = kn_ref[...]
        vnew[0:ds * N_HEADS] = vn_ref[...]
        row = lax.broadcasted_iota(I32, (hr, PAGE_SIZE), 0)
        key = lax.broadcasted_iota(I32, (hr, PAGE_SIZE), 1)
        keep = key <= (row % ds)
        for h in range(N_HEADS):
            rows = slice(h * hr, (h + 1) * hr)
            s = jnp.where(keep, _dot_nt(qall[rows], head_rows(knew, h)), NEG_INF)
            m = jnp.max(s, axis=-1, keepdims=True)
            p = jnp.exp(s - m)
            m_ref[rows] = m
            l_ref[rows] = jnp.sum(p, axis=-1, keepdims=True)
            acc_ref[rows] = _dot(p.astype(BF16), head_rows(vnew, h))

    q_all, m_all, l_all, acc_all = qall[...], m_ref[...], l_ref[...], acc_ref[...]
    heads = range(N_HEADS)
    rows = [slice(h * hr, (h + 1) * hr) for h in heads]
    s = [_dot_nt(q_all[rows[h]],
                 jnp.concatenate([head_rows(k_refs[i], h) for i in range(npg)], axis=0))
         for h in heads]
    s = jnp.concatenate(s, axis=0)
    m_new = jnp.maximum(m_all, jnp.max(s, axis=-1, keepdims=True))
    alpha = jnp.exp(m_all - m_new)
    p = jnp.exp(s - m_new)
    l_new = alpha * l_all + jnp.sum(p, axis=-1, keepdims=True)
    pb = p.astype(BF16)
    pv = [_dot(pb[rows[h]],
               jnp.concatenate([head_rows(v_refs[i], h) for i in range(npg)], axis=0))
          for h in heads]
    acc_new = alpha * acc_all + jnp.concatenate(pv, axis=0)
    m_ref[...] = m_new
    l_ref[...] = l_new
    acc_ref[...] = acc_new
    _wait_row_copies(x_ref, xs_ref, sem, tokens_per_step)

    @pl.when(j == n_steps - 1)
    def _():
        lam = _diff_lambda(lq1[...], lk1[...], lq2[...], lk2[...], lam_init)
        on = acc_new / l_new
        for h in range(N_HEADS):
            r0 = h * hr
            o = on[r0:r0 + ds] - lam * on[r0 + ds:r0 + 2 * ds]
            ms = jnp.mean(o * o, axis=-1, keepdims=True)
            o_ref[:, h * V_DIM:(h + 1) * V_DIM] = (
                o * lax.rsqrt(ms + NORM_EPS) * g_ref[...] * (1.0 - lam_init))


def _attn_sample(q, k_new, v_new, cache_k, cache_v, page_table, lam_vecs, subln_g, lam_init,
                 x_tiles, dest, pstart, counts, n_rows, reserve):
    b, ds, _ = q.shape
    n_pages = page_table.shape[1]
    n_steps = n_pages // PAGES_PER_STEP
    nrow = N_HEADS * 2 * ds
    pt = page_table.reshape(-1)
    total_steps = b * n_steps
    n = x_tiles.shape[0]
    tokens_per_step = n // total_steps
    assert tokens_per_step * total_steps == n and LANES % tokens_per_step == 0
    per_block = LANES // tokens_per_step
    tile = (PACK_TILE, LANES)

    def page_map(i, bi, j, pt_ref, *_):
        return (pt_ref[bi * n_pages + j * PAGES_PER_STEP + i], 0, 0)

    page_specs = [pl.BlockSpec((None, PAGE_ROWS, V_DIM), functools.partial(page_map, i))
                  for i in range(PAGES_PER_STEP)]
    token_block = lambda bi, j, *_: (bi * n_steps + j) // per_block
    small = lambda w: pl.BlockSpec((1, w), lambda bi, j, *_: (0, 0))
    new_spec = pl.BlockSpec((None, ds * N_HEADS, V_DIM), lambda bi, j, *_: (bi, 0, 0))
    qo_spec = pl.BlockSpec((None, ds, ATTN_WIDTH), lambda bi, j, *_: (bi, 0, 0))
    grid_spec = pltpu.PrefetchScalarGridSpec(
        num_scalar_prefetch=3,
        grid=(b, n_steps),
        in_specs=[qo_spec, new_spec, new_spec] + page_specs + page_specs
        + [small(HEAD_DIM)] * 4 + [small(V_DIM)]
        + [pl.BlockSpec((TOP_K, LANES), lambda bi, j, *_: (0, token_block(bi, j)),
                        memory_space=pltpu.SMEM),
           pl.BlockSpec((LANES,) + tile, lambda bi, j, *_: (token_block(bi, j), 0, 0))],
        out_specs=[qo_spec, pl.BlockSpec(memory_space=pl.ANY)],
        scratch_shapes=[pltpu.VMEM((nrow, V_DIM), BF16),
                        pltpu.VMEM((PAGE_ROWS, V_DIM), F32),
                        pltpu.VMEM((PAGE_ROWS, V_DIM), F32),
                        pltpu.VMEM((nrow, 1), F32),
                        pltpu.VMEM((nrow, 1), F32),
                        pltpu.VMEM((nrow, V_DIM), F32),
                        pltpu.VMEM((TAIL_PIECES[0],) + tile, U32),
                        pltpu.SemaphoreType.DMA, pltpu.SemaphoreType.DMA],
    )
    return pl.pallas_call(
        functools.partial(_attn_s_body, lam_init, n_steps, tokens_per_step, reserve),
        name="attn_sample",
        grid_spec=grid_spec,
        out_shape=[jax.ShapeDtypeStruct((b, ds, ATTN_WIDTH), F32),
                   jax.ShapeDtypeStruct((n_rows,) + tile, U32)],
        compiler_params=_params(("arbitrary", "arbitrary")),
    )(pt, pstart, counts, q, k_new, v_new, *([cache_k] * PAGES_PER_STEP),
      *([cache_v] * PAGES_PER_STEP), *lam_vecs, subln_g, dest, x_tiles)


def _conv_taps(ext_ref, w_ref, bias, rows):
    off = HIST - (CONV_WIDTH - 1)
    acc = jnp.broadcast_to(bias, (rows, CONV_CH))
    for jt in range(CONV_WIDTH):
        acc = acc + w_ref[jt:jt + 1, :] * ext_ref[pl.ds(jt + off, rows), :]
    return acc


CONV_ROWS = 64
SUBLANES = 8


def _conv_p_body(u_ref, h_ref, w_ref, b_ref, g_ref, beta_ref, c_ref, ext_ref, sh_ref):
    t = pl.program_id(1)
    tm = u_ref.shape[0]
    hist = h_ref[...]
    ext_ref[0:HIST] = jnp.where(t == 0, jnp.zeros_like(hist), hist)
    ext_ref[HIST:HIST + tm] = u_ref[...]
    span = sh_ref.shape[1]
    for r in range(1, SUBLANES):
        sh_ref[r - 1] = ext_ref[pl.ds(r, span), :]
    off = HIST - (CONV_WIDTH - 1)
    bias = b_ref[...]
    for c0 in range(0, tm, CONV_ROWS):
        acc = jnp.broadcast_to(bias, (CONV_ROWS, CONV_CH))
        for jt in range(CONV_WIDTH):
            r = (jt + off) % SUBLANES
            base = c0 + jt + off - r
            src = ext_ref if r == 0 else sh_ref.at[r - 1]
            acc = acc + w_ref[jt:jt + 1, :] * src[pl.ds(base, CONV_ROWS), :]
        c_ref[c0:c0 + CONV_ROWS, :] = _silu(_layer_norm(acc, g_ref[...], beta_ref[...]))


def _conv_prompt(u, w_dw, b_dw, g, beta, batch, seq, tm):
    n = u.shape[0]
    nt = seq // tm
    per = tm // HIST
    cur = lambda b, t: (b * nt + t, 0)
    prev = lambda b, t: (jnp.maximum((b * nt + t) * per - 1, 0), 0)
    small = lambda r: pl.BlockSpec((r, CONV_CH), lambda b, t: (0, 0))
    return pl.pallas_call(
        _conv_p_body,
        name="conv_prompt",
        grid=(batch, nt),
        in_specs=[pl.BlockSpec((tm, CONV_CH), cur), pl.BlockSpec((HIST, CONV_CH), prev),
                  small(CONV_WIDTH), small(1), small(1), small(1)],
        out_specs=pl.BlockSpec((tm, CONV_CH), cur),
        out_shape=jax.ShapeDtypeStruct((n, CONV_CH), F32),
        scratch_shapes=[pltpu.VMEM((HIST + tm, CONV_CH), F32),
                        pltpu.VMEM((SUBLANES - 1, HIST + tm - SUBLANES, CONV_CH), F32)],
        compiler_params=_params(("parallel", "parallel")),
    )(u, u, w_dw, b_dw, g, beta)


def _conv_s_body(u_ref, st_ref, w_ref, b_ref, g_ref, beta_ref, c_ref, ns_ref, ext_ref):
    ds = u_ref.shape[0]
    keep = CONV_WIDTH - 1
    off = HIST - keep
    ext_ref[0:off] = jnp.zeros((off, CONV_CH), F32)
    ext_ref[off:HIST] = st_ref[...]
    ext_ref[HIST:HIST + ds] = u_ref[...]
    y = _conv_taps(ext_ref, w_ref, b_ref[...], ds)
    c_ref[...] = _silu(_layer_norm(y, g_ref[...], beta_ref[...]))
    ns_ref[...] = ext_ref[HIST + ds - keep:HIST + ds]


def _conv_sample(u, state, w_dw, b_dw, g, beta):
    b, ds, _ = u.shape
    keep = CONV_WIDTH - 1
    per_b = lambda r: pl.BlockSpec((None, r, CONV_CH), lambda i: (i, 0, 0))
    small = lambda r: pl.BlockSpec((r, CONV_CH), lambda i: (0, 0))
    return pl.pallas_call(
        _conv_s_body,
        name="conv_sample",
        grid=(b,),
        in_specs=[per_b(ds), per_b(keep), small(CONV_WIDTH), small(1), small(1), small(1)],
        out_specs=[per_b(ds), per_b(keep)],
        out_shape=[jax.ShapeDtypeStruct((b, ds, CONV_CH), F32),
                   jax.ShapeDtypeStruct((b, keep, CONV_CH), F32)],
        scratch_shapes=[pltpu.VMEM((HIST + ds, CONV_CH), F32)],
        compiler_params=_params(("parallel",)),
    )(u, state, w_dw, b_dw, g, beta)


def _first_index(hit, iota, limit, axis):
    return jnp.min(jnp.where(hit, iota, limit), axis=axis, keepdims=True)


def _route(x1, wr_hi, wr_lo, bias):
    tm = x1.shape[0]
    x_hi = x1.astype(BF16)
    x_lo = (x1 - x_hi.astype(F32)).astype(BF16)
    both = _dot_nt(jnp.concatenate([wr_hi, wr_lo], axis=0), x_hi)
    logits = both[:N_EXPERTS] + (both[N_EXPERTS:] + _dot_nt(wr_hi, x_lo))
    scores = _sigmoid(logits)
    sel = scores + bias
    sel3 = sel.reshape(N_GROUPS, GROUP_SIZE, tm)
    member = lax.broadcasted_iota(I32, sel3.shape, 1)
    m1 = jnp.max(sel3, axis=1, keepdims=True)
    i1 = _first_index(sel3 == m1, member, GROUP_SIZE, 1)
    m2 = jnp.max(jnp.where(member == i1, -jnp.inf, sel3), axis=1, keepdims=True)
    gs = jnp.broadcast_to(m1 + m2, sel3.shape).reshape(N_EXPERTS, tm)
    eiota = lax.broadcasted_iota(I32, (N_EXPERTS, tm), 0)
    giota = eiota // GROUP_SIZE
    gmask = jnp.zeros((N_EXPERTS, tm), jnp.bool_)
    for _ in range(TOPK_GROUPS):
        m = jnp.max(gs, axis=0, keepdims=True)
        gi = _first_index(gs == m, giota, N_GROUPS, 0)
        pick = giota == gi
        gmask = jnp.logical_or(gmask, pick)
        gs = jnp.where(pick, -jnp.inf, gs)
    selm = jnp.where(gmask, sel, NEG_INF)
    idx_rows, w_rows = [], []
    for _ in range(TOP_K):
        m = jnp.max(selm, axis=0, keepdims=True)
        ei = _first_index(selm == m, eiota, N_EXPERTS, 0)
        pick = eiota == ei
        idx_rows.append(ei)
        w_rows.append(jnp.sum(jnp.where(pick, scores, 0.0), axis=0, keepdims=True))
        selm = jnp.where(pick, -jnp.inf, selm)
    idx = jnp.concatenate(idx_rows, axis=0)
    w = jnp.concatenate(w_rows, axis=0)
    w = w / jnp.sum(w, axis=0, keepdims=True) * ROUTED_SCALE
    return idx, w


def _mix_body(o_ref, c_ref, x_ref, wo_a, wo_c, g_ref, b_ref, wrh_ref, wrl_ref, br_ref, cnt0_ref,
              x1_ref, xrt_ref, idx_ref, wts_ref, rank_ref, cnt_ref, base_ref):
    i = pl.program_id(0)
    tm = x_ref.shape[0]

    @pl.when(i == 0)
    def _():
        base_ref[...] = cnt0_ref[...]

    mix = _dot(o_ref[...].astype(BF16), wo_a[...]) + _dot(c_ref[...].astype(BF16), wo_c[...])
    x1 = _layer_norm(DEEPNORM_ALPHA * x_ref[...] + mix, g_ref[...], b_ref[...])
    x1_ref[...] = x1
    _store_packed_tiles(xrt_ref, x1)

    idx, w = _route(x1, wrh_ref[...], wrl_ref[...], br_ref[...])
    idx_ref[...] = idx
    w_rows = jnp.concatenate([w, jnp.zeros((LANES - TOP_K, tm), F32)], axis=0)
    wts_ref[...] = w_rows.T

    eiota = lax.broadcasted_iota(I32, (N_EXPERTS, tm), 0)
    chosen = jnp.zeros((N_EXPERTS, tm), F32)
    for k in range(TOP_K):
        chosen = chosen + jnp.where(eiota == idx[k:k + 1, :], 1.0, 0.0)
    r = lax.broadcasted_iota(I32, (tm, tm), 0)
    c = lax.broadcasted_iota(I32, (tm, tm), 1)
    before = jnp.where(r < c, 1.0, 0.0).astype(BF16)
    base = base_ref[:, 0:1]
    rank_full = _dot(chosen.astype(BF16), before) + base
    rows = [jnp.sum(jnp.where(eiota == idx[k:k + 1, :], rank_full, 0.0), axis=0, keepdims=True)
            for k in range(TOP_K)]
    rank_ref[...] = jnp.concatenate(rows, axis=0).astype(I32)
    total = base + jnp.sum(chosen, axis=1, keepdims=True)
    base_ref[...] = jnp.broadcast_to(total, base_ref.shape)
    cnt_ref[...] = jnp.broadcast_to(total, cnt_ref.shape)


def _mix(o, c, x, wo_a, wo_c, ln_g, ln_b, wr_hi, wr_lo, b_router, cnt0, tm):
    n = x.shape[0]
    rows = lambda w: pl.BlockSpec((tm, w), lambda i: (i, 0))
    full = lambda a: pl.BlockSpec(a.shape, lambda i: (0, 0))
    cols = pl.BlockSpec((TOP_K, tm), lambda i: (0, i))
    cnt = pl.BlockSpec((N_EXPERTS, LANES), lambda i: (0, 0))
    return pl.pallas_call(
        _mix_body,
        name="mix",
        grid=(n // tm,),
        in_specs=[rows(ATTN_WIDTH), rows(CONV_CH), rows(D_MODEL), full(wo_a), full(wo_c),
                  full(ln_g), full(ln_b), full(wr_hi), full(wr_lo), full(b_router), cnt],
        out_specs=[rows(D_MODEL), pl.BlockSpec((tm * PACK_TILE, LANES), lambda i: (i, 0)), cols,
                   rows(LANES), cols, cnt],
        out_shape=[jax.ShapeDtypeStruct((n, D_MODEL), F32),
                   jax.ShapeDtypeStruct((n * PACK_TILE, LANES), U32),
                   jax.ShapeDtypeStruct((TOP_K, n), I32),
                   jax.ShapeDtypeStruct((n, LANES), F32),
                   jax.ShapeDtypeStruct((TOP_K, n), I32),
                   jax.ShapeDtypeStruct((N_EXPERTS, LANES), F32)],
        scratch_shapes=[pltpu.VMEM((N_EXPERTS, LANES), F32)],
        compiler_params=_params(("arbitrary",)),
    )(o, c, x, wo_a, wo_c, ln_g, ln_b, wr_hi, wr_lo, b_router, cnt0)


def _dest_body(idx_ref, rank_ref, ps_ref, dest_ref):
    idx = idx_ref[...]
    tm = idx.shape[1]
    eiota = lax.broadcasted_iota(I32, (N_EXPERTS, tm), 0)
    ps = ps_ref[...]
    rows = [jnp.sum(jnp.where(eiota == idx[k:k + 1, :], ps, 0.0), axis=0, keepdims=True)
            for k in range(TOP_K)]
    dest_ref[...] = jnp.concatenate(rows, axis=0).astype(I32) + rank_ref[...]


def _dest(idx, rank, pstart_col, tm):
    n = idx.shape[1]
    cols = pl.BlockSpec((TOP_K, tm), lambda i: (0, i))
    return pl.pallas_call(
        _dest_body,
        name="dest",
        grid=(n // tm,),
        in_specs=[cols, cols, pl.BlockSpec((N_EXPERTS, 1), lambda i: (0, 0))],
        out_specs=cols,
        out_shape=jax.ShapeDtypeStruct((TOP_K, n), I32),
        compiler_params=_params(("parallel",)),
    )(idx, rank, pstart_col)


TAIL_PIECES = tuple(1 << s for s in reversed(range(EXPERT_BLOCK.bit_length())))
ROW_UNROLL = 8


def _zero_segment_tails(ps_ref, cnt_ref, xs_ref, zbuf, zsem, reserve):
    zbuf[...] = jnp.zeros(zbuf.shape, U32)

    def tail(wait, e, carry):
        cnt = cnt_ref[e]
        base = ps_ref[e] + cnt
        seg = (cnt + reserve + EXPERT_BLOCK - 1) // EXPERT_BLOCK * EXPERT_BLOCK
        pad = seg - cnt
        for p in TAIL_PIECES:
            @pl.when((pad & p) != 0)
            def _():
                cp = pltpu.make_async_copy(zbuf.at[pl.ds(0, p)], xs_ref.at[pl.ds(base, p)], zsem)
                if wait:
                    cp.wait()
                else:
                    cp.start()
            base = base + (pad & p)
        return carry

    lax.fori_loop(0, N_EXPERTS, functools.partial(tail, False), 0)
    lax.fori_loop(0, N_EXPERTS, functools.partial(tail, True), 0)


def _start_row_copies(dest_ref, x_ref, xs_ref, sem, first, count):
    def issue(g, carry):
        for u in range(ROW_UNROLL):
            t = first + g * ROW_UNROLL + u
            for k in range(TOP_K):
                pltpu.make_async_copy(x_ref.at[t], xs_ref.at[dest_ref[k, t]],
                                      sem).start(priority=k % 2)
        return carry

    lax.fori_loop(0, count // ROW_UNROLL, issue, 0)


def _wait_row_copies(x_ref, xs_ref, sem, count):
    for _ in range(TOP_K):
        pltpu.make_async_copy(x_ref.at[pl.ds(0, count)], xs_ref.at[pl.ds(0, count)], sem).wait()


def _dispatch_body(dest_ref, x_ref, xs_prev_ref, xs_ref, sem):
    del xs_prev_ref
    ts = x_ref.shape[0]
    _start_row_copies(dest_ref, x_ref, xs_ref, sem, 0, ts)
    _wait_row_copies(x_ref, xs_ref, sem, ts)


def _dispatch(x_tiles, dest, xs_prev, ts):
    n = x_tiles.shape[0]
    return pl.pallas_call(
        _dispatch_body,
        name="dispatch",
        grid=(n // ts,),
        in_specs=[pl.BlockSpec((TOP_K, ts), lambda i: (0, i), memory_space=pltpu.SMEM),
                  pl.BlockSpec((ts, PACK_TILE, LANES), lambda i: (i, 0, 0)),
                  pl.BlockSpec(memory_space=pl.ANY)],
        out_specs=pl.BlockSpec(memory_space=pl.ANY),
        out_shape=jax.ShapeDtypeStruct(xs_prev.shape, xs_prev.dtype),
        scratch_shapes=[pltpu.SemaphoreType.DMA],
        input_output_aliases={2: 0},
        compiler_params=_params(("arbitrary",)),
    )(dest, x_tiles, xs_prev)


def _expert_body(be_ref, nb_ref, valid_ref, xs_ref, wg_ref, wu_ref, wd_ref, ys_ref, wg_b, wu_b,
                 wd_b):
    del nb_ref
    i = pl.program_id(0)
    valid = valid_ref[i]
    half = EXPERT_BLOCK // 2

    @pl.when(jnp.logical_or(i == 0, be_ref[i] != be_ref[jnp.maximum(i - 1, 0)]))
    def _():
        wg_b[...] = wg_ref[...].astype(BF16)
        wu_b[...] = wu_ref[...].astype(BF16)
        wd_b[...] = wd_ref[...].astype(BF16)

    def run(m):
        x = _load_packed_tiles(xs_ref, m)
        h = (_silu(_dot(x, wg_b[...])) * _dot(x, wu_b[...])).astype(BF16)
        _store_row_tiles(ys_ref, _dot(h, wd_b[...]))

    @pl.when(valid > half)
    def _():
        run(EXPERT_BLOCK)

    @pl.when(jnp.logical_and(valid > 0, valid <= half))
    def _():
        run(half)


def _experts(xs, block_e, nb_used, valid, wg, wu, wd):
    rows = xs.shape[0] // PACK_TILE
    n_blocks = rows // EXPERT_BLOCK
    used = lambda i, be, nb, va: (jnp.minimum(i, nb[0] - 1), 0)
    weights = lambda i, be, nb, va: (be[i], 0, 0)
    in_rows = pl.BlockSpec((EXPERT_BLOCK * PACK_TILE, LANES), used)
    grid_spec = pltpu.PrefetchScalarGridSpec(
        num_scalar_prefetch=3,
        grid=(n_blocks,),
        in_specs=[in_rows,
                  pl.BlockSpec((None, D_MODEL, D_EXPERT), weights),
                  pl.BlockSpec((None, D_MODEL, D_EXPERT), weights),
                  pl.BlockSpec((None, D_EXPERT, D_MODEL), weights)],
        out_specs=pl.BlockSpec((EXPERT_BLOCK * ROW_TILE, LANES), used),
        scratch_shapes=[pltpu.VMEM((D_MODEL, D_EXPERT), BF16), pltpu.VMEM((D_MODEL, D_EXPERT), BF16),
                        pltpu.VMEM((D_EXPERT, D_MODEL), BF16)],
    )
    return pl.pallas_call(
        _expert_body,
        name="experts",
        grid_spec=grid_spec,
        out_shape=jax.ShapeDtypeStruct((rows * ROW_TILE, LANES), F32),
        compiler_params=_params(("arbitrary",)),
    )(block_e, nb_used, valid, xs, wg, wu, wd)


def _combine_body(dest_ref, dnext_ref, wts_ref, x1_ref, ys_ref, wsg, wsu, wsd, g_ref, b_ref,
                  y_ref, buf, sems):
    i = pl.program_id(0)
    n_tiles = pl.num_programs(0)
    tc = x1_ref.shape[0]
    slot = i % 2

    def issue(d_ref, s, g, carry):
        for u in range(ROW_UNROLL):
            t = g * ROW_UNROLL + u
            for k in range(TOP_K):
                pltpu.make_async_copy(_token_tile(ys_ref, d_ref[k, t]),
                                      _token_tile(buf.at[s, k], t),
                                      sems.at[s]).start(priority=k % 2)
        return carry

    n_groups = tc // ROW_UNROLL

    @pl.when(i == 0)
    def _():
        lax.fori_loop(0, n_groups, functools.partial(issue, dest_ref, slot), 0)

    @pl.when(i + 1 < n_tiles)
    def _():
        lax.fori_loop(0, n_groups, functools.partial(issue, dnext_ref, 1 - slot), 0)

    x1 = x1_ref[...]
    xb = x1.astype(BF16)
    hs = (_silu(_dot(xb, wsg[...])) * _dot(xb, wsu[...])).astype(BF16)
    shared = _dot(hs, wsd[...])

    for k in range(TOP_K):
        pltpu.make_async_copy(ys_ref.at[pl.ds(0, tc * ROW_TILE)], buf.at[slot, k],
                              sems.at[slot]).wait()

    w = wts_ref[...]
    routed = w[:, 0:1] * _load_row_tiles(buf.at[slot, 0], tc)
    for k in range(1, TOP_K):
        routed = routed + w[:, k:k + 1] * _load_row_tiles(buf.at[slot, k], tc)
    y_ref[...] = _layer_norm(DEEPNORM_ALPHA * x1 + (routed + shared), g_ref[...], b_ref[...])


def _combine(dest, wts_t, x1, ys, wsg, wsu, wsd, ln_g, ln_b, tc):
    n = x1.shape[0]
    last = n // tc - 1
    dest_cur = pl.BlockSpec((TOP_K, tc), lambda i: (0, i), memory_space=pltpu.SMEM)
    dest_next = pl.BlockSpec((TOP_K, tc), lambda i: (0, jnp.minimum(i + 1, last)),
                             memory_space=pltpu.SMEM)
    full = lambda a: pl.BlockSpec(a.shape, lambda i: (0, 0))
    rows = lambda w: pl.BlockSpec((tc, w), lambda i: (i, 0))
    row_bytes = D_MODEL * 4
    need = tc * row_bytes * (2 * TOP_K + 4) + 14 * 1024 * 1024
    return pl.pallas_call(
        _combine_body,
        name="combine",
        grid=(n // tc,),
        in_specs=[dest_cur, dest_next, rows(LANES), rows(D_MODEL),
                  pl.BlockSpec(memory_space=pl.ANY),
                  full(wsg), full(wsu), full(wsd), full(ln_g), full(ln_b)],
        out_specs=rows(D_MODEL),
        out_shape=jax.ShapeDtypeStruct((n, D_MODEL), F32),
        scratch_shapes=[pltpu.VMEM((2, TOP_K, tc * ROW_TILE, LANES), F32),
                        pltpu.SemaphoreType.DMA((2,))],
        compiler_params=_params(("arbitrary",), max(VMEM_LIMIT, need)),
    )(dest, dest, wts_t, x1, ys, wsg, wsu, wsd, ln_g, ln_b)


def _segments(cnt_first, n_first, reserve):
    blk = EXPERT_BLOCK
    assert reserve <= blk
    n_blocks = -(-(n_first * TOP_K + N_EXPERTS * reserve) // blk) + N_EXPERTS
    counts = cnt_first[:, 0].astype(I32)
    padded = (counts + reserve + blk - 1) // blk * blk
    ends = jnp.cumsum(padded)
    pstart = ends - padded
    nb_used = (ends[-1] // blk).astype(I32).reshape(1)
    block_row0 = jnp.arange(n_blocks, dtype=I32) * blk
    block_e = jnp.minimum(jnp.sum((ends[None, :] <= block_row0[:, None]).astype(I32), axis=1),
                          N_EXPERTS - 1).astype(I32)
    return pstart, counts, block_e, nb_used, n_blocks * blk


def kernel(x_prompt, x_sample, cache_k, cache_v, state_conv, page_table, w_in, lam_q1, lam_k1,
           lam_q2, lam_k2, subln_g, w_dw, b_dw, conv_ln_g, conv_ln_b, w_o, ln1_g, ln1_b,
           w_router, b_router, w_gate, w_up, w_down, w_sh_gate, w_sh_up, w_sh_down, ln2_g,
           ln2_b):
    batch, seq, _ = x_prompt.shape
    dec_b, dec_s, _ = x_sample.shape
    n_p, n_s = batch * seq, dec_b * dec_s
    layer = 0
    lam_init = _lambda_init(layer)
    row = lambda a: a[layer].reshape(1, -1)

    w_in_b = w_in[layer].astype(BF16)
    wo = w_o[layer].astype(BF16)
    wo_a, wo_c = wo[:ATTN_WIDTH], wo[ATTN_WIDTH:]
    wr_t = w_router[layer].T
    wr_hi = wr_t.astype(BF16)
    wr_lo = (wr_t - wr_hi.astype(F32)).astype(BF16)
    br = b_router[layer].reshape(-1, 1)
    moe_w = (w_gate[layer], w_up[layer], w_down[layer]) + tuple(
        w[layer].astype(BF16) for w in (w_sh_gate, w_sh_up, w_sh_down))
    lam_vecs = (row(lam_q1), row(lam_k1), row(lam_q2), row(lam_k2))
    g_sub = row(subln_g)
    conv_w = (w_dw[layer], row(b_dw), row(conv_ln_g), row(conv_ln_b))
    ln1 = (row(ln1_g), row(ln1_b))
    ln2 = (row(ln2_g), row(ln2_b))

    xp = x_prompt.reshape(n_p, D_MODEL)
    wq_t = w_in_b[:, :QK_WIDTH].T
    wv_t = w_in_b[:, 2 * QK_WIDTH:2 * QK_WIDTH + ATTN_WIDTH].T
    qt_p, k_p, v_p, u_p, kb_p, vt_p = _inproj(xp, w_in_b, ATTN_TILE, (wq_t, wv_t))
    o_p = _attn_prompt(qt_p, kb_p, vt_p, lam_vecs, g_sub.reshape(V_DIM, 1), lam_init, batch, seq)
    c_p = _conv_prompt(u_p, *conv_w, batch, seq, 512)
    cnt0 = jnp.zeros((N_EXPERTS, LANES), F32)
    x1_p, xt_p, idx_p, wts_p, rank_p, cnt_p = _mix(o_p, c_p, xp, wo_a, wo_c, *ln1, wr_hi, wr_lo,
                                                   br, cnt0, 512)
    pstart, counts_p, block_e, nb_used, n_rows = _segments(cnt_p, n_p, n_s)
    pstart_col = pstart.astype(F32).reshape(N_EXPERTS, 1)
    dest_p = _dest(idx_p, rank_p, pstart_col, 256)

    xs = x_sample.reshape(n_s, D_MODEL)
    q_s, k_s, v_s, u_s = _inproj(xs, w_in_b, n_s)
    pool = cache_k.shape[1]
    page_rows = lambda a: a[layer].reshape(pool, PAGE_ROWS, V_DIM)
    new_rows = lambda a: a.reshape(dec_b, dec_s * N_HEADS, V_DIM)
    o_s, rows_in = _attn_sample(
        q_s.reshape(dec_b, dec_s, QK_WIDTH), new_rows(k_s), new_rows(v_s), page_rows(cache_k),
        page_rows(cache_v), page_table, lam_vecs, g_sub, lam_init,
        xt_p.reshape(n_p, PACK_TILE, LANES), dest_p, pstart, counts_p, n_rows, n_s)
    c_s, st_s = _conv_sample(u_s.reshape(dec_b, dec_s, CONV_CH), state_conv[layer], *conv_w)
    x1_s, xt_s, idx_s, wts_s, rank_s, cnt_all = _mix(
        o_s.reshape(n_s, ATTN_WIDTH), c_s.reshape(n_s, CONV_CH), xs, wo_a, wo_c, *ln1, wr_hi,
        wr_lo, br, cnt_p, n_s)
    dest_s = _dest(idx_s, rank_s, pstart_col, 128)
    rows_in = _dispatch(xt_s.reshape(n_s, PACK_TILE, LANES), dest_s, rows_in, 128)

    wg, wu, wd, wsg, wsu, wsd = moe_w
    block_first = jnp.arange(n_rows // EXPERT_BLOCK, dtype=I32) * EXPERT_BLOCK
    seg_rows = cnt_all[:, 0].astype(I32)[block_e] - (block_first - pstart[block_e])
    valid = jnp.where(block_first < nb_used[0] * EXPERT_BLOCK,
                      jnp.clip(seg_rows, 0, EXPERT_BLOCK), 0).astype(I32)
    rows_out = _experts(rows_in.reshape(n_rows * PACK_TILE, LANES), block_e, nb_used, valid, wg,
                        wu, wd)
    y_p = _combine(dest_p, wts_p, x1_p, rows_out, wsg, wsu, wsd, *ln2, 512)
    y_s = _combine(dest_s, wts_s, x1_s, rows_out, wsg, wsu, wsd, *ln2, 128)

    keep = CONV_WIDTH - 1
    u_p3 = u_p.reshape(batch, seq, CONV_CH)
    return (y_p.reshape(batch, seq, D_MODEL),
            y_s.reshape(dec_b, dec_s, D_MODEL),
            k_p.reshape(1, batch, seq, N_HEADS, V_DIM),
            v_p.reshape(1, batch, seq, N_HEADS, V_DIM),
            u_p3[:, seq - keep:, :][None],
            k_s.reshape(1, dec_b, dec_s, N_HEADS, V_DIM),
            v_s.reshape(1, dec_b, dec_s, N_HEADS, V_DIM),
            st_s[None])
```

```python
import functools
import math

import jax
import jax.numpy as jnp
from jax import lax
from jax.experimental import pallas as pl
from jax.experimental.pallas import tpu as pltpu

F32 = jnp.float32
BF16 = jnp.bfloat16
I32 = jnp.int32
U32 = jnp.uint32

D_MODEL = 1024
N_HEADS = 4
HEAD_DIM = 64
V_DIM = 128
ATTN_WIDTH = N_HEADS * V_DIM
QK_WIDTH = N_HEADS * 2 * HEAD_DIM
ATTN_SCALE = HEAD_DIM ** -0.5
LOG2_E = math.log2(math.e)
CONV_CH = D_MODEL - ATTN_WIDTH
CONV_WIDTH = 31
N_EXPERTS = 64
N_GROUPS = 8
GROUP_SIZE = N_EXPERTS // N_GROUPS
TOPK_GROUPS = 4
TOP_K = 8
D_EXPERT = D_MODEL // 4
ROUTED_SCALE = 2.5
DEPTH = 1
DEEPNORM_ALPHA = (2 * DEPTH) ** 0.25
NORM_EPS = 1e-5
NEG_INF = -1e30
PAGE_SIZE = 128

LANES = 128
ROW_TILE = D_MODEL // LANES
PACK_TILE = ROW_TILE // 2
VMEM_LIMIT = 48 * 1024 * 1024
PAGES_PER_STEP = 32
ATTN_TILE = 512
PAGE_ROWS = PAGE_SIZE * N_HEADS
EXPERT_BLOCK = 1024
HIST = 32


def _lambda_init(layer):
    return 0.8 - 0.6 * math.exp(-0.3 * layer)


def _sigmoid(x):
    return 1.0 / (1.0 + jnp.exp(-x))


def _silu(x):
    return x * _sigmoid(x)


def _layer_norm(x, g, b):
    mu = jnp.mean(x, axis=-1, keepdims=True)
    xc = x - mu
    var = jnp.mean(xc * xc, axis=-1, keepdims=True)
    return xc * lax.rsqrt(var + NORM_EPS) * g + b


def _dot(a, b):
    return jnp.dot(a, b, preferred_element_type=F32)


def _dot_nt(a, b):
    return lax.dot_general(a, b, (((1,), (1,)), ((), ())), preferred_element_type=F32)


def _diff_lambda(lq1, lk1, lq2, lk2, lam_init):
    a = jnp.exp(jnp.sum(lq1 * lk1, axis=-1, keepdims=True))
    b = jnp.exp(jnp.sum(lq2 * lk2, axis=-1, keepdims=True))
    return a - b + lam_init


def _params(dims, vmem_limit=VMEM_LIMIT):
    return pltpu.CompilerParams(dimension_semantics=dims, vmem_limit_bytes=vmem_limit)


def _store_row_tiles(ref, x):
    m = x.shape[0]
    for j in range(ROW_TILE):
        ref[pl.ds(j, m, stride=ROW_TILE), :] = x[:, j * LANES:(j + 1) * LANES]


def _load_row_tiles(ref, m):
    return jnp.concatenate([ref[pl.ds(j, m, stride=ROW_TILE), :] for j in range(ROW_TILE)],
                           axis=1)


def _token_tile(ref, t):
    return ref.at[pl.ds(pl.multiple_of(t * ROW_TILE, ROW_TILE), ROW_TILE)]


def _store_packed_tiles(ref, x):
    m = x.shape[0]
    bits = pltpu.bitcast(x.astype(BF16).astype(F32), U32)
    half = D_MODEL // 2
    for j in range(PACK_TILE):
        lo = bits[:, j * LANES:(j + 1) * LANES] >> 16
        hi = bits[:, half + j * LANES:half + (j + 1) * LANES]
        ref[pl.ds(j, m, stride=PACK_TILE), :] = lo | hi


def _load_packed_tiles(ref, m):
    words = [ref[pl.ds(j, m, stride=PACK_TILE), :] for j in range(PACK_TILE)]
    lo = [pltpu.bitcast(w << 16, F32) for w in words]
    hi = [pltpu.bitcast(w & jnp.uint32(0xFFFF0000), F32) for w in words]
    return jnp.concatenate(lo + hi, axis=1).astype(BF16)


def _store_head_rows(ref, x):
    m = x.shape[0]
    for h in range(N_HEADS):
        ref[pl.ds(h, m, stride=N_HEADS), :] = x[:, h * V_DIM:(h + 1) * V_DIM]


def _inproj_body(x_ref, w_ref, *refs):
    xb = x_ref[...].astype(BF16)

    def mm(c0):
        return _dot(xb, w_ref[:, c0:c0 + QK_WIDTH])

    k = mm(QK_WIDTH)
    if len(refs) == 4:
        q_ref, k_ref, v_ref, u_ref = refs
        q_ref[...] = mm(0) * ATTN_SCALE
    else:
        wqt_ref, wvt_ref, q_ref, k_ref, v_ref, u_ref, kb_ref, vt_ref = refs
        q_t = _dot_nt(wqt_ref[...], xb) * (ATTN_SCALE * LOG2_E)
        v_t = _dot_nt(wvt_ref[...], xb).astype(BF16)
        for c in range(q_ref.shape[0]):
            cols = slice(c * ATTN_TILE, (c + 1) * ATTN_TILE)
            q_ref[c] = q_t[:, cols]
            vt_ref[c] = v_t[:, cols]
        kb_ref[...] = k.astype(BF16)
    _store_head_rows(k_ref, k)
    _store_head_rows(v_ref, mm(2 * QK_WIDTH))
    a = mm(2 * QK_WIDTH + ATTN_WIDTH)
    b = mm(2 * QK_WIDTH + ATTN_WIDTH + CONV_CH)
    u_ref[...] = a * _sigmoid(b)


def _inproj(x, w_in_b, tm, transposed_w=()):
    n = x.shape[0]
    wide = QK_WIDTH
    blk = pl.BlockSpec((tm, wide), lambda i: (i, 0))
    hblk = pl.BlockSpec((tm * N_HEADS, V_DIM), lambda i: (i, 0))
    full = lambda a: pl.BlockSpec(a.shape, lambda i: (0, 0))
    row_major = jax.ShapeDtypeStruct((n, wide), F32)
    head_rows = jax.ShapeDtypeStruct((n * N_HEADS, V_DIM), F32)
    out_specs = [blk, hblk, hblk, blk]
    out_shape = [row_major, head_rows, head_rows, row_major]
    if transposed_w:
        t_blk = pl.BlockSpec((tm // ATTN_TILE, wide, ATTN_TILE), lambda i: (i, 0, 0))
        t_shape = (n // ATTN_TILE, wide, ATTN_TILE)
        out_specs = [t_blk, hblk, hblk, blk, blk, t_blk]
        out_shape = [jax.ShapeDtypeStruct(t_shape, F32), head_rows, head_rows,
                     row_major, jax.ShapeDtypeStruct((n, wide), BF16),
                     jax.ShapeDtypeStruct(t_shape, BF16)]
    return pl.pallas_call(
        _inproj_body,
        name="inproj",
        grid=(n // tm,),
        in_specs=[pl.BlockSpec((tm, D_MODEL), lambda i: (i, 0)), full(w_in_b)]
        + [full(w) for w in transposed_w],
        out_specs=out_specs,
        out_shape=out_shape,
        compiler_params=_params(("parallel",)),
    )(x, w_in_b, *transposed_w)


def _attn_p_body(lam_init, q_ref, k_ref, v_ref, lq1, lk1, lq2, lk2, g_ref, o_ref, acc_ref):
    qi = pl.program_id(2)
    tq = q_ref.shape[1]
    tk = v_ref.shape[2]
    q_t = q_ref[...]
    feat = lax.broadcasted_iota(I32, q_t.shape, 0)
    q_both = jnp.concatenate([jnp.where(feat < HEAD_DIM, q_t, 0.0),
                              jnp.where(feat >= HEAD_DIM, q_t, 0.0)], axis=1).astype(BF16)
    acc_ref[...] = jnp.zeros(acc_ref.shape, F32)

    def chunk(masked, j, carry):
        m_old, l_old = carry
        kc = k_ref[pl.ds(pl.multiple_of(j * tk, tk), tk), :]
        s_t = _dot(kc, q_both)
        if masked:
            key = lax.broadcasted_iota(I32, s_t.shape, 0)
            qry = lax.broadcasted_iota(I32, s_t.shape, 1) % tq
            s_t = jnp.where(key <= qry, s_t, NEG_INF)
        m_new = jnp.maximum(m_old, jnp.max(s_t, axis=0, keepdims=True))
        alpha = jnp.exp2(m_old - m_new)
        p_t = jnp.exp2(s_t - m_new)
        l_new = alpha * l_old + jnp.sum(p_t, axis=0, keepdims=True)
        acc_ref[...] = alpha * acc_ref[...] + _dot(v_ref[j], p_t.astype(BF16))
        return m_new, l_new

    init = (jnp.full((1, 2 * tq), -jnp.inf, F32), jnp.zeros((1, 2 * tq), F32))
    carry = lax.fori_loop(0, qi, functools.partial(chunk, False), init)
    _, l_all = chunk(True, qi, carry)

    lam = _diff_lambda(lq1[...], lk1[...], lq2[...], lk2[...], lam_init)
    on = acc_ref[...] / l_all
    o_t = on[:, :tq] - lam * on[:, tq:]
    ms = jnp.mean(o_t * o_t, axis=0, keepdims=True)
    o_t = o_t * lax.rsqrt(ms + NORM_EPS) * g_ref[...] * (1.0 - lam_init)
    o_ref[...] = o_t.T


def _attn_prompt(q_t, kb, v_t, lam_vecs, subln_g_col, lam_init, batch, seq):
    t = q_t.shape[2]
    n = kb.shape[0]
    nq = seq // t
    small = lambda w: pl.BlockSpec((1, w), lambda b, h, i: (0, 0))
    return pl.pallas_call(
        functools.partial(_attn_p_body, lam_init),
        name="attn_prompt",
        grid=(batch, N_HEADS, nq),
        in_specs=[pl.BlockSpec((None, V_DIM, t), lambda b, h, i: (b * nq + i, h, 0)),
                  pl.BlockSpec((seq, V_DIM), lambda b, h, i: (b, h)),
                  pl.BlockSpec((nq, V_DIM, t), lambda b, h, i: (b, h, 0))]
        + [small(HEAD_DIM)] * 4 + [pl.BlockSpec((V_DIM, 1), lambda b, h, i: (0, 0))],
        out_specs=pl.BlockSpec((t, V_DIM), lambda b, h, i: (b * nq + i, h)),
        out_shape=jax.ShapeDtypeStruct((n, ATTN_WIDTH), F32),
        scratch_shapes=[pltpu.VMEM((V_DIM, 2 * t), F32)],
        compiler_params=_params(("parallel", "parallel", "parallel")),
    )(q_t, kb, v_t, *lam_vecs, subln_g_col)


def _attn_s_body(lam_init, n_steps, tokens_per_step, reserve, pt_ref, ps_ref, cnt_ref, q_ref,
                 kn_ref, vn_ref, *rest):
    del pt_ref
    npg = PAGES_PER_STEP
    k_refs = rest[:npg]
    v_refs = rest[npg:2 * npg]
    (lq1, lk1, lq2, lk2, g_ref, dest_ref, x_ref, o_ref, xs_ref, qall, knew, vnew, m_ref, l_ref,
     acc_ref, zbuf, sem, zsem) = rest[2 * npg:]
    j = pl.program_id(1)
    ds = q_ref.shape[0]
    hr = 2 * ds
    step = pl.program_id(0) * n_steps + j

    @pl.when(step == 0)
    def _():
        _zero_segment_tails(ps_ref, cnt_ref, xs_ref, zbuf, zsem, reserve)

    first_token = (step % (x_ref.shape[0] // tokens_per_step)) * tokens_per_step
    _start_row_copies(dest_ref, x_ref, xs_ref, sem, first_token, tokens_per_step)

    def head_rows(ref, h):
        return ref[pl.ds(h, PAGE_SIZE, stride=N_HEADS), :].astype(BF16)

    @pl.when(j == 0)
    def _():
        pieces = []
        for h in range(N_HEADS):
            qh = q_ref[:, h * V_DIM:(h + 1) * V_DIM]
            lane = lax.broadcasted_iota(I32, qh.shape, 1)
            pieces.append(jnp.where(lane < HEAD_DIM, qh, 0.0))
            pieces.append(jnp.where(lane >= HEAD_DIM, qh, 0.0))
        qa = jnp.concatenate(pieces, axis=0).astype(BF16)
        qall[...] = qa
        knew[...] = jnp.zeros(knew.shape, F32)
        vnew[...] = jnp.zeros(vnew.shape, F32)
        knew[0:ds * N_HEADS] = kn_ref[...]
        vnew[0:ds * N_HEADS] = vn_ref[...]
        row = lax.broadcasted_iota(I32, (N_HEADS * hr, PAGE_SIZE), 0)
        key = lax.broadcasted_iota(I32, (N_HEADS * hr, PAGE_SIZE), 1)
        head_slices = [slice(h * hr, (h + 1) * hr) for h in range(N_HEADS)]
        s = jnp.concatenate([_dot_nt(qa[r], head_rows(knew, h))
                             for h, r in enumerate(head_slices)], axis=0)
        s = jnp.where(key <= (row % ds), s, NEG_INF)
        m = jnp.max(s, axis=-1, keepdims=True)
        p = jnp.exp(s - m)
        pb = p.astype(BF16)
        m_ref[...] = m
        l_ref[...] = jnp.sum(p, axis=-1, keepdims=True)
        acc_ref[...] = jnp.concatenate([_dot(pb[r], head_rows(vnew, h))
                                        for h, r in enumerate(head_slices)], axis=0)

    q_all, m_all, l_all, acc_all = qall[...], m_ref[...], l_ref[...], acc_ref[...]
    heads = range(N_HEADS)
    rows = [slice(h * hr, (h + 1) * hr) for h in heads]
    s = [_dot_nt(q_all[rows[h]],
                 jnp.concatenate([head_rows(k_refs[i], h) for i in range(npg)], axis=0))
         for h in heads]
    s = jnp.concatenate(s, axis=0)
    m_new = jnp.maximum(m_all, jnp.max(s, axis=-1, keepdims=True))
    alpha = jnp.exp(m_all - m_new)
    p = jnp.exp(s - m_new)
    l_new = alpha * l_all + jnp.sum(p, axis=-1, keepdims=True)
    pb = p.astype(BF16)
    pv = [_dot(pb[rows[h]],
               jnp.concatenate([head_rows(v_refs[i], h) for i in range(npg)], axis=0))
          for h in heads]
    acc_new = alpha * acc_all + jnp.concatenate(pv, axis=0)
    m_ref[...] = m_new
    l_ref[...] = l_new
    acc_ref[...] = acc_new
    _wait_row_copies(x_ref, xs_ref, sem, tokens_per_step)

    @pl.when(j == n_steps - 1)
    def _():
        lam = _diff_lambda(lq1[...], lk1[...], lq2[...], lk2[...], lam_init)
        on = acc_new / l_new
        for h in range(N_HEADS):
            r0 = h * hr
            o = on[r0:r0 + ds] - lam * on[r0 + ds:r0 + 2 * ds]
            ms = jnp.mean(o * o, axis=-1, keepdims=True)
            o_ref[:, h * V_DIM:(h + 1) * V_DIM] = (
                o * lax.rsqrt(ms + NORM_EPS) * g_ref[...] * (1.0 - lam_init))


def _attn_sample(q, k_new, v_new, cache_k, cache_v, page_table, lam_vecs, subln_g, lam_init,
                 x_tiles, dest, pstart, counts, n_rows, reserve):
    b, ds, _ = q.shape
    n_pages = page_table.shape[1]
    n_steps = n_pages // PAGES_PER_STEP
    nrow = N_HEADS * 2 * ds
    pt = page_table.reshape(-1)
    total_steps = b * n_steps
    n = x_tiles.shape[0]
    tokens_per_step = n // total_steps
    assert tokens_per_step * total_steps == n and LANES % tokens_per_step == 0
    per_block = LANES // tokens_per_step
    tile = (PACK_TILE, LANES)

    def page_map(i, bi, j, pt_ref, *_):
        return (pt_ref[bi * n_pages + j * PAGES_PER_STEP + i], 0, 0)

    page_specs = [pl.BlockSpec((None, PAGE_ROWS, V_DIM), functools.partial(page_map, i))
                  for i in range(PAGES_PER_STEP)]
    token_block = lambda bi, j, *_: (bi * n_steps + j) // per_block
    small = lambda w: pl.BlockSpec((1, w), lambda bi, j, *_: (0, 0))
    new_spec = pl.BlockSpec((None, ds * N_HEADS, V_DIM), lambda bi, j, *_: (bi, 0, 0))
    qo_spec = pl.BlockSpec((None, ds, ATTN_WIDTH), lambda bi, j, *_: (bi, 0, 0))
    grid_spec = pltpu.PrefetchScalarGridSpec(
        num_scalar_prefetch=3,
        grid=(b, n_steps),
        in_specs=[qo_spec, new_spec, new_spec] + page_specs + page_specs
        + [small(HEAD_DIM)] * 4 + [small(V_DIM)]
        + [pl.BlockSpec((TOP_K, LANES), lambda bi, j, *_: (0, token_block(bi, j)),
                        memory_space=pltpu.SMEM),
           pl.BlockSpec((LANES,) + tile, lambda bi, j, *_: (token_block(bi, j), 0, 0))],
        out_specs=[qo_spec, pl.BlockSpec(memory_space=pl.ANY)],
        scratch_shapes=[pltpu.VMEM((nrow, V_DIM), BF16),
                        pltpu.VMEM((PAGE_ROWS, V_DIM), F32),
                        pltpu.VMEM((PAGE_ROWS, V_DIM), F32),
                        pltpu.VMEM((nrow, 1), F32),
                        pltpu.VMEM((nrow, 1), F32),
                        pltpu.VMEM((nrow, V_DIM), F32),
                        pltpu.VMEM((TAIL_PIECES[0],) + tile, U32),
                        pltpu.SemaphoreType.DMA, pltpu.SemaphoreType.DMA],
    )
    return pl.pallas_call(
        functools.partial(_attn_s_body, lam_init, n_steps, tokens_per_step, reserve),
        name="attn_sample",
        grid_spec=grid_spec,
        out_shape=[jax.ShapeDtypeStruct((b, ds, ATTN_WIDTH), F32),
                   jax.ShapeDtypeStruct((n_rows,) + tile, U32)],
        compiler_params=_params(("arbitrary", "arbitrary")),
    )(pt, pstart, counts, q, k_new, v_new, *([cache_k] * PAGES_PER_STEP),
      *([cache_v] * PAGES_PER_STEP), *lam_vecs, subln_g, dest, x_tiles)


def _conv_taps(ext_ref, w_ref, bias, rows):
    off = HIST - (CONV_WIDTH - 1)
    acc = jnp.broadcast_to(bias, (rows, CONV_CH))
    for jt in range(CONV_WIDTH):
        acc = acc + w_ref[jt:jt + 1, :] * ext_ref[pl.ds(jt + off, rows), :]
    return acc


CONV_ROWS = 64
SUBLANES = 8


def _conv_p_body(u_ref, h_ref, w_ref, b_ref, g_ref, beta_ref, c_ref, ext_ref, sh_ref):
    t = pl.program_id(1)
    tm = u_ref.shape[0]
    hist = h_ref[...]
    ext_ref[0:HIST] = jnp.where(t == 0, jnp.zeros_like(hist), hist)
    ext_ref[HIST:HIST + tm] = u_ref[...]
    span = sh_ref.shape[1]
    for r in range(1, SUBLANES):
        sh_ref[r - 1] = ext_ref[pl.ds(r, span), :]
    off = HIST - (CONV_WIDTH - 1)
    bias = b_ref[...]
    for c0 in range(0, tm, CONV_ROWS):
        acc = jnp.broadcast_to(bias, (CONV_ROWS, CONV_CH))
        for jt in range(CONV_WIDTH):
            r = (jt + off) % SUBLANES
            base = c0 + jt + off - r
            src = ext_ref if r == 0 else sh_ref.at[r - 1]
            acc = acc + w_ref[jt:jt + 1, :] * src[pl.ds(base, CONV_ROWS), :]
        c_ref[c0:c0 + CONV_ROWS, :] = _silu(_layer_norm(acc, g_ref[...], beta_ref[...]))


def _conv_prompt(u, w_dw, b_dw, g, beta, batch, seq, tm):
    n = u.shape[0]
    nt = seq // tm
    per = tm // HIST
    cur = lambda b, t: (b * nt + t, 0)
    prev = lambda b, t: (jnp.maximum((b * nt + t) * per - 1, 0), 0)
    small = lambda r: pl.BlockSpec((r, CONV_CH), lambda b, t: (0, 0))
    return pl.pallas_call(
        _conv_p_body,
        name="conv_prompt",
        grid=(batch, nt),
        in_specs=[pl.BlockSpec((tm, CONV_CH), cur), pl.BlockSpec((HIST, CONV_CH), prev),
                  small(CONV_WIDTH), small(1), small(1), small(1)],
        out_specs=pl.BlockSpec((tm, CONV_CH), cur),
        out_shape=jax.ShapeDtypeStruct((n, CONV_CH), F32),
        scratch_shapes=[pltpu.VMEM((HIST + tm, CONV_CH), F32),
                        pltpu.VMEM((SUBLANES - 1, HIST + tm - SUBLANES, CONV_CH), F32)],
        compiler_params=_params(("parallel", "parallel")),
    )(u, u, w_dw, b_dw, g, beta)


def _conv_s_body(u_ref, st_ref, w_ref, b_ref, g_ref, beta_ref, c_ref, ns_ref, ext_ref):
    ds = u_ref.shape[0]
    keep = CONV_WIDTH - 1
    off = HIST - keep
    ext_ref[0:off] = jnp.zeros((off, CONV_CH), F32)
    ext_ref[off:HIST] = st_ref[...]
    ext_ref[HIST:HIST + ds] = u_ref[...]
    y = _conv_taps(ext_ref, w_ref, b_ref[...], ds)
    c_ref[...] = _silu(_layer_norm(y, g_ref[...], beta_ref[...]))
    ns_ref[...] = ext_ref[HIST + ds - keep:HIST + ds]


def _conv_sample(u, state, w_dw, b_dw, g, beta):
    b, ds, _ = u.shape
    keep = CONV_WIDTH - 1
    per_b = lambda r: pl.BlockSpec((None, r, CONV_CH), lambda i: (i, 0, 0))
    small = lambda r: pl.BlockSpec((r, CONV_CH), lambda i: (0, 0))
    return pl.pallas_call(
        _conv_s_body,
        name="conv_sample",
        grid=(b,),
        in_specs=[per_b(ds), per_b(keep), small(CONV_WIDTH), small(1), small(1), small(1)],
        out_specs=[per_b(ds), per_b(keep)],
        out_shape=[jax.ShapeDtypeStruct((b, ds, CONV_CH), F32),
                   jax.ShapeDtypeStruct((b, keep, CONV_CH), F32)],
        scratch_shapes=[pltpu.VMEM((HIST + ds, CONV_CH), F32)],
        compiler_params=_params(("parallel",)),
    )(u, state, w_dw, b_dw, g, beta)


def _first_index(hit, iota, limit, axis):
    return jnp.min(jnp.where(hit, iota, limit), axis=axis, keepdims=True)


def _route(x1, wr_hi, wr_lo, bias):
    tm = x1.shape[0]
    x_hi = x1.astype(BF16)
    x_lo = (x1 - x_hi.astype(F32)).astype(BF16)
    both = _dot_nt(jnp.concatenate([wr_hi, wr_lo], axis=0), x_hi)
    logits = both[:N_EXPERTS] + (both[N_EXPERTS:] + _dot_nt(wr_hi, x_lo))
    scores = _sigmoid(logits)
    sel = scores + bias
    sel3 = sel.reshape(N_GROUPS, GROUP_SIZE, tm)
    member = lax.broadcasted_iota(I32, sel3.shape, 1)
    m1 = jnp.max(sel3, axis=1, keepdims=True)
    i1 = _first_index(sel3 == m1, member, GROUP_SIZE, 1)
    m2 = jnp.max(jnp.where(member == i1, -jnp.inf, sel3), axis=1, keepdims=True)
    gs = jnp.broadcast_to(m1 + m2, sel3.shape).reshape(N_EXPERTS, tm)
    eiota = lax.broadcasted_iota(I32, (N_EXPERTS, tm), 0)
    giota = eiota // GROUP_SIZE
    gmask = jnp.zeros((N_EXPERTS, tm), jnp.bool_)
    for _ in range(TOPK_GROUPS):
        m = jnp.max(gs, axis=0, keepdims=True)
        gi = _first_index(gs == m, giota, N_GROUPS, 0)
        pick = giota == gi
        gmask = jnp.logical_or(gmask, pick)
        gs = jnp.where(pick, -jnp.inf, gs)
    selm = jnp.where(gmask, sel, NEG_INF)
    idx_rows, w_rows = [], []
    for _ in range(TOP_K):
        m = jnp.max(selm, axis=0, keepdims=True)
        ei = _first_index(selm == m, eiota, N_EXPERTS, 0)
        pick = eiota == ei
        idx_rows.append(ei)
        w_rows.append(jnp.sum(jnp.where(pick, scores, 0.0), axis=0, keepdims=True))
        selm = jnp.where(pick, -jnp.inf, selm)
    idx = jnp.concatenate(idx_rows, axis=0)
    w = jnp.concatenate(w_rows, axis=0)
    w = w / jnp.sum(w, axis=0, keepdims=True) * ROUTED_SCALE
    return idx, w


def _mix_body(o_ref, c_ref, x_ref, wo_a, wo_c, g_ref, b_ref, wrh_ref, wrl_ref, br_ref, cnt0_ref,
              x1_ref, xrt_ref, idx_ref, wts_ref, rank_ref, cnt_ref, base_ref):
    i = pl.program_id(0)
    tm = x_ref.shape[0]

    @pl.when(i == 0)
    def _():
        base_ref[...] = cnt0_ref[...]

    mix = _dot(o_ref[...].astype(BF16), wo_a[...]) + _dot(c_ref[...].astype(BF16), wo_c[...])
    x1 = _layer_norm(DEEPNORM_ALPHA * x_ref[...] + mix, g_ref[...], b_ref[...])
    x1_ref[...] = x1
    _store_packed_tiles(xrt_ref, x1)

    idx, w = _route(x1, wrh_ref[...], wrl_ref[...], br_ref[...])
    idx_ref[...] = idx
    w_rows = jnp.concatenate([w, jnp.zeros((LANES - TOP_K, tm), F32)], axis=0)
    wts_ref[...] = w_rows.T

    eiota = lax.broadcasted_iota(I32, (N_EXPERTS, tm), 0)
    chosen = jnp.zeros((N_EXPERTS, tm), F32)
    for k in range(TOP_K):
        chosen = chosen + jnp.where(eiota == idx[k:k + 1, :], 1.0, 0.0)
    r = lax.broadcasted_iota(I32, (tm, tm), 0)
    c = lax.broadcasted_iota(I32, (tm, tm), 1)
    before = jnp.where(r < c, 1.0, 0.0).astype(BF16)
    base = base_ref[:, 0:1]
    rank_full = _dot(chosen.astype(BF16), before) + base
    rows = [jnp.sum(jnp.where(eiota == idx[k:k + 1, :], rank_full, 0.0), axis=0, keepdims=True)
            for k in range(TOP_K)]
    rank_ref[...] = jnp.concatenate(rows, axis=0).astype(I32)
    total = base + jnp.sum(chosen, axis=1, keepdims=True)
    base_ref[...] = jnp.broadcast_to(total, base_ref.shape)
    cnt_ref[...] = jnp.broadcast_to(total, cnt_ref.shape)


def _mix(o, c, x, wo_a, wo_c, ln_g, ln_b, wr_hi, wr_lo, b_router, cnt0, tm):
    n = x.shape[0]
    rows = lambda w: pl.BlockSpec((tm, w), lambda i: (i, 0))
    full = lambda a: pl.BlockSpec(a.shape, lambda i: (0, 0))
    cols = pl.BlockSpec((TOP_K, tm), lambda i: (0, i))
    cnt = pl.BlockSpec((N_EXPERTS, LANES), lambda i: (0, 0))
    return pl.pallas_call(
        _mix_body,
        name="mix",
        grid=(n // tm,),
        in_specs=[rows(ATTN_WIDTH), rows(CONV_CH), rows(D_MODEL), full(wo_a), full(wo_c),
                  full(ln_g), full(ln_b), full(wr_hi), full(wr_lo), full(b_router), cnt],
        out_specs=[rows(D_MODEL), pl.BlockSpec((tm * PACK_TILE, LANES), lambda i: (i, 0)), cols,
                   rows(LANES), cols, cnt],
        out_shape=[jax.ShapeDtypeStruct((n, D_MODEL), F32),
                   jax.ShapeDtypeStruct((n * PACK_TILE, LANES), U32),
                   jax.ShapeDtypeStruct((TOP_K, n), I32),
                   jax.ShapeDtypeStruct((n, LANES), F32),
                   jax.ShapeDtypeStruct((TOP_K, n), I32),
                   jax.ShapeDtypeStruct((N_EXPERTS, LANES), F32)],
        scratch_shapes=[pltpu.VMEM((N_EXPERTS, LANES), F32)],
        compiler_params=_params(("arbitrary",)),
    )(o, c, x, wo_a, wo_c, ln_g, ln_b, wr_hi, wr_lo, b_router, cnt0)


def _dest_body(idx_ref, rank_ref, ps_ref, dest_ref):
    idx = idx_ref[...]
    tm = idx.shape[1]
    eiota = lax.broadcasted_iota(I32, (N_EXPERTS, tm), 0)
    ps = ps_ref[...]
    rows = [jnp.sum(jnp.where(eiota == idx[k:k + 1, :], ps, 0.0), axis=0, keepdims=True)
            for k in range(TOP_K)]
    dest_ref[...] = jnp.concatenate(rows, axis=0).astype(I32) + rank_ref[...]


def _dest(idx, rank, pstart_col, tm):
    n = idx.shape[1]
    cols = pl.BlockSpec((TOP_K, tm), lambda i: (0, i))
    return pl.pallas_call(
        _dest_body,
        name="dest",
        grid=(n // tm,),
        in_specs=[cols, cols, pl.BlockSpec((N_EXPERTS, 1), lambda i: (0, 0))],
        out_specs=cols,
        out_shape=jax.ShapeDtypeStruct((TOP_K, n), I32),
        compiler_params=_params(("parallel",)),
    )(idx, rank, pstart_col)


TAIL_PIECES = tuple(1 << s for s in reversed(range(EXPERT_BLOCK.bit_length())))
ROW_UNROLL = 8


def _zero_segment_tails(ps_ref, cnt_ref, xs_ref, zbuf, zsem, reserve):
    zbuf[...] = jnp.zeros(zbuf.shape, U32)

    def tail(wait, e, carry):
        cnt = cnt_ref[e]
        base = ps_ref[e] + cnt
        seg = (cnt + reserve + EXPERT_BLOCK - 1) // EXPERT_BLOCK * EXPERT_BLOCK
        pad = seg - cnt
        for p in TAIL_PIECES:
            @pl.when((pad & p) != 0)
            def _():
                cp = pltpu.make_async_copy(zbuf.at[pl.ds(0, p)], xs_ref.at[pl.ds(base, p)], zsem)
                if wait:
                    cp.wait()
                else:
                    cp.start()
            base = base + (pad & p)
        return carry

    lax.fori_loop(0, N_EXPERTS, functools.partial(tail, False), 0)
    lax.fori_loop(0, N_EXPERTS, functools.partial(tail, True), 0)


def _start_row_copies(dest_ref, x_ref, xs_ref, sem, first, count):
    def issue(g, carry):
        for u in range(ROW_UNROLL):
            t = first + g * ROW_UNROLL + u
            for k in range(TOP_K):
                pltpu.make_async_copy(x_ref.at[t], xs_ref.at[dest_ref[k, t]],
                                      sem).start(priority=k % 2)
        return carry

    lax.fori_loop(0, count // ROW_UNROLL, issue, 0)


def _wait_row_copies(x_ref, xs_ref, sem, count):
    for _ in range(TOP_K):
        pltpu.make_async_copy(x_ref.at[pl.ds(0, count)], xs_ref.at[pl.ds(0, count)], sem).wait()


def _dispatch_body(dest_ref, x_ref, xs_prev_ref, xs_ref, sem):
    del xs_prev_ref
    ts = x_ref.shape[0]
    _start_row_copies(dest_ref, x_ref, xs_ref, sem, 0, ts)
    _wait_row_copies(x_ref, xs_ref, sem, ts)


def _dispatch(x_tiles, dest, xs_prev, ts):
    n = x_tiles.shape[0]
    return pl.pallas_call(
        _dispatch_body,
        name="dispatch",
        grid=(n // ts,),
        in_specs=[pl.BlockSpec((TOP_K, ts), lambda i: (0, i), memory_space=pltpu.SMEM),
                  pl.BlockSpec((ts, PACK_TILE, LANES), lambda i: (i, 0, 0)),
                  pl.BlockSpec(memory_space=pl.ANY)],
        out_specs=pl.BlockSpec(memory_space=pl.ANY),
        out_shape=jax.ShapeDtypeStruct(xs_prev.shape, xs_prev.dtype),
        scratch_shapes=[pltpu.SemaphoreType.DMA],
        input_output_aliases={2: 0},
        compiler_params=_params(("arbitrary",)),
    )(dest, x_tiles, xs_prev)


def _expert_body(be_ref, nb_ref, xs_ref, wg_ref, wu_ref, wd_ref, ys_ref, wg_b, wu_b, wd_b):
    i = pl.program_id(0)

    @pl.when(jnp.logical_or(i == 0, be_ref[i] != be_ref[jnp.maximum(i - 1, 0)]))
    def _():
        wg_b[...] = wg_ref[...].astype(BF16)
        wu_b[...] = wu_ref[...].astype(BF16)
        wd_b[...] = wd_ref[...].astype(BF16)

    @pl.when(i < nb_ref[0])
    def _():
        x = _load_packed_tiles(xs_ref, EXPERT_BLOCK)
        h = (_silu(_dot(x, wg_b[...])) * _dot(x, wu_b[...])).astype(BF16)
        _store_row_tiles(ys_ref, _dot(h, wd_b[...]))


def _experts(xs, block_e, nb_used, wg, wu, wd):
    rows = xs.shape[0] // PACK_TILE
    n_blocks = rows // EXPERT_BLOCK
    used = lambda i, be, nb: (jnp.minimum(i, nb[0] - 1), 0)
    in_rows = pl.BlockSpec((EXPERT_BLOCK * PACK_TILE, LANES), used)
    grid_spec = pltpu.PrefetchScalarGridSpec(
        num_scalar_prefetch=2,
        grid=(n_blocks,),
        in_specs=[in_rows,
                  pl.BlockSpec((None, D_MODEL, D_EXPERT), lambda i, be, nb: (be[i], 0, 0)),
                  pl.BlockSpec((None, D_MODEL, D_EXPERT), lambda i, be, nb: (be[i], 0, 0)),
                  pl.BlockSpec((None, D_EXPERT, D_MODEL), lambda i, be, nb: (be[i], 0, 0))],
        out_specs=pl.BlockSpec((EXPERT_BLOCK * ROW_TILE, LANES), used),
        scratch_shapes=[pltpu.VMEM((D_MODEL, D_EXPERT), BF16), pltpu.VMEM((D_MODEL, D_EXPERT), BF16),
                        pltpu.VMEM((D_EXPERT, D_MODEL), BF16)],
    )
    return pl.pallas_call(
        _expert_body,
        name="experts",
        grid_spec=grid_spec,
        out_shape=jax.ShapeDtypeStruct((rows * ROW_TILE, LANES), F32),
        compiler_params=_params(("arbitrary",)),
    )(block_e, nb_used, xs, wg, wu, wd)


def _combine_body(dest_ref, dnext_ref, wts_ref, x1_ref, ys_ref, wsg, wsu, wsd, g_ref, b_ref,
                  y_ref, buf, sems):
    i = pl.program_id(0)
    n_tiles = pl.num_programs(0)
    tc = x1_ref.shape[0]
    slot = i % 2

    def issue(d_ref, s, g, carry):
        for u in range(ROW_UNROLL):
            t = g * ROW_UNROLL + u
            for k in range(TOP_K):
                pltpu.make_async_copy(_token_tile(ys_ref, d_ref[k, t]),
                                      _token_tile(buf.at[s, k], t),
                                      sems.at[s]).start(priority=k % 2)
        return carry

    n_groups = tc // ROW_UNROLL

    @pl.when(i == 0)
    def _():
        lax.fori_loop(0, n_groups, functools.partial(issue, dest_ref, slot), 0)

    @pl.when(i + 1 < n_tiles)
    def _():
        lax.fori_loop(0, n_groups, functools.partial(issue, dnext_ref, 1 - slot), 0)

    x1 = x1_ref[...]
    xb = x1.astype(BF16)
    hs = (_silu(_dot(xb, wsg[...])) * _dot(xb, wsu[...])).astype(BF16)
    shared = _dot(hs, wsd[...])

    for k in range(TOP_K):
        pltpu.make_async_copy(ys_ref.at[pl.ds(0, tc * ROW_TILE)], buf.at[slot, k],
                              sems.at[slot]).wait()

    w = wts_ref[...]
    routed = w[:, 0:1] * _load_row_tiles(buf.at[slot, 0], tc)
    for k in range(1, TOP_K):
        routed = routed + w[:, k:k + 1] * _load_row_tiles(buf.at[slot, k], tc)
    y_ref[...] = _layer_norm(DEEPNORM_ALPHA * x1 + (routed + shared), g_ref[...], b_ref[...])


def _combine(dest, wts_t, x1, ys, wsg, wsu, wsd, ln_g, ln_b, tc):
    n = x1.shape[0]
    last = n // tc - 1
    dest_cur = pl.BlockSpec((TOP_K, tc), lambda i: (0, i), memory_space=pltpu.SMEM)
    dest_next = pl.BlockSpec((TOP_K, tc), lambda i: (0, jnp.minimum(i + 1, last)),
                             memory_space=pltpu.SMEM)
    full = lambda a: pl.BlockSpec(a.shape, lambda i: (0, 0))
    rows = lambda w: pl.BlockSpec((tc, w), lambda i: (i, 0))
    row_bytes = D_MODEL * 4
    need = tc * row_bytes * (2 * TOP_K + 4) + 14 * 1024 * 1024
    return pl.pallas_call(
        _combine_body,
        name="combine",
        grid=(n // tc,),
        in_specs=[dest_cur, dest_next, rows(LANES), rows(D_MODEL),
                  pl.BlockSpec(memory_space=pl.ANY),
                  full(wsg), full(wsu), full(wsd), full(ln_g), full(ln_b)],
        out_specs=rows(D_MODEL),
        out_shape=jax.ShapeDtypeStruct((n, D_MODEL), F32),
        scratch_shapes=[pltpu.VMEM((2, TOP_K, tc * ROW_TILE, LANES), F32),
                        pltpu.SemaphoreType.DMA((2,))],
        compiler_params=_params(("arbitrary",), max(VMEM_LIMIT, need)),
    )(dest, dest, wts_t, x1, ys, wsg, wsu, wsd, ln_g, ln_b)


def _segments(cnt_first, n_first, reserve):
    blk = EXPERT_BLOCK
    assert reserve <= blk
    n_blocks = -(-(n_first * TOP_K + N_EXPERTS * reserve) // blk) + N_EXPERTS
    counts = cnt_first[:, 0].astype(I32)
    padded = (counts + reserve + blk - 1) // blk * blk
    ends = jnp.cumsum(padded)
    pstart = ends - padded
    nb_used = (ends[-1] // blk).astype(I32).reshape(1)
    block_row0 = jnp.arange(n_blocks, dtype=I32) * blk
    block_e = jnp.minimum(jnp.sum((ends[None, :] <= block_row0[:, None]).astype(I32), axis=1),
                          N_EXPERTS - 1).astype(I32)
    return pstart, counts, block_e, nb_used, n_blocks * blk


def kernel(x_prompt, x_sample, cache_k, cache_v, state_conv, page_table, w_in, lam_q1, lam_k1,
           lam_q2, lam_k2, subln_g, w_dw, b_dw, conv_ln_g, conv_ln_b, w_o, ln1_g, ln1_b,
           w_router, b_router, w_gate, w_up, w_down, w_sh_gate, w_sh_up, w_sh_down, ln2_g,
           ln2_b):
    batch, seq, _ = x_prompt.shape
    dec_b, dec_s, _ = x_sample.shape
    n_p, n_s = batch * seq, dec_b * dec_s
    layer = 0
    lam_init = _lambda_init(layer)
    row = lambda a: a[layer].reshape(1, -1)

    w_in_b = w_in[layer].astype(BF16)
    wo = w_o[layer].astype(BF16)
    wo_a, wo_c = wo[:ATTN_WIDTH], wo[ATTN_WIDTH:]
    wr_t = w_router[layer].T
    wr_hi = wr_t.astype(BF16)
    wr_lo = (wr_t - wr_hi.astype(F32)).astype(BF16)
    br = b_router[layer].reshape(-1, 1)
    moe_w = (w_gate[layer], w_up[layer], w_down[layer]) + tuple(
        w[layer].astype(BF16) for w in (w_sh_gate, w_sh_up, w_sh_down))
    lam_vecs = (row(lam_q1), row(lam_k1), row(lam_q2), row(lam_k2))
    g_sub = row(subln_g)
    conv_w = (w_dw[layer], row(b_dw), row(conv_ln_g), row(conv_ln_b))
    ln1 = (row(ln1_g), row(ln1_b))
    ln2 = (row(ln2_g), row(ln2_b))

    xp = x_prompt.reshape(n_p, D_MODEL)
    wq_t = w_in_b[:, :QK_WIDTH].T
    wv_t = w_in_b[:, 2 * QK_WIDTH:2 * QK_WIDTH + ATTN_WIDTH].T
    qt_p, k_p, v_p, u_p, kb_p, vt_p = _inproj(xp, w_in_b, ATTN_TILE, (wq_t, wv_t))
    o_p = _attn_prompt(qt_p, kb_p, vt_p, lam_vecs, g_sub.reshape(V_DIM, 1), lam_init, batch, seq)
    c_p = _conv_prompt(u_p, *conv_w, batch, seq, 512)
    cnt0 = jnp.zeros((N_EXPERTS, LANES), F32)
    x1_p, xt_p, idx_p, wts_p, rank_p, cnt_p = _mix(o_p, c_p, xp, wo_a, wo_c, *ln1, wr_hi, wr_lo,
                                                   br, cnt0, 512)
    pstart, counts_p, block_e, nb_used, n_rows = _segments(cnt_p, n_p, n_s)
    pstart_col = pstart.astype(F32).reshape(N_EXPERTS, 1)
    dest_p = _dest(idx_p, rank_p, pstart_col, 256)

    xs = x_sample.reshape(n_s, D_MODEL)
    q_s, k_s, v_s, u_s = _inproj(xs, w_in_b, n_s)
    pool = cache_k.shape[1]
    page_rows = lambda a: a[layer].reshape(pool, PAGE_ROWS, V_DIM)
    new_rows = lambda a: a.reshape(dec_b, dec_s * N_HEADS, V_DIM)
    o_s, rows_in = _attn_sample(
        q_s.reshape(dec_b, dec_s, QK_WIDTH), new_rows(k_s), new_rows(v_s), page_rows(cache_k),
        page_rows(cache_v), page_table, lam_vecs, g_sub, lam_init,
        xt_p.reshape(n_p, PACK_TILE, LANES), dest_p, pstart, counts_p, n_rows, n_s)
    c_s, st_s = _conv_sample(u_s.reshape(dec_b, dec_s, CONV_CH), state_conv[layer], *conv_w)
    x1_s, xt_s, idx_s, wts_s, rank_s, _ = _mix(
        o_s.reshape(n_s, ATTN_WIDTH), c_s.reshape(n_s, CONV_CH), xs, wo_a, wo_c, *ln1, wr_hi,
        wr_lo, br, cnt_p, n_s)
    dest_s = _dest(idx_s, rank_s, pstart_col, 128)
    rows_in = _dispatch(xt_s.reshape(n_s, PACK_TILE, LANES), dest_s, rows_in, 128)

    wg, wu, wd, wsg, wsu, wsd = moe_w
    rows_out = _experts(rows_in.reshape(n_rows * PACK_TILE, LANES), block_e, nb_used, wg, wu, wd)
    y_p = _combine(dest_p, wts_p, x1_p, rows_out, wsg, wsu, wsd, *ln2, 512)
    y_s = _combine(dest_s, wts_s, x1_s, rows_out, wsg, wsu, wsd, *ln2, 128)

    keep = CONV_WIDTH - 1
    u_p3 = u_p.reshape(batch, seq, CONV_CH)
    return (y_p.reshape(batch, seq, D_MODEL),
            y_s.reshape(dec_b, dec_s, D_MODEL),
            k_p.reshape(1, batch, seq, N_HEADS, V_DIM),
            v_p.reshape(1, batch, seq, N_HEADS, V_DIM),
            u_p3[:, seq - keep:, :][None],
            k_s.reshape(1, dec_b, dec_s, N_HEADS, V_DIM),
            v_s.reshape(1, dec_b, dec_s, N_HEADS, V_DIM),
            st_s[None])
```
